```python
import math
import jax
import jax.numpy as jnp
from jax import lax
import numpy as np

D_MODEL = 4096
BATCH = 2
SEQ = 4096
DEPTH = 1
DEC_BATCH = 128
DEC_SEQ = 1
PAST_LEN = 8192
PAGE_SIZE = 128

HEAD_DIM = 64
N_ATTN_HEADS = 32
N_KV_HEADS = 4
GQA_GROUP = N_ATTN_HEADS // N_KV_HEADS
WINDOW = 128
ATTN_BLOCK = 128
N_BUCKETS = 32
MAX_DISTANCE = 128
N_RET_HEADS = 8
RET_KDIM = 128
RET_VDIM = 256
RET_CHUNK = 128
ROPE_BASE = 10000.0
D_FF = 11008
CONV_W = 3
PLE_DIM = 256
EPS = 1e-6

ATTN_WIDTH = N_ATTN_HEADS * HEAD_DIM
KV_WIDTH = N_KV_HEADS * HEAD_DIM
RET_QK_WIDTH = N_RET_HEADS * RET_KDIM
RET_V_WIDTH = N_RET_HEADS * RET_VDIM
MIX_WIDTH = ATTN_WIDTH + RET_V_WIDTH
IN_SIZES = (ATTN_WIDTH, KV_WIDTH, KV_WIDTH, RET_QK_WIDTH, RET_QK_WIDTH, RET_V_WIDTH, RET_V_WIDTH)
IN_WIDTH = sum(IN_SIZES)
IN_SPLITS = tuple(int(s) for s in np.cumsum(IN_SIZES)[:-1])

kernel_name = 'hybrid_swa_retention_convffn_step'


def rmsnorm(x, g):
    xf = x.astype(jnp.float32)
    y = xf * lax.rsqrt(jnp.mean(xf * xf, axis=-1, keepdims=True) + EPS)
    return (y * g.astype(jnp.float32)).astype(x.dtype)


def t5_bucket(dist):
    n = jnp.maximum(dist, 0)
    max_exact = N_BUCKETS // 2
    nf = jnp.maximum(n, 1).astype(jnp.float32)
    large = max_exact + (jnp.log(nf / max_exact) / math.log(MAX_DISTANCE / max_exact)
                         * (N_BUCKETS - max_exact)).astype(jnp.int32)
    return jnp.where(n < max_exact, n, jnp.minimum(large, N_BUCKETS - 1))


def window_attention(q, k, v, q_pos, k_pos, rel_bias, sinks):
    n_blk, nq = q_pos.shape
    nk = k_pos.shape[1]
    dist = q_pos[:, :, None] - k_pos[:, None, :]
    valid = (dist >= 0) & (dist <= WINDOW) & (k_pos[:, None, :] >= 0)
    bias = jnp.moveaxis(rel_bias.astype(jnp.float32)[t5_bucket(dist)], -1, 1)
    bias = bias.reshape(n_blk, N_KV_HEADS, GQA_GROUP, nq, nk)
    s = jnp.einsum('bnqhgd,bnkhd->bnhgqk', q, k,
                   preferred_element_type=jnp.float32) * HEAD_DIM ** -0.5 + bias
    s = jnp.where(valid[:, None, None], s, -jnp.inf)
    sink = sinks.astype(jnp.float32).reshape(1, 1, N_KV_HEADS, GQA_GROUP, 1, 1)
    m = jnp.maximum(jnp.max(s, axis=-1, keepdims=True), sink)
    e = jnp.exp(s - m)
    probs = e / (jnp.sum(e, axis=-1, keepdims=True) + jnp.exp(sink - m))
    return jnp.einsum('bnhgqk,bnkhd->bnqhgd', probs.astype(v.dtype), v)


def attn_prompt(q, k, v, pos, rel_bias, sinks):
    b, t = q.shape[:2]
    nb = t // ATTN_BLOCK
    qb = q.reshape(b, nb, ATTN_BLOCK, N_KV_HEADS, GQA_GROUP, HEAD_DIM)

    def with_prev(a):
        a = a.reshape(b, nb, ATTN_BLOCK, N_KV_HEADS, HEAD_DIM)
        prev = jnp.concatenate([jnp.zeros_like(a[:, :1]), a[:, :-1]], axis=1)
        return jnp.concatenate([prev, a], axis=2)

    q_pos = pos.reshape(nb, ATTN_BLOCK)
    k_pos = jnp.concatenate([q_pos - ATTN_BLOCK, q_pos], axis=1)
    o = window_attention(qb, with_prev(k), with_prev(v), q_pos, k_pos, rel_bias, sinks)
    return o.reshape(b, t, ATTN_WIDTH)


def attn_sample(q, k, v, win_k, win_v, pos, rel_bias, sinks):
    b, s = q.shape[:2]
    wb = win_k.shape[1]
    kk = jnp.concatenate([win_k.astype(k.dtype), k], axis=1)
    vv = jnp.concatenate([win_v.astype(v.dtype), v], axis=1)
    q_pos = pos[None]
    k_pos = (pos[0] - wb + jnp.arange(wb + s, dtype=jnp.int32))[None]
    o = window_attention(q.reshape(b, 1, s, N_KV_HEADS, GQA_GROUP, HEAD_DIM),
                         kk[:, None], vv[:, None], q_pos, k_pos, rel_bias, sinks)
    return o.reshape(b, s, ATTN_WIDTH), kk[:, -wb:], vv[:, -wb:]


def rotary(x, pos):
    half = x.shape[-1] // 2
    inv = ROPE_BASE ** (-jnp.arange(half, dtype=jnp.float32) / half)
    ang = pos.astype(jnp.float32)[:, None] * inv[None]
    cos = jnp.cos(ang)[None, :, None, :]
    sin = jnp.sin(ang)[None, :, None, :]
    xf = x.astype(jnp.float32)
    x1, x2 = xf[..., :half], xf[..., half:]
    return jnp.concatenate([x1 * cos - x2 * sin, x2 * cos + x1 * sin], axis=-1).astype(x.dtype)


def retention_chunk(s0, q, k, v, log_decay):
    L = q.shape[1]
    idx = jnp.arange(L, dtype=jnp.float32)
    diff = idx[:, None] - idx[None, :]
    dmask = jnp.where(diff[None] >= 0, jnp.exp(diff[None] * log_decay[:, None, None]), 0.0)
    qf, kf, vf = q.astype(jnp.float32), k.astype(jnp.float32), v.astype(jnp.float32)
    scores = jnp.einsum('blhd,bmhd->bhlm', qf, kf) * dmask
    o_intra = jnp.einsum('bhlm,bmhe->blhe', scores, vf)
    q_dec = jnp.exp((idx + 1.0)[:, None] * log_decay[None])
    o_inter = jnp.einsum('blhd,bhde->blhe', qf * q_dec[None, :, :, None], s0)
    k_dec = jnp.exp((L - 1.0 - idx)[:, None] * log_decay[None])
    s1 = (jnp.exp(L * log_decay)[None, :, None, None] * s0
          + jnp.einsum('blhd,blhe->bhde', kf * k_dec[None, :, :, None], vf))
    return s1, o_intra + o_inter


def retention(s0, q, k, v, log_decay):
    b, t = q.shape[:2]
    chunk = RET_CHUNK if t % RET_CHUNK == 0 else t
    nc = t // chunk

    def to_chunks(a):
        return jnp.swapaxes(a.reshape(b, nc, chunk, *a.shape[2:]), 0, 1)

    def step(s, qkv):
        return retention_chunk(s, qkv[0], qkv[1], qkv[2], log_decay)

    s1, o = lax.scan(step, s0, (to_chunks(q), to_chunks(k), to_chunks(v)))
    return s1, jnp.swapaxes(o, 0, 1).reshape(b, t, N_RET_HEADS, RET_VDIM)


def conv_ffn(h, prev, w_up, conv_w, conv_b, w_down):
    u = h @ w_up
    t = u.shape[1]
    up = jnp.concatenate([prev.astype(u.dtype), u], axis=1)
    c = conv_b.astype(u.dtype)
    for j in range(CONV_W):
        c = c + conv_w[j] * up[:, j:j + t]
    g, val = jnp.split(c, 2, axis=-1)
    y = (jax.nn.gelu(g, approximate=False) * val) @ w_down
    return y, up[:, -(CONV_W - 1):]


def decoder_layer(x, p, pos, past, lp, rel_bias, log_decay):
    (g_mix, w_in, g_q, g_k, sinks, w_out, g_ffn, w_up, conv_w, conv_b, w_down,
     g_ple, w_ple_gate, w_ple_proj) = lp
    b, t, _ = x.shape
    h = rmsnorm(x, g_mix)
    aq, ak, av, rq, rk, rv, rg = jnp.split(h @ w_in, IN_SPLITS, axis=-1)
    aq = rmsnorm(aq.reshape(b, t, N_ATTN_HEADS, HEAD_DIM), g_q)
    ak = rmsnorm(ak.reshape(b, t, N_KV_HEADS, HEAD_DIM), g_k)
    av = av.reshape(b, t, N_KV_HEADS, HEAD_DIM)
    rq = rotary(rq.reshape(b, t, N_RET_HEADS, RET_KDIM), pos)
    rk = rotary(rk.reshape(b, t, N_RET_HEADS, RET_KDIM), pos) * RET_KDIM ** -0.5
    rv = rv.reshape(b, t, N_RET_HEADS, RET_VDIM)
    if past is None:
        attn = attn_prompt(aq, ak, av, pos, rel_bias, sinks)
        new_k, new_v = ak[:, -WINDOW:], av[:, -WINDOW:]
        s0 = jnp.zeros((b, N_RET_HEADS, RET_KDIM, RET_VDIM), jnp.float32)
        conv_prev = jnp.zeros((b, CONV_W - 1, 2 * D_FF), x.dtype)
    else:
        win_k, win_v, s_past, conv_prev = past
        attn, new_k, new_v = attn_sample(aq, ak, av, win_k, win_v, pos, rel_bias, sinks)
        s0 = s_past.astype(jnp.float32)
    s1, ro = retention(s0, rq, rk, rv, log_decay)
    ro = ro * lax.rsqrt(jnp.mean(ro * ro, axis=-1, keepdims=True) + EPS)
    ret = (jax.nn.silu(rg.astype(jnp.float32)) * ro.reshape(b, t, RET_V_WIDTH)).astype(x.dtype)
    x = x + jnp.concatenate([attn.astype(x.dtype), ret], axis=-1) @ w_out
    f, conv_new = conv_ffn(rmsnorm(x, g_ffn), conv_prev, w_up, conv_w, conv_b, w_down)
    x = x + f
    gate = jax.nn.sigmoid(rmsnorm(x, g_ple) @ w_ple_gate)
    x = x + gate * (p @ w_ple_proj)
    return x, new_k, new_v, s1, conv_new


def setup_inputs(seed: int = 0) -> dict:
    key = jax.random.key(seed)
    ks = jax.random.split(key, 24)
    f32 = jnp.float32
    win = min(WINDOW, PAST_LEN)

    def nrm(k, shape, scale):
        return jax.random.normal(k, shape, f32) * scale

    return {
        'x_prompt': nrm(ks[0], (BATCH, SEQ, D_MODEL), 1.0),
        'x_sample': nrm(ks[1], (DEC_BATCH, DEC_SEQ, D_MODEL), 1.0),
        'p_prompt': nrm(ks[2], (DEPTH, BATCH, SEQ, PLE_DIM), 1.0),
        'p_sample': nrm(ks[3], (DEPTH, DEC_BATCH, DEC_SEQ, PLE_DIM), 1.0),
        'cache_win_k': nrm(ks[4], (DEPTH, DEC_BATCH, win, N_KV_HEADS, HEAD_DIM), 1.0),
        'cache_win_v': nrm(ks[5], (DEPTH, DEC_BATCH, win, N_KV_HEADS, HEAD_DIM), 1.0),
        'state_ret': nrm(ks[6], (DEPTH, DEC_BATCH, N_RET_HEADS, RET_KDIM, RET_VDIM), 1.0),
        'state_conv': nrm(ks[7], (DEPTH, DEC_BATCH, CONV_W - 1, 2 * D_FF), 1.0),
        'rel_bias': nrm(ks[8], (N_BUCKETS, N_ATTN_HEADS), 0.5),
        'g_mix': 1.0 + nrm(ks[9], (DEPTH, D_MODEL), 0.02),
        'w_in': nrm(ks[10], (DEPTH, D_MODEL, IN_WIDTH), D_MODEL ** -0.5),
        'g_q': 1.0 + nrm(ks[11], (DEPTH, HEAD_DIM), 0.02),
        'g_k': 1.0 + nrm(ks[12], (DEPTH, HEAD_DIM), 0.02),
        'sinks': nrm(ks[13], (DEPTH, N_ATTN_HEADS), 0.5),
        'w_out': nrm(ks[14], (DEPTH, MIX_WIDTH, D_MODEL), MIX_WIDTH ** -0.5),
        'g_ffn': 1.0 + nrm(ks[15], (DEPTH, D_MODEL), 0.02),
        'w_up': nrm(ks[16], (DEPTH, D_MODEL, 2 * D_FF), D_MODEL ** -0.5),
        'conv_w': nrm(ks[17], (DEPTH, CONV_W, 2 * D_FF), CONV_W ** -0.5),
        'conv_b': nrm(ks[18], (DEPTH, 2 * D_FF), 0.01),
        'w_down': nrm(ks[19], (DEPTH, D_FF, D_MODEL), D_FF ** -0.5),
        'g_ple': 1.0 + nrm(ks[20], (DEPTH, D_MODEL), 0.02),
        'w_ple_gate': nrm(ks[21], (DEPTH, D_MODEL, D_MODEL), D_MODEL ** -0.5),
        'w_ple_proj': nrm(ks[22], (DEPTH, PLE_DIM, D_MODEL), PLE_DIM ** -0.5),
    }


def reference(x_prompt, x_sample, p_prompt, p_sample, cache_win_k, cache_win_v, state_ret, state_conv,
              rel_bias, g_mix, w_in, g_q, g_k, sinks, w_out, g_ffn, w_up, conv_w, conv_b, w_down,
              g_ple, w_ple_gate, w_ple_proj):
    log_decay = jnp.log(1.0 - 2.0 ** (-5.0 - jnp.arange(N_RET_HEADS, dtype=jnp.float32)))
    pos_prompt = jnp.arange(x_prompt.shape[1], dtype=jnp.int32)
    pos_sample = PAST_LEN + jnp.arange(x_sample.shape[1], dtype=jnp.int32)
    yp, ys = x_prompt, x_sample
    kp_l, vp_l, rp_l, cp_l, ks_l, vs_l, rs_l, cs_l = [], [], [], [], [], [], [], []
    for l in range(DEPTH):
        lp = (g_mix[l], w_in[l], g_q[l], g_k[l], sinks[l], w_out[l], g_ffn[l], w_up[l], conv_w[l],
              conv_b[l], w_down[l], g_ple[l], w_ple_gate[l], w_ple_proj[l])
        yp, kp, vp, rp, cp = decoder_layer(yp, p_prompt[l], pos_prompt, None, lp, rel_bias, log_decay)
        past = (cache_win_k[l], cache_win_v[l], state_ret[l], state_conv[l])
        ys, ksn, vsn, rsn, csn = decoder_layer(ys, p_sample[l], pos_sample, past, lp, rel_bias, log_decay)
        kp_l.append(kp.astype(cache_win_k.dtype))
        vp_l.append(vp.astype(cache_win_v.dtype))
        rp_l.append(rp.astype(state_ret.dtype))
        cp_l.append(cp.astype(state_conv.dtype))
        ks_l.append(ksn.astype(cache_win_k.dtype))
        vs_l.append(vsn.astype(cache_win_v.dtype))
        rs_l.append(rsn.astype(state_ret.dtype))
        cs_l.append(csn.astype(state_conv.dtype))
    return (yp, ys, jnp.stack(kp_l), jnp.stack(vp_l), jnp.stack(rp_l), jnp.stack(cp_l),
            jnp.stack(ks_l), jnp.stack(vs_l), jnp.stack(rs_l), jnp.stack(cs_l))
```

```python
import functools
import math

import jax
import jax.numpy as jnp
from jax import lax
from jax.experimental import pallas as pl
from jax.experimental.pallas import tpu as pltpu

F32 = jnp.float32
BF16 = jnp.bfloat16

D_MODEL = 4096
HEAD_DIM = 64
N_ATTN_HEADS = 32
N_KV_HEADS = 4
GQA_GROUP = N_ATTN_HEADS // N_KV_HEADS
WINDOW = 128
ATTN_BLOCK = 128
N_BUCKETS = 32
MAX_DISTANCE = 128
N_RET_HEADS = 8
RET_KDIM = 128
RET_VDIM = 256
RET_CHUNK = 128
ROPE_BASE = 10000.0
D_FF = 11008
CONV_W = 3
EPS = 1e-6

ATTN_WIDTH = N_ATTN_HEADS * HEAD_DIM
KV_WIDTH = N_KV_HEADS * HEAD_DIM
RET_QK_WIDTH = N_RET_HEADS * RET_KDIM
RET_V_WIDTH = N_RET_HEADS * RET_VDIM
IN_SIZES = (ATTN_WIDTH, KV_WIDTH, KV_WIDTH, RET_QK_WIDTH, RET_QK_WIDTH, RET_V_WIDTH, RET_V_WIDTH)
IN_WIDTH = sum(IN_SIZES)
IN_SPLITS = tuple(sum(IN_SIZES[:n + 1]) for n in range(len(IN_SIZES) - 1))
PAST_LEN = 8192

COL_AQ = 0
COL_RV = ATTN_WIDTH
COL_RG = COL_RV + RET_V_WIDTH
COL_RQ = COL_RG + RET_V_WIDTH
COL_RK = COL_RQ + RET_QK_WIDTH
COL_AK = COL_RK + RET_QK_WIDTH
COL_AV = COL_AK + KV_WIDTH

VMEM_LIMIT_BYTES = 56 * 1024 * 1024
FF_TILE = 256
N_FF_TILES = D_FF // FF_TILE
NORM_ROWS = 32


def _params(n_axes):
    return pltpu.CompilerParams(dimension_semantics=("arbitrary",) * n_axes,
                                vmem_limit_bytes=VMEM_LIMIT_BYTES)


def _rms(x, g):
    y = x * lax.rsqrt(jnp.mean(x * x, axis=-1, keepdims=True) + EPS)
    return y * g


def _norm_rows_to_bf16(x_ref, g_ref, h_ref):
    def body(c, carry):
        r0 = pl.multiple_of(c * NORM_ROWS, NORM_ROWS)
        h_ref[pl.ds(r0, NORM_ROWS), :] = _rms(x_ref[pl.ds(r0, NORM_ROWS), :], g_ref[...]).astype(h_ref.dtype)
        return carry
    lax.fori_loop(0, x_ref.shape[0] // NORM_ROWS, body, 0)


def _norm_mm_kernel(x_ref, g_ref, w_ref, o_ref, h_ref):
    @pl.when(pl.program_id(1) == 0)
    def _():
        _norm_rows_to_bf16(x_ref, g_ref, h_ref)
    o_ref[...] = jnp.dot(h_ref[...], w_ref[...], preferred_element_type=F32)


def _norm_matmul(x, g, w, tm, tn):
    m, k = x.shape
    n = w.shape[1]
    return pl.pallas_call(
        _norm_mm_kernel,
        out_shape=jax.ShapeDtypeStruct((m, n), F32),
        grid=(m // tm, n // tn),
        in_specs=[pl.BlockSpec((tm, k), lambda i, j: (i, 0)),
                  pl.BlockSpec((1, k), lambda i, j: (0, 0)),
                  pl.BlockSpec((k, tn), lambda i, j: (0, j))],
        out_specs=pl.BlockSpec((tm, tn), lambda i, j: (i, j)),
        scratch_shapes=[pltpu.VMEM((tm, k), BF16)],
        compiler_params=_params(2),
        name="in_proj",
    )(x, g, w)


def _outproj_kernel(a_ref, r_ref, w_ref, x_ref, o_ref):
    mix = jnp.concatenate([a_ref[...], r_ref[...]], axis=1)
    o_ref[...] = x_ref[...] + jnp.dot(mix, w_ref[...], preferred_element_type=F32)


def _out_proj(a, r, w, x, tm, tn):
    m = a.shape[0]
    ka, kr = a.shape[1], r.shape[1]
    n = w.shape[1]
    return pl.pallas_call(
        _outproj_kernel,
        out_shape=jax.ShapeDtypeStruct((m, n), F32),
        grid=(m // tm, n // tn),
        in_specs=[pl.BlockSpec((tm, ka), lambda i, j: (i, 0)),
                  pl.BlockSpec((tm, kr), lambda i, j: (i, 0)),
                  pl.BlockSpec((ka + kr, tn), lambda i, j: (0, j)),
                  pl.BlockSpec((tm, tn), lambda i, j: (i, j))],
        out_specs=pl.BlockSpec((tm, tn), lambda i, j: (i, j)),
        compiler_params=_params(2),
        name="out_proj",
    )(a, r, w, x)


def _down_kernel(a_ref, w_ref, x_ref, o_ref):
    o_ref[...] = x_ref[...] + jnp.dot(a_ref[...], w_ref[...], preferred_element_type=F32)


def _down_proj(a, w, x, tm, tn):
    m, k = a.shape
    n = w.shape[1]
    return pl.pallas_call(
        _down_kernel,
        out_shape=jax.ShapeDtypeStruct((m, n), F32),
        grid=(m // tm, n // tn),
        in_specs=[pl.BlockSpec((tm, k), lambda i, j: (i, 0)),
                  pl.BlockSpec((k, tn), lambda i, j: (0, j)),
                  pl.BlockSpec((tm, tn), lambda i, j: (i, j))],
        out_specs=pl.BlockSpec((tm, tn), lambda i, j: (i, j)),
        compiler_params=_params(2),
        name="down_proj",
    )(a, w, x)


def _ple_kernel(x_ref, g_ref, wg_ref, p_ref, wp_ref, xr_ref, o_ref, h_ref):
    @pl.when(pl.program_id(1) == 0)
    def _():
        _norm_rows_to_bf16(x_ref, g_ref, h_ref)
    z = jnp.dot(h_ref[...], wg_ref[...], preferred_element_type=F32)
    pp = jnp.dot(p_ref[...].astype(BF16), wp_ref[...], preferred_element_type=F32)
    o_ref[...] = xr_ref[...] + jax.nn.sigmoid(z) * pp


def _ple(x, g, wg, p, wp, tm, tn):
    m, k = x.shape
    n = wg.shape[1]
    kp = p.shape[1]
    return pl.pallas_call(
        _ple_kernel,
        out_shape=jax.ShapeDtypeStruct((m, n), F32),
        grid=(m // tm, n // tn),
        in_specs=[pl.BlockSpec((tm, k), lambda i, j: (i, 0)),
                  pl.BlockSpec((1, k), lambda i, j: (0, 0)),
                  pl.BlockSpec((k, tn), lambda i, j: (0, j)),
                  pl.BlockSpec((tm, kp), lambda i, j: (i, 0)),
                  pl.BlockSpec((kp, tn), lambda i, j: (0, j)),
                  pl.BlockSpec((tm, tn), lambda i, j: (i, j))],
        out_specs=pl.BlockSpec((tm, tn), lambda i, j: (i, j)),
        scratch_shapes=[pltpu.VMEM((tm, k), BF16)],
        compiler_params=_params(2),
        name="ple",
    )(x, g, wg, p, wp, x)


def _gelu_erf(x):
    return 0.5 * x * (1.0 + lax.erf(x * math.sqrt(0.5)))


def _conv_taps(cb_ref, cw_ref, um2, um1, u):
    c = cb_ref[...] + cw_ref[0:1, :] * um2
    c = c + cw_ref[1:2, :] * um1
    return c + cw_ref[2:3, :] * u


def _convglu_prompt_kernel(x_ref, g_ref, wg_ref, wv_ref, cwg_ref, cwv_ref, cbg_ref, cbv_ref,
                           a_ref, sg_ref, sv_ref, h_ref, carry_g_ref, carry_v_ref, *, tiles_per_seq):
    i = pl.program_id(0)
    j = pl.program_id(1)

    @pl.when(j == 0)
    def _():
        _norm_rows_to_bf16(x_ref, g_ref, h_ref)

    tm = x_ref.shape[0]
    seq_start = (i % tiles_per_seq) == 0
    row = lax.broadcasted_iota(jnp.int32, (tm, 1), 0)

    def half(w_ref, cw_ref, cb_ref, carry_ref, s_ref):
        u = jnp.dot(h_ref[...], w_ref[...], preferred_element_type=F32)
        prev = jnp.where(seq_start, 0.0, carry_ref[j])
        tail = u[tm - (CONV_W - 1):, :]
        carry_ref[j] = tail
        s_ref[0] = tail
        um1 = jnp.where(row == 0, prev[1:2, :], pltpu.roll(u, 1, axis=0))
        um2 = jnp.where(row == 0, prev[0:1, :],
                        jnp.where(row == 1, prev[1:2, :], pltpu.roll(u, 2, axis=0)))
        return _conv_taps(cb_ref, cw_ref, um2, um1, u)

    cg = half(wg_ref, cwg_ref, cbg_ref, carry_g_ref, sg_ref)
    cv = half(wv_ref, cwv_ref, cbv_ref, carry_v_ref, sv_ref)
    a_ref[...] = (_gelu_erf(cg) * cv).astype(a_ref.dtype)


def _convglu_prompt(x, g, w_up, conv_w, conv_b, n_seq, tm):
    m, k = x.shape
    seq = m // n_seq
    tiles_per_seq = seq // tm
    nt = N_FF_TILES
    tn = FF_TILE
    a, sg, sv = pl.pallas_call(
        functools.partial(_convglu_prompt_kernel, tiles_per_seq=tiles_per_seq),
        out_shape=(jax.ShapeDtypeStruct((m, D_FF), BF16),
                   jax.ShapeDtypeStruct((m // tm, CONV_W - 1, D_FF), F32),
                   jax.ShapeDtypeStruct((m // tm, CONV_W - 1, D_FF), F32)),
        grid=(m // tm, nt),
        in_specs=[pl.BlockSpec((tm, k), lambda i, j: (i, 0)),
                  pl.BlockSpec((1, k), lambda i, j: (0, 0)),
                  pl.BlockSpec((k, tn), lambda i, j: (0, j)),
                  pl.BlockSpec((k, tn), lambda i, j: (0, j + nt)),
                  pl.BlockSpec((CONV_W, tn), lambda i, j: (0, j)),
                  pl.BlockSpec((CONV_W, tn), lambda i, j: (0, j + nt)),
                  pl.BlockSpec((1, tn), lambda i, j: (0, j)),
                  pl.BlockSpec((1, tn), lambda i, j: (0, j + nt))],
        out_specs=(pl.BlockSpec((tm, tn), lambda i, j: (i, j)),
                   pl.BlockSpec((1, CONV_W - 1, tn), lambda i, j: (i, 0, j)),
                   pl.BlockSpec((1, CONV_W - 1, tn), lambda i, j: (i, 0, j))),
        scratch_shapes=[pltpu.VMEM((tm, k), BF16),
                        pltpu.VMEM((nt, CONV_W - 1, tn), F32),
                        pltpu.VMEM((nt, CONV_W - 1, tn), F32)],
        compiler_params=_params(2),
        name="convglu_prompt",
    )(x, g, w_up, w_up, conv_w, conv_w, conv_b, conv_b)
    tails = jnp.concatenate([sg, sv], axis=-1)
    return a, tails[tiles_per_seq - 1::tiles_per_seq]


def _convglu_sample_kernel(x_ref, g_ref, wg_ref, wv_ref, cwg_ref, cwv_ref, cbg_ref, cbv_ref,
                           p0g_ref, p0v_ref, p1g_ref, p1v_ref, a_ref, ug_ref, uv_ref, h_ref):
    @pl.when(pl.program_id(0) == 0)
    def _():
        _norm_rows_to_bf16(x_ref, g_ref, h_ref)

    def half(w_ref, cw_ref, cb_ref, p0_ref, p1_ref, u_ref):
        u = jnp.dot(h_ref[...], w_ref[...], preferred_element_type=F32)
        u_ref[...] = u
        return _conv_taps(cb_ref, cw_ref, p0_ref[...], p1_ref[...], u)

    cg = half(wg_ref, cwg_ref, cbg_ref, p0g_ref, p1g_ref, ug_ref)
    cv = half(wv_ref, cwv_ref, cbv_ref, p0v_ref, p1v_ref, uv_ref)
    a_ref[...] = (_gelu_erf(cg) * cv).astype(a_ref.dtype)


def _convglu_sample(x, g, w_up, conv_w, conv_b, state_conv):
    m, k = x.shape
    nt = N_FF_TILES
    tn = FF_TILE
    prev = state_conv.reshape(m, (CONV_W - 1) * 2 * D_FF)
    col = lambda off: (lambda j: (0, j + off))
    a, ug, uv = pl.pallas_call(
        _convglu_sample_kernel,
        out_shape=(jax.ShapeDtypeStruct((m, D_FF), BF16),
                   jax.ShapeDtypeStruct((m, D_FF), F32),
                   jax.ShapeDtypeStruct((m, D_FF), F32)),
        grid=(nt,),
        in_specs=[pl.BlockSpec((m, k), lambda j: (0, 0)),
                  pl.BlockSpec((1, k), lambda j: (0, 0)),
                  pl.BlockSpec((k, tn), col(0)),
                  pl.BlockSpec((k, tn), col(nt)),
                  pl.BlockSpec((CONV_W, tn), col(0)),
                  pl.BlockSpec((CONV_W, tn), col(nt)),
                  pl.BlockSpec((1, tn), col(0)),
                  pl.BlockSpec((1, tn), col(nt)),
                  pl.BlockSpec((m, tn), col(0)),
                  pl.BlockSpec((m, tn), col(nt)),
                  pl.BlockSpec((m, tn), col(2 * nt)),
                  pl.BlockSpec((m, tn), col(3 * nt))],
        out_specs=(pl.BlockSpec((m, tn), col(0)),
                   pl.BlockSpec((m, tn), col(0)),
                   pl.BlockSpec((m, tn), col(0))),
        scratch_shapes=[pltpu.VMEM((m, k), BF16)],
        compiler_params=_params(1),
        name="convglu_sample",
    )(x, g, w_up, w_up, conv_w, conv_w, conv_b, conv_b, prev, prev, prev, prev)
    u = jnp.concatenate([ug, uv], axis=-1)
    conv_new = jnp.stack([state_conv[:, 1, :], u], axis=1)
    return a, conv_new


def _attn_prompt_kernel(q_ref, kc_ref, kp_ref, vc_ref, vp_ref, bucket_ref, rb_ref, sinks_ref,
                        gq_ref, gk_ref, o_ref, kn_ref, bias_ref, *, blocks_per_seq):
    r = pl.program_id(0)
    blk = ATTN_BLOCK

    @pl.when(r == 0)
    def _():
        bucket = bucket_ref[...]

        def per_head(h, carry):
            acc = jnp.full(bucket.shape, -jnp.inf, F32)
            for b in range(N_BUCKETS):
                acc = jnp.where(bucket == b, rb_ref[b, h], acc)
            bias_ref[pl.ds(pl.multiple_of(h * blk, blk), blk), :] = acc
            return carry
        lax.fori_loop(0, N_ATTN_HEADS, per_head, 0)

    n_masked = jnp.where((r % blocks_per_seq) == 0, blk, 0)
    key_col = lax.broadcasted_iota(jnp.int32, (blk, 2 * blk), 1)
    before_start = key_col < n_masked
    gq = gq_ref[...]
    gk = gk_ref[...]

    kn_parts = []
    for j in range(N_KV_HEADS):
        hd = slice(j * HEAD_DIM, (j + 1) * HEAD_DIM)
        kn_c = _rms(kc_ref[:, hd], gk)
        kn_p = _rms(kp_ref[:, hd], gk)
        kn_parts.append(kn_c)
        kcat = jnp.concatenate([kn_p, kn_c], axis=0).astype(BF16)
        vcat = jnp.concatenate([vp_ref[:, hd], vc_ref[:, hd]], axis=0).astype(BF16)
        heads = range(j * GQA_GROUP, (j + 1) * GQA_GROUP)
        qs = jnp.concatenate(
            [_rms(q_ref[:, h * HEAD_DIM:(h + 1) * HEAD_DIM], gq) for h in heads], axis=0).astype(BF16)
        s = lax.dot_general(qs, kcat, (((1,), (1,)), ((), ())), preferred_element_type=F32)
        probs = []
        for g, h in enumerate(heads):
            sg = s[g * blk:(g + 1) * blk, :] * HEAD_DIM ** -0.5 + bias_ref[h * blk:(h + 1) * blk, :]
            sg = jnp.where(before_start, -jnp.inf, sg)
            sink = sinks_ref[h]
            m = jnp.maximum(jnp.max(sg, axis=-1, keepdims=True), sink)
            e = jnp.exp(sg - m)
            p = e / (jnp.sum(e, axis=-1, keepdims=True) + jnp.exp(sink - m))
            probs.append(p.astype(BF16))
        o = jnp.dot(jnp.concatenate(probs, axis=0), vcat, preferred_element_type=F32)
        for g, h in enumerate(heads):
            o_ref[:, h * HEAD_DIM:(h + 1) * HEAD_DIM] = o[g * blk:(g + 1) * blk, :].astype(o_ref.dtype)
    kn_ref[...] = jnp.concatenate(kn_parts, axis=1)


def _attn_prompt(proj, bucket_tile, rel_bias, sinks, g_q, g_k, n_seq):
    m = proj.shape[0]
    blk = ATTN_BLOCK
    nb = m // blk
    blocks_per_seq = nb // n_seq
    kcol = COL_AK // KV_WIDTH
    vcol = COL_AV // KV_WIDTH
    prev = lambda r: jnp.maximum(r - 1, 0)
    smem = pl.BlockSpec(memory_space=pltpu.SMEM)
    return pl.pallas_call(
        functools.partial(_attn_prompt_kernel, blocks_per_seq=blocks_per_seq),
        out_shape=(jax.ShapeDtypeStruct((m, ATTN_WIDTH), BF16),
                   jax.ShapeDtypeStruct((m, KV_WIDTH), F32)),
        grid=(nb,),
        in_specs=[pl.BlockSpec((blk, ATTN_WIDTH), lambda r: (r, COL_AQ // ATTN_WIDTH)),
                  pl.BlockSpec((blk, KV_WIDTH), lambda r: (r, kcol)),
                  pl.BlockSpec((blk, KV_WIDTH), lambda r: (prev(r), kcol)),
                  pl.BlockSpec((blk, KV_WIDTH), lambda r: (r, vcol)),
                  pl.BlockSpec((blk, KV_WIDTH), lambda r: (prev(r), vcol)),
                  pl.BlockSpec((blk, 2 * blk), lambda r: (0, 0)),
                  smem, smem,
                  pl.BlockSpec((1, HEAD_DIM), lambda r: (0, 0)),
                  pl.BlockSpec((1, HEAD_DIM), lambda r: (0, 0))],
        out_specs=(pl.BlockSpec((blk, ATTN_WIDTH), lambda r: (r, 0)),
                   pl.BlockSpec((blk, KV_WIDTH), lambda r: (r, 0))),
        scratch_shapes=[pltpu.VMEM((N_ATTN_HEADS * blk, 2 * blk), F32)],
        compiler_params=_params(1),
        name="attn_prompt",
    )(proj, proj, proj, proj, proj, bucket_tile, rel_bias, sinks, g_q, g_k)


def _attn_sample_kernel(q_ref, knew_ref, vnew_ref, ck_ref, cv_ref, bucket_ref, rbt_ref, sinks_ref,
                        gq_ref, gk_ref, o_ref, cko_ref, cvo_ref):
    bt = q_ref.shape[0]
    win = ck_ref.shape[1]
    bucket = bucket_ref[...]
    bias = jnp.full((N_ATTN_HEADS, bucket.shape[1]), -jnp.inf, F32)
    for b in range(N_BUCKETS):
        bias = jnp.where(bucket == b, rbt_ref[:, b:b + 1], bias)
    bias_c = bias[:, :win]
    bias_n = bias[:, win:win + 1]
    sink = sinks_ref[...]
    head_group = jnp.right_shift(lax.broadcasted_iota(jnp.int32, (N_ATTN_HEADS, HEAD_DIM), 0),
                                 int(math.log2(GQA_GROUP)))
    lane_group = jnp.right_shift(lax.broadcasted_iota(jnp.int32, (1, KV_WIDTH), 1),
                                 int(math.log2(HEAD_DIM)))
    gk = gk_ref[...]
    scale = HEAD_DIM ** -0.5

    for b in range(bt):
        k_row = knew_ref[b]
        k_sq = k_row * k_row
        inv = jnp.zeros_like(k_row)
        for j in range(N_KV_HEADS):
            ms = jnp.sum(jnp.where(lane_group == j, k_sq, 0.0), axis=-1, keepdims=True) / HEAD_DIM
            inv = jnp.where(lane_group == j, lax.rsqrt(ms + EPS), inv)
        kn_row = k_row * inv * gk
        v_row = vnew_ref[b]

        qn = _rms(q_ref[b], gq_ref[...])
        q_bd = jnp.concatenate([jnp.where(head_group == j, qn, 0.0) for j in range(N_KV_HEADS)], axis=1)
        kc = ck_ref[b]
        vc = cv_ref[b]
        s_c = lax.dot_general(q_bd.astype(BF16), kc.astype(BF16), (((1,), (1,)), ((), ())),
                              preferred_element_type=F32) * scale + bias_c
        s_n = jnp.sum(q_bd * kn_row, axis=-1, keepdims=True) * scale + bias_n
        m = jnp.maximum(jnp.maximum(jnp.max(s_c, axis=-1, keepdims=True), s_n), sink)
        e_c = jnp.exp(s_c - m)
        e_n = jnp.exp(s_n - m)
        denom = jnp.sum(e_c, axis=-1, keepdims=True) + e_n + jnp.exp(sink - m)
        o_full = jnp.dot((e_c / denom).astype(BF16), vc.astype(BF16), preferred_element_type=F32)
        o_full = o_full + (e_n / denom) * v_row
        o = jnp.zeros((N_ATTN_HEADS, HEAD_DIM), F32)
        for j in range(N_KV_HEADS):
            o = jnp.where(head_group == j, o_full[:, j * HEAD_DIM:(j + 1) * HEAD_DIM], o)
        o_ref[b] = o.astype(o_ref.dtype)

        cko_ref[b, pl.ds(0, win - 1), :] = ck_ref[b, pl.ds(1, win - 1), :]
        cko_ref[b, pl.ds(win - 1, 1), :] = kn_row
        cvo_ref[b, pl.ds(0, win - 1), :] = cv_ref[b, pl.ds(1, win - 1), :]
        cvo_ref[b, pl.ds(win - 1, 1), :] = v_row


def _attn_sample(q, k_new, v_new, cache_k, cache_v, bucket_row, rel_bias_t, sinks_col, g_q, g_k_row, bt):
    nb = q.shape[0]
    win = cache_k.shape[1]
    full = lambda shape: pl.BlockSpec(shape, lambda i: (0,) * len(shape))
    return pl.pallas_call(
        _attn_sample_kernel,
        out_shape=(jax.ShapeDtypeStruct((nb, N_ATTN_HEADS, HEAD_DIM), BF16),
                   jax.ShapeDtypeStruct(cache_k.shape, F32),
                   jax.ShapeDtypeStruct(cache_v.shape, F32)),
        grid=(nb // bt,),
        in_specs=[pl.BlockSpec((bt, N_ATTN_HEADS, HEAD_DIM), lambda i: (i, 0, 0)),
                  pl.BlockSpec((bt, 1, KV_WIDTH), lambda i: (i, 0, 0)),
                  pl.BlockSpec((bt, 1, KV_WIDTH), lambda i: (i, 0, 0)),
                  pl.BlockSpec((bt, win, KV_WIDTH), lambda i: (i, 0, 0)),
                  pl.BlockSpec((bt, win, KV_WIDTH), lambda i: (i, 0, 0)),
                  full(bucket_row.shape), full(rel_bias_t.shape), full(sinks_col.shape),
                  full(g_q.shape), full(g_k_row.shape)],
        out_specs=(pl.BlockSpec((bt, N_ATTN_HEADS, HEAD_DIM), lambda i: (i, 0, 0)),
                   pl.BlockSpec((bt, win, KV_WIDTH), lambda i: (i, 0, 0)),
                   pl.BlockSpec((bt, win, KV_WIDTH), lambda i: (i, 0, 0))),
        compiler_params=_params(1),
        name="attn_sample",
    )(q, k_new, v_new, cache_k, cache_v, bucket_row, rel_bias_t, sinks_col, g_q, g_k_row)


def _ret_prompt_kernel(q_ref, k_ref, v_ref, gate_ref, cos_ref, sin_ref, dmask_ref, qdec_ref, kdec_ref,
                       cdec_ref, o_ref, state_ref, s_ref):
    c = pl.program_id(1)

    @pl.when(c == 0)
    def _():
        s_ref[...] = jnp.zeros_like(s_ref)

    cosf = cos_ref[...]
    sinf = sin_ref[...]
    half = RET_KDIM // 2
    nt = (((1,), (1,)), ((), ()))
    tn = (((0,), (0,)), ((), ()))

    def rotary(x):
        return x * cosf + pltpu.roll(x, half, axis=1) * sinf

    for h in range(N_RET_HEADS):
        kd = slice(h * RET_KDIM, (h + 1) * RET_KDIM)
        vd = slice(h * RET_VDIM, (h + 1) * RET_VDIM)
        q = rotary(q_ref[:, kd])
        k = rotary(k_ref[:, kd]) * RET_KDIM ** -0.5
        v = v_ref[:, vd].astype(BF16)
        s0 = s_ref[h]
        scores = lax.dot_general(q.astype(BF16), k.astype(BF16), nt,
                                 preferred_element_type=F32) * dmask_ref[h]
        o_intra = jnp.dot(scores.astype(BF16), v, preferred_element_type=F32)
        o_inter = jnp.dot((q * qdec_ref[h]).astype(BF16), s0.astype(BF16), preferred_element_type=F32)
        s_ref[h] = cdec_ref[h] * s0 + lax.dot_general((k * kdec_ref[h]).astype(BF16), v, tn,
                                                       preferred_element_type=F32)
        ro = o_intra + o_inter
        ro = ro * lax.rsqrt(jnp.mean(ro * ro, axis=-1, keepdims=True) + EPS)
        o_ref[:, vd] = (jax.nn.silu(gate_ref[:, vd]) * ro).astype(o_ref.dtype)

    @pl.when(c == pl.num_programs(1) - 1)
    def _():
        state_ref[0] = s_ref[...]


def _ret_prompt(proj, cosf, sinf, dmask, qdec, kdec, cdec, n_seq):
    m = proj.shape[0]
    ch = RET_CHUNK
    nc = m // n_seq // ch
    row = lambda b, c: b * nc + c
    const3 = pl.BlockSpec((N_RET_HEADS, ch, ch), lambda b, c: (0, 0, 0))
    return pl.pallas_call(
        _ret_prompt_kernel,
        out_shape=(jax.ShapeDtypeStruct((m, RET_V_WIDTH), BF16),
                   jax.ShapeDtypeStruct((n_seq, N_RET_HEADS, RET_KDIM, RET_VDIM), F32)),
        grid=(n_seq, nc),
        in_specs=[pl.BlockSpec((ch, RET_QK_WIDTH), lambda b, c: (row(b, c), COL_RQ // RET_QK_WIDTH)),
                  pl.BlockSpec((ch, RET_QK_WIDTH), lambda b, c: (row(b, c), COL_RK // RET_QK_WIDTH)),
                  pl.BlockSpec((ch, RET_V_WIDTH), lambda b, c: (row(b, c), COL_RV // RET_V_WIDTH)),
                  pl.BlockSpec((ch, RET_V_WIDTH), lambda b, c: (row(b, c), COL_RG // RET_V_WIDTH)),
                  pl.BlockSpec((ch, RET_KDIM), lambda b, c: (c, 0)),
                  pl.BlockSpec((ch, RET_KDIM), lambda b, c: (c, 0)),
                  const3, const3, const3,
                  pl.BlockSpec(memory_space=pltpu.SMEM)],
        out_specs=(pl.BlockSpec((ch, RET_V_WIDTH), lambda b, c: (row(b, c), 0)),
                   pl.BlockSpec((1, N_RET_HEADS, RET_KDIM, RET_VDIM), lambda b, c: (b, 0, 0, 0))),
        scratch_shapes=[pltpu.VMEM((N_RET_HEADS, RET_KDIM, RET_VDIM), F32)],
        compiler_params=_params(2),
        name="ret_prompt",
    )(proj, proj, proj, proj, cosf, sinf, dmask, qdec, kdec, cdec)


def _ret_sample_kernel(qt_ref, kt_ref, cos_ref, sin_ref, v_ref, gate_ref, s_ref, qdec_ref, kdec_ref,
                       cdec_ref, o_ref, so_ref):
    bt = s_ref.shape[0]
    half = RET_KDIM // 2
    cos = cos_ref[...]
    sin = sin_ref[...]

    def rotary(x):
        x1, x2 = x[:half, :], x[half:, :]
        return jnp.concatenate([x1 * cos - x2 * sin, x2 * cos + x1 * sin], axis=0)

    for h in range(N_RET_HEADS):
        vd = slice(h * RET_VDIM, (h + 1) * RET_VDIM)
        q_all = rotary(qt_ref[0, h])
        k_all = rotary(kt_ref[0, h]) * RET_KDIM ** -0.5
        for b in range(bt):
            q = q_all[:, b:b + 1]
            k = k_all[:, b:b + 1]
            v = v_ref[0, b:b + 1, vd]
            s0 = s_ref[b, h]
            o_intra = jnp.sum(q * k, axis=0, keepdims=True) * v
            o_inter = jnp.sum((q * qdec_ref[h]) * s0, axis=0, keepdims=True)
            so_ref[b, h] = cdec_ref[h] * s0 + (k * kdec_ref[h]) * v
            ro = o_intra + o_inter
            ro = ro * lax.rsqrt(jnp.mean(ro * ro, axis=-1, keepdims=True) + EPS)
            o_ref[0, b:b + 1, vd] = (jax.nn.silu(gate_ref[0, b:b + 1, vd]) * ro).astype(o_ref.dtype)


def _ret_sample(qt, kt, cos_col, sin_col, v, gate, state, qdec, kdec, cdec):
    nbt, _, _, bt = qt.shape
    smem = pl.BlockSpec(memory_space=pltpu.SMEM)
    st_spec = pl.BlockSpec((bt, N_RET_HEADS, RET_KDIM, RET_VDIM), lambda i: (i, 0, 0, 0))
    qk_spec = pl.BlockSpec((1, N_RET_HEADS, RET_KDIM, bt), lambda i: (i, 0, 0, 0))
    row_spec = pl.BlockSpec((1, bt, RET_V_WIDTH), lambda i: (i, 0, 0))
    col_spec = pl.BlockSpec(cos_col.shape, lambda i: (0, 0))
    return pl.pallas_call(
        _ret_sample_kernel,
        out_shape=(jax.ShapeDtypeStruct((nbt, bt, RET_V_WIDTH), BF16),
                   jax.ShapeDtypeStruct(state.shape, F32)),
        grid=(nbt,),
        in_specs=[qk_spec, qk_spec, col_spec, col_spec, row_spec, row_spec, st_spec, smem, smem, smem],
        out_specs=(row_spec, st_spec),
        compiler_params=_params(1),
        name="ret_sample",
    )(qt, kt, cos_col, sin_col, v, gate, state, qdec, kdec, cdec)


def _t5_bucket(dist):
    n = jnp.maximum(dist, 0)
    max_exact = N_BUCKETS // 2
    nf = jnp.maximum(n, 1).astype(F32)
    large = max_exact + (jnp.log(nf / max_exact) / math.log(MAX_DISTANCE / max_exact)
                         * (N_BUCKETS - max_exact)).astype(jnp.int32)
    return jnp.where(n < max_exact, n, jnp.minimum(large, N_BUCKETS - 1))


def _rope_tables(pos):
    half = RET_KDIM // 2
    inv = ROPE_BASE ** (-jnp.arange(half, dtype=F32) / half)
    ang = pos.astype(F32)[:, None] * inv[None]
    return jnp.cos(ang), jnp.sin(ang)


def _layer(xp, xs, pp, ps, cache_k, cache_v, state_ret, state_conv, rel_bias, lp, n_seq):
    (g_mix, w_in, g_q, g_k, sinks, w_out, g_ffn, w_up, conv_w, conv_b, w_down,
     g_ple, w_ple_gate, w_ple_proj) = lp
    mp = xp.shape[0]
    ms = xs.shape[0]
    seq = mp // n_seq
    win = cache_k.shape[1]

    aq, ak, av, rq, rk, rv, rg = jnp.split(w_in, IN_SPLITS, axis=1)
    w_in_b = jnp.concatenate([aq, rv, rg, rq, rk, ak, av], axis=1).astype(BF16)
    w_out_b = w_out.astype(BF16)
    w_up_b = w_up.astype(BF16)
    w_down_b = w_down.astype(BF16)
    w_gate_b = w_ple_gate.astype(BF16)
    w_pproj_b = w_ple_proj.astype(BF16)
    row = lambda a: a.reshape(1, -1)

    log_decay = jnp.log(1.0 - 2.0 ** (-5.0 - jnp.arange(N_RET_HEADS, dtype=F32)))
    idx = jnp.arange(RET_CHUNK, dtype=F32)
    diff = idx[:, None] - idx[None, :]
    dmask = jnp.where(diff[None] >= 0, jnp.exp(diff[None] * log_decay[:, None, None]), 0.0)
    q_dec = jnp.exp((idx + 1.0)[:, None] * log_decay[None])
    k_dec = jnp.exp((RET_CHUNK - 1.0 - idx)[:, None] * log_decay[None])
    qdec_t = jnp.broadcast_to(q_dec.T[:, :, None], (N_RET_HEADS, RET_CHUNK, RET_KDIM))
    kdec_t = jnp.broadcast_to(k_dec.T[:, :, None], (N_RET_HEADS, RET_CHUNK, RET_KDIM))
    cdec = jnp.exp(RET_CHUNK * log_decay)
    cos_p, sin_p = _rope_tables(jnp.arange(seq, dtype=jnp.int32))
    cosf = jnp.concatenate([cos_p, cos_p], axis=1)
    sinf = jnp.concatenate([-sin_p, sin_p], axis=1)
    one = jnp.arange(1, dtype=F32)
    qdec_s = jnp.exp((one + 1.0)[:, None] * log_decay[None])[0]
    kdec_s = jnp.exp((1.0 - 1.0 - one)[:, None] * log_decay[None])[0]
    cdec_s = jnp.exp(1.0 * log_decay)
    cos_s, sin_s = _rope_tables(PAST_LEN + jnp.arange(1, dtype=jnp.int32))
    cos_col, sin_col = cos_s.reshape(-1, 1), sin_s.reshape(-1, 1)

    qi = jnp.arange(ATTN_BLOCK, dtype=jnp.int32)
    ki = jnp.arange(2 * ATTN_BLOCK, dtype=jnp.int32) - ATTN_BLOCK
    dist = qi[:, None] - ki[None, :]
    bucket_tile = jnp.where((dist >= 0) & (dist <= WINDOW), _t5_bucket(dist), -1)
    dist_s = win - jnp.arange(2 * ATTN_BLOCK, dtype=jnp.int32)
    bucket_row = jnp.where((dist_s >= 0) & (dist_s <= WINDOW), _t5_bucket(dist_s), -1).reshape(1, -1)

    proj_p = _norm_matmul(xp, row(g_mix), w_in_b, 512, 512)
    proj_s = _norm_matmul(xs, row(g_mix), w_in_b, ms, 512)

    attn_p, kn_p = _attn_prompt(proj_p, bucket_tile, rel_bias, sinks, row(g_q), row(g_k), n_seq)
    ret_p, ret_state_p = _ret_prompt(proj_p, cosf, sinf, dmask, qdec_t, kdec_t, cdec, n_seq)

    bt_a = 8
    attn_s, ck_new, cv_new = _attn_sample(
        proj_s[:, COL_AQ:COL_AQ + ATTN_WIDTH].reshape(ms, N_ATTN_HEADS, HEAD_DIM),
        proj_s[:, COL_AK:COL_AK + KV_WIDTH].reshape(ms, 1, KV_WIDTH),
        proj_s[:, COL_AV:COL_AV + KV_WIDTH].reshape(ms, 1, KV_WIDTH),
        cache_k.reshape(ms, win, KV_WIDTH), cache_v.reshape(ms, win, KV_WIDTH),
        bucket_row, rel_bias.T, sinks.reshape(-1, 1), row(g_q), jnp.tile(g_k, N_KV_HEADS).reshape(1, -1), bt_a)
    attn_s = attn_s.reshape(ms, ATTN_WIDTH)

    bt_r = 4
    to_cols = lambda a: a.reshape(ms // bt_r, bt_r, N_RET_HEADS, RET_KDIM).transpose(0, 2, 3, 1)
    ret_s, ret_state_s = _ret_sample(
        to_cols(proj_s[:, COL_RQ:COL_RQ + RET_QK_WIDTH]), to_cols(proj_s[:, COL_RK:COL_RK + RET_QK_WIDTH]),
        cos_col, sin_col,
        proj_s[:, COL_RV:COL_RV + RET_V_WIDTH].reshape(ms // bt_r, bt_r, RET_V_WIDTH),
        proj_s[:, COL_RG:COL_RG + RET_V_WIDTH].reshape(ms // bt_r, bt_r, RET_V_WIDTH),
        state_ret, qdec_s, kdec_s, cdec_s)
    ret_s = ret_s.reshape(ms, RET_V_WIDTH)

    xp = _out_proj(attn_p, ret_p, w_out_b, xp, 512, 512)
    xs = _out_proj(attn_s, ret_s, w_out_b, xs, ms, 512)

    act_p, conv_p = _convglu_prompt(xp, row(g_ffn), w_up_b, conv_w, row(conv_b), n_seq, 512)
    act_s, conv_s = _convglu_sample(xs, row(g_ffn), w_up_b, conv_w, row(conv_b), state_conv)
    xp = _down_proj(act_p, w_down_b, xp, 512, 256)
    xs = _down_proj(act_s, w_down_b, xs, ms, 256)

    xp = _ple(xp, row(g_ple), w_gate_b, pp, w_pproj_b, 512, 512)
    xs = _ple(xs, row(g_ple), w_gate_b, ps, w_pproj_b, ms, 512)

    kp_new = kn_p.reshape(n_seq, seq, N_KV_HEADS, HEAD_DIM)[:, -WINDOW:]
    vp_new = proj_p[:, COL_AV:COL_AV + KV_WIDTH].reshape(n_seq, seq, N_KV_HEADS, HEAD_DIM)[:, -WINDOW:]
    ks_new = ck_new.reshape(ms, win, N_KV_HEADS, HEAD_DIM)
    vs_new = cv_new.reshape(ms, win, N_KV_HEADS, HEAD_DIM)
    return xp, xs, kp_new, vp_new, ret_state_p, conv_p, ks_new, vs_new, ret_state_s, conv_s


def kernel(x_prompt, x_sample, p_prompt, p_sample, cache_win_k, cache_win_v, state_ret, state_conv, rel_bias, g_mix, w_in, g_q, g_k, sinks, w_out, g_ffn, w_up, conv_w, conv_b, w_down, g_ple, w_ple_gate, w_ple_proj):
    depth = g_mix.shape[0]
    n_seq, seq, d = x_prompt.shape
    nb, dec_seq, _ = x_sample.shape
    assert dec_seq == 1 and seq % 512 == 0 and d == D_MODEL
    xp = x_prompt.reshape(n_seq * seq, d)
    xs = x_sample.reshape(nb, d)
    outs = [[] for _ in range(8)]
    for l in range(depth):
        lp = (g_mix[l], w_in[l], g_q[l], g_k[l], sinks[l], w_out[l], g_ffn[l], w_up[l], conv_w[l],
              conv_b[l], w_down[l], g_ple[l], w_ple_gate[l], w_ple_proj[l])
        res = _layer(xp, xs, p_prompt[l].reshape(n_seq * seq, -1), p_sample[l].reshape(nb, -1),
                     cache_win_k[l], cache_win_v[l], state_ret[l], state_conv[l], rel_bias, lp, n_seq)
        xp, xs = res[0], res[1]
        for o, r in zip(outs, res[2:]):
            o.append(r)
    stacked = [jnp.stack(o) for o in outs]
    return (xp.reshape(n_seq, seq, d), xs.reshape(nb, 1, d), *stacked)
```

```python
import functools
import math

import jax
import jax.numpy as jnp
from jax import lax
from jax.experimental import pallas as pl
from jax.experimental.pallas import tpu as pltpu

F32 = jnp.float32
BF16 = jnp.bfloat16

D_MODEL = 4096
HEAD_DIM = 64
N_ATTN_HEADS = 32
N_KV_HEADS = 4
GQA_GROUP = N_ATTN_HEADS // N_KV_HEADS
WINDOW = 128
ATTN_BLOCK = 128
N_BUCKETS = 32
MAX_DISTANCE = 128
N_RET_HEADS = 8
RET_KDIM = 128
RET_VDIM = 256
RET_CHUNK = 128
ROPE_BASE = 10000.0
D_FF = 11008
CONV_W = 3
EPS = 1e-6

ATTN_WIDTH = N_ATTN_HEADS * HEAD_DIM
KV_WIDTH = N_KV_HEADS * HEAD_DIM
RET_QK_WIDTH = N_RET_HEADS * RET_KDIM
RET_V_WIDTH = N_RET_HEADS * RET_VDIM
IN_SIZES = (ATTN_WIDTH, KV_WIDTH, KV_WIDTH, RET_QK_WIDTH, RET_QK_WIDTH, RET_V_WIDTH, RET_V_WIDTH)
IN_WIDTH = sum(IN_SIZES)
IN_SPLITS = tuple(sum(IN_SIZES[:n + 1]) for n in range(len(IN_SIZES) - 1))
PAST_LEN = 8192

COL_AQ = 0
COL_RV = ATTN_WIDTH
COL_RG = COL_RV + RET_V_WIDTH
COL_RQ = COL_RG + RET_V_WIDTH
COL_RK = COL_RQ + RET_QK_WIDTH
COL_AK = COL_RK + RET_QK_WIDTH
COL_AV = COL_AK + KV_WIDTH

VMEM_LIMIT_BYTES = 56 * 1024 * 1024
FF_TILE = 256
N_FF_TILES = D_FF // FF_TILE
NORM_ROWS = 32
NORM_TILE = 256
ROW_TILE = 1024
COL_TILE = 512
DOWN_K_PARTS = 2


def _row_tile(m):
    return min(m, ROW_TILE)


def _params(n_axes):
    return pltpu.CompilerParams(dimension_semantics=("arbitrary",) * n_axes,
                                vmem_limit_bytes=VMEM_LIMIT_BYTES)


def _rms(x, g):
    y = x * lax.rsqrt(jnp.mean(x * x, axis=-1, keepdims=True) + EPS)
    return y * g


def _norm_rows_to_bf16(x_ref, g_ref, h_ref):
    def body(c, carry):
        r0 = pl.multiple_of(c * NORM_ROWS, NORM_ROWS)
        h_ref[pl.ds(r0, NORM_ROWS), :] = _rms(x_ref[pl.ds(r0, NORM_ROWS), :], g_ref[...]).astype(h_ref.dtype)
        return carry
    lax.fori_loop(0, x_ref.shape[0] // NORM_ROWS, body, 0)


def _norm(x, g, tr):
    m, k = x.shape
    return pl.pallas_call(
        _norm_rows_to_bf16,
        out_shape=jax.ShapeDtypeStruct((m, k), BF16),
        grid=(m // tr,),
        in_specs=[pl.BlockSpec((tr, k), lambda i: (i, 0)),
                  pl.BlockSpec((1, k), lambda i: (0, 0))],
        out_specs=pl.BlockSpec((tr, k), lambda i: (i, 0)),
        compiler_params=_params(1),
        name="rmsnorm",
    )(x, g)


def _mm_kernel(h_ref, w_ref, o_ref):
    o_ref[...] = jnp.dot(h_ref[...], w_ref[...], preferred_element_type=F32)


def _matmul(h, w, tm, tn):
    m, k = h.shape
    n = w.shape[1]
    return pl.pallas_call(
        _mm_kernel,
        out_shape=jax.ShapeDtypeStruct((m, n), F32),
        grid=(m // tm, n // tn),
        in_specs=[pl.BlockSpec((tm, k), lambda i, j: (i, 0)),
                  pl.BlockSpec((k, tn), lambda i, j: (0, j))],
        out_specs=pl.BlockSpec((tm, tn), lambda i, j: (i, j)),
        compiler_params=_params(2),
        name="in_proj",
    )(h, w)


def _outproj_kernel(a_ref, r_ref, w_ref, x_ref, o_ref):
    mix = jnp.concatenate([a_ref[...], r_ref[...]], axis=1)
    o_ref[...] = x_ref[...] + jnp.dot(mix, w_ref[...], preferred_element_type=F32)


def _out_proj(a, r, w, x, tm, tn):
    m = a.shape[0]
    ka, kr = a.shape[1], r.shape[1]
    n = w.shape[1]
    return pl.pallas_call(
        _outproj_kernel,
        out_shape=jax.ShapeDtypeStruct((m, n), F32),
        grid=(m // tm, n // tn),
        in_specs=[pl.BlockSpec((tm, ka), lambda i, j: (i, 0)),
                  pl.BlockSpec((tm, kr), lambda i, j: (i, 0)),
                  pl.BlockSpec((ka + kr, tn), lambda i, j: (0, j)),
                  pl.BlockSpec((tm, tn), lambda i, j: (i, j))],
        out_specs=pl.BlockSpec((tm, tn), lambda i, j: (i, j)),
        compiler_params=_params(2),
        name="out_proj",
    )(a, r, w, x)


def _down_kernel(a_ref, w_ref, x_ref, o_ref):
    o_ref[...] = x_ref[...] + jnp.dot(a_ref[...], w_ref[...], preferred_element_type=F32)


def _down_proj(a, w, x, tm, tn, k_parts, part):
    m = a.shape[0]
    k = a.shape[1] // k_parts
    n = w.shape[1]
    return pl.pallas_call(
        _down_kernel,
        out_shape=jax.ShapeDtypeStruct((m, n), F32),
        grid=(m // tm, n // tn),
        in_specs=[pl.BlockSpec((tm, k), lambda i, j: (i, part)),
                  pl.BlockSpec((k, tn), lambda i, j: (part, j)),
                  pl.BlockSpec((tm, tn), lambda i, j: (i, j))],
        out_specs=pl.BlockSpec((tm, tn), lambda i, j: (i, j)),
        compiler_params=_params(2),
        name="down_proj",
    )(a, w, x)


def _ple_kernel(h_ref, wg_ref, p_ref, wp_ref, x_ref, o_ref):
    z = jnp.dot(h_ref[...], wg_ref[...], preferred_element_type=F32)
    pp = jnp.dot(p_ref[...].astype(BF16), wp_ref[...], preferred_element_type=F32)
    o_ref[...] = x_ref[...] + jax.nn.sigmoid(z) * pp


def _ple(x, h, wg, p, wp, tm, tn):
    m, k = h.shape
    n = wg.shape[1]
    kp = p.shape[1]
    return pl.pallas_call(
        _ple_kernel,
        out_shape=jax.ShapeDtypeStruct((m, n), F32),
        grid=(m // tm, n // tn),
        in_specs=[pl.BlockSpec((tm, k), lambda i, j: (i, 0)),
                  pl.BlockSpec((k, tn), lambda i, j: (0, j)),
                  pl.BlockSpec((tm, kp), lambda i, j: (i, 0)),
                  pl.BlockSpec((kp, tn), lambda i, j: (0, j)),
                  pl.BlockSpec((tm, tn), lambda i, j: (i, j))],
        out_specs=pl.BlockSpec((tm, tn), lambda i, j: (i, j)),
        compiler_params=_params(2),
        name="ple",
    )(h, wg, p, wp, x)


def _gelu_erf(x):
    return 0.5 * x * (1.0 + lax.erf(x * math.sqrt(0.5)))


def _conv_taps(cb_ref, cw_ref, um2, um1, u):
    c = cb_ref[...] + cw_ref[0:1, :] * um2
    c = c + cw_ref[1:2, :] * um1
    return c + cw_ref[2:3, :] * u


def _convglu_prompt_kernel(h_ref, wg_ref, wv_ref, cwg_ref, cwv_ref, cbg_ref, cbv_ref,
                           a_ref, sg_ref, sv_ref, carry_g_ref, carry_v_ref, *, tiles_per_seq):
    i = pl.program_id(0)
    j = pl.program_id(1)
    tm = h_ref.shape[0]
    seq_start = (i % tiles_per_seq) == 0
    row = lax.broadcasted_iota(jnp.int32, (tm, 1), 0)

    def half(w_ref, cw_ref, cb_ref, carry_ref, s_ref):
        u = jnp.dot(h_ref[...], w_ref[...], preferred_element_type=F32)
        prev = jnp.where(seq_start, 0.0, carry_ref[j])
        tail = u[tm - (CONV_W - 1):, :]
        carry_ref[j] = tail
        s_ref[0] = tail
        um1 = jnp.where(row == 0, prev[1:2, :], pltpu.roll(u, 1, axis=0))
        um2 = jnp.where(row == 0, prev[0:1, :],
                        jnp.where(row == 1, prev[1:2, :], pltpu.roll(u, 2, axis=0)))
        return _conv_taps(cb_ref, cw_ref, um2, um1, u)

    cg = half(wg_ref, cwg_ref, cbg_ref, carry_g_ref, sg_ref)
    cv = half(wv_ref, cwv_ref, cbv_ref, carry_v_ref, sv_ref)
    a_ref[...] = (_gelu_erf(cg) * cv).astype(a_ref.dtype)


def _convglu_prompt(h, w_up, conv_w, conv_b, n_seq, tm):
    m, k = h.shape
    seq = m // n_seq
    tiles_per_seq = seq // tm
    nt = N_FF_TILES
    tn = FF_TILE
    a, sg, sv = pl.pallas_call(
        functools.partial(_convglu_prompt_kernel, tiles_per_seq=tiles_per_seq),
        out_shape=(jax.ShapeDtypeStruct((m, D_FF), BF16),
                   jax.ShapeDtypeStruct((m // tm, CONV_W - 1, D_FF), F32),
                   jax.ShapeDtypeStruct((m // tm, CONV_W - 1, D_FF), F32)),
        grid=(m // tm, nt),
        in_specs=[pl.BlockSpec((tm, k), lambda i, j: (i, 0)),
                  pl.BlockSpec((k, tn), lambda i, j: (0, j)),
                  pl.BlockSpec((k, tn), lambda i, j: (0, j + nt)),
                  pl.BlockSpec((CONV_W, tn), lambda i, j: (0, j)),
                  pl.BlockSpec((CONV_W, tn), lambda i, j: (0, j + nt)),
                  pl.BlockSpec((1, tn), lambda i, j: (0, j)),
                  pl.BlockSpec((1, tn), lambda i, j: (0, j + nt))],
        out_specs=(pl.BlockSpec((tm, tn), lambda i, j: (i, j)),
                   pl.BlockSpec((1, CONV_W - 1, tn), lambda i, j: (i, 0, j)),
                   pl.BlockSpec((1, CONV_W - 1, tn), lambda i, j: (i, 0, j))),
        scratch_shapes=[pltpu.VMEM((nt, CONV_W - 1, tn), F32),
                        pltpu.VMEM((nt, CONV_W - 1, tn), F32)],
        compiler_params=_params(2),
        name="convglu_prompt",
    )(h, w_up, w_up, conv_w, conv_w, conv_b, conv_b)
    tails = jnp.concatenate([sg, sv], axis=-1)
    return a, tails[tiles_per_seq - 1::tiles_per_seq]


def _convglu_sample_kernel(h_ref, wg_ref, wv_ref, cwg_ref, cwv_ref, cbg_ref, cbv_ref,
                           p0g_ref, p0v_ref, p1g_ref, p1v_ref, a_ref, ug_ref, uv_ref):
    def half(w_ref, cw_ref, cb_ref, p0_ref, p1_ref, u_ref):
        u = jnp.dot(h_ref[...], w_ref[...], preferred_element_type=F32)
        u_ref[...] = u
        return _conv_taps(cb_ref, cw_ref, p0_ref[...], p1_ref[...], u)

    cg = half(wg_ref, cwg_ref, cbg_ref, p0g_ref, p1g_ref, ug_ref)
    cv = half(wv_ref, cwv_ref, cbv_ref, p0v_ref, p1v_ref, uv_ref)
    a_ref[...] = (_gelu_erf(cg) * cv).astype(a_ref.dtype)


def _convglu_sample(h, w_up, conv_w, conv_b, state_conv):
    m, k = h.shape
    nt = N_FF_TILES
    tn = FF_TILE
    prev = state_conv.reshape(m, (CONV_W - 1) * 2 * D_FF)
    col = lambda off: (lambda j: (0, j + off))
    a, ug, uv = pl.pallas_call(
        _convglu_sample_kernel,
        out_shape=(jax.ShapeDtypeStruct((m, D_FF), BF16),
                   jax.ShapeDtypeStruct((m, D_FF), F32),
                   jax.ShapeDtypeStruct((m, D_FF), F32)),
        grid=(nt,),
        in_specs=[pl.BlockSpec((m, k), lambda j: (0, 0)),
                  pl.BlockSpec((k, tn), col(0)),
                  pl.BlockSpec((k, tn), col(nt)),
                  pl.BlockSpec((CONV_W, tn), col(0)),
                  pl.BlockSpec((CONV_W, tn), col(nt)),
                  pl.BlockSpec((1, tn), col(0)),
                  pl.BlockSpec((1, tn), col(nt)),
                  pl.BlockSpec((m, tn), col(0)),
                  pl.BlockSpec((m, tn), col(nt)),
                  pl.BlockSpec((m, tn), col(2 * nt)),
                  pl.BlockSpec((m, tn), col(3 * nt))],
        out_specs=(pl.BlockSpec((m, tn), col(0)),
                   pl.BlockSpec((m, tn), col(0)),
                   pl.BlockSpec((m, tn), col(0))),
        compiler_params=_params(1),
        name="convglu_sample",
    )(h, w_up, w_up, conv_w, conv_w, conv_b, conv_b, prev, prev, prev, prev)
    u = jnp.concatenate([ug, uv], axis=-1)
    conv_new = jnp.stack([state_conv[:, 1, :], u], axis=1)
    return a, conv_new


def _attn_prompt_kernel(q_ref, kc_ref, kp_ref, vc_ref, vp_ref, bucket_ref, rb_ref, sinks_ref,
                        gq_ref, gk_ref, o_ref, kn_ref, bias_ref, *, blocks_per_seq):
    r = pl.program_id(0)
    blk = ATTN_BLOCK

    @pl.when(r == 0)
    def _():
        bucket = bucket_ref[...]

        def per_head(h, carry):
            acc = jnp.full(bucket.shape, -jnp.inf, F32)
            for b in range(N_BUCKETS):
                acc = jnp.where(bucket == b, rb_ref[b, h], acc)
            bias_ref[pl.ds(pl.multiple_of(h * blk, blk), blk), :] = acc
            return carry
        lax.fori_loop(0, N_ATTN_HEADS, per_head, 0)

    n_masked = jnp.where((r % blocks_per_seq) == 0, blk, 0)
    key_col = lax.broadcasted_iota(jnp.int32, (blk, 2 * blk), 1)
    before_start = key_col < n_masked
    gq = gq_ref[...]
    gk = gk_ref[...]

    kn_parts = []
    for j in range(N_KV_HEADS):
        hd = slice(j * HEAD_DIM, (j + 1) * HEAD_DIM)
        kn_c = _rms(kc_ref[:, hd], gk)
        kn_p = _rms(kp_ref[:, hd], gk)
        kn_parts.append(kn_c)
        kcat = jnp.concatenate([kn_p, kn_c], axis=0).astype(BF16)
        vcat = jnp.concatenate([vp_ref[:, hd], vc_ref[:, hd]], axis=0).astype(BF16)
        heads = range(j * GQA_GROUP, (j + 1) * GQA_GROUP)
        qs = jnp.concatenate(
            [_rms(q_ref[:, h * HEAD_DIM:(h + 1) * HEAD_DIM], gq) for h in heads], axis=0).astype(BF16)
        s = lax.dot_general(qs, kcat, (((1,), (1,)), ((), ())), preferred_element_type=F32)
        probs = []
        for g, h in enumerate(heads):
            sg = s[g * blk:(g + 1) * blk, :] * HEAD_DIM ** -0.5 + bias_ref[h * blk:(h + 1) * blk, :]
            sg = jnp.where(before_start, -jnp.inf, sg)
            sink = sinks_ref[h]
            m = jnp.maximum(jnp.max(sg, axis=-1, keepdims=True), sink)
            e = jnp.exp(sg - m)
            p = e / (jnp.sum(e, axis=-1, keepdims=True) + jnp.exp(sink - m))
            probs.append(p.astype(BF16))
        o = jnp.dot(jnp.concatenate(probs, axis=0), vcat, preferred_element_type=F32)
        for g, h in enumerate(heads):
            o_ref[:, h * HEAD_DIM:(h + 1) * HEAD_DIM] = o[g * blk:(g + 1) * blk, :].astype(o_ref.dtype)
    kn_ref[...] = jnp.concatenate(kn_parts, axis=1)


def _attn_prompt(proj, bucket_tile, rel_bias, sinks, g_q, g_k, n_seq):
    m = proj.shape[0]
    blk = ATTN_BLOCK
    nb = m // blk
    blocks_per_seq = nb // n_seq
    kcol = COL_AK // KV_WIDTH
    vcol = COL_AV // KV_WIDTH
    prev = lambda r: jnp.maximum(r - 1, 0)
    smem = pl.BlockSpec(memory_space=pltpu.SMEM)
    return pl.pallas_call(
        functools.partial(_attn_prompt_kernel, blocks_per_seq=blocks_per_seq),
        out_shape=(jax.ShapeDtypeStruct((m, ATTN_WIDTH), BF16),
                   jax.ShapeDtypeStruct((m, KV_WIDTH), F32)),
        grid=(nb,),
        in_specs=[pl.BlockSpec((blk, ATTN_WIDTH), lambda r: (r, COL_AQ // ATTN_WIDTH)),
                  pl.BlockSpec((blk, KV_WIDTH), lambda r: (r, kcol)),
                  pl.BlockSpec((blk, KV_WIDTH), lambda r: (prev(r), kcol)),
                  pl.BlockSpec((blk, KV_WIDTH), lambda r: (r, vcol)),
                  pl.BlockSpec((blk, KV_WIDTH), lambda r: (prev(r), vcol)),
                  pl.BlockSpec((blk, 2 * blk), lambda r: (0, 0)),
                  smem, smem,
                  pl.BlockSpec((1, HEAD_DIM), lambda r: (0, 0)),
                  pl.BlockSpec((1, HEAD_DIM), lambda r: (0, 0))],
        out_specs=(pl.BlockSpec((blk, ATTN_WIDTH), lambda r: (r, 0)),
                   pl.BlockSpec((blk, KV_WIDTH), lambda r: (r, 0))),
        scratch_shapes=[pltpu.VMEM((N_ATTN_HEADS * blk, 2 * blk), F32)],
        compiler_params=_params(1),
        name="attn_prompt",
    )(proj, proj, proj, proj, proj, bucket_tile, rel_bias, sinks, g_q, g_k)


def _attn_sample_kernel(q_ref, knew_ref, vnew_ref, ck_ref, cv_ref, bucket_ref, rbt_ref, sinks_ref,
                        gq_ref, gk_ref, o_ref, cko_ref, cvo_ref):
    bt = q_ref.shape[0]
    win = ck_ref.shape[1]
    bucket = bucket_ref[...]
    bias = jnp.full((N_ATTN_HEADS, bucket.shape[1]), -jnp.inf, F32)
    for b in range(N_BUCKETS):
        bias = jnp.where(bucket == b, rbt_ref[:, b:b + 1], bias)
    bias_c = bias[:, :win]
    bias_n = bias[:, win:win + 1]
    sink = sinks_ref[...]
    head_group = jnp.right_shift(lax.broadcasted_iota(jnp.int32, (N_ATTN_HEADS, HEAD_DIM), 0),
                                 int(math.log2(GQA_GROUP)))
    lane_group = jnp.right_shift(lax.broadcasted_iota(jnp.int32, (1, KV_WIDTH), 1),
                                 int(math.log2(HEAD_DIM)))
    gk = gk_ref[...]
    scale = HEAD_DIM ** -0.5

    for b in range(bt):
        k_row = knew_ref[b]
        k_sq = k_row * k_row
        inv = jnp.zeros_like(k_row)
        for j in range(N_KV_HEADS):
            ms = jnp.sum(jnp.where(lane_group == j, k_sq, 0.0), axis=-1, keepdims=True) / HEAD_DIM
            inv = jnp.where(lane_group == j, lax.rsqrt(ms + EPS), inv)
        kn_row = k_row * inv * gk
        v_row = vnew_ref[b]

        qn = _rms(q_ref[b], gq_ref[...])
        q_bd = jnp.concatenate([jnp.where(head_group == j, qn, 0.0) for j in range(N_KV_HEADS)], axis=1)
        kc = ck_ref[b]
        vc = cv_ref[b]
        s_c = lax.dot_general(q_bd.astype(BF16), kc.astype(BF16), (((1,), (1,)), ((), ())),
                              preferred_element_type=F32) * scale + bias_c
        s_n = jnp.sum(q_bd * kn_row, axis=-1, keepdims=True) * scale + bias_n
        m = jnp.maximum(jnp.maximum(jnp.max(s_c, axis=-1, keepdims=True), s_n), sink)
        e_c = jnp.exp(s_c - m)
        e_n = jnp.exp(s_n - m)
        denom = jnp.sum(e_c, axis=-1, keepdims=True) + e_n + jnp.exp(sink - m)
        o_full = jnp.dot((e_c / denom).astype(BF16), vc.astype(BF16), preferred_element_type=F32)
        o_full = o_full + (e_n / denom) * v_row
        o = jnp.zeros((N_ATTN_HEADS, HEAD_DIM), F32)
        for j in range(N_KV_HEADS):
            o = jnp.where(head_group == j, o_full[:, j * HEAD_DIM:(j + 1) * HEAD_DIM], o)
        o_ref[b] = o.astype(o_ref.dtype)

        cko_ref[b, pl.ds(0, win - 1), :] = ck_ref[b, pl.ds(1, win - 1), :]
        cko_ref[b, pl.ds(win - 1, 1), :] = kn_row
        cvo_ref[b, pl.ds(0, win - 1), :] = cv_ref[b, pl.ds(1, win - 1), :]
        cvo_ref[b, pl.ds(win - 1, 1), :] = v_row


def _attn_sample(q, k_new, v_new, cache_k, cache_v, bucket_row, rel_bias_t, sinks_col, g_q, g_k_row, bt):
    nb = q.shape[0]
    win = cache_k.shape[1]
    full = lambda shape: pl.BlockSpec(shape, lambda i: (0,) * len(shape))
    return pl.pallas_call(
        _attn_sample_kernel,
        out_shape=(jax.ShapeDtypeStruct((nb, N_ATTN_HEADS, HEAD_DIM), BF16),
                   jax.ShapeDtypeStruct(cache_k.shape, F32),
                   jax.ShapeDtypeStruct(cache_v.shape, F32)),
        grid=(nb // bt,),
        in_specs=[pl.BlockSpec((bt, N_ATTN_HEADS, HEAD_DIM), lambda i: (i, 0, 0)),
                  pl.BlockSpec((bt, 1, KV_WIDTH), lambda i: (i, 0, 0)),
                  pl.BlockSpec((bt, 1, KV_WIDTH), lambda i: (i, 0, 0)),
                  pl.BlockSpec((bt, win, KV_WIDTH), lambda i: (i, 0, 0)),
                  pl.BlockSpec((bt, win, KV_WIDTH), lambda i: (i, 0, 0)),
                  full(bucket_row.shape), full(rel_bias_t.shape), full(sinks_col.shape),
                  full(g_q.shape), full(g_k_row.shape)],
        out_specs=(pl.BlockSpec((bt, N_ATTN_HEADS, HEAD_DIM), lambda i: (i, 0, 0)),
                   pl.BlockSpec((bt, win, KV_WIDTH), lambda i: (i, 0, 0)),
                   pl.BlockSpec((bt, win, KV_WIDTH), lambda i: (i, 0, 0))),
        compiler_params=_params(1),
        name="attn_sample",
    )(q, k_new, v_new, cache_k, cache_v, bucket_row, rel_bias_t, sinks_col, g_q, g_k_row)


def _ret_prompt_kernel(q_ref, k_ref, v_ref, gate_ref, cos_ref, sin_ref, dmask_ref, qdec_ref, kdec_ref,
                       cdec_ref, o_ref, state_ref, s_ref):
    c = pl.program_id(1)

    @pl.when(c == 0)
    def _():
        s_ref[...] = jnp.zeros_like(s_ref)

    cosf = cos_ref[...]
    sinf = sin_ref[...]
    half = RET_KDIM // 2
    nt = (((1,), (1,)), ((), ()))
    tn = (((0,), (0,)), ((), ()))

    def rotary(x):
        return x * cosf + pltpu.roll(x, half, axis=1) * sinf

    for h in range(N_RET_HEADS):
        kd = slice(h * RET_KDIM, (h + 1) * RET_KDIM)
        vd = slice(h * RET_VDIM, (h + 1) * RET_VDIM)
        q = rotary(q_ref[:, kd])
        k = rotary(k_ref[:, kd]) * RET_KDIM ** -0.5
        v = v_ref[:, vd].astype(BF16)
        s0 = s_ref[h]
        scores = lax.dot_general(q.astype(BF16), k.astype(BF16), nt,
                                 preferred_element_type=F32) * dmask_ref[h]
        o_intra = jnp.dot(scores.astype(BF16), v, preferred_element_type=F32)
        o_inter = jnp.dot((q * qdec_ref[h]).astype(BF16), s0.astype(BF16), preferred_element_type=F32)
        s_ref[h] = cdec_ref[h] * s0 + lax.dot_general((k * kdec_ref[h]).astype(BF16), v, tn,
                                                       preferred_element_type=F32)
        ro = o_intra + o_inter
        ro = ro * lax.rsqrt(jnp.mean(ro * ro, axis=-1, keepdims=True) + EPS)
        o_ref[:, vd] = (jax.nn.silu(gate_ref[:, vd]) * ro).astype(o_ref.dtype)

    @pl.when(c == pl.num_programs(1) - 1)
    def _():
        state_ref[0] = s_ref[...]


def _ret_prompt(proj, cosf, sinf, dmask, qdec, kdec, cdec, n_seq):
    m = proj.shape[0]
    ch = RET_CHUNK
    nc = m // n_seq // ch
    row = lambda b, c: b * nc + c
    const3 = pl.BlockSpec((N_RET_HEADS, ch, ch), lambda b, c: (0, 0, 0))
    return pl.pallas_call(
        _ret_prompt_kernel,
        out_shape=(jax.ShapeDtypeStruct((m, RET_V_WIDTH), BF16),
                   jax.ShapeDtypeStruct((n_seq, N_RET_HEADS, RET_KDIM, RET_VDIM), F32)),
        grid=(n_seq, nc),
        in_specs=[pl.BlockSpec((ch, RET_QK_WIDTH), lambda b, c: (row(b, c), COL_RQ // RET_QK_WIDTH)),
                  pl.BlockSpec((ch, RET_QK_WIDTH), lambda b, c: (row(b, c), COL_RK // RET_QK_WIDTH)),
                  pl.BlockSpec((ch, RET_V_WIDTH), lambda b, c: (row(b, c), COL_RV // RET_V_WIDTH)),
                  pl.BlockSpec((ch, RET_V_WIDTH), lambda b, c: (row(b, c), COL_RG // RET_V_WIDTH)),
                  pl.BlockSpec((ch, RET_KDIM), lambda b, c: (c, 0)),
                  pl.BlockSpec((ch, RET_KDIM), lambda b, c: (c, 0)),
                  const3, const3, const3,
                  pl.BlockSpec(memory_space=pltpu.SMEM)],
        out_specs=(pl.BlockSpec((ch, RET_V_WIDTH), lambda b, c: (row(b, c), 0)),
                   pl.BlockSpec((1, N_RET_HEADS, RET_KDIM, RET_VDIM), lambda b, c: (b, 0, 0, 0))),
        scratch_shapes=[pltpu.VMEM((N_RET_HEADS, RET_KDIM, RET_VDIM), F32)],
        compiler_params=_params(2),
        name="ret_prompt",
    )(proj, proj, proj, proj, cosf, sinf, dmask, qdec, kdec, cdec)


def _ret_sample_kernel(qt_ref, kt_ref, cos_ref, sin_ref, v_ref, gate_ref, s_ref, qdec_ref, kdec_ref,
                       cdec_ref, o_ref, so_ref):
    bt = s_ref.shape[0]
    half = RET_KDIM // 2
    cos = cos_ref[...]
    sin = sin_ref[...]

    def rotary(x):
        x1, x2 = x[:half, :], x[half:, :]
        return jnp.concatenate([x1 * cos - x2 * sin, x2 * cos + x1 * sin], axis=0)

    for h in range(N_RET_HEADS):
        vd = slice(h * RET_VDIM, (h + 1) * RET_VDIM)
        q_all = rotary(qt_ref[0, h])
        k_all = rotary(kt_ref[0, h]) * RET_KDIM ** -0.5
        for b in range(bt):
            q = q_all[:, b:b + 1]
            k = k_all[:, b:b + 1]
            v = v_ref[0, b:b + 1, vd]
            s0 = s_ref[b, h]
            o_intra = jnp.sum(q * k, axis=0, keepdims=True) * v
            o_inter = jnp.sum((q * qdec_ref[h]) * s0, axis=0, keepdims=True)
            so_ref[b, h] = cdec_ref[h] * s0 + (k * kdec_ref[h]) * v
            ro = o_intra + o_inter
            ro = ro * lax.rsqrt(jnp.mean(ro * ro, axis=-1, keepdims=True) + EPS)
            o_ref[0, b:b + 1, vd] = (jax.nn.silu(gate_ref[0, b:b + 1, vd]) * ro).astype(o_ref.dtype)


def _ret_sample(qt, kt, cos_col, sin_col, v, gate, state, qdec, kdec, cdec):
    nbt, _, _, bt = qt.shape
    smem = pl.BlockSpec(memory_space=pltpu.SMEM)
    st_spec = pl.BlockSpec((bt, N_RET_HEADS, RET_KDIM, RET_VDIM), lambda i: (i, 0, 0, 0))
    qk_spec = pl.BlockSpec((1, N_RET_HEADS, RET_KDIM, bt), lambda i: (i, 0, 0, 0))
    row_spec = pl.BlockSpec((1, bt, RET_V_WIDTH), lambda i: (i, 0, 0))
    col_spec = pl.BlockSpec(cos_col.shape, lambda i: (0, 0))
    return pl.pallas_call(
        _ret_sample_kernel,
        out_shape=(jax.ShapeDtypeStruct((nbt, bt, RET_V_WIDTH), BF16),
                   jax.ShapeDtypeStruct(state.shape, F32)),
        grid=(nbt,),
        in_specs=[qk_spec, qk_spec, col_spec, col_spec, row_spec, row_spec, st_spec, smem, smem, smem],
        out_specs=(row_spec, st_spec),
        compiler_params=_params(1),
        name="ret_sample",
    )(qt, kt, cos_col, sin_col, v, gate, state, qdec, kdec, cdec)


def _t5_bucket(dist):
    n = jnp.maximum(dist, 0)
    max_exact = N_BUCKETS // 2
    nf = jnp.maximum(n, 1).astype(F32)
    large = max_exact + (jnp.log(nf / max_exact) / math.log(MAX_DISTANCE / max_exact)
                         * (N_BUCKETS - max_exact)).astype(jnp.int32)
    return jnp.where(n < max_exact, n, jnp.minimum(large, N_BUCKETS - 1))


def _rope_tables(pos):
    half = RET_KDIM // 2
    inv = ROPE_BASE ** (-jnp.arange(half, dtype=F32) / half)
    ang = pos.astype(F32)[:, None] * inv[None]
    return jnp.cos(ang), jnp.sin(ang)


def _layer(xp, xs, pp, ps, cache_k, cache_v, state_ret, state_conv, rel_bias, lp, n_seq):
    (g_mix, w_in, g_q, g_k, sinks, w_out, g_ffn, w_up, conv_w, conv_b, w_down,
     g_ple, w_ple_gate, w_ple_proj) = lp
    mp = xp.shape[0]
    ms = xs.shape[0]
    seq = mp // n_seq
    win = cache_k.shape[1]

    aq, ak, av, rq, rk, rv, rg = jnp.split(w_in, IN_SPLITS, axis=1)
    w_in_b = jnp.concatenate([aq, rv, rg, rq, rk, ak, av], axis=1).astype(BF16)
    w_out_b = w_out.astype(BF16)
    w_up_b = w_up.astype(BF16)
    w_down_b = w_down.astype(BF16)
    w_gate_b = w_ple_gate.astype(BF16)
    w_pproj_b = w_ple_proj.astype(BF16)
    row = lambda a: a.reshape(1, -1)

    log_decay = jnp.log(1.0 - 2.0 ** (-5.0 - jnp.arange(N_RET_HEADS, dtype=F32)))
    idx = jnp.arange(RET_CHUNK, dtype=F32)
    diff = idx[:, None] - idx[None, :]
    dmask = jnp.where(diff[None] >= 0, jnp.exp(diff[None] * log_decay[:, None, None]), 0.0)
    q_dec = jnp.exp((idx + 1.0)[:, None] * log_decay[None])
    k_dec = jnp.exp((RET_CHUNK - 1.0 - idx)[:, None] * log_decay[None])
    qdec_t = jnp.broadcast_to(q_dec.T[:, :, None], (N_RET_HEADS, RET_CHUNK, RET_KDIM))
    kdec_t = jnp.broadcast_to(k_dec.T[:, :, None], (N_RET_HEADS, RET_CHUNK, RET_KDIM))
    cdec = jnp.exp(RET_CHUNK * log_decay)
    cos_p, sin_p = _rope_tables(jnp.arange(seq, dtype=jnp.int32))
    cosf = jnp.concatenate([cos_p, cos_p], axis=1)
    sinf = jnp.concatenate([-sin_p, sin_p], axis=1)
    one = jnp.arange(1, dtype=F32)
    qdec_s = jnp.exp((one + 1.0)[:, None] * log_decay[None])[0]
    kdec_s = jnp.exp((1.0 - 1.0 - one)[:, None] * log_decay[None])[0]
    cdec_s = jnp.exp(1.0 * log_decay)
    cos_s, sin_s = _rope_tables(PAST_LEN + jnp.arange(1, dtype=jnp.int32))
    cos_col, sin_col = cos_s.reshape(-1, 1), sin_s.reshape(-1, 1)

    qi = jnp.arange(ATTN_BLOCK, dtype=jnp.int32)
    ki = jnp.arange(2 * ATTN_BLOCK, dtype=jnp.int32) - ATTN_BLOCK
    dist = qi[:, None] - ki[None, :]
    bucket_tile = jnp.where((dist >= 0) & (dist <= WINDOW), _t5_bucket(dist), -1)
    dist_s = win - jnp.arange(2 * ATTN_BLOCK, dtype=jnp.int32)
    bucket_row = jnp.where((dist_s >= 0) & (dist_s <= WINDOW), _t5_bucket(dist_s), -1).reshape(1, -1)

    tmp, tms = _row_tile(mp), _row_tile(ms)
    proj_p = _matmul(_norm(xp, row(g_mix), NORM_TILE), w_in_b, tmp, COL_TILE)
    proj_s = _matmul(_norm(xs, row(g_mix), tms), w_in_b, tms, COL_TILE)

    attn_p, kn_p = _attn_prompt(proj_p, bucket_tile, rel_bias, sinks, row(g_q), row(g_k), n_seq)
    ret_p, ret_state_p = _ret_prompt(proj_p, cosf, sinf, dmask, qdec_t, kdec_t, cdec, n_seq)

    bt_a = 8
    attn_s, ck_new, cv_new = _attn_sample(
        proj_s[:, COL_AQ:COL_AQ + ATTN_WIDTH].reshape(ms, N_ATTN_HEADS, HEAD_DIM),
        proj_s[:, COL_AK:COL_AK + KV_WIDTH].reshape(ms, 1, KV_WIDTH),
        proj_s[:, COL_AV:COL_AV + KV_WIDTH].reshape(ms, 1, KV_WIDTH),
        cache_k.reshape(ms, win, KV_WIDTH), cache_v.reshape(ms, win, KV_WIDTH),
        bucket_row, rel_bias.T, sinks.reshape(-1, 1), row(g_q), jnp.tile(g_k, N_KV_HEADS).reshape(1, -1), bt_a)
    attn_s = attn_s.reshape(ms, ATTN_WIDTH)

    bt_r = 4
    to_cols = lambda a: a.reshape(ms // bt_r, bt_r, N_RET_HEADS, RET_KDIM).transpose(0, 2, 3, 1)
    ret_s, ret_state_s = _ret_sample(
        to_cols(proj_s[:, COL_RQ:COL_RQ + RET_QK_WIDTH]), to_cols(proj_s[:, COL_RK:COL_RK + RET_QK_WIDTH]),
        cos_col, sin_col,
        proj_s[:, COL_RV:COL_RV + RET_V_WIDTH].reshape(ms // bt_r, bt_r, RET_V_WIDTH),
        proj_s[:, COL_RG:COL_RG + RET_V_WIDTH].reshape(ms // bt_r, bt_r, RET_V_WIDTH),
        state_ret, qdec_s, kdec_s, cdec_s)
    ret_s = ret_s.reshape(ms, RET_V_WIDTH)

    xp = _out_proj(attn_p, ret_p, w_out_b, xp, tmp, COL_TILE)
    xs = _out_proj(attn_s, ret_s, w_out_b, xs, tms, COL_TILE)

    act_p, conv_p = _convglu_prompt(_norm(xp, row(g_ffn), NORM_TILE), w_up_b, conv_w, row(conv_b), n_seq, tmp)
    act_s, conv_s = _convglu_sample(_norm(xs, row(g_ffn), tms), w_up_b, conv_w, row(conv_b), state_conv)
    for part in range(DOWN_K_PARTS):
        xp = _down_proj(act_p, w_down_b, xp, tmp, COL_TILE, DOWN_K_PARTS, part)
        xs = _down_proj(act_s, w_down_b, xs, tms, COL_TILE, DOWN_K_PARTS, part)

    xp = _ple(xp, _norm(xp, row(g_ple), NORM_TILE), w_gate_b, pp, w_pproj_b, tmp, COL_TILE)
    xs = _ple(xs, _norm(xs, row(g_ple), tms), w_gate_b, ps, w_pproj_b, tms, COL_TILE)

    kp_new = kn_p.reshape(n_seq, seq, N_KV_HEADS, HEAD_DIM)[:, -WINDOW:]
    vp_new = proj_p[:, COL_AV:COL_AV + KV_WIDTH].reshape(n_seq, seq, N_KV_HEADS, HEAD_DIM)[:, -WINDOW:]
    ks_new = ck_new.reshape(ms, win, N_KV_HEADS, HEAD_DIM)
    vs_new = cv_new.reshape(ms, win, N_KV_HEADS, HEAD_DIM)
    return xp, xs, kp_new, vp_new, ret_state_p, conv_p, ks_new, vs_new, ret_state_s, conv_s


def kernel(x_prompt, x_sample, p_prompt, p_sample, cache_win_k, cache_win_v, state_ret, state_conv, rel_bias, g_mix, w_in, g_q, g_k, sinks, w_out, g_ffn, w_up, conv_w, conv_b, w_down, g_ple, w_ple_gate, w_ple_proj):
    depth = g_mix.shape[0]
    n_seq, seq, d = x_prompt.shape
    nb, dec_seq, _ = x_sample.shape
    assert dec_seq == 1 and seq % 512 == 0 and d == D_MODEL
    xp = x_prompt.reshape(n_seq * seq, d)
    xs = x_sample.reshape(nb, d)
    outs = [[] for _ in range(8)]
    for l in range(depth):
        lp = (g_mix[l], w_in[l], g_q[l], g_k[l], sinks[l], w_out[l], g_ffn[l], w_up[l], conv_w[l],
              conv_b[l], w_down[l], g_ple[l], w_ple_gate[l], w_ple_proj[l])
        res = _layer(xp, xs, p_prompt[l].reshape(n_seq * seq, -1), p_sample[l].reshape(nb, -1),
                     cache_win_k[l], cache_win_v[l], state_ret[l], state_conv[l], rel_bias, lp, n_seq)
        xp, xs = res[0], res[1]
        for o, r in zip(outs, res[2:]):
            o.append(r)
    stacked = [jnp.stack(o) for o in outs]
    return (xp.reshape(n_seq, seq, d), xs.reshape(nb, 1, d), *stacked)
```

```python
import functools
import math

import jax
import jax.numpy as jnp
from jax import lax
from jax.experimental import pallas as pl
from jax.experimental.pallas import tpu as pltpu

F32 = jnp.float32
BF16 = jnp.bfloat16

D_MODEL = 4096
HEAD_DIM = 64
N_ATTN_HEADS = 32
N_KV_HEADS = 4
GQA_GROUP = N_ATTN_HEADS // N_KV_HEADS
WINDOW = 128
ATTN_BLOCK = 128
N_BUCKETS = 32
MAX_DISTANCE = 128
N_RET_HEADS = 8
RET_KDIM = 128
RET_VDIM = 256
RET_CHUNK = 128
ROPE_BASE = 10000.0
D_FF = 11008
CONV_W = 3
EPS = 1e-6

ATTN_WIDTH = N_ATTN_HEADS * HEAD_DIM
KV_WIDTH = N_KV_HEADS * HEAD_DIM
RET_QK_WIDTH = N_RET_HEADS * RET_KDIM
RET_V_WIDTH = N_RET_HEADS * RET_VDIM
IN_SIZES = (ATTN_WIDTH, KV_WIDTH, KV_WIDTH, RET_QK_WIDTH, RET_QK_WIDTH, RET_V_WIDTH, RET_V_WIDTH)
IN_WIDTH = sum(IN_SIZES)
IN_SPLITS = tuple(sum(IN_SIZES[:n + 1]) for n in range(len(IN_SIZES) - 1))
PAST_LEN = 8192

COL_AQ = 0
COL_RV = ATTN_WIDTH
COL_RG = COL_RV + RET_V_WIDTH
COL_RQ = COL_RG + RET_V_WIDTH
COL_RK = COL_RQ + RET_QK_WIDTH
COL_AK = COL_RK + RET_QK_WIDTH
COL_AV = COL_AK + KV_WIDTH

VMEM_LIMIT_BYTES = 56 * 1024 * 1024
FF_TILE = 256
N_FF_TILES = D_FF // FF_TILE
NORM_ROWS = 32
NORM_TILE = 256
ROW_TILE = 1024
COL_TILE = 512
UP_ROW_TILE = 2048
CONV_CHUNK = 1024
DOWN_K_PARTS = 2


def _row_tile(m):
    return min(m, ROW_TILE)


def _in_src_blocks():
    start = dict(zip(("aq", "ak", "av", "rq", "rk", "rv", "rg"), (0,) + IN_SPLITS))
    order = (("aq", ATTN_WIDTH), ("rv", RET_V_WIDTH), ("rg", RET_V_WIDTH), ("rq", RET_QK_WIDTH),
             ("rk", RET_QK_WIDTH), ("ak", KV_WIDTH), ("av", KV_WIDTH))
    cols = [c for name, width in order for c in range(start[name], start[name] + width, KV_WIDTH)]
    per_tile = COL_TILE // KV_WIDTH
    blocks = []
    for b in range(0, len(cols), per_tile):
        group = cols[b:b + per_tile]
        assert group[0] % COL_TILE == 0 and all(c == group[0] + u * KV_WIDTH for u, c in enumerate(group))
        blocks.append(group[0] // COL_TILE)
    return tuple(blocks)


IN_SRC_BLOCKS = _in_src_blocks()


def _params(n_axes):
    return pltpu.CompilerParams(dimension_semantics=("arbitrary",) * n_axes,
                                vmem_limit_bytes=VMEM_LIMIT_BYTES)


def _rms(x, g):
    y = x * lax.rsqrt(jnp.mean(x * x, axis=-1, keepdims=True) + EPS)
    return y * g


def _norm_rows_to_bf16(x_ref, g_ref, h_ref):
    def body(c, carry):
        r0 = pl.multiple_of(c * NORM_ROWS, NORM_ROWS)
        h_ref[pl.ds(r0, NORM_ROWS), :] = _rms(x_ref[pl.ds(r0, NORM_ROWS), :], g_ref[...]).astype(h_ref.dtype)
        return carry
    lax.fori_loop(0, x_ref.shape[0] // NORM_ROWS, body, 0)


def _norm(x, g, tr):
    m, k = x.shape
    return pl.pallas_call(
        _norm_rows_to_bf16,
        out_shape=jax.ShapeDtypeStruct((m, k), BF16),
        grid=(m // tr,),
        in_specs=[pl.BlockSpec((tr, k), lambda i: (i, 0)),
                  pl.BlockSpec((1, k), lambda i: (0, 0))],
        out_specs=pl.BlockSpec((tr, k), lambda i: (i, 0)),
        compiler_params=_params(1),
        name="rmsnorm",
    )(x, g)


def _mxu_weights(w_ref, wb_ref):
    if not wb_ref:
        return w_ref[...]
    wb = w_ref[...].astype(BF16)
    wb_ref[0][...] = wb
    return wb


def _mm_kernel(h_ref, w_ref, o_ref, *wb_ref):
    o_ref[...] = jnp.dot(h_ref[...], _mxu_weights(w_ref, wb_ref), preferred_element_type=F32)


def _matmul(h, w, tm, tn, src_blocks=None):
    m, k = h.shape
    n = w.shape[1]
    emit = w.dtype != BF16
    if emit:
        w_map = lambda i, j: (0, sum(jnp.where(j == d, s, 0) for d, s in enumerate(src_blocks)))
    else:
        w_map = lambda i, j: (0, j)
    out_shape = [jax.ShapeDtypeStruct((m, n), F32)]
    out_specs = [pl.BlockSpec((tm, tn), lambda i, j: (i, j))]
    if emit:
        assert m == tm
        out_shape.append(jax.ShapeDtypeStruct((k, n), BF16))
        out_specs.append(pl.BlockSpec((k, tn), lambda i, j: (0, j)))
    return pl.pallas_call(
        _mm_kernel,
        out_shape=out_shape,
        grid=(m // tm, n // tn),
        in_specs=[pl.BlockSpec((tm, k), lambda i, j: (i, 0)),
                  pl.BlockSpec((k, tn), w_map)],
        out_specs=out_specs,
        compiler_params=_params(2),
        name="in_proj",
    )(h, w)


def _outproj_kernel(a_ref, r_ref, w_ref, x_ref, o_ref, *wb_ref):
    mix = jnp.concatenate([a_ref[...], r_ref[...]], axis=1)
    o_ref[...] = x_ref[...] + jnp.dot(mix, _mxu_weights(w_ref, wb_ref), preferred_element_type=F32)


def _out_proj(a, r, w, x, tm, tn):
    m = a.shape[0]
    ka, kr = a.shape[1], r.shape[1]
    n = w.shape[1]
    emit = w.dtype != BF16
    out_shape = [jax.ShapeDtypeStruct((m, n), F32)]
    out_specs = [pl.BlockSpec((tm, tn), lambda i, j: (i, j))]
    if emit:
        assert m == tm
        out_shape.append(jax.ShapeDtypeStruct(w.shape, BF16))
        out_specs.append(pl.BlockSpec((ka + kr, tn), lambda i, j: (0, j)))
    return pl.pallas_call(
        _outproj_kernel,
        out_shape=out_shape,
        grid=(m // tm, n // tn),
        in_specs=[pl.BlockSpec((tm, ka), lambda i, j: (i, 0)),
                  pl.BlockSpec((tm, kr), lambda i, j: (i, 0)),
                  pl.BlockSpec((ka + kr, tn), lambda i, j: (0, j)),
                  pl.BlockSpec((tm, tn), lambda i, j: (i, j))],
        out_specs=out_specs,
        compiler_params=_params(2),
        name="out_proj",
    )(a, r, w, x)


def _down_kernel(a_ref, w_ref, x_ref, o_ref, *wb_ref):
    o_ref[...] = x_ref[...] + jnp.dot(a_ref[...], _mxu_weights(w_ref, wb_ref), preferred_element_type=F32)


def _down_proj(a, w, x, tm, tn, k_parts, part):
    m = a.shape[0]
    k = a.shape[1] // k_parts
    n = w.shape[1]
    emit = w.dtype != BF16
    w_part = part if emit else 0
    out_shape = [jax.ShapeDtypeStruct((m, n), F32)]
    out_specs = [pl.BlockSpec((tm, tn), lambda i, j: (i, j))]
    if emit:
        assert m == tm
        out_shape.append(jax.ShapeDtypeStruct((k, n), BF16))
        out_specs.append(pl.BlockSpec((k, tn), lambda i, j: (0, j)))
    return pl.pallas_call(
        _down_kernel,
        out_shape=out_shape,
        grid=(m // tm, n // tn),
        in_specs=[pl.BlockSpec((tm, k), lambda i, j: (i, part)),
                  pl.BlockSpec((k, tn), lambda i, j: (w_part, j)),
                  pl.BlockSpec((tm, tn), lambda i, j: (i, j))],
        out_specs=out_specs,
        compiler_params=_params(2),
        name="down_proj",
    )(a, w, x)


def _ple_kernel(h_ref, wg_ref, p_ref, wp_ref, x_ref, o_ref, *wb_refs):
    z = jnp.dot(h_ref[...], _mxu_weights(wg_ref, wb_refs[:1]), preferred_element_type=F32)
    pp = jnp.dot(p_ref[...].astype(BF16), _mxu_weights(wp_ref, wb_refs[1:]), preferred_element_type=F32)
    o_ref[...] = x_ref[...] + jax.nn.sigmoid(z) * pp


def _ple(x, h, wg, p, wp, tm, tn):
    m, k = h.shape
    n = wg.shape[1]
    kp = p.shape[1]
    emit = wg.dtype != BF16
    out_shape = [jax.ShapeDtypeStruct((m, n), F32)]
    out_specs = [pl.BlockSpec((tm, tn), lambda i, j: (i, j))]
    if emit:
        assert m == tm
        out_shape += [jax.ShapeDtypeStruct(wg.shape, BF16), jax.ShapeDtypeStruct(wp.shape, BF16)]
        out_specs += [pl.BlockSpec((k, tn), lambda i, j: (0, j)), pl.BlockSpec((kp, tn), lambda i, j: (0, j))]
    return pl.pallas_call(
        _ple_kernel,
        out_shape=out_shape,
        grid=(m // tm, n // tn),
        in_specs=[pl.BlockSpec((tm, k), lambda i, j: (i, 0)),
                  pl.BlockSpec((k, tn), lambda i, j: (0, j)),
                  pl.BlockSpec((tm, kp), lambda i, j: (i, 0)),
                  pl.BlockSpec((kp, tn), lambda i, j: (0, j)),
                  pl.BlockSpec((tm, tn), lambda i, j: (i, j))],
        out_specs=out_specs,
        compiler_params=_params(2),
        name="ple",
    )(h, wg, p, wp, x)


def _gelu_erf(x):
    return 0.5 * x * (1.0 + lax.erf(x * math.sqrt(0.5)))


def _conv_taps(cb_ref, cw_ref, um2, um1, u):
    c = cb_ref[...] + cw_ref[0:1, :] * um2
    c = c + cw_ref[1:2, :] * um1
    return c + cw_ref[2:3, :] * u


def _convglu_prompt_kernel(h_ref, wg_ref, wv_ref, cwg_ref, cwv_ref, cbg_ref, cbv_ref,
                           a_ref, sg_ref, sv_ref, carry_g_ref, carry_v_ref, *, tiles_per_seq):
    i = pl.program_id(0)
    j = pl.program_id(1)
    tm = h_ref.shape[0]
    rows = min(tm, CONV_CHUNK)
    seq_start = (i % tiles_per_seq) == 0
    row = lax.broadcasted_iota(jnp.int32, (rows, 1), 0)

    def half(h, w_ref, cw_ref, cb_ref, prev):
        u = jnp.dot(h, w_ref[...], preferred_element_type=F32)
        um1 = jnp.where(row == 0, prev[1:2, :], pltpu.roll(u, 1, axis=0))
        um2 = jnp.where(row == 0, prev[0:1, :],
                        jnp.where(row == 1, prev[1:2, :], pltpu.roll(u, 2, axis=0)))
        return _conv_taps(cb_ref, cw_ref, um2, um1, u), u[rows - (CONV_W - 1):, :]

    prev_g = jnp.where(seq_start, 0.0, carry_g_ref[j])
    prev_v = jnp.where(seq_start, 0.0, carry_v_ref[j])
    for c in range(tm // rows):
        rs = slice(c * rows, (c + 1) * rows)
        h = h_ref[rs, :]
        cg, prev_g = half(h, wg_ref, cwg_ref, cbg_ref, prev_g)
        cv, prev_v = half(h, wv_ref, cwv_ref, cbv_ref, prev_v)
        a_ref[rs, :] = (_gelu_erf(cg) * cv).astype(a_ref.dtype)
    carry_g_ref[j] = prev_g
    carry_v_ref[j] = prev_v
    sg_ref[0] = prev_g
    sv_ref[0] = prev_v


def _convglu_prompt(h, w_gate, w_value, conv_w, conv_b, n_seq, tm):
    m, k = h.shape
    seq = m // n_seq
    tiles_per_seq = seq // tm
    nt = N_FF_TILES
    tn = FF_TILE
    a, sg, sv = pl.pallas_call(
        functools.partial(_convglu_prompt_kernel, tiles_per_seq=tiles_per_seq),
        out_shape=(jax.ShapeDtypeStruct((m, D_FF), BF16),
                   jax.ShapeDtypeStruct((m // tm, CONV_W - 1, D_FF), F32),
                   jax.ShapeDtypeStruct((m // tm, CONV_W - 1, D_FF), F32)),
        grid=(m // tm, nt),
        in_specs=[pl.BlockSpec((tm, k), lambda i, j: (i, 0)),
                  pl.BlockSpec((k, tn), lambda i, j: (0, j)),
                  pl.BlockSpec((k, tn), lambda i, j: (0, j)),
                  pl.BlockSpec((CONV_W, tn), lambda i, j: (0, j)),
                  pl.BlockSpec((CONV_W, tn), lambda i, j: (0, j + nt)),
                  pl.BlockSpec((1, tn), lambda i, j: (0, j)),
                  pl.BlockSpec((1, tn), lambda i, j: (0, j + nt))],
        out_specs=(pl.BlockSpec((tm, tn), lambda i, j: (i, j)),
                   pl.BlockSpec((1, CONV_W - 1, tn), lambda i, j: (i, 0, j)),
                   pl.BlockSpec((1, CONV_W - 1, tn), lambda i, j: (i, 0, j))),
        scratch_shapes=[pltpu.VMEM((nt, CONV_W - 1, tn), F32),
                        pltpu.VMEM((nt, CONV_W - 1, tn), F32)],
        compiler_params=_params(2),
        name="convglu_prompt",
    )(h, w_gate, w_value, conv_w, conv_w, conv_b, conv_b)
    tails = jnp.concatenate([sg, sv], axis=-1)
    return a, tails[tiles_per_seq - 1::tiles_per_seq]


def _convglu_sample_kernel(h_ref, wg_ref, wv_ref, cwg_ref, cwv_ref, cbg_ref, cbv_ref,
                           p0g_ref, p0v_ref, p1g_ref, p1v_ref, a_ref, ug_ref, uv_ref, wbg_ref, wbv_ref):
    def half(w_ref, wb_ref, cw_ref, cb_ref, p0_ref, p1_ref, u_ref):
        u = jnp.dot(h_ref[...], _mxu_weights(w_ref, (wb_ref,)), preferred_element_type=F32)
        u_ref[...] = u
        return _conv_taps(cb_ref, cw_ref, p0_ref[...], p1_ref[...], u)

    cg = half(wg_ref, wbg_ref, cwg_ref, cbg_ref, p0g_ref, p1g_ref, ug_ref)
    cv = half(wv_ref, wbv_ref, cwv_ref, cbv_ref, p0v_ref, p1v_ref, uv_ref)
    a_ref[...] = (_gelu_erf(cg) * cv).astype(a_ref.dtype)


def _convglu_sample(h, w_up, conv_w, conv_b, state_conv):
    m, k = h.shape
    nt = N_FF_TILES
    tn = FF_TILE
    prev = state_conv.reshape(m, (CONV_W - 1) * 2 * D_FF)
    col = lambda off: (lambda j: (0, j + off))
    a, ug, uv, w_gate, w_value = pl.pallas_call(
        _convglu_sample_kernel,
        out_shape=(jax.ShapeDtypeStruct((m, D_FF), BF16),
                   jax.ShapeDtypeStruct((m, D_FF), F32),
                   jax.ShapeDtypeStruct((m, D_FF), F32),
                   jax.ShapeDtypeStruct((k, D_FF), BF16),
                   jax.ShapeDtypeStruct((k, D_FF), BF16)),
        grid=(nt,),
        in_specs=[pl.BlockSpec((m, k), lambda j: (0, 0)),
                  pl.BlockSpec((k, tn), col(0)),
                  pl.BlockSpec((k, tn), col(nt)),
                  pl.BlockSpec((CONV_W, tn), col(0)),
                  pl.BlockSpec((CONV_W, tn), col(nt)),
                  pl.BlockSpec((1, tn), col(0)),
                  pl.BlockSpec((1, tn), col(nt)),
                  pl.BlockSpec((m, tn), col(0)),
                  pl.BlockSpec((m, tn), col(nt)),
                  pl.BlockSpec((m, tn), col(2 * nt)),
                  pl.BlockSpec((m, tn), col(3 * nt))],
        out_specs=(pl.BlockSpec((m, tn), col(0)),
                   pl.BlockSpec((m, tn), col(0)),
                   pl.BlockSpec((m, tn), col(0)),
                   pl.BlockSpec((k, tn), col(0)),
                   pl.BlockSpec((k, tn), col(0))),
        compiler_params=_params(1),
        name="convglu_sample",
    )(h, w_up, w_up, conv_w, conv_w, conv_b, conv_b, prev, prev, prev, prev)
    u = jnp.concatenate([ug, uv], axis=-1)
    conv_new = jnp.stack([state_conv[:, 1, :], u], axis=1)
    return a, conv_new, w_gate, w_value


def _attn_prompt_kernel(q_ref, kc_ref, kp_ref, vc_ref, vp_ref, bucket_ref, rb_ref, sinks_ref,
                        gq_ref, gk_ref, o_ref, kn_ref, bias_ref, *, blocks_per_seq):
    r = pl.program_id(0)
    blk = ATTN_BLOCK

    @pl.when(r == 0)
    def _():
        bucket = bucket_ref[...]

        def per_head(h, carry):
            acc = jnp.full(bucket.shape, -jnp.inf, F32)
            for b in range(N_BUCKETS):
                acc = jnp.where(bucket == b, rb_ref[b, h], acc)
            bias_ref[pl.ds(pl.multiple_of(h * blk, blk), blk), :] = acc
            return carry
        lax.fori_loop(0, N_ATTN_HEADS, per_head, 0)

    n_masked = jnp.where((r % blocks_per_seq) == 0, blk, 0)
    key_col = lax.broadcasted_iota(jnp.int32, (blk, 2 * blk), 1)
    before_start = key_col < n_masked
    gq = gq_ref[...]
    gk = gk_ref[...]

    kn_parts = []
    for j in range(N_KV_HEADS):
        hd = slice(j * HEAD_DIM, (j + 1) * HEAD_DIM)
        kn_c = _rms(kc_ref[:, hd], gk)
        kn_p = _rms(kp_ref[:, hd], gk)
        kn_parts.append(kn_c)
        kcat = jnp.concatenate([kn_p, kn_c], axis=0).astype(BF16)
        vcat = jnp.concatenate([vp_ref[:, hd], vc_ref[:, hd]], axis=0).astype(BF16)
        heads = range(j * GQA_GROUP, (j + 1) * GQA_GROUP)
        qs = jnp.concatenate(
            [_rms(q_ref[:, h * HEAD_DIM:(h + 1) * HEAD_DIM], gq) for h in heads], axis=0).astype(BF16)
        s = lax.dot_general(qs, kcat, (((1,), (1,)), ((), ())), preferred_element_type=F32)
        probs = []
        for g, h in enumerate(heads):
            sg = s[g * blk:(g + 1) * blk, :] * HEAD_DIM ** -0.5 + bias_ref[h * blk:(h + 1) * blk, :]
            sg = jnp.where(before_start, -jnp.inf, sg)
            sink = sinks_ref[h]
            m = jnp.maximum(jnp.max(sg, axis=-1, keepdims=True), sink)
            e = jnp.exp(sg - m)
            p = e / (jnp.sum(e, axis=-1, keepdims=True) + jnp.exp(sink - m))
            probs.append(p.astype(BF16))
        o = jnp.dot(jnp.concatenate(probs, axis=0), vcat, preferred_element_type=F32)
        for g, h in enumerate(heads):
            o_ref[:, h * HEAD_DIM:(h + 1) * HEAD_DIM] = o[g * blk:(g + 1) * blk, :].astype(o_ref.dtype)
    kn_ref[...] = jnp.concatenate(kn_parts, axis=1)


def _attn_prompt(proj, bucket_tile, rel_bias, sinks, g_q, g_k, n_seq):
    m = proj.shape[0]
    blk = ATTN_BLOCK
    nb = m // blk
    blocks_per_seq = nb // n_seq
    kcol = COL_AK // KV_WIDTH
    vcol = COL_AV // KV_WIDTH
    prev = lambda r: jnp.maximum(r - 1, 0)
    smem = pl.BlockSpec(memory_space=pltpu.SMEM)
    return pl.pallas_call(
        functools.partial(_attn_prompt_kernel, blocks_per_seq=blocks_per_seq),
        out_shape=(jax.ShapeDtypeStruct((m, ATTN_WIDTH), BF16),
                   jax.ShapeDtypeStruct((m, KV_WIDTH), F32)),
        grid=(nb,),
        in_specs=[pl.BlockSpec((blk, ATTN_WIDTH), lambda r: (r, COL_AQ // ATTN_WIDTH)),
                  pl.BlockSpec((blk, KV_WIDTH), lambda r: (r, kcol)),
                  pl.BlockSpec((blk, KV_WIDTH), lambda r: (prev(r), kcol)),
                  pl.BlockSpec((blk, KV_WIDTH), lambda r: (r, vcol)),
                  pl.BlockSpec((blk, KV_WIDTH), lambda r: (prev(r), vcol)),
                  pl.BlockSpec((blk, 2 * blk), lambda r: (0, 0)),
                  smem, smem,
                  pl.BlockSpec((1, HEAD_DIM), lambda r: (0, 0)),
                  pl.BlockSpec((1, HEAD_DIM), lambda r: (0, 0))],
        out_specs=(pl.BlockSpec((blk, ATTN_WIDTH), lambda r: (r, 0)),
                   pl.BlockSpec((blk, KV_WIDTH), lambda r: (r, 0))),
        scratch_shapes=[pltpu.VMEM((N_ATTN_HEADS * blk, 2 * blk), F32)],
        compiler_params=_params(1),
        name="attn_prompt",
    )(proj, proj, proj, proj, proj, bucket_tile, rel_bias, sinks, g_q, g_k)


def _attn_sample_kernel(q_ref, knew_ref, vnew_ref, ck_ref, cv_ref, bucket_ref, rbt_ref, sinks_ref,
                        gq_ref, gk_ref, o_ref, cko_ref, cvo_ref):
    bt = q_ref.shape[0]
    win = ck_ref.shape[1]
    bucket = bucket_ref[...]
    bias = jnp.full((N_ATTN_HEADS, bucket.shape[1]), -jnp.inf, F32)
    for b in range(N_BUCKETS):
        bias = jnp.where(bucket == b, rbt_ref[:, b:b + 1], bias)
    bias_c = bias[:, :win]
    bias_n = bias[:, win:win + 1]
    sink = sinks_ref[...]
    head_group = jnp.right_shift(lax.broadcasted_iota(jnp.int32, (N_ATTN_HEADS, HEAD_DIM), 0),
                                 int(math.log2(GQA_GROUP)))
    lane_group = jnp.right_shift(lax.broadcasted_iota(jnp.int32, (1, KV_WIDTH), 1),
                                 int(math.log2(HEAD_DIM)))
    gk = gk_ref[...]
    scale = HEAD_DIM ** -0.5

    for b in range(bt):
        k_row = knew_ref[b]
        k_sq = k_row * k_row
        inv = jnp.zeros_like(k_row)
        for j in range(N_KV_HEADS):
            ms = jnp.sum(jnp.where(lane_group == j, k_sq, 0.0), axis=-1, keepdims=True) / HEAD_DIM
            inv = jnp.where(lane_group == j, lax.rsqrt(ms + EPS), inv)
        kn_row = k_row * inv * gk
        v_row = vnew_ref[b]

        qn = _rms(q_ref[b], gq_ref[...])
        q_bd = jnp.concatenate([jnp.where(head_group == j, qn, 0.0) for j in range(N_KV_HEADS)], axis=1)
        kc = ck_ref[b]
        vc = cv_ref[b]
        s_c = lax.dot_general(q_bd.astype(BF16), kc.astype(BF16), (((1,), (1,)), ((), ())),
                              preferred_element_type=F32) * scale + bias_c
        s_n = jnp.sum(q_bd * kn_row, axis=-1, keepdims=True) * scale + bias_n
        m = jnp.maximum(jnp.maximum(jnp.max(s_c, axis=-1, keepdims=True), s_n), sink)
        e_c = jnp.exp(s_c - m)
        e_n = jnp.exp(s_n - m)
        denom = jnp.sum(e_c, axis=-1, keepdims=True) + e_n + jnp.exp(sink - m)
        o_full = jnp.dot((e_c / denom).astype(BF16), vc.astype(BF16), preferred_element_type=F32)
        o_full = o_full + (e_n / denom) * v_row
        o = jnp.zeros((N_ATTN_HEADS, HEAD_DIM), F32)
        for j in range(N_KV_HEADS):
            o = jnp.where(head_group == j, o_full[:, j * HEAD_DIM:(j + 1) * HEAD_DIM], o)
        o_ref[b] = o.astype(o_ref.dtype)

        cko_ref[b, pl.ds(0, win - 1), :] = ck_ref[b, pl.ds(1, win - 1), :]
        cko_ref[b, pl.ds(win - 1, 1), :] = kn_row
        cvo_ref[b, pl.ds(0, win - 1), :] = cv_ref[b, pl.ds(1, win - 1), :]
        cvo_ref[b, pl.ds(win - 1, 1), :] = v_row


def _attn_sample(q, k_new, v_new, cache_k, cache_v, bucket_row, rel_bias_t, sinks_col, g_q, g_k_row, bt):
    nb = q.shape[0]
    win = cache_k.shape[1]
    full = lambda shape: pl.BlockSpec(shape, lambda i: (0,) * len(shape))
    return pl.pallas_call(
        _attn_sample_kernel,
        out_shape=(jax.ShapeDtypeStruct((nb, N_ATTN_HEADS, HEAD_DIM), BF16),
                   jax.ShapeDtypeStruct(cache_k.shape, F32),
                   jax.ShapeDtypeStruct(cache_v.shape, F32)),
        grid=(nb // bt,),
        in_specs=[pl.BlockSpec((bt, N_ATTN_HEADS, HEAD_DIM), lambda i: (i, 0, 0)),
                  pl.BlockSpec((bt, 1, KV_WIDTH), lambda i: (i, 0, 0)),
                  pl.BlockSpec((bt, 1, KV_WIDTH), lambda i: (i, 0, 0)),
                  pl.BlockSpec((bt, win, KV_WIDTH), lambda i: (i, 0, 0)),
                  pl.BlockSpec((bt, win, KV_WIDTH), lambda i: (i, 0, 0)),
                  full(bucket_row.shape), full(rel_bias_t.shape), full(sinks_col.shape),
                  full(g_q.shape), full(g_k_row.shape)],
        out_specs=(pl.BlockSpec((bt, N_ATTN_HEADS, HEAD_DIM), lambda i: (i, 0, 0)),
                   pl.BlockSpec((bt, win, KV_WIDTH), lambda i: (i, 0, 0)),
                   pl.BlockSpec((bt, win, KV_WIDTH), lambda i: (i, 0, 0))),
        compiler_params=_params(1),
        name="attn_sample",
    )(q, k_new, v_new, cache_k, cache_v, bucket_row, rel_bias_t, sinks_col, g_q, g_k_row)


def _ret_prompt_kernel(q_ref, k_ref, v_ref, gate_ref, cos_ref, sin_ref, dmask_ref, qdec_ref, kdec_ref,
                       cdec_ref, o_ref, state_ref, s_ref):
    c = pl.program_id(1)

    @pl.when(c == 0)
    def _():
        s_ref[...] = jnp.zeros_like(s_ref)

    cosf = cos_ref[...]
    sinf = sin_ref[...]
    half = RET_KDIM // 2
    nt = (((1,), (1,)), ((), ()))
    tn = (((0,), (0,)), ((), ()))

    def rotary(x):
        return x * cosf + pltpu.roll(x, half, axis=1) * sinf

    for h in range(N_RET_HEADS):
        kd = slice(h * RET_KDIM, (h + 1) * RET_KDIM)
        vd = slice(h * RET_VDIM, (h + 1) * RET_VDIM)
        q = rotary(q_ref[:, kd])
        k = rotary(k_ref[:, kd]) * RET_KDIM ** -0.5
        v = v_ref[:, vd].astype(BF16)
        s0 = s_ref[h]
        scores = lax.dot_general(q.astype(BF16), k.astype(BF16), nt,
                                 preferred_element_type=F32) * dmask_ref[h]
        o_intra = jnp.dot(scores.astype(BF16), v, preferred_element_type=F32)
        o_inter = jnp.dot((q * qdec_ref[h]).astype(BF16), s0.astype(BF16), preferred_element_type=F32)
        s_ref[h] = cdec_ref[h] * s0 + lax.dot_general((k * kdec_ref[h]).astype(BF16), v, tn,
                                                       preferred_element_type=F32)
        ro = o_intra + o_inter
        ro = ro * lax.rsqrt(jnp.mean(ro * ro, axis=-1, keepdims=True) + EPS)
        o_ref[:, vd] = (jax.nn.silu(gate_ref[:, vd]) * ro).astype(o_ref.dtype)

    @pl.when(c == pl.num_programs(1) - 1)
    def _():
        state_ref[0] = s_ref[...]


def _ret_prompt(proj, cosf, sinf, dmask, qdec, kdec, cdec, n_seq):
    m = proj.shape[0]
    ch = RET_CHUNK
    nc = m // n_seq // ch
    row = lambda b, c: b * nc + c
    const3 = pl.BlockSpec((N_RET_HEADS, ch, ch), lambda b, c: (0, 0, 0))
    return pl.pallas_call(
        _ret_prompt_kernel,
        out_shape=(jax.ShapeDtypeStruct((m, RET_V_WIDTH), BF16),
                   jax.ShapeDtypeStruct((n_seq, N_RET_HEADS, RET_KDIM, RET_VDIM), F32)),
        grid=(n_seq, nc),
        in_specs=[pl.BlockSpec((ch, RET_QK_WIDTH), lambda b, c: (row(b, c), COL_RQ // RET_QK_WIDTH)),
                  pl.BlockSpec((ch, RET_QK_WIDTH), lambda b, c: (row(b, c), COL_RK // RET_QK_WIDTH)),
                  pl.BlockSpec((ch, RET_V_WIDTH), lambda b, c: (row(b, c), COL_RV // RET_V_WIDTH)),
                  pl.BlockSpec((ch, RET_V_WIDTH), lambda b, c: (row(b, c), COL_RG // RET_V_WIDTH)),
                  pl.BlockSpec((ch, RET_KDIM), lambda b, c: (c, 0)),
                  pl.BlockSpec((ch, RET_KDIM), lambda b, c: (c, 0)),
                  const3, const3, const3,
                  pl.BlockSpec(memory_space=pltpu.SMEM)],
        out_specs=(pl.BlockSpec((ch, RET_V_WIDTH), lambda b, c: (row(b, c), 0)),
                   pl.BlockSpec((1, N_RET_HEADS, RET_KDIM, RET_VDIM), lambda b, c: (b, 0, 0, 0))),
        scratch_shapes=[pltpu.VMEM((N_RET_HEADS, RET_KDIM, RET_VDIM), F32)],
        compiler_params=_params(2),
        name="ret_prompt",
    )(proj, proj, proj, proj, cosf, sinf, dmask, qdec, kdec, cdec)


def _ret_sample_kernel(qt_ref, kt_ref, cos_ref, sin_ref, v_ref, gate_ref, s_ref, qdec_ref, kdec_ref,
                       cdec_ref, o_ref, so_ref):
    bt = s_ref.shape[0]
    half = RET_KDIM // 2
    cos = cos_ref[...]
    sin = sin_ref[...]

    def rotary(x):
        x1, x2 = x[:half, :], x[half:, :]
        return jnp.concatenate([x1 * cos - x2 * sin, x2 * cos + x1 * sin], axis=0)

    for h in range(N_RET_HEADS):
        vd = slice(h * RET_VDIM, (h + 1) * RET_VDIM)
        q_all = rotary(qt_ref[0, h])
        k_all = rotary(kt_ref[0, h]) * RET_KDIM ** -0.5
        for b in range(bt):
            q = q_all[:, b:b + 1]
            k = k_all[:, b:b + 1]
            v = v_ref[0, b:b + 1, vd]
            s0 = s_ref[b, h]
            o_intra = jnp.sum(q * k, axis=0, keepdims=True) * v
            o_inter = jnp.sum((q * qdec_ref[h]) * s0, axis=0, keepdims=True)
            so_ref[b, h] = cdec_ref[h] * s0 + (k * kdec_ref[h]) * v
            ro = o_intra + o_inter
            ro = ro * lax.rsqrt(jnp.mean(ro * ro, axis=-1, keepdims=True) + EPS)
            o_ref[0, b:b + 1, vd] = (jax.nn.silu(gate_ref[0, b:b + 1, vd]) * ro).astype(o_ref.dtype)


def _ret_sample(qt, kt, cos_col, sin_col, v, gate, state, qdec, kdec, cdec):
    nbt, _, _, bt = qt.shape
    smem = pl.BlockSpec(memory_space=pltpu.SMEM)
    st_spec = pl.BlockSpec((bt, N_RET_HEADS, RET_KDIM, RET_VDIM), lambda i: (i, 0, 0, 0))
    qk_spec = pl.BlockSpec((1, N_RET_HEADS, RET_KDIM, bt), lambda i: (i, 0, 0, 0))
    row_spec = pl.BlockSpec((1, bt, RET_V_WIDTH), lambda i: (i, 0, 0))
    col_spec = pl.BlockSpec(cos_col.shape, lambda i: (0, 0))
    return pl.pallas_call(
        _ret_sample_kernel,
        out_shape=(jax.ShapeDtypeStruct((nbt, bt, RET_V_WIDTH), BF16),
                   jax.ShapeDtypeStruct(state.shape, F32)),
        grid=(nbt,),
        in_specs=[qk_spec, qk_spec, col_spec, col_spec, row_spec, row_spec, st_spec, smem, smem, smem],
        out_specs=(row_spec, st_spec),
        compiler_params=_params(1),
        name="ret_sample",
    )(qt, kt, cos_col, sin_col, v, gate, state, qdec, kdec, cdec)


def _t5_bucket(dist):
    n = jnp.maximum(dist, 0)
    max_exact = N_BUCKETS // 2
    nf = jnp.maximum(n, 1).astype(F32)
    large = max_exact + (jnp.log(nf / max_exact) / math.log(MAX_DISTANCE / max_exact)
                         * (N_BUCKETS - max_exact)).astype(jnp.int32)
    return jnp.where(n < max_exact, n, jnp.minimum(large, N_BUCKETS - 1))


def _rope_tables(pos):
    half = RET_KDIM // 2
    inv = ROPE_BASE ** (-jnp.arange(half, dtype=F32) / half)
    ang = pos.astype(F32)[:, None] * inv[None]
    return jnp.cos(ang), jnp.sin(ang)


def _layer(xp, xs, pp, ps, cache_k, cache_v, state_ret, state_conv, rel_bias, lp, n_seq):
    (g_mix, w_in, g_q, g_k, sinks, w_out, g_ffn, w_up, conv_w, conv_b, w_down,
     g_ple, w_ple_gate, w_ple_proj) = lp
    mp = xp.shape[0]
    ms = xs.shape[0]
    seq = mp // n_seq
    win = cache_k.shape[1]

    row = lambda a: a.reshape(1, -1)

    log_decay = jnp.log(1.0 - 2.0 ** (-5.0 - jnp.arange(N_RET_HEADS, dtype=F32)))
    idx = jnp.arange(RET_CHUNK, dtype=F32)
    diff = idx[:, None] - idx[None, :]
    dmask = jnp.where(diff[None] >= 0, jnp.exp(diff[None] * log_decay[:, None, None]), 0.0)
    q_dec = jnp.exp((idx + 1.0)[:, None] * log_decay[None])
    k_dec = jnp.exp((RET_CHUNK - 1.0 - idx)[:, None] * log_decay[None])
    qdec_t = jnp.broadcast_to(q_dec.T[:, :, None], (N_RET_HEADS, RET_CHUNK, RET_KDIM))
    kdec_t = jnp.broadcast_to(k_dec.T[:, :, None], (N_RET_HEADS, RET_CHUNK, RET_KDIM))
    cdec = jnp.exp(RET_CHUNK * log_decay)
    cos_p, sin_p = _rope_tables(jnp.arange(seq, dtype=jnp.int32))
    cosf = jnp.concatenate([cos_p, cos_p], axis=1)
    sinf = jnp.concatenate([-sin_p, sin_p], axis=1)
    one = jnp.arange(1, dtype=F32)
    qdec_s = jnp.exp((one + 1.0)[:, None] * log_decay[None])[0]
    kdec_s = jnp.exp((1.0 - 1.0 - one)[:, None] * log_decay[None])[0]
    cdec_s = jnp.exp(1.0 * log_decay)
    cos_s, sin_s = _rope_tables(PAST_LEN + jnp.arange(1, dtype=jnp.int32))
    cos_col, sin_col = cos_s.reshape(-1, 1), sin_s.reshape(-1, 1)

    qi = jnp.arange(ATTN_BLOCK, dtype=jnp.int32)
    ki = jnp.arange(2 * ATTN_BLOCK, dtype=jnp.int32) - ATTN_BLOCK
    dist = qi[:, None] - ki[None, :]
    bucket_tile = jnp.where((dist >= 0) & (dist <= WINDOW), _t5_bucket(dist), -1)
    dist_s = win - jnp.arange(2 * ATTN_BLOCK, dtype=jnp.int32)
    bucket_row = jnp.where((dist_s >= 0) & (dist_s <= WINDOW), _t5_bucket(dist_s), -1).reshape(1, -1)

    tmp, tms = _row_tile(mp), _row_tile(ms)
    proj_s, w_in_b = _matmul(_norm(xs, row(g_mix), tms), w_in, tms, COL_TILE, IN_SRC_BLOCKS)
    proj_p, = _matmul(_norm(xp, row(g_mix), NORM_TILE), w_in_b, tmp, COL_TILE)

    attn_p, kn_p = _attn_prompt(proj_p, bucket_tile, rel_bias, sinks, row(g_q), row(g_k), n_seq)
    ret_p, ret_state_p = _ret_prompt(proj_p, cosf, sinf, dmask, qdec_t, kdec_t, cdec, n_seq)

    bt_a = 8
    attn_s, ck_new, cv_new = _attn_sample(
        proj_s[:, COL_AQ:COL_AQ + ATTN_WIDTH].reshape(ms, N_ATTN_HEADS, HEAD_DIM),
        proj_s[:, COL_AK:COL_AK + KV_WIDTH].reshape(ms, 1, KV_WIDTH),
        proj_s[:, COL_AV:COL_AV + KV_WIDTH].reshape(ms, 1, KV_WIDTH),
        cache_k.reshape(ms, win, KV_WIDTH), cache_v.reshape(ms, win, KV_WIDTH),
        bucket_row, rel_bias.T, sinks.reshape(-1, 1), row(g_q), jnp.tile(g_k, N_KV_HEADS).reshape(1, -1), bt_a)
    attn_s = attn_s.reshape(ms, ATTN_WIDTH)

    bt_r = 4
    to_cols = lambda a: a.reshape(ms // bt_r, bt_r, N_RET_HEADS, RET_KDIM).transpose(0, 2, 3, 1)
    ret_s, ret_state_s = _ret_sample(
        to_cols(proj_s[:, COL_RQ:COL_RQ + RET_QK_WIDTH]), to_cols(proj_s[:, COL_RK:COL_RK + RET_QK_WIDTH]),
        cos_col, sin_col,
        proj_s[:, COL_RV:COL_RV + RET_V_WIDTH].reshape(ms // bt_r, bt_r, RET_V_WIDTH),
        proj_s[:, COL_RG:COL_RG + RET_V_WIDTH].reshape(ms // bt_r, bt_r, RET_V_WIDTH),
        state_ret, qdec_s, kdec_s, cdec_s)
    ret_s = ret_s.reshape(ms, RET_V_WIDTH)

    xs, w_out_b = _out_proj(attn_s, ret_s, w_out, xs, tms, COL_TILE)
    xp, = _out_proj(attn_p, ret_p, w_out_b, xp, tmp, COL_TILE)

    act_s, conv_s, w_gate_b, w_value_b = _convglu_sample(_norm(xs, row(g_ffn), tms), w_up, conv_w, row(conv_b),
                                                         state_conv)
    act_p, conv_p = _convglu_prompt(_norm(xp, row(g_ffn), NORM_TILE), w_gate_b, w_value_b, conv_w, row(conv_b),
                                    n_seq, UP_ROW_TILE)
    for part in range(DOWN_K_PARTS):
        xs, w_down_b = _down_proj(act_s, w_down, xs, tms, COL_TILE, DOWN_K_PARTS, part)
        xp, = _down_proj(act_p, w_down_b, xp, tmp, COL_TILE, DOWN_K_PARTS, part)

    xs, w_pgate_b, w_pproj_b = _ple(xs, _norm(xs, row(g_ple), tms), w_ple_gate, ps, w_ple_proj, tms, COL_TILE)
    xp, = _ple(xp, _norm(xp, row(g_ple), NORM_TILE), w_pgate_b, pp, w_pproj_b, tmp, COL_TILE)

    kp_new = kn_p.reshape(n_seq, seq, N_KV_HEADS, HEAD_DIM)[:, -WINDOW:]
    vp_new = proj_p[:, COL_AV:COL_AV + KV_WIDTH].reshape(n_seq, seq, N_KV_HEADS, HEAD_DIM)[:, -WINDOW:]
    ks_new = ck_new.reshape(ms, win, N_KV_HEADS, HEAD_DIM)
    vs_new = cv_new.reshape(ms, win, N_KV_HEADS, HEAD_DIM)
    return xp, xs, kp_new, vp_new, ret_state_p, conv_p, ks_new, vs_new, ret_state_s, conv_s


def kernel(x_prompt, x_sample, p_prompt, p_sample, cache_win_k, cache_win_v, state_ret, state_conv, rel_bias, g_mix, w_in, g_q, g_k, sinks, w_out, g_ffn, w_up, conv_w, conv_b, w_down, g_ple, w_ple_gate, w_ple_proj):
    depth = g_mix.shape[0]
    n_seq, seq, d = x_prompt.shape
    nb, dec_seq, _ = x_sample.shape
    assert dec_seq == 1 and seq % 512 == 0 and d == D_MODEL
    xp = x_prompt.reshape(n_seq * seq, d)
    xs = x_sample.reshape(nb, d)
    outs = [[] for _ in range(8)]
    for l in range(depth):
        lp = (g_mix[l], w_in[l], g_q[l], g_k[l], sinks[l], w_out[l], g_ffn[l], w_up[l], conv_w[l],
              conv_b[l], w_down[l], g_ple[l], w_ple_gate[l], w_ple_proj[l])
        res = _layer(xp, xs, p_prompt[l].reshape(n_seq * seq, -1), p_sample[l].reshape(nb, -1),
                     cache_win_k[l], cache_win_v[l], state_ret[l], state_conv[l], rel_bias, lp, n_seq)
        xp, xs = res[0], res[1]
        for o, r in zip(outs, res[2:]):
            o.append(r)
    stacked = [jnp.stack(o) for o in outs]
    return (xp.reshape(n_seq, seq, d), xs.reshape(nb, 1, d), *stacked)
```

```python
import functools
import math

import jax
import jax.numpy as jnp
from jax import lax
from jax.experimental import pallas as pl
from jax.experimental.pallas import tpu as pltpu

F32 = jnp.float32
BF16 = jnp.bfloat16

D_MODEL = 4096
HEAD_DIM = 64
N_ATTN_HEADS = 32
N_KV_HEADS = 4
GQA_GROUP = N_ATTN_HEADS // N_KV_HEADS
WINDOW = 128
ATTN_BLOCK = 128
N_BUCKETS = 32
MAX_DISTANCE = 128
N_RET_HEADS = 8
RET_KDIM = 128
RET_VDIM = 256
RET_CHUNK = 128
ROPE_BASE = 10000.0
D_FF = 11008
CONV_W = 3
EPS = 1e-6

ATTN_WIDTH = N_ATTN_HEADS * HEAD_DIM
KV_WIDTH = N_KV_HEADS * HEAD_DIM
RET_QK_WIDTH = N_RET_HEADS * RET_KDIM
RET_V_WIDTH = N_RET_HEADS * RET_VDIM
IN_SIZES = (ATTN_WIDTH, KV_WIDTH, KV_WIDTH, RET_QK_WIDTH, RET_QK_WIDTH, RET_V_WIDTH, RET_V_WIDTH)
IN_WIDTH = sum(IN_SIZES)
IN_SPLITS = tuple(sum(IN_SIZES[:n + 1]) for n in range(len(IN_SIZES) - 1))
PAST_LEN = 8192

COL_AQ = 0
COL_RV = ATTN_WIDTH
COL_RG = COL_RV + RET_V_WIDTH
COL_RQ = COL_RG + RET_V_WIDTH
COL_RK = COL_RQ + RET_QK_WIDTH
COL_AK = COL_RK + RET_QK_WIDTH
COL_AV = COL_AK + KV_WIDTH

VMEM_LIMIT_BYTES = 56 * 1024 * 1024
FF_TILE = 256
N_FF_TILES = D_FF // FF_TILE
NORM_ROWS = 32
NORM_TILE = 256
ROW_TILE = 1024
COL_TILE = 512
UP_ROW_TILE = 2048
CONV_CHUNK = 1024
DOWN_K_PARTS = 2


def _row_tile(m):
    return min(m, ROW_TILE)


def _in_src_blocks():
    start = dict(zip(("aq", "ak", "av", "rq", "rk", "rv", "rg"), (0,) + IN_SPLITS))
    order = (("aq", ATTN_WIDTH), ("rv", RET_V_WIDTH), ("rg", RET_V_WIDTH), ("rq", RET_QK_WIDTH),
             ("rk", RET_QK_WIDTH), ("ak", KV_WIDTH), ("av", KV_WIDTH))
    cols = [c for name, width in order for c in range(start[name], start[name] + width, KV_WIDTH)]
    per_tile = COL_TILE // KV_WIDTH
    blocks = []
    for b in range(0, len(cols), per_tile):
        group = cols[b:b + per_tile]
        assert group[0] % COL_TILE == 0 and all(c == group[0] + u * KV_WIDTH for u, c in enumerate(group))
        blocks.append(group[0] // COL_TILE)
    return tuple(blocks)


IN_SRC_BLOCKS = _in_src_blocks()


def _params(n_axes):
    return pltpu.CompilerParams(dimension_semantics=("arbitrary",) * n_axes,
                                vmem_limit_bytes=VMEM_LIMIT_BYTES)


def _rms(x, g):
    y = x * lax.rsqrt(jnp.mean(x * x, axis=-1, keepdims=True) + EPS)
    return y * g


def _norm_rows_to_bf16(x_ref, g_ref, h_ref):
    def body(c, carry):
        r0 = pl.multiple_of(c * NORM_ROWS, NORM_ROWS)
        h_ref[pl.ds(r0, NORM_ROWS), :] = _rms(x_ref[pl.ds(r0, NORM_ROWS), :], g_ref[...]).astype(h_ref.dtype)
        return carry
    lax.fori_loop(0, x_ref.shape[0] // NORM_ROWS, body, 0)


def _norm(x, g, tr):
    m, k = x.shape
    return pl.pallas_call(
        _norm_rows_to_bf16,
        out_shape=jax.ShapeDtypeStruct((m, k), BF16),
        grid=(m // tr,),
        in_specs=[pl.BlockSpec((tr, k), lambda i: (i, 0)),
                  pl.BlockSpec((1, k), lambda i: (0, 0))],
        out_specs=pl.BlockSpec((tr, k), lambda i: (i, 0)),
        compiler_params=_params(1),
        name="rmsnorm",
    )(x, g)


def _mxu_weights(w_ref, wb_ref):
    if not wb_ref:
        return w_ref[...]
    wb = w_ref[...].astype(BF16)
    wb_ref[0][...] = wb
    return wb


def _mm_kernel(h_ref, w_ref, o_ref, *wb_ref):
    o_ref[...] = jnp.dot(h_ref[...], _mxu_weights(w_ref, wb_ref), preferred_element_type=F32)


def _matmul(h, w, tm, tn, src_blocks=None):
    m, k = h.shape
    n = w.shape[1]
    emit = w.dtype != BF16
    if emit:
        w_map = lambda i, j: (0, sum(jnp.where(j == d, s, 0) for d, s in enumerate(src_blocks)))
    else:
        w_map = lambda i, j: (0, j)
    out_shape = [jax.ShapeDtypeStruct((m, n), F32)]
    out_specs = [pl.BlockSpec((tm, tn), lambda i, j: (i, j))]
    if emit:
        assert m == tm
        out_shape.append(jax.ShapeDtypeStruct((k, n), BF16))
        out_specs.append(pl.BlockSpec((k, tn), lambda i, j: (0, j)))
    return pl.pallas_call(
        _mm_kernel,
        out_shape=out_shape,
        grid=(m // tm, n // tn),
        in_specs=[pl.BlockSpec((tm, k), lambda i, j: (i, 0)),
                  pl.BlockSpec((k, tn), w_map)],
        out_specs=out_specs,
        compiler_params=_params(2),
        name="in_proj",
    )(h, w)


def _outproj_kernel(a_ref, r_ref, w_ref, x_ref, o_ref, *wb_ref):
    mix = jnp.concatenate([a_ref[...], r_ref[...]], axis=1)
    o_ref[...] = x_ref[...] + jnp.dot(mix, _mxu_weights(w_ref, wb_ref), preferred_element_type=F32)


def _out_proj(a, r, w, x, tm, tn):
    m = a.shape[0]
    ka, kr = a.shape[1], r.shape[1]
    n = w.shape[1]
    emit = w.dtype != BF16
    out_shape = [jax.ShapeDtypeStruct((m, n), F32)]
    out_specs = [pl.BlockSpec((tm, tn), lambda i, j: (i, j))]
    if emit:
        assert m == tm
        out_shape.append(jax.ShapeDtypeStruct(w.shape, BF16))
        out_specs.append(pl.BlockSpec((ka + kr, tn), lambda i, j: (0, j)))
    return pl.pallas_call(
        _outproj_kernel,
        out_shape=out_shape,
        grid=(m // tm, n // tn),
        in_specs=[pl.BlockSpec((tm, ka), lambda i, j: (i, 0)),
                  pl.BlockSpec((tm, kr), lambda i, j: (i, 0)),
                  pl.BlockSpec((ka + kr, tn), lambda i, j: (0, j)),
                  pl.BlockSpec((tm, tn), lambda i, j: (i, j))],
        out_specs=out_specs,
        compiler_params=_params(2),
        name="out_proj",
    )(a, r, w, x)


def _down_kernel(a_ref, w_ref, x_ref, o_ref, *wb_ref):
    o_ref[...] = x_ref[...] + jnp.dot(a_ref[...], _mxu_weights(w_ref, wb_ref), preferred_element_type=F32)


def _down_proj(a, w, x, tm, tn, k_parts, part):
    m = a.shape[0]
    k = a.shape[1] // k_parts
    n = w.shape[1]
    emit = w.dtype != BF16
    w_part = part if emit else 0
    out_shape = [jax.ShapeDtypeStruct((m, n), F32)]
    out_specs = [pl.BlockSpec((tm, tn), lambda i, j: (i, j))]
    if emit:
        assert m == tm
        out_shape.append(jax.ShapeDtypeStruct((k, n), BF16))
        out_specs.append(pl.BlockSpec((k, tn), lambda i, j: (0, j)))
    return pl.pallas_call(
        _down_kernel,
        out_shape=out_shape,
        grid=(m // tm, n // tn),
        in_specs=[pl.BlockSpec((tm, k), lambda i, j: (i, part)),
                  pl.BlockSpec((k, tn), lambda i, j: (w_part, j)),
                  pl.BlockSpec((tm, tn), lambda i, j: (i, j))],
        out_specs=out_specs,
        compiler_params=_params(2),
        name="down_proj",
    )(a, w, x)


def _ple_kernel(h_ref, wg_ref, p_ref, wp_ref, x_ref, o_ref, *wb_refs):
    z = jnp.dot(h_ref[...], _mxu_weights(wg_ref, wb_refs[:1]), preferred_element_type=F32)
    pp = jnp.dot(p_ref[...].astype(BF16), _mxu_weights(wp_ref, wb_refs[1:]), preferred_element_type=F32)
    o_ref[...] = x_ref[...] + jax.nn.sigmoid(z) * pp


def _ple(x, h, wg, p, wp, tm, tn):
    m, k = h.shape
    n = wg.shape[1]
    kp = p.shape[1]
    emit = wg.dtype != BF16
    out_shape = [jax.ShapeDtypeStruct((m, n), F32)]
    out_specs = [pl.BlockSpec((tm, tn), lambda i, j: (i, j))]
    if emit:
        assert m == tm
        out_shape += [jax.ShapeDtypeStruct(wg.shape, BF16), jax.ShapeDtypeStruct(wp.shape, BF16)]
        out_specs += [pl.BlockSpec((k, tn), lambda i, j: (0, j)), pl.BlockSpec((kp, tn), lambda i, j: (0, j))]
    return pl.pallas_call(
        _ple_kernel,
        out_shape=out_shape,
        grid=(m // tm, n // tn),
        in_specs=[pl.BlockSpec((tm, k), lambda i, j: (i, 0)),
                  pl.BlockSpec((k, tn), lambda i, j: (0, j)),
                  pl.BlockSpec((tm, kp), lambda i, j: (i, 0)),
                  pl.BlockSpec((kp, tn), lambda i, j: (0, j)),
                  pl.BlockSpec((tm, tn), lambda i, j: (i, j))],
        out_specs=out_specs,
        compiler_params=_params(2),
        name="ple",
    )(h, wg, p, wp, x)


def _gelu_erf(x):
    return 0.5 * x * (1.0 + lax.erf(x * math.sqrt(0.5)))


def _conv_taps(cb_ref, cw_ref, um2, um1, u):
    c = cb_ref[...] + cw_ref[0:1, :] * um2
    c = c + cw_ref[1:2, :] * um1
    return c + cw_ref[2:3, :] * u


def _convglu_prompt_kernel(h_ref, wg_ref, wv_ref, cwg_ref, cwv_ref, cbg_ref, cbv_ref,
                           a_ref, sg_ref, sv_ref, carry_g_ref, carry_v_ref, *, tiles_per_seq):
    i = pl.program_id(0)
    j = pl.program_id(1)
    tm = h_ref.shape[0]
    rows = min(tm, CONV_CHUNK)
    seq_start = (i % tiles_per_seq) == 0
    sub = 8
    row = lax.broadcasted_iota(jnp.int32, (sub, 1), 0)

    def half(h, w_ref, cw_ref, cb_ref, prev):
        u = jnp.dot(h, w_ref[...], preferred_element_type=F32)
        r1 = pltpu.roll(u, 1, axis=0)
        r2 = pltpu.roll(u, 2, axis=0)
        head1 = jnp.where(row == 0, prev[1:2, :], r1[:sub, :])
        head2 = jnp.where(row == 0, prev[0:1, :], jnp.where(row == 1, prev[1:2, :], r2[:sub, :]))
        um1 = jnp.concatenate([head1, r1[sub:, :]], axis=0)
        um2 = jnp.concatenate([head2, r2[sub:, :]], axis=0)
        return _conv_taps(cb_ref, cw_ref, um2, um1, u), u[rows - (CONV_W - 1):, :]

    prev_g = jnp.where(seq_start, 0.0, carry_g_ref[j])
    prev_v = jnp.where(seq_start, 0.0, carry_v_ref[j])
    for c in range(tm // rows):
        rs = slice(c * rows, (c + 1) * rows)
        h = h_ref[rs, :]
        cg, prev_g = half(h, wg_ref, cwg_ref, cbg_ref, prev_g)
        cv, prev_v = half(h, wv_ref, cwv_ref, cbv_ref, prev_v)
        a_ref[rs, :] = (_gelu_erf(cg) * cv).astype(a_ref.dtype)
    carry_g_ref[j] = prev_g
    carry_v_ref[j] = prev_v
    sg_ref[0] = prev_g
    sv_ref[0] = prev_v


def _convglu_prompt(h, w_gate, w_value, conv_w, conv_b, n_seq, tm):
    m, k = h.shape
    seq = m // n_seq
    tiles_per_seq = seq // tm
    nt = N_FF_TILES
    tn = FF_TILE
    a, sg, sv = pl.pallas_call(
        functools.partial(_convglu_prompt_kernel, tiles_per_seq=tiles_per_seq),
        out_shape=(jax.ShapeDtypeStruct((m, D_FF), BF16),
                   jax.ShapeDtypeStruct((m // tm, CONV_W - 1, D_FF), F32),
                   jax.ShapeDtypeStruct((m // tm, CONV_W - 1, D_FF), F32)),
        grid=(m // tm, nt),
        in_specs=[pl.BlockSpec((tm, k), lambda i, j: (i, 0)),
                  pl.BlockSpec((k, tn), lambda i, j: (0, j)),
                  pl.BlockSpec((k, tn), lambda i, j: (0, j)),
                  pl.BlockSpec((CONV_W, tn), lambda i, j: (0, j)),
                  pl.BlockSpec((CONV_W, tn), lambda i, j: (0, j + nt)),
                  pl.BlockSpec((1, tn), lambda i, j: (0, j)),
                  pl.BlockSpec((1, tn), lambda i, j: (0, j + nt))],
        out_specs=(pl.BlockSpec((tm, tn), lambda i, j: (i, j)),
                   pl.BlockSpec((1, CONV_W - 1, tn), lambda i, j: (i, 0, j)),
                   pl.BlockSpec((1, CONV_W - 1, tn), lambda i, j: (i, 0, j))),
        scratch_shapes=[pltpu.VMEM((nt, CONV_W - 1, tn), F32),
                        pltpu.VMEM((nt, CONV_W - 1, tn), F32)],
        compiler_params=_params(2),
        name="convglu_prompt",
    )(h, w_gate, w_value, conv_w, conv_w, conv_b, conv_b)
    tails = jnp.concatenate([sg, sv], axis=-1)
    return a, tails[tiles_per_seq - 1::tiles_per_seq]


def _convglu_sample_kernel(h_ref, wg_ref, wv_ref, cwg_ref, cwv_ref, cbg_ref, cbv_ref,
                           p0g_ref, p0v_ref, p1g_ref, p1v_ref, a_ref, ug_ref, uv_ref, wbg_ref, wbv_ref):
    def half(w_ref, wb_ref, cw_ref, cb_ref, p0_ref, p1_ref, u_ref):
        u = jnp.dot(h_ref[...], _mxu_weights(w_ref, (wb_ref,)), preferred_element_type=F32)
        u_ref[...] = u
        return _conv_taps(cb_ref, cw_ref, p0_ref[...], p1_ref[...], u)

    cg = half(wg_ref, wbg_ref, cwg_ref, cbg_ref, p0g_ref, p1g_ref, ug_ref)
    cv = half(wv_ref, wbv_ref, cwv_ref, cbv_ref, p0v_ref, p1v_ref, uv_ref)
    a_ref[...] = (_gelu_erf(cg) * cv).astype(a_ref.dtype)


def _convglu_sample(h, w_up, conv_w, conv_b, state_conv):
    m, k = h.shape
    nt = N_FF_TILES
    tn = FF_TILE
    prev = state_conv.reshape(m, (CONV_W - 1) * 2 * D_FF)
    col = lambda off: (lambda j: (0, j + off))
    a, ug, uv, w_gate, w_value = pl.pallas_call(
        _convglu_sample_kernel,
        out_shape=(jax.ShapeDtypeStruct((m, D_FF), BF16),
                   jax.ShapeDtypeStruct((m, D_FF), F32),
                   jax.ShapeDtypeStruct((m, D_FF), F32),
                   jax.ShapeDtypeStruct((k, D_FF), BF16),
                   jax.ShapeDtypeStruct((k, D_FF), BF16)),
        grid=(nt,),
        in_specs=[pl.BlockSpec((m, k), lambda j: (0, 0)),
                  pl.BlockSpec((k, tn), col(0)),
                  pl.BlockSpec((k, tn), col(nt)),
                  pl.BlockSpec((CONV_W, tn), col(0)),
                  pl.BlockSpec((CONV_W, tn), col(nt)),
                  pl.BlockSpec((1, tn), col(0)),
                  pl.BlockSpec((1, tn), col(nt)),
                  pl.BlockSpec((m, tn), col(0)),
                  pl.BlockSpec((m, tn), col(nt)),
                  pl.BlockSpec((m, tn), col(2 * nt)),
                  pl.BlockSpec((m, tn), col(3 * nt))],
        out_specs=(pl.BlockSpec((m, tn), col(0)),
                   pl.BlockSpec((m, tn), col(0)),
                   pl.BlockSpec((m, tn), col(0)),
                   pl.BlockSpec((k, tn), col(0)),
                   pl.BlockSpec((k, tn), col(0))),
        compiler_params=_params(1),
        name="convglu_sample",
    )(h, w_up, w_up, conv_w, conv_w, conv_b, conv_b, prev, prev, prev, prev)
    u = jnp.concatenate([ug, uv], axis=-1)
    conv_new = jnp.stack([state_conv[:, 1, :], u], axis=1)
    return a, conv_new, w_gate, w_value


def _segment_mean_square(x, seg_ones):
    sq = x * x
    hi = sq.astype(BF16)
    lo = (sq - hi.astype(F32)).astype(BF16)
    total = (jnp.dot(hi, seg_ones, preferred_element_type=F32)
             + jnp.dot(lo, seg_ones, preferred_element_type=F32))
    return total / HEAD_DIM


def _attn_prompt_kernel(q_ref, kc_ref, vc_ref, bucket_ref, seg_ones_ref, rb_ref, sinks_ref, gq_ref, gk_ref,
                        o_ref, kn_ref, bias_ref, kdup_ref, vdup_ref, *, blocks_per_seq):
    r = pl.program_id(0)
    blk = ATTN_BLOCK
    pair = 2 * HEAD_DIM
    n_bias_rows = N_ATTN_HEADS * blk
    seq_start = (r % blocks_per_seq) == 0

    @pl.when(r == 0)
    def _():
        bucket = bucket_ref[...]
        prev_cols = lax.broadcasted_iota(jnp.int32, bucket.shape, 1) < blk

        def per_head(h, carry):
            acc = jnp.full(bucket.shape, -jnp.inf, F32)
            for b in range(N_BUCKETS):
                acc = jnp.where(bucket == b, rb_ref[b, h], acc)
            row0 = pl.multiple_of(h * blk, blk)
            bias_ref[pl.ds(row0, blk), :] = acc
            bias_ref[pl.ds(n_bias_rows + row0, blk), :] = jnp.where(prev_cols, -jnp.inf, acc)
            return carry
        lax.fori_loop(0, N_ATTN_HEADS, per_head, 0)
        vdup_ref[:, :, pair:] = jnp.ones((N_KV_HEADS, 2 * blk, pair), BF16)

    @pl.when(seq_start)
    def _():
        kdup_ref[:, :blk, :] = jnp.zeros((N_KV_HEADS, blk, pair), BF16)
        vdup_ref[:, :blk, :pair] = jnp.zeros((N_KV_HEADS, blk, pair), BF16)

    @pl.when(jnp.logical_not(seq_start))
    def _():
        kdup_ref[:, :blk, :] = kdup_ref[:, blk:, :]
        vdup_ref[:, :blk, :pair] = vdup_ref[:, blk:, :pair]

    seg_ones = seg_ones_ref[...]
    low_half = lax.broadcasted_iota(jnp.int32, (blk, pair), 1) < HEAD_DIM

    def duplicate_half(x, odd):
        swapped = pltpu.roll(x, HEAD_DIM, axis=1)
        return jnp.where(low_half, swapped, x) if odd else jnp.where(low_half, x, swapped)

    kc = kc_ref[...]
    kn = kc * lax.rsqrt(_segment_mean_square(kc, seg_ones) + EPS) * gk_ref[...]
    kn_ref[...] = kn
    for j in range(N_KV_HEADS):
        col = slice((j // 2) * pair, (j // 2 + 1) * pair)
        kdup_ref[j, blk:, :] = duplicate_half(kn[:, col], j % 2).astype(BF16)
        vdup_ref[j, blk:, :pair] = duplicate_half(vc_ref[:, col], j % 2).astype(BF16)

    bias_base = jnp.where(seq_start, n_bias_rows, 0)
    chunk = 2 * pair
    heads_per_chunk = chunk // HEAD_DIM
    for j in range(N_KV_HEADS):
        heads = range(j * GQA_GROUP, (j + 1) * GQA_GROUP)
        lhs = []
        for c in range(j * GQA_GROUP // heads_per_chunk, (j + 1) * GQA_GROUP // heads_per_chunk):
            qc = q_ref[:, c * chunk:(c + 1) * chunk]
            qn = qc * lax.rsqrt(_segment_mean_square(qc, seg_ones) + EPS) * (gq_ref[...] * HEAD_DIM ** -0.5)
            for p in range(chunk // pair):
                qp = qn[:, p * pair:(p + 1) * pair]
                lhs.append(jnp.where(low_half, qp, 0.0).astype(BF16))
                lhs.append(jnp.where(low_half, 0.0, qp).astype(BF16))
        s = lax.dot_general(jnp.concatenate(lhs, axis=0), kdup_ref[j], (((1,), (1,)), ((), ())),
                            preferred_element_type=F32)
        exps, sink_terms = [], []
        for g, h in enumerate(heads):
            row0 = pl.multiple_of(bias_base + h * blk, blk)
            sg = s[g * blk:(g + 1) * blk, :] + bias_ref[pl.ds(row0, blk), :]
            sink = sinks_ref[h]
            m = jnp.maximum(jnp.max(sg, axis=-1, keepdims=True), sink)
            exps.append(jnp.exp(sg - m).astype(BF16))
            sink_terms.append(jnp.exp(sink - m))
        o = jnp.dot(jnp.concatenate(exps, axis=0), vdup_ref[j], preferred_element_type=F32)
        for g in range(0, GQA_GROUP, 2):
            even = o[g * blk:(g + 1) * blk, :]
            odd = o[(g + 1) * blk:(g + 2) * blk, :]
            num = jnp.where(low_half, even[:, :pair], odd[:, :pair])
            den = (jnp.where(low_half, even[:, pair:], odd[:, pair:])
                   + jnp.where(low_half, sink_terms[g], sink_terms[g + 1]))
            c0 = (j * GQA_GROUP + g) * HEAD_DIM
            o_ref[:, c0:c0 + pair] = (num / den).astype(o_ref.dtype)


def _attn_prompt(proj, bucket_tile, rel_bias, sinks, g_q, g_k, n_seq):
    m = proj.shape[0]
    blk = ATTN_BLOCK
    nb = m // blk
    blocks_per_seq = nb // n_seq
    lanes = 4 * HEAD_DIM
    seg = jnp.arange(lanes, dtype=jnp.int32) // HEAD_DIM
    seg_ones = (seg[:, None] == seg[None, :]).astype(BF16)
    tile4 = lambda g: jnp.tile(g.reshape(-1), lanes // HEAD_DIM).reshape(1, lanes)
    smem = pl.BlockSpec(memory_space=pltpu.SMEM)
    return pl.pallas_call(
        functools.partial(_attn_prompt_kernel, blocks_per_seq=blocks_per_seq),
        out_shape=(jax.ShapeDtypeStruct((m, ATTN_WIDTH), BF16),
                   jax.ShapeDtypeStruct((m, KV_WIDTH), F32)),
        grid=(nb,),
        in_specs=[pl.BlockSpec((blk, ATTN_WIDTH), lambda r: (r, COL_AQ // ATTN_WIDTH)),
                  pl.BlockSpec((blk, KV_WIDTH), lambda r: (r, COL_AK // KV_WIDTH)),
                  pl.BlockSpec((blk, KV_WIDTH), lambda r: (r, COL_AV // KV_WIDTH)),
                  pl.BlockSpec((blk, 2 * blk), lambda r: (0, 0)),
                  pl.BlockSpec((lanes, lanes), lambda r: (0, 0)),
                  smem, smem,
                  pl.BlockSpec((1, lanes), lambda r: (0, 0)),
                  pl.BlockSpec((1, lanes), lambda r: (0, 0))],
        out_specs=(pl.BlockSpec((blk, ATTN_WIDTH), lambda r: (r, 0)),
                   pl.BlockSpec((blk, KV_WIDTH), lambda r: (r, 0))),
        scratch_shapes=[pltpu.VMEM((2 * N_ATTN_HEADS * blk, 2 * blk), F32),
                        pltpu.VMEM((N_KV_HEADS, 2 * blk, 2 * HEAD_DIM), BF16),
                        pltpu.VMEM((N_KV_HEADS, 2 * blk, 4 * HEAD_DIM), BF16)],
        compiler_params=_params(1),
        name="attn_prompt",
    )(proj, proj, proj, bucket_tile, seg_ones, rel_bias, sinks, tile4(g_q), tile4(g_k))


def _attn_sample_kernel(q_ref, knew_ref, vnew_ref, ck_ref, cv_ref, bucket_ref, rbt_ref, sinks_ref,
                        gq_ref, gk_ref, o_ref, cko_ref, cvo_ref):
    bt = q_ref.shape[0]
    win = ck_ref.shape[1]
    bucket = bucket_ref[...]
    bias = jnp.full((N_ATTN_HEADS, bucket.shape[1]), -jnp.inf, F32)
    for b in range(N_BUCKETS):
        bias = jnp.where(bucket == b, rbt_ref[:, b:b + 1], bias)
    bias_c = bias[:, :win]
    bias_n = bias[:, win:win + 1]
    sink = sinks_ref[...]
    head_group = jnp.right_shift(lax.broadcasted_iota(jnp.int32, (N_ATTN_HEADS, HEAD_DIM), 0),
                                 int(math.log2(GQA_GROUP)))
    lane_group = jnp.right_shift(lax.broadcasted_iota(jnp.int32, (1, KV_WIDTH), 1),
                                 int(math.log2(HEAD_DIM)))
    gk = gk_ref[...]
    scale = HEAD_DIM ** -0.5

    for b in range(bt):
        k_row = knew_ref[b]
        k_sq = k_row * k_row
        inv = jnp.zeros_like(k_row)
        for j in range(N_KV_HEADS):
            ms = jnp.sum(jnp.where(lane_group == j, k_sq, 0.0), axis=-1, keepdims=True) / HEAD_DIM
            inv = jnp.where(lane_group == j, lax.rsqrt(ms + EPS), inv)
        kn_row = k_row * inv * gk
        v_row = vnew_ref[b]

        qn = _rms(q_ref[b], gq_ref[...])
        q_bd = jnp.concatenate([jnp.where(head_group == j, qn, 0.0) for j in range(N_KV_HEADS)], axis=1)
        kc = ck_ref[b]
        vc = cv_ref[b]
        s_c = lax.dot_general(q_bd.astype(BF16), kc.astype(BF16), (((1,), (1,)), ((), ())),
                              preferred_element_type=F32) * scale + bias_c
        s_n = jnp.sum(q_bd * kn_row, axis=-1, keepdims=True) * scale + bias_n
        m = jnp.maximum(jnp.maximum(jnp.max(s_c, axis=-1, keepdims=True), s_n), sink)
        e_c = jnp.exp(s_c - m)
        e_n = jnp.exp(s_n - m)
        denom = jnp.sum(e_c, axis=-1, keepdims=True) + e_n + jnp.exp(sink - m)
        o_full = jnp.dot((e_c / denom).astype(BF16), vc.astype(BF16), preferred_element_type=F32)
        o_full = o_full + (e_n / denom) * v_row
        o = jnp.zeros((N_ATTN_HEADS, HEAD_DIM), F32)
        for j in range(N_KV_HEADS):
            o = jnp.where(head_group == j, o_full[:, j * HEAD_DIM:(j + 1) * HEAD_DIM], o)
        o_ref[b] = o.astype(o_ref.dtype)

        cko_ref[b, pl.ds(0, win - 1), :] = ck_ref[b, pl.ds(1, win - 1), :]
        cko_ref[b, pl.ds(win - 1, 1), :] = kn_row
        cvo_ref[b, pl.ds(0, win - 1), :] = cv_ref[b, pl.ds(1, win - 1), :]
        cvo_ref[b, pl.ds(win - 1, 1), :] = v_row


def _attn_sample(q, k_new, v_new, cache_k, cache_v, bucket_row, rel_bias_t, sinks_col, g_q, g_k_row, bt):
    nb = q.shape[0]
    win = cache_k.shape[1]
    full = lambda shape: pl.BlockSpec(shape, lambda i: (0,) * len(shape))
    return pl.pallas_call(
        _attn_sample_kernel,
        out_shape=(jax.ShapeDtypeStruct((nb, N_ATTN_HEADS, HEAD_DIM), BF16),
                   jax.ShapeDtypeStruct(cache_k.shape, F32),
                   jax.ShapeDtypeStruct(cache_v.shape, F32)),
        grid=(nb // bt,),
        in_specs=[pl.BlockSpec((bt, N_ATTN_HEADS, HEAD_DIM), lambda i: (i, 0, 0)),
                  pl.BlockSpec((bt, 1, KV_WIDTH), lambda i: (i, 0, 0)),
                  pl.BlockSpec((bt, 1, KV_WIDTH), lambda i: (i, 0, 0)),
                  pl.BlockSpec((bt, win, KV_WIDTH), lambda i: (i, 0, 0)),
                  pl.BlockSpec((bt, win, KV_WIDTH), lambda i: (i, 0, 0)),
                  full(bucket_row.shape), full(rel_bias_t.shape), full(sinks_col.shape),
                  full(g_q.shape), full(g_k_row.shape)],
        out_specs=(pl.BlockSpec((bt, N_ATTN_HEADS, HEAD_DIM), lambda i: (i, 0, 0)),
                   pl.BlockSpec((bt, win, KV_WIDTH), lambda i: (i, 0, 0)),
                   pl.BlockSpec((bt, win, KV_WIDTH), lambda i: (i, 0, 0))),
        compiler_params=_params(1),
        name="attn_sample",
    )(q, k_new, v_new, cache_k, cache_v, bucket_row, rel_bias_t, sinks_col, g_q, g_k_row)


def _ret_prompt_kernel(q_ref, k_ref, v_ref, gate_ref, cos_ref, sin_ref, dmask_ref, qdec_ref, kdec_ref,
                       cdec_ref, o_ref, state_ref, s_ref):
    c = pl.program_id(1)

    @pl.when(c == 0)
    def _():
        s_ref[...] = jnp.zeros_like(s_ref)

    cosf = cos_ref[...]
    sinf = sin_ref[...]
    half = RET_KDIM // 2
    nt = (((1,), (1,)), ((), ()))
    tn = (((0,), (0,)), ((), ()))

    def rotary(x):
        return x * cosf + pltpu.roll(x, half, axis=1) * sinf

    for h in range(N_RET_HEADS):
        kd = slice(h * RET_KDIM, (h + 1) * RET_KDIM)
        vd = slice(h * RET_VDIM, (h + 1) * RET_VDIM)
        q = rotary(q_ref[:, kd])
        k = rotary(k_ref[:, kd]) * RET_KDIM ** -0.5
        v = v_ref[:, vd].astype(BF16)
        s0 = s_ref[h]
        scores = lax.dot_general(q.astype(BF16), k.astype(BF16), nt,
                                 preferred_element_type=F32) * dmask_ref[h]
        o_intra = jnp.dot(scores.astype(BF16), v, preferred_element_type=F32)
        o_inter = jnp.dot((q * qdec_ref[h]).astype(BF16), s0.astype(BF16), preferred_element_type=F32)
        s_ref[h] = cdec_ref[h] * s0 + lax.dot_general((k * kdec_ref[h]).astype(BF16), v, tn,
                                                       preferred_element_type=F32)
        ro = o_intra + o_inter
        ro = ro * lax.rsqrt(jnp.mean(ro * ro, axis=-1, keepdims=True) + EPS)
        o_ref[:, vd] = (jax.nn.silu(gate_ref[:, vd]) * ro).astype(o_ref.dtype)

    @pl.when(c == pl.num_programs(1) - 1)
    def _():
        state_ref[0] = s_ref[...]


def _ret_prompt(proj, cosf, sinf, dmask, qdec, kdec, cdec, n_seq):
    m = proj.shape[0]
    ch = RET_CHUNK
    nc = m // n_seq // ch
    row = lambda b, c: b * nc + c
    const3 = pl.BlockSpec((N_RET_HEADS, ch, ch), lambda b, c: (0, 0, 0))
    return pl.pallas_call(
        _ret_prompt_kernel,
        out_shape=(jax.ShapeDtypeStruct((m, RET_V_WIDTH), BF16),
                   jax.ShapeDtypeStruct((n_seq, N_RET_HEADS, RET_KDIM, RET_VDIM), F32)),
        grid=(n_seq, nc),
        in_specs=[pl.BlockSpec((ch, RET_QK_WIDTH), lambda b, c: (row(b, c), COL_RQ // RET_QK_WIDTH)),
                  pl.BlockSpec((ch, RET_QK_WIDTH), lambda b, c: (row(b, c), COL_RK // RET_QK_WIDTH)),
                  pl.BlockSpec((ch, RET_V_WIDTH), lambda b, c: (row(b, c), COL_RV // RET_V_WIDTH)),
                  pl.BlockSpec((ch, RET_V_WIDTH), lambda b, c: (row(b, c), COL_RG // RET_V_WIDTH)),
                  pl.BlockSpec((ch, RET_KDIM), lambda b, c: (c, 0)),
                  pl.BlockSpec((ch, RET_KDIM), lambda b, c: (c, 0)),
                  const3, const3, const3,
                  pl.BlockSpec(memory_space=pltpu.SMEM)],
        out_specs=(pl.BlockSpec((ch, RET_V_WIDTH), lambda b, c: (row(b, c), 0)),
                   pl.BlockSpec((1, N_RET_HEADS, RET_KDIM, RET_VDIM), lambda b, c: (b, 0, 0, 0))),
        scratch_shapes=[pltpu.VMEM((N_RET_HEADS, RET_KDIM, RET_VDIM), F32)],
        compiler_params=_params(2),
        name="ret_prompt",
    )(proj, proj, proj, proj, cosf, sinf, dmask, qdec, kdec, cdec)


def _ret_sample_kernel(qt_ref, kt_ref, cos_ref, sin_ref, v_ref, gate_ref, s_ref, qdec_ref, kdec_ref,
                       cdec_ref, o_ref, so_ref):
    bt = s_ref.shape[0]
    half = RET_KDIM // 2
    cos = cos_ref[...]
    sin = sin_ref[...]

    def rotary(x):
        x1, x2 = x[:half, :], x[half:, :]
        return jnp.concatenate([x1 * cos - x2 * sin, x2 * cos + x1 * sin], axis=0)

    for h in range(N_RET_HEADS):
        vd = slice(h * RET_VDIM, (h + 1) * RET_VDIM)
        q_all = rotary(qt_ref[0, h])
        k_all = rotary(kt_ref[0, h]) * RET_KDIM ** -0.5
        for b in range(bt):
            q = q_all[:, b:b + 1]
            k = k_all[:, b:b + 1]
            v = v_ref[0, b:b + 1, vd]
            s0 = s_ref[b, h]
            o_intra = jnp.sum(q * k, axis=0, keepdims=True) * v
            o_inter = jnp.sum((q * qdec_ref[h]) * s0, axis=0, keepdims=True)
            so_ref[b, h] = cdec_ref[h] * s0 + (k * kdec_ref[h]) * v
            ro = o_intra + o_inter
            ro = ro * lax.rsqrt(jnp.mean(ro * ro, axis=-1, keepdims=True) + EPS)
            o_ref[0, b:b + 1, vd] = (jax.nn.silu(gate_ref[0, b:b + 1, vd]) * ro).astype(o_ref.dtype)


def _ret_sample(qt, kt, cos_col, sin_col, v, gate, state, qdec, kdec, cdec):
    nbt, _, _, bt = qt.shape
    smem = pl.BlockSpec(memory_space=pltpu.SMEM)
    st_spec = pl.BlockSpec((bt, N_RET_HEADS, RET_KDIM, RET_VDIM), lambda i: (i, 0, 0, 0))
    qk_spec = pl.BlockSpec((1, N_RET_HEADS, RET_KDIM, bt), lambda i: (i, 0, 0, 0))
    row_spec = pl.BlockSpec((1, bt, RET_V_WIDTH), lambda i: (i, 0, 0))
    col_spec = pl.BlockSpec(cos_col.shape, lambda i: (0, 0))
    return pl.pallas_call(
        _ret_sample_kernel,
        out_shape=(jax.ShapeDtypeStruct((nbt, bt, RET_V_WIDTH), BF16),
                   jax.ShapeDtypeStruct(state.shape, F32)),
        grid=(nbt,),
        in_specs=[qk_spec, qk_spec, col_spec, col_spec, row_spec, row_spec, st_spec, smem, smem, smem],
        out_specs=(row_spec, st_spec),
        compiler_params=_params(1),
        name="ret_sample",
    )(qt, kt, cos_col, sin_col, v, gate, state, qdec, kdec, cdec)


def _t5_bucket(dist):
    n = jnp.maximum(dist, 0)
    max_exact = N_BUCKETS // 2
    nf = jnp.maximum(n, 1).astype(F32)
    large = max_exact + (jnp.log(nf / max_exact) / math.log(MAX_DISTANCE / max_exact)
                         * (N_BUCKETS - max_exact)).astype(jnp.int32)
    return jnp.where(n < max_exact, n, jnp.minimum(large, N_BUCKETS - 1))


def _rope_tables(pos):
    half = RET_KDIM // 2
    inv = ROPE_BASE ** (-jnp.arange(half, dtype=F32) / half)
    ang = pos.astype(F32)[:, None] * inv[None]
    return jnp.cos(ang), jnp.sin(ang)


def _layer(xp, xs, pp, ps, cache_k, cache_v, state_ret, state_conv, rel_bias, lp, n_seq):
    (g_mix, w_in, g_q, g_k, sinks, w_out, g_ffn, w_up, conv_w, conv_b, w_down,
     g_ple, w_ple_gate, w_ple_proj) = lp
    mp = xp.shape[0]
    ms = xs.shape[0]
    seq = mp // n_seq
    win = cache_k.shape[1]

    row = lambda a: a.reshape(1, -1)

    log_decay = jnp.log(1.0 - 2.0 ** (-5.0 - jnp.arange(N_RET_HEADS, dtype=F32)))
    idx = jnp.arange(RET_CHUNK, dtype=F32)
    diff = idx[:, None] - idx[None, :]
    dmask = jnp.where(diff[None] >= 0, jnp.exp(diff[None] * log_decay[:, None, None]), 0.0)
    q_dec = jnp.exp((idx + 1.0)[:, None] * log_decay[None])
    k_dec = jnp.exp((RET_CHUNK - 1.0 - idx)[:, None] * log_decay[None])
    qdec_t = jnp.broadcast_to(q_dec.T[:, :, None], (N_RET_HEADS, RET_CHUNK, RET_KDIM))
    kdec_t = jnp.broadcast_to(k_dec.T[:, :, None], (N_RET_HEADS, RET_CHUNK, RET_KDIM))
    cdec = jnp.exp(RET_CHUNK * log_decay)
    cos_p, sin_p = _rope_tables(jnp.arange(seq, dtype=jnp.int32))
    cosf = jnp.concatenate([cos_p, cos_p], axis=1)
    sinf = jnp.concatenate([-sin_p, sin_p], axis=1)
    one = jnp.arange(1, dtype=F32)
    qdec_s = jnp.exp((one + 1.0)[:, None] * log_decay[None])[0]
    kdec_s = jnp.exp((1.0 - 1.0 - one)[:, None] * log_decay[None])[0]
    cdec_s = jnp.exp(1.0 * log_decay)
    cos_s, sin_s = _rope_tables(PAST_LEN + jnp.arange(1, dtype=jnp.int32))
    cos_col, sin_col = cos_s.reshape(-1, 1), sin_s.reshape(-1, 1)

    qi = jnp.arange(ATTN_BLOCK, dtype=jnp.int32)
    ki = jnp.arange(2 * ATTN_BLOCK, dtype=jnp.int32) - ATTN_BLOCK
    dist = qi[:, None] - ki[None, :]
    bucket_tile = jnp.where((dist >= 0) & (dist <= WINDOW), _t5_bucket(dist), -1)
    dist_s = win - jnp.arange(2 * ATTN_BLOCK, dtype=jnp.int32)
    bucket_row = jnp.where((dist_s >= 0) & (dist_s <= WINDOW), _t5_bucket(dist_s), -1).reshape(1, -1)

    tmp, tms = _row_tile(mp), _row_tile(ms)
    proj_s, w_in_b = _matmul(_norm(xs, row(g_mix), tms), w_in, tms, COL_TILE, IN_SRC_BLOCKS)
    proj_p, = _matmul(_norm(xp, row(g_mix), NORM_TILE), w_in_b, tmp, COL_TILE)

    attn_p, kn_p = _attn_prompt(proj_p, bucket_tile, rel_bias, sinks, row(g_q), row(g_k), n_seq)
    ret_p, ret_state_p = _ret_prompt(proj_p, cosf, sinf, dmask, qdec_t, kdec_t, cdec, n_seq)

    bt_a = 8
    attn_s, ck_new, cv_new = _attn_sample(
        proj_s[:, COL_AQ:COL_AQ + ATTN_WIDTH].reshape(ms, N_ATTN_HEADS, HEAD_DIM),
        proj_s[:, COL_AK:COL_AK + KV_WIDTH].reshape(ms, 1, KV_WIDTH),
        proj_s[:, COL_AV:COL_AV + KV_WIDTH].reshape(ms, 1, KV_WIDTH),
        cache_k.reshape(ms, win, KV_WIDTH), cache_v.reshape(ms, win, KV_WIDTH),
        bucket_row, rel_bias.T, sinks.reshape(-1, 1), row(g_q), jnp.tile(g_k, N_KV_HEADS).reshape(1, -1), bt_a)
    attn_s = attn_s.reshape(ms, ATTN_WIDTH)

    bt_r = 4
    to_cols = lambda a: a.reshape(ms // bt_r, bt_r, N_RET_HEADS, RET_KDIM).transpose(0, 2, 3, 1)
    ret_s, ret_state_s = _ret_sample(
        to_cols(proj_s[:, COL_RQ:COL_RQ + RET_QK_WIDTH]), to_cols(proj_s[:, COL_RK:COL_RK + RET_QK_WIDTH]),
        cos_col, sin_col,
        proj_s[:, COL_RV:COL_RV + RET_V_WIDTH].reshape(ms // bt_r, bt_r, RET_V_WIDTH),
        proj_s[:, COL_RG:COL_RG + RET_V_WIDTH].reshape(ms // bt_r, bt_r, RET_V_WIDTH),
        state_ret, qdec_s, kdec_s, cdec_s)
    ret_s = ret_s.reshape(ms, RET_V_WIDTH)

    xs, w_out_b = _out_proj(attn_s, ret_s, w_out, xs, tms, COL_TILE)
    xp, = _out_proj(attn_p, ret_p, w_out_b, xp, tmp, COL_TILE)

    act_s, conv_s, w_gate_b, w_value_b = _convglu_sample(_norm(xs, row(g_ffn), tms), w_up, conv_w, row(conv_b),
                                                         state_conv)
    act_p, conv_p = _convglu_prompt(_norm(xp, row(g_ffn), NORM_TILE), w_gate_b, w_value_b, conv_w, row(conv_b),
                                    n_seq, UP_ROW_TILE)
    for part in range(DOWN_K_PARTS):
        xs, w_down_b = _down_proj(act_s, w_down, xs, tms, COL_TILE, DOWN_K_PARTS, part)
        xp, = _down_proj(act_p, w_down_b, xp, tmp, COL_TILE, DOWN_K_PARTS, part)

    xs, w_pgate_b, w_pproj_b = _ple(xs, _norm(xs, row(g_ple), tms), w_ple_gate, ps, w_ple_proj, tms, COL_TILE)
    xp, = _ple(xp, _norm(xp, row(g_ple), NORM_TILE), w_pgate_b, pp, w_pproj_b, tmp, COL_TILE)

    kp_new = kn_p.reshape(n_seq, seq, N_KV_HEADS, HEAD_DIM)[:, -WINDOW:]
    vp_new = proj_p[:, COL_AV:COL_AV + KV_WIDTH].reshape(n_seq, seq, N_KV_HEADS, HEAD_DIM)[:, -WINDOW:]
    ks_new = ck_new.reshape(ms, win, N_KV_HEADS, HEAD_DIM)
    vs_new = cv_new.reshape(ms, win, N_KV_HEADS, HEAD_DIM)
    return xp, xs, kp_new, vp_new, ret_state_p, conv_p, ks_new, vs_new, ret_state_s, conv_s


def kernel(x_prompt, x_sample, p_prompt, p_sample, cache_win_k, cache_win_v, state_ret, state_conv, rel_bias, g_mix, w_in, g_q, g_k, sinks, w_out, g_ffn, w_up, conv_w, conv_b, w_down, g_ple, w_ple_gate, w_ple_proj):
    depth = g_mix.shape[0]
    n_seq, seq, d = x_prompt.shape
    nb, dec_seq, _ = x_sample.shape
    assert dec_seq == 1 and seq % 512 == 0 and d == D_MODEL
    xp = x_prompt.reshape(n_seq * seq, d)
    xs = x_sample.reshape(nb, d)
    outs = [[] for _ in range(8)]
    for l in range(depth):
        lp = (g_mix[l], w_in[l], g_q[l], g_k[l], sinks[l], w_out[l], g_ffn[l], w_up[l], conv_w[l],
              conv_b[l], w_down[l], g_ple[l], w_ple_gate[l], w_ple_proj[l])
        res = _layer(xp, xs, p_prompt[l].reshape(n_seq * seq, -1), p_sample[l].reshape(nb, -1),
                     cache_win_k[l], cache_win_v[l], state_ret[l], state_conv[l], rel_bias, lp, n_seq)
        xp, xs = res[0], res[1]
        for o, r in zip(outs, res[2:]):
            o.append(r)
    stacked = [jnp.stack(o) for o in outs]
    return (xp.reshape(n_seq, seq, d), xs.reshape(nb, 1, d), *stacked)
```

```python
import functools
import math

import jax
import jax.numpy as jnp
from jax import lax
from jax.experimental import pallas as pl
from jax.experimental.pallas import tpu as pltpu

F32 = jnp.float32
BF16 = jnp.bfloat16

D_MODEL = 4096
HEAD_DIM = 64
N_ATTN_HEADS = 32
N_KV_HEADS = 4
GQA_GROUP = N_ATTN_HEADS // N_KV_HEADS
WINDOW = 128
ATTN_BLOCK = 128
N_BUCKETS = 32
MAX_DISTANCE = 128
N_RET_HEADS = 8
RET_KDIM = 128
RET_VDIM = 256
RET_CHUNK = 128
ROPE_BASE = 10000.0
D_FF = 11008
CONV_W = 3
EPS = 1e-6

ATTN_WIDTH = N_ATTN_HEADS * HEAD_DIM
KV_WIDTH = N_KV_HEADS * HEAD_DIM
RET_QK_WIDTH = N_RET_HEADS * RET_KDIM
RET_V_WIDTH = N_RET_HEADS * RET_VDIM
IN_SIZES = (ATTN_WIDTH, KV_WIDTH, KV_WIDTH, RET_QK_WIDTH, RET_QK_WIDTH, RET_V_WIDTH, RET_V_WIDTH)
IN_WIDTH = sum(IN_SIZES)
IN_SPLITS = tuple(sum(IN_SIZES[:n + 1]) for n in range(len(IN_SIZES) - 1))
PAST_LEN = 8192

COL_AQ = 0
COL_RV = ATTN_WIDTH
COL_RG = COL_RV + RET_V_WIDTH
COL_RQ = COL_RG + RET_V_WIDTH
COL_RK = COL_RQ + RET_QK_WIDTH
COL_AK = COL_RK + RET_QK_WIDTH
COL_AV = COL_AK + KV_WIDTH

VMEM_LIMIT_BYTES = 56 * 1024 * 1024
FF_TILE = 256
N_FF_TILES = D_FF // FF_TILE
NORM_ROWS = 32
NORM_TILE = 256
STAT_LANES = 128
IN_ROW_TILE = 2048
ROW_TILE = 1024
COL_TILE = 512
UP_ROW_TILE = 2048
CONV_CHUNK = 1024
DOWN_K_PARTS = 2


def _row_tile(m):
    return min(m, ROW_TILE)


def _in_src_blocks():
    start = dict(zip(("aq", "ak", "av", "rq", "rk", "rv", "rg"), (0,) + IN_SPLITS))
    order = (("aq", ATTN_WIDTH), ("rv", RET_V_WIDTH), ("rg", RET_V_WIDTH), ("rq", RET_QK_WIDTH),
             ("rk", RET_QK_WIDTH), ("ak", KV_WIDTH), ("av", KV_WIDTH))
    cols = [c for name, width in order for c in range(start[name], start[name] + width, KV_WIDTH)]
    per_tile = COL_TILE // KV_WIDTH
    blocks = []
    for b in range(0, len(cols), per_tile):
        group = cols[b:b + per_tile]
        assert group[0] % COL_TILE == 0 and all(c == group[0] + u * KV_WIDTH for u, c in enumerate(group))
        blocks.append(group[0] // COL_TILE)
    return tuple(blocks)


IN_SRC_BLOCKS = _in_src_blocks()


def _params(n_axes):
    return pltpu.CompilerParams(dimension_semantics=("arbitrary",) * n_axes,
                                vmem_limit_bytes=VMEM_LIMIT_BYTES)


def _rms(x, g):
    y = x * lax.rsqrt(jnp.mean(x * x, axis=-1, keepdims=True) + EPS)
    return y * g


def _norm_rows_to_bf16(x_ref, g_ref, h_ref):
    def body(c, carry):
        r0 = pl.multiple_of(c * NORM_ROWS, NORM_ROWS)
        h_ref[pl.ds(r0, NORM_ROWS), :] = _rms(x_ref[pl.ds(r0, NORM_ROWS), :], g_ref[...]).astype(h_ref.dtype)
        return carry
    lax.fori_loop(0, x_ref.shape[0] // NORM_ROWS, body, 0)


def _norm(x, g, tr):
    m, k = x.shape
    return pl.pallas_call(
        _norm_rows_to_bf16,
        out_shape=jax.ShapeDtypeStruct((m, k), BF16),
        grid=(m // tr,),
        in_specs=[pl.BlockSpec((tr, k), lambda i: (i, 0)),
                  pl.BlockSpec((1, k), lambda i: (0, 0))],
        out_specs=pl.BlockSpec((tr, k), lambda i: (i, 0)),
        compiler_params=_params(1),
        name="rmsnorm",
    )(x, g)


def _mxu_weights(w_ref, wb_ref):
    if not wb_ref:
        return w_ref[...]
    wb = w_ref[...].astype(BF16)
    wb_ref[0][...] = wb
    return wb


def _mm_kernel(h_ref, w_ref, o_ref, *wb_ref):
    o_ref[...] = jnp.dot(h_ref[...], _mxu_weights(w_ref, wb_ref), preferred_element_type=F32)


def _matmul(h, w, tm, tn, src_blocks=None):
    m, k = h.shape
    n = w.shape[1]
    emit = w.dtype != BF16
    if emit:
        w_map = lambda i, j: (0, sum(jnp.where(j == d, s, 0) for d, s in enumerate(src_blocks)))
    else:
        w_map = lambda i, j: (0, j)
    out_shape = [jax.ShapeDtypeStruct((m, n), F32)]
    out_specs = [pl.BlockSpec((tm, tn), lambda i, j: (i, j))]
    if emit:
        assert m == tm
        out_shape.append(jax.ShapeDtypeStruct((k, n), BF16))
        out_specs.append(pl.BlockSpec((k, tn), lambda i, j: (0, j)))
    return pl.pallas_call(
        _mm_kernel,
        out_shape=out_shape,
        grid=(m // tm, n // tn),
        in_specs=[pl.BlockSpec((tm, k), lambda i, j: (i, 0)),
                  pl.BlockSpec((k, tn), w_map)],
        out_specs=out_specs,
        compiler_params=_params(2),
        name="in_proj",
    )(h, w)


def _row_scale(ssq_ref, width, n_cols):
    r = lax.rsqrt(ssq_ref[...] / width + EPS)
    return jnp.concatenate([r] * (n_cols // STAT_LANES), axis=1)


def _residual_mm_kernel(*refs, n_lhs, emit_weights, norm_stats):
    lhs_refs = refs[:n_lhs]
    w_ref, x_ref = refs[n_lhs], refs[n_lhs + 1]
    rest = list(refs[n_lhs + 2:])
    g_ref = rest.pop(0) if norm_stats else None
    o_ref = rest.pop(0)
    wb_ref = (rest.pop(0),) if emit_weights else ()
    lhs = jnp.concatenate([r[...] for r in lhs_refs], axis=1) if n_lhs > 1 else lhs_refs[0][...]
    o = x_ref[...] + jnp.dot(lhs, _mxu_weights(w_ref, wb_ref), preferred_element_type=F32)
    o_ref[...] = o
    if norm_stats:
        hb_ref, ssq_ref = rest
        hb_ref[...] = (o * g_ref[...]).astype(hb_ref.dtype)
        part = jnp.broadcast_to(jnp.sum(o * o, axis=-1, keepdims=True), ssq_ref.shape)
        j = pl.program_id(1)

        @pl.when(j == 0)
        def _():
            ssq_ref[...] = part

        @pl.when(j > 0)
        def _():
            ssq_ref[...] += part


def _residual_matmul(lhs, lhs_specs, w, w_spec, wb_shape, x, g_next, tm, tn, name):
    m, n = x.shape
    emit = w.dtype != BF16
    stats = g_next is not None
    tile = pl.BlockSpec((tm, tn), lambda i, j: (i, j))
    operands = [*lhs, w, x]
    in_specs = [*lhs_specs, w_spec, tile]
    out_shape = [jax.ShapeDtypeStruct((m, n), F32)]
    out_specs = [tile]
    if stats:
        operands.append(g_next)
        in_specs.append(pl.BlockSpec((1, tn), lambda i, j: (0, j)))
    if emit:
        assert m == tm
        out_shape.append(jax.ShapeDtypeStruct(wb_shape, BF16))
        out_specs.append(pl.BlockSpec((wb_shape[0], tn), lambda i, j: (0, j)))
    if stats:
        out_shape += [jax.ShapeDtypeStruct((m, n), BF16), jax.ShapeDtypeStruct((m, STAT_LANES), F32)]
        out_specs += [tile, pl.BlockSpec((tm, STAT_LANES), lambda i, j: (i, 0))]
    return pl.pallas_call(
        functools.partial(_residual_mm_kernel, n_lhs=len(lhs), emit_weights=emit, norm_stats=stats),
        out_shape=out_shape,
        grid=(m // tm, n // tn),
        in_specs=in_specs,
        out_specs=out_specs,
        compiler_params=_params(2),
        name=name,
    )(*operands)


def _out_proj(a, r, w, x, tm, tn, g_next=None):
    ka, kr = a.shape[1], r.shape[1]
    lhs_specs = [pl.BlockSpec((tm, ka), lambda i, j: (i, 0)), pl.BlockSpec((tm, kr), lambda i, j: (i, 0))]
    w_spec = pl.BlockSpec((ka + kr, tn), lambda i, j: (0, j))
    return _residual_matmul([a, r], lhs_specs, w, w_spec, w.shape, x, g_next, tm, tn, "out_proj")


def _down_proj(a, w, x, tm, tn, k_parts, part, g_next=None):
    k = a.shape[1] // k_parts
    w_part = part if w.dtype != BF16 else 0
    lhs_specs = [pl.BlockSpec((tm, k), lambda i, j: (i, part))]
    w_spec = pl.BlockSpec((k, tn), lambda i, j: (w_part, j))
    return _residual_matmul([a], lhs_specs, w, w_spec, (k, w.shape[1]), x, g_next, tm, tn, "down_proj")


def _ple_kernel(*refs, emit_weights, row_scaled):
    h_ref, wg_ref, p_ref, wp_ref, x_ref = refs[:5]
    rest = list(refs[5:])
    ssq_ref = rest.pop(0) if row_scaled else None
    o_ref = rest.pop(0)
    wb_refs = tuple(rest) if emit_weights else ()
    z = jnp.dot(h_ref[...], _mxu_weights(wg_ref, wb_refs[:1]), preferred_element_type=F32)
    if row_scaled:
        z = z * _row_scale(ssq_ref, h_ref.shape[1], z.shape[1])
    pp = jnp.dot(p_ref[...].astype(BF16), _mxu_weights(wp_ref, wb_refs[1:]), preferred_element_type=F32)
    o_ref[...] = x_ref[...] + jax.nn.sigmoid(z) * pp


def _ple(x, h, ssq, wg, p, wp, tm, tn):
    m, k = h.shape
    n = wg.shape[1]
    kp = p.shape[1]
    emit = wg.dtype != BF16
    scaled = ssq is not None
    operands = [h, wg, p, wp, x]
    in_specs = [pl.BlockSpec((tm, k), lambda i, j: (i, 0)),
                pl.BlockSpec((k, tn), lambda i, j: (0, j)),
                pl.BlockSpec((tm, kp), lambda i, j: (i, 0)),
                pl.BlockSpec((kp, tn), lambda i, j: (0, j)),
                pl.BlockSpec((tm, tn), lambda i, j: (i, j))]
    if scaled:
        operands.append(ssq)
        in_specs.append(pl.BlockSpec((tm, STAT_LANES), lambda i, j: (i, 0)))
    out_shape = [jax.ShapeDtypeStruct((m, n), F32)]
    out_specs = [pl.BlockSpec((tm, tn), lambda i, j: (i, j))]
    if emit:
        assert m == tm
        out_shape += [jax.ShapeDtypeStruct(wg.shape, BF16), jax.ShapeDtypeStruct(wp.shape, BF16)]
        out_specs += [pl.BlockSpec((k, tn), lambda i, j: (0, j)), pl.BlockSpec((kp, tn), lambda i, j: (0, j))]
    return pl.pallas_call(
        functools.partial(_ple_kernel, emit_weights=emit, row_scaled=scaled),
        out_shape=out_shape,
        grid=(m // tm, n // tn),
        in_specs=in_specs,
        out_specs=out_specs,
        compiler_params=_params(2),
        name="ple",
    )(*operands)


def _gelu_erf(x):
    return 0.5 * x * (1.0 + lax.erf(x * math.sqrt(0.5)))


def _conv_taps(cb_ref, cw_ref, um2, um1, u):
    c = cb_ref[...] + cw_ref[0:1, :] * um2
    c = c + cw_ref[1:2, :] * um1
    return c + cw_ref[2:3, :] * u


def _convglu_prompt_kernel(h_ref, ssq_ref, wg_ref, wv_ref, cwg_ref, cwv_ref, cbg_ref, cbv_ref,
                           a_ref, sg_ref, sv_ref, carry_g_ref, carry_v_ref, *, tiles_per_seq):
    i = pl.program_id(0)
    j = pl.program_id(1)
    tm = h_ref.shape[0]
    rows = min(tm, CONV_CHUNK)
    seq_start = (i % tiles_per_seq) == 0
    sub = 8
    row = lax.broadcasted_iota(jnp.int32, (sub, 1), 0)

    def half(h, scale, w_ref, cw_ref, cb_ref, prev):
        u = jnp.dot(h, w_ref[...], preferred_element_type=F32) * scale
        r1 = pltpu.roll(u, 1, axis=0)
        r2 = pltpu.roll(u, 2, axis=0)
        head1 = jnp.where(row == 0, prev[1:2, :], r1[:sub, :])
        head2 = jnp.where(row == 0, prev[0:1, :], jnp.where(row == 1, prev[1:2, :], r2[:sub, :]))
        um1 = jnp.concatenate([head1, r1[sub:, :]], axis=0)
        um2 = jnp.concatenate([head2, r2[sub:, :]], axis=0)
        return _conv_taps(cb_ref, cw_ref, um2, um1, u), u[rows - (CONV_W - 1):, :]

    prev_g = jnp.where(seq_start, 0.0, carry_g_ref[j])
    prev_v = jnp.where(seq_start, 0.0, carry_v_ref[j])
    for c in range(tm // rows):
        rs = slice(c * rows, (c + 1) * rows)
        h = h_ref[rs, :]
        scale = _row_scale(ssq_ref.at[rs, :], h_ref.shape[1], wg_ref.shape[1])
        cg, prev_g = half(h, scale, wg_ref, cwg_ref, cbg_ref, prev_g)
        cv, prev_v = half(h, scale, wv_ref, cwv_ref, cbv_ref, prev_v)
        a_ref[rs, :] = (_gelu_erf(cg) * cv).astype(a_ref.dtype)
    carry_g_ref[j] = prev_g
    carry_v_ref[j] = prev_v
    sg_ref[0] = prev_g
    sv_ref[0] = prev_v


def _convglu_prompt(h, ssq, w_gate, w_value, conv_w, conv_b, n_seq, tm):
    m, k = h.shape
    seq = m // n_seq
    tiles_per_seq = seq // tm
    nt = N_FF_TILES
    tn = FF_TILE
    a, sg, sv = pl.pallas_call(
        functools.partial(_convglu_prompt_kernel, tiles_per_seq=tiles_per_seq),
        out_shape=(jax.ShapeDtypeStruct((m, D_FF), BF16),
                   jax.ShapeDtypeStruct((m // tm, CONV_W - 1, D_FF), F32),
                   jax.ShapeDtypeStruct((m // tm, CONV_W - 1, D_FF), F32)),
        grid=(m // tm, nt),
        in_specs=[pl.BlockSpec((tm, k), lambda i, j: (i, 0)),
                  pl.BlockSpec((tm, STAT_LANES), lambda i, j: (i, 0)),
                  pl.BlockSpec((k, tn), lambda i, j: (0, j)),
                  pl.BlockSpec((k, tn), lambda i, j: (0, j)),
                  pl.BlockSpec((CONV_W, tn), lambda i, j: (0, j)),
                  pl.BlockSpec((CONV_W, tn), lambda i, j: (0, j + nt)),
                  pl.BlockSpec((1, tn), lambda i, j: (0, j)),
                  pl.BlockSpec((1, tn), lambda i, j: (0, j + nt))],
        out_specs=(pl.BlockSpec((tm, tn), lambda i, j: (i, j)),
                   pl.BlockSpec((1, CONV_W - 1, tn), lambda i, j: (i, 0, j)),
                   pl.BlockSpec((1, CONV_W - 1, tn), lambda i, j: (i, 0, j))),
        scratch_shapes=[pltpu.VMEM((nt, CONV_W - 1, tn), F32),
                        pltpu.VMEM((nt, CONV_W - 1, tn), F32)],
        compiler_params=_params(2),
        name="convglu_prompt",
    )(h, ssq, w_gate, w_value, conv_w, conv_w, conv_b, conv_b)
    tails = jnp.concatenate([sg, sv], axis=-1)
    return a, tails[tiles_per_seq - 1::tiles_per_seq]


def _convglu_sample_kernel(h_ref, wg_ref, wv_ref, cwg_ref, cwv_ref, cbg_ref, cbv_ref,
                           p0g_ref, p0v_ref, p1g_ref, p1v_ref, a_ref, ug_ref, uv_ref, wbg_ref, wbv_ref):
    def half(w_ref, wb_ref, cw_ref, cb_ref, p0_ref, p1_ref, u_ref):
        u = jnp.dot(h_ref[...], _mxu_weights(w_ref, (wb_ref,)), preferred_element_type=F32)
        u_ref[...] = u
        return _conv_taps(cb_ref, cw_ref, p0_ref[...], p1_ref[...], u)

    cg = half(wg_ref, wbg_ref, cwg_ref, cbg_ref, p0g_ref, p1g_ref, ug_ref)
    cv = half(wv_ref, wbv_ref, cwv_ref, cbv_ref, p0v_ref, p1v_ref, uv_ref)
    a_ref[...] = (_gelu_erf(cg) * cv).astype(a_ref.dtype)


def _convglu_sample(h, w_up, conv_w, conv_b, state_conv):
    m, k = h.shape
    nt = N_FF_TILES
    tn = FF_TILE
    prev = state_conv.reshape(m, (CONV_W - 1) * 2 * D_FF)
    col = lambda off: (lambda j: (0, j + off))
    a, ug, uv, w_gate, w_value = pl.pallas_call(
        _convglu_sample_kernel,
        out_shape=(jax.ShapeDtypeStruct((m, D_FF), BF16),
                   jax.ShapeDtypeStruct((m, D_FF), F32),
                   jax.ShapeDtypeStruct((m, D_FF), F32),
                   jax.ShapeDtypeStruct((k, D_FF), BF16),
                   jax.ShapeDtypeStruct((k, D_FF), BF16)),
        grid=(nt,),
        in_specs=[pl.BlockSpec((m, k), lambda j: (0, 0)),
                  pl.BlockSpec((k, tn), col(0)),
                  pl.BlockSpec((k, tn), col(nt)),
                  pl.BlockSpec((CONV_W, tn), col(0)),
                  pl.BlockSpec((CONV_W, tn), col(nt)),
                  pl.BlockSpec((1, tn), col(0)),
                  pl.BlockSpec((1, tn), col(nt)),
                  pl.BlockSpec((m, tn), col(0)),
                  pl.BlockSpec((m, tn), col(nt)),
                  pl.BlockSpec((m, tn), col(2 * nt)),
                  pl.BlockSpec((m, tn), col(3 * nt))],
        out_specs=(pl.BlockSpec((m, tn), col(0)),
                   pl.BlockSpec((m, tn), col(0)),
                   pl.BlockSpec((m, tn), col(0)),
                   pl.BlockSpec((k, tn), col(0)),
                   pl.BlockSpec((k, tn), col(0))),
        compiler_params=_params(1),
        name="convglu_sample",
    )(h, w_up, w_up, conv_w, conv_w, conv_b, conv_b, prev, prev, prev, prev)
    u = jnp.concatenate([ug, uv], axis=-1)
    conv_new = jnp.stack([state_conv[:, 1, :], u], axis=1)
    return a, conv_new, w_gate, w_value


def _segment_mean_square(x, seg_ones):
    sq = x * x
    hi = sq.astype(BF16)
    lo = (sq - hi.astype(F32)).astype(BF16)
    total = (jnp.dot(hi, seg_ones, preferred_element_type=F32)
             + jnp.dot(lo, seg_ones, preferred_element_type=F32))
    return total / HEAD_DIM


def _attn_prompt_kernel(q_ref, kc_ref, vc_ref, bucket_ref, seg_ones_ref, rb_ref, sinks_ref, gq_ref, gk_ref,
                        o_ref, kn_ref, bias_ref, kdup_ref, vdup_ref, *, blocks_per_seq):
    r = pl.program_id(0)
    blk = ATTN_BLOCK
    pair = 2 * HEAD_DIM
    n_bias_rows = N_ATTN_HEADS * blk
    seq_start = (r % blocks_per_seq) == 0

    @pl.when(r == 0)
    def _():
        bucket = bucket_ref[...]
        prev_cols = lax.broadcasted_iota(jnp.int32, bucket.shape, 1) < blk

        def per_head(h, carry):
            acc = jnp.full(bucket.shape, -jnp.inf, F32)
            for b in range(N_BUCKETS):
                acc = jnp.where(bucket == b, rb_ref[b, h], acc)
            row0 = pl.multiple_of(h * blk, blk)
            bias_ref[pl.ds(row0, blk), :] = acc
            bias_ref[pl.ds(n_bias_rows + row0, blk), :] = jnp.where(prev_cols, -jnp.inf, acc)
            return carry
        lax.fori_loop(0, N_ATTN_HEADS, per_head, 0)
        vdup_ref[:, :, pair:] = jnp.ones((N_KV_HEADS, 2 * blk, pair), BF16)

    @pl.when(seq_start)
    def _():
        kdup_ref[:, :blk, :] = jnp.zeros((N_KV_HEADS, blk, pair), BF16)
        vdup_ref[:, :blk, :pair] = jnp.zeros((N_KV_HEADS, blk, pair), BF16)

    @pl.when(jnp.logical_not(seq_start))
    def _():
        kdup_ref[:, :blk, :] = kdup_ref[:, blk:, :]
        vdup_ref[:, :blk, :pair] = vdup_ref[:, blk:, :pair]

    seg_ones = seg_ones_ref[...]
    low_half = lax.broadcasted_iota(jnp.int32, (blk, pair), 1) < HEAD_DIM

    def duplicate_half(x, odd):
        swapped = pltpu.roll(x, HEAD_DIM, axis=1)
        return jnp.where(low_half, swapped, x) if odd else jnp.where(low_half, x, swapped)

    kc = kc_ref[...]
    kn = kc * lax.rsqrt(_segment_mean_square(kc, seg_ones) + EPS) * gk_ref[...]
    kn_ref[...] = kn
    for j in range(N_KV_HEADS):
        col = slice((j // 2) * pair, (j // 2 + 1) * pair)
        kdup_ref[j, blk:, :] = duplicate_half(kn[:, col], j % 2).astype(BF16)
        vdup_ref[j, blk:, :pair] = duplicate_half(vc_ref[:, col], j % 2).astype(BF16)

    bias_base = jnp.where(seq_start, n_bias_rows, 0)
    chunk = 2 * pair
    heads_per_chunk = chunk // HEAD_DIM
    for j in range(N_KV_HEADS):
        heads = range(j * GQA_GROUP, (j + 1) * GQA_GROUP)
        lhs = []
        for c in range(j * GQA_GROUP // heads_per_chunk, (j + 1) * GQA_GROUP // heads_per_chunk):
            qc = q_ref[:, c * chunk:(c + 1) * chunk]
            qn = qc * lax.rsqrt(_segment_mean_square(qc, seg_ones) + EPS) * (gq_ref[...] * HEAD_DIM ** -0.5)
            for p in range(chunk // pair):
                qp = qn[:, p * pair:(p + 1) * pair]
                lhs.append(jnp.where(low_half, qp, 0.0).astype(BF16))
                lhs.append(jnp.where(low_half, 0.0, qp).astype(BF16))
        s = lax.dot_general(jnp.concatenate(lhs, axis=0), kdup_ref[j], (((1,), (1,)), ((), ())),
                            preferred_element_type=F32)
        exps, sink_terms = [], []
        for g, h in enumerate(heads):
            row0 = pl.multiple_of(bias_base + h * blk, blk)
            sg = s[g * blk:(g + 1) * blk, :] + bias_ref[pl.ds(row0, blk), :]
            sink = sinks_ref[h]
            m = jnp.maximum(jnp.max(sg, axis=-1, keepdims=True), sink)
            exps.append(jnp.exp(sg - m).astype(BF16))
            sink_terms.append(jnp.exp(sink - m))
        o = jnp.dot(jnp.concatenate(exps, axis=0), vdup_ref[j], preferred_element_type=F32)
        for g in range(0, GQA_GROUP, 2):
            even = o[g * blk:(g + 1) * blk, :]
            odd = o[(g + 1) * blk:(g + 2) * blk, :]
            num = jnp.where(low_half, even[:, :pair], odd[:, :pair])
            den = (jnp.where(low_half, even[:, pair:], odd[:, pair:])
                   + jnp.where(low_half, sink_terms[g], sink_terms[g + 1]))
            c0 = (j * GQA_GROUP + g) * HEAD_DIM
            o_ref[:, c0:c0 + pair] = (num / den).astype(o_ref.dtype)


def _attn_prompt(proj, bucket_tile, rel_bias, sinks, g_q, g_k, n_seq):
    m = proj.shape[0]
    blk = ATTN_BLOCK
    nb = m // blk
    blocks_per_seq = nb // n_seq
    lanes = 4 * HEAD_DIM
    seg = jnp.arange(lanes, dtype=jnp.int32) // HEAD_DIM
    seg_ones = (seg[:, None] == seg[None, :]).astype(BF16)
    tile4 = lambda g: jnp.tile(g.reshape(-1), lanes // HEAD_DIM).reshape(1, lanes)
    smem = pl.BlockSpec(memory_space=pltpu.SMEM)
    return pl.pallas_call(
        functools.partial(_attn_prompt_kernel, blocks_per_seq=blocks_per_seq),
        out_shape=(jax.ShapeDtypeStruct((m, ATTN_WIDTH), BF16),
                   jax.ShapeDtypeStruct((m, KV_WIDTH), F32)),
        grid=(nb,),
        in_specs=[pl.BlockSpec((blk, ATTN_WIDTH), lambda r: (r, COL_AQ // ATTN_WIDTH)),
                  pl.BlockSpec((blk, KV_WIDTH), lambda r: (r, COL_AK // KV_WIDTH)),
                  pl.BlockSpec((blk, KV_WIDTH), lambda r: (r, COL_AV // KV_WIDTH)),
                  pl.BlockSpec((blk, 2 * blk), lambda r: (0, 0)),
                  pl.BlockSpec((lanes, lanes), lambda r: (0, 0)),
                  smem, smem,
                  pl.BlockSpec((1, lanes), lambda r: (0, 0)),
                  pl.BlockSpec((1, lanes), lambda r: (0, 0))],
        out_specs=(pl.BlockSpec((blk, ATTN_WIDTH), lambda r: (r, 0)),
                   pl.BlockSpec((blk, KV_WIDTH), lambda r: (r, 0))),
        scratch_shapes=[pltpu.VMEM((2 * N_ATTN_HEADS * blk, 2 * blk), F32),
                        pltpu.VMEM((N_KV_HEADS, 2 * blk, 2 * HEAD_DIM), BF16),
                        pltpu.VMEM((N_KV_HEADS, 2 * blk, 4 * HEAD_DIM), BF16)],
        compiler_params=_params(1),
        name="attn_prompt",
    )(proj, proj, proj, bucket_tile, seg_ones, rel_bias, sinks, tile4(g_q), tile4(g_k))


def _attn_sample_kernel(q_ref, knew_ref, vnew_ref, ck_ref, cv_ref, bucket_ref, rbt_ref, sinks_ref,
                        gq_ref, gk_ref, o_ref, cko_ref, cvo_ref):
    bt = q_ref.shape[0]
    win = ck_ref.shape[1]
    bucket = bucket_ref[...]
    bias = jnp.full((N_ATTN_HEADS, bucket.shape[1]), -jnp.inf, F32)
    for b in range(N_BUCKETS):
        bias = jnp.where(bucket == b, rbt_ref[:, b:b + 1], bias)
    bias_c = bias[:, :win]
    bias_n = bias[:, win:win + 1]
    sink = sinks_ref[...]
    head_group = jnp.right_shift(lax.broadcasted_iota(jnp.int32, (N_ATTN_HEADS, HEAD_DIM), 0),
                                 int(math.log2(GQA_GROUP)))
    lane_group = jnp.right_shift(lax.broadcasted_iota(jnp.int32, (1, KV_WIDTH), 1),
                                 int(math.log2(HEAD_DIM)))
    gk = gk_ref[...]
    scale = HEAD_DIM ** -0.5

    for b in range(bt):
        k_row = knew_ref[b]
        k_sq = k_row * k_row
        inv = jnp.zeros_like(k_row)
        for j in range(N_KV_HEADS):
            ms = jnp.sum(jnp.where(lane_group == j, k_sq, 0.0), axis=-1, keepdims=True) / HEAD_DIM
            inv = jnp.where(lane_group == j, lax.rsqrt(ms + EPS), inv)
        kn_row = k_row * inv * gk
        v_row = vnew_ref[b]

        qn = _rms(q_ref[b], gq_ref[...])
        q_bd = jnp.concatenate([jnp.where(head_group == j, qn, 0.0) for j in range(N_KV_HEADS)], axis=1)
        kc = ck_ref[b]
        vc = cv_ref[b]
        s_c = lax.dot_general(q_bd.astype(BF16), kc.astype(BF16), (((1,), (1,)), ((), ())),
                              preferred_element_type=F32) * scale + bias_c
        s_n = jnp.sum(q_bd * kn_row, axis=-1, keepdims=True) * scale + bias_n
        m = jnp.maximum(jnp.maximum(jnp.max(s_c, axis=-1, keepdims=True), s_n), sink)
        e_c = jnp.exp(s_c - m)
        e_n = jnp.exp(s_n - m)
        denom = jnp.sum(e_c, axis=-1, keepdims=True) + e_n + jnp.exp(sink - m)
        o_full = jnp.dot((e_c / denom).astype(BF16), vc.astype(BF16), preferred_element_type=F32)
        o_full = o_full + (e_n / denom) * v_row
        o = jnp.zeros((N_ATTN_HEADS, HEAD_DIM), F32)
        for j in range(N_KV_HEADS):
            o = jnp.where(head_group == j, o_full[:, j * HEAD_DIM:(j + 1) * HEAD_DIM], o)
        o_ref[b] = o.astype(o_ref.dtype)

        cko_ref[b, pl.ds(0, win - 1), :] = ck_ref[b, pl.ds(1, win - 1), :]
        cko_ref[b, pl.ds(win - 1, 1), :] = kn_row
        cvo_ref[b, pl.ds(0, win - 1), :] = cv_ref[b, pl.ds(1, win - 1), :]
        cvo_ref[b, pl.ds(win - 1, 1), :] = v_row


def _attn_sample(q, k_new, v_new, cache_k, cache_v, bucket_row, rel_bias_t, sinks_col, g_q, g_k_row, bt):
    nb = q.shape[0]
    win = cache_k.shape[1]
    full = lambda shape: pl.BlockSpec(shape, lambda i: (0,) * len(shape))
    return pl.pallas_call(
        _attn_sample_kernel,
        out_shape=(jax.ShapeDtypeStruct((nb, N_ATTN_HEADS, HEAD_DIM), BF16),
                   jax.ShapeDtypeStruct(cache_k.shape, F32),
                   jax.ShapeDtypeStruct(cache_v.shape, F32)),
        grid=(nb // bt,),
        in_specs=[pl.BlockSpec((bt, N_ATTN_HEADS, HEAD_DIM), lambda i: (i, 0, 0)),
                  pl.BlockSpec((bt, 1, KV_WIDTH), lambda i: (i, 0, 0)),
                  pl.BlockSpec((bt, 1, KV_WIDTH), lambda i: (i, 0, 0)),
                  pl.BlockSpec((bt, win, KV_WIDTH), lambda i: (i, 0, 0)),
                  pl.BlockSpec((bt, win, KV_WIDTH), lambda i: (i, 0, 0)),
                  full(bucket_row.shape), full(rel_bias_t.shape), full(sinks_col.shape),
                  full(g_q.shape), full(g_k_row.shape)],
        out_specs=(pl.BlockSpec((bt, N_ATTN_HEADS, HEAD_DIM), lambda i: (i, 0, 0)),
                   pl.BlockSpec((bt, win, KV_WIDTH), lambda i: (i, 0, 0)),
                   pl.BlockSpec((bt, win, KV_WIDTH), lambda i: (i, 0, 0))),
        compiler_params=_params(1),
        name="attn_sample",
    )(q, k_new, v_new, cache_k, cache_v, bucket_row, rel_bias_t, sinks_col, g_q, g_k_row)


def _ret_prompt_kernel(q_ref, k_ref, v_ref, gate_ref, cos_ref, sin_ref, dmask_ref, qdec_ref, kdec_ref,
                       cdec_ref, o_ref, state_ref, s_ref):
    c = pl.program_id(1)

    @pl.when(c == 0)
    def _():
        s_ref[...] = jnp.zeros_like(s_ref)

    cosf = cos_ref[...]
    sinf = sin_ref[...]
    half = RET_KDIM // 2
    nt = (((1,), (1,)), ((), ()))
    tn = (((0,), (0,)), ((), ()))

    def rotary(x):
        return x * cosf + pltpu.roll(x, half, axis=1) * sinf

    for h in range(N_RET_HEADS):
        kd = slice(h * RET_KDIM, (h + 1) * RET_KDIM)
        vd = slice(h * RET_VDIM, (h + 1) * RET_VDIM)
        q = rotary(q_ref[:, kd])
        k = rotary(k_ref[:, kd]) * RET_KDIM ** -0.5
        v = v_ref[:, vd].astype(BF16)
        s0 = s_ref[h]
        scores = lax.dot_general(q.astype(BF16), k.astype(BF16), nt,
                                 preferred_element_type=F32) * dmask_ref[h]
        o_intra = jnp.dot(scores.astype(BF16), v, preferred_element_type=F32)
        o_inter = jnp.dot((q * qdec_ref[h]).astype(BF16), s0.astype(BF16), preferred_element_type=F32)
        s_ref[h] = cdec_ref[h] * s0 + lax.dot_general((k * kdec_ref[h]).astype(BF16), v, tn,
                                                       preferred_element_type=F32)
        ro = o_intra + o_inter
        ro = ro * lax.rsqrt(jnp.mean(ro * ro, axis=-1, keepdims=True) + EPS)
        o_ref[:, vd] = (jax.nn.silu(gate_ref[:, vd]) * ro).astype(o_ref.dtype)

    @pl.when(c == pl.num_programs(1) - 1)
    def _():
        state_ref[0] = s_ref[...]


def _ret_prompt(proj, cosf, sinf, dmask, qdec, kdec, cdec, n_seq):
    m = proj.shape[0]
    ch = RET_CHUNK
    nc = m // n_seq // ch
    row = lambda b, c: b * nc + c
    const3 = pl.BlockSpec((N_RET_HEADS, ch, ch), lambda b, c: (0, 0, 0))
    return pl.pallas_call(
        _ret_prompt_kernel,
        out_shape=(jax.ShapeDtypeStruct((m, RET_V_WIDTH), BF16),
                   jax.ShapeDtypeStruct((n_seq, N_RET_HEADS, RET_KDIM, RET_VDIM), F32)),
        grid=(n_seq, nc),
        in_specs=[pl.BlockSpec((ch, RET_QK_WIDTH), lambda b, c: (row(b, c), COL_RQ // RET_QK_WIDTH)),
                  pl.BlockSpec((ch, RET_QK_WIDTH), lambda b, c: (row(b, c), COL_RK // RET_QK_WIDTH)),
                  pl.BlockSpec((ch, RET_V_WIDTH), lambda b, c: (row(b, c), COL_RV // RET_V_WIDTH)),
                  pl.BlockSpec((ch, RET_V_WIDTH), lambda b, c: (row(b, c), COL_RG // RET_V_WIDTH)),
                  pl.BlockSpec((ch, RET_KDIM), lambda b, c: (c, 0)),
                  pl.BlockSpec((ch, RET_KDIM), lambda b, c: (c, 0)),
                  const3, const3, const3,
                  pl.BlockSpec(memory_space=pltpu.SMEM)],
        out_specs=(pl.BlockSpec((ch, RET_V_WIDTH), lambda b, c: (row(b, c), 0)),
                   pl.BlockSpec((1, N_RET_HEADS, RET_KDIM, RET_VDIM), lambda b, c: (b, 0, 0, 0))),
        scratch_shapes=[pltpu.VMEM((N_RET_HEADS, RET_KDIM, RET_VDIM), F32)],
        compiler_params=_params(2),
        name="ret_prompt",
    )(proj, proj, proj, proj, cosf, sinf, dmask, qdec, kdec, cdec)


def _ret_sample_kernel(qt_ref, kt_ref, cos_ref, sin_ref, v_ref, gate_ref, s_ref, qdec_ref, kdec_ref,
                       cdec_ref, o_ref, so_ref):
    bt = s_ref.shape[0]
    half = RET_KDIM // 2
    cos = cos_ref[...]
    sin = sin_ref[...]

    def rotary(x):
        x1, x2 = x[:half, :], x[half:, :]
        return jnp.concatenate([x1 * cos - x2 * sin, x2 * cos + x1 * sin], axis=0)

    for h in range(N_RET_HEADS):
        vd = slice(h * RET_VDIM, (h + 1) * RET_VDIM)
        q_all = rotary(qt_ref[0, h])
        k_all = rotary(kt_ref[0, h]) * RET_KDIM ** -0.5
        for b in range(bt):
            q = q_all[:, b:b + 1]
            k = k_all[:, b:b + 1]
            v = v_ref[0, b:b + 1, vd]
            s0 = s_ref[b, h]
            o_intra = jnp.sum(q * k, axis=0, keepdims=True) * v
            o_inter = jnp.sum((q * qdec_ref[h]) * s0, axis=0, keepdims=True)
            so_ref[b, h] = cdec_ref[h] * s0 + (k * kdec_ref[h]) * v
            ro = o_intra + o_inter
            ro = ro * lax.rsqrt(jnp.mean(ro * ro, axis=-1, keepdims=True) + EPS)
            o_ref[0, b:b + 1, vd] = (jax.nn.silu(gate_ref[0, b:b + 1, vd]) * ro).astype(o_ref.dtype)


def _ret_sample(qt, kt, cos_col, sin_col, v, gate, state, qdec, kdec, cdec):
    nbt, _, _, bt = qt.shape
    smem = pl.BlockSpec(memory_space=pltpu.SMEM)
    st_spec = pl.BlockSpec((bt, N_RET_HEADS, RET_KDIM, RET_VDIM), lambda i: (i, 0, 0, 0))
    qk_spec = pl.BlockSpec((1, N_RET_HEADS, RET_KDIM, bt), lambda i: (i, 0, 0, 0))
    row_spec = pl.BlockSpec((1, bt, RET_V_WIDTH), lambda i: (i, 0, 0))
    col_spec = pl.BlockSpec(cos_col.shape, lambda i: (0, 0))
    return pl.pallas_call(
        _ret_sample_kernel,
        out_shape=(jax.ShapeDtypeStruct((nbt, bt, RET_V_WIDTH), BF16),
                   jax.ShapeDtypeStruct(state.shape, F32)),
        grid=(nbt,),
        in_specs=[qk_spec, qk_spec, col_spec, col_spec, row_spec, row_spec, st_spec, smem, smem, smem],
        out_specs=(row_spec, st_spec),
        compiler_params=_params(1),
        name="ret_sample",
    )(qt, kt, cos_col, sin_col, v, gate, state, qdec, kdec, cdec)


def _t5_bucket(dist):
    n = jnp.maximum(dist, 0)
    max_exact = N_BUCKETS // 2
    nf = jnp.maximum(n, 1).astype(F32)
    large = max_exact + (jnp.log(nf / max_exact) / math.log(MAX_DISTANCE / max_exact)
                         * (N_BUCKETS - max_exact)).astype(jnp.int32)
    return jnp.where(n < max_exact, n, jnp.minimum(large, N_BUCKETS - 1))


def _rope_tables(pos):
    half = RET_KDIM // 2
    inv = ROPE_BASE ** (-jnp.arange(half, dtype=F32) / half)
    ang = pos.astype(F32)[:, None] * inv[None]
    return jnp.cos(ang), jnp.sin(ang)


def _layer(xp, xs, pp, ps, cache_k, cache_v, state_ret, state_conv, rel_bias, lp, n_seq):
    (g_mix, w_in, g_q, g_k, sinks, w_out, g_ffn, w_up, conv_w, conv_b, w_down,
     g_ple, w_ple_gate, w_ple_proj) = lp
    mp = xp.shape[0]
    ms = xs.shape[0]
    seq = mp // n_seq
    win = cache_k.shape[1]

    row = lambda a: a.reshape(1, -1)

    log_decay = jnp.log(1.0 - 2.0 ** (-5.0 - jnp.arange(N_RET_HEADS, dtype=F32)))
    idx = jnp.arange(RET_CHUNK, dtype=F32)
    diff = idx[:, None] - idx[None, :]
    dmask = jnp.where(diff[None] >= 0, jnp.exp(diff[None] * log_decay[:, None, None]), 0.0)
    q_dec = jnp.exp((idx + 1.0)[:, None] * log_decay[None])
    k_dec = jnp.exp((RET_CHUNK - 1.0 - idx)[:, None] * log_decay[None])
    qdec_t = jnp.broadcast_to(q_dec.T[:, :, None], (N_RET_HEADS, RET_CHUNK, RET_KDIM))
    kdec_t = jnp.broadcast_to(k_dec.T[:, :, None], (N_RET_HEADS, RET_CHUNK, RET_KDIM))
    cdec = jnp.exp(RET_CHUNK * log_decay)
    cos_p, sin_p = _rope_tables(jnp.arange(seq, dtype=jnp.int32))
    cosf = jnp.concatenate([cos_p, cos_p], axis=1)
    sinf = jnp.concatenate([-sin_p, sin_p], axis=1)
    one = jnp.arange(1, dtype=F32)
    qdec_s = jnp.exp((one + 1.0)[:, None] * log_decay[None])[0]
    kdec_s = jnp.exp((1.0 - 1.0 - one)[:, None] * log_decay[None])[0]
    cdec_s = jnp.exp(1.0 * log_decay)
    cos_s, sin_s = _rope_tables(PAST_LEN + jnp.arange(1, dtype=jnp.int32))
    cos_col, sin_col = cos_s.reshape(-1, 1), sin_s.reshape(-1, 1)

    qi = jnp.arange(ATTN_BLOCK, dtype=jnp.int32)
    ki = jnp.arange(2 * ATTN_BLOCK, dtype=jnp.int32) - ATTN_BLOCK
    dist = qi[:, None] - ki[None, :]
    bucket_tile = jnp.where((dist >= 0) & (dist <= WINDOW), _t5_bucket(dist), -1)
    dist_s = win - jnp.arange(2 * ATTN_BLOCK, dtype=jnp.int32)
    bucket_row = jnp.where((dist_s >= 0) & (dist_s <= WINDOW), _t5_bucket(dist_s), -1).reshape(1, -1)

    tmp, tms = _row_tile(mp), _row_tile(ms)
    proj_s, w_in_b = _matmul(_norm(xs, row(g_mix), tms), w_in, tms, COL_TILE, IN_SRC_BLOCKS)
    proj_p, = _matmul(_norm(xp, row(g_mix), NORM_TILE), w_in_b, min(mp, IN_ROW_TILE), COL_TILE)

    attn_p, kn_p = _attn_prompt(proj_p, bucket_tile, rel_bias, sinks, row(g_q), row(g_k), n_seq)
    ret_p, ret_state_p = _ret_prompt(proj_p, cosf, sinf, dmask, qdec_t, kdec_t, cdec, n_seq)

    bt_a = 8
    attn_s, ck_new, cv_new = _attn_sample(
        proj_s[:, COL_AQ:COL_AQ + ATTN_WIDTH].reshape(ms, N_ATTN_HEADS, HEAD_DIM),
        proj_s[:, COL_AK:COL_AK + KV_WIDTH].reshape(ms, 1, KV_WIDTH),
        proj_s[:, COL_AV:COL_AV + KV_WIDTH].reshape(ms, 1, KV_WIDTH),
        cache_k.reshape(ms, win, KV_WIDTH), cache_v.reshape(ms, win, KV_WIDTH),
        bucket_row, rel_bias.T, sinks.reshape(-1, 1), row(g_q), jnp.tile(g_k, N_KV_HEADS).reshape(1, -1), bt_a)
    attn_s = attn_s.reshape(ms, ATTN_WIDTH)

    bt_r = 4
    to_cols = lambda a: a.reshape(ms // bt_r, bt_r, N_RET_HEADS, RET_KDIM).transpose(0, 2, 3, 1)
    ret_s, ret_state_s = _ret_sample(
        to_cols(proj_s[:, COL_RQ:COL_RQ + RET_QK_WIDTH]), to_cols(proj_s[:, COL_RK:COL_RK + RET_QK_WIDTH]),
        cos_col, sin_col,
        proj_s[:, COL_RV:COL_RV + RET_V_WIDTH].reshape(ms // bt_r, bt_r, RET_V_WIDTH),
        proj_s[:, COL_RG:COL_RG + RET_V_WIDTH].reshape(ms // bt_r, bt_r, RET_V_WIDTH),
        state_ret, qdec_s, kdec_s, cdec_s)
    ret_s = ret_s.reshape(ms, RET_V_WIDTH)

    xs, w_out_b = _out_proj(attn_s, ret_s, w_out, xs, tms, COL_TILE)
    xp, hp, ssq_p = _out_proj(attn_p, ret_p, w_out_b, xp, tmp, COL_TILE, row(g_ffn))

    act_s, conv_s, w_gate_b, w_value_b = _convglu_sample(_norm(xs, row(g_ffn), tms), w_up, conv_w, row(conv_b),
                                                         state_conv)
    act_p, conv_p = _convglu_prompt(hp, ssq_p, w_gate_b, w_value_b, conv_w, row(conv_b), n_seq, UP_ROW_TILE)
    for part in range(DOWN_K_PARTS):
        last = part == DOWN_K_PARTS - 1
        xs, w_down_b = _down_proj(act_s, w_down, xs, tms, COL_TILE, DOWN_K_PARTS, part)
        xp, *stats_p = _down_proj(act_p, w_down_b, xp, tmp, COL_TILE, DOWN_K_PARTS, part,
                                  row(g_ple) if last else None)
    hp, ssq_p = stats_p

    xs, w_pgate_b, w_pproj_b = _ple(xs, _norm(xs, row(g_ple), tms), None, w_ple_gate, ps, w_ple_proj, tms, COL_TILE)
    xp, = _ple(xp, hp, ssq_p, w_pgate_b, pp, w_pproj_b, tmp, COL_TILE)

    kp_new = kn_p.reshape(n_seq, seq, N_KV_HEADS, HEAD_DIM)[:, -WINDOW:]
    vp_new = proj_p[:, COL_AV:COL_AV + KV_WIDTH].reshape(n_seq, seq, N_KV_HEADS, HEAD_DIM)[:, -WINDOW:]
    ks_new = ck_new.reshape(ms, win, N_KV_HEADS, HEAD_DIM)
    vs_new = cv_new.reshape(ms, win, N_KV_HEADS, HEAD_DIM)
    return xp, xs, kp_new, vp_new, ret_state_p, conv_p, ks_new, vs_new, ret_state_s, conv_s


def kernel(x_prompt, x_sample, p_prompt, p_sample, cache_win_k, cache_win_v, state_ret, state_conv, rel_bias, g_mix, w_in, g_q, g_k, sinks, w_out, g_ffn, w_up, conv_w, conv_b, w_down, g_ple, w_ple_gate, w_ple_proj):
    depth = g_mix.shape[0]
    n_seq, seq, d = x_prompt.shape
    nb, dec_seq, _ = x_sample.shape
    assert dec_seq == 1 and seq % 512 == 0 and d == D_MODEL
    xp = x_prompt.reshape(n_seq * seq, d)
    xs = x_sample.reshape(nb, d)
    outs = [[] for _ in range(8)]
    for l in range(depth):
        lp = (g_mix[l], w_in[l], g_q[l], g_k[l], sinks[l], w_out[l], g_ffn[l], w_up[l], conv_w[l],
              conv_b[l], w_down[l], g_ple[l], w_ple_gate[l], w_ple_proj[l])
        res = _layer(xp, xs, p_prompt[l].reshape(n_seq * seq, -1), p_sample[l].reshape(nb, -1),
                     cache_win_k[l], cache_win_v[l], state_ret[l], state_conv[l], rel_bias, lp, n_seq)
        xp, xs = res[0], res[1]
        for o, r in zip(outs, res[2:]):
            o.append(r)
    stacked = [jnp.stack(o) for o in outs]
    return (xp.reshape(n_seq, seq, d), xs.reshape(nb, 1, d), *stacked)
```

```python
import functools
import math

import jax
import jax.numpy as jnp
import numpy as np
from jax import lax
from jax.experimental import pallas as pl
from jax.experimental.pallas import tpu as pltpu

F32 = jnp.float32
BF16 = jnp.bfloat16

D_MODEL = 4096
HEAD_DIM = 64
N_ATTN_HEADS = 32
N_KV_HEADS = 4
GQA_GROUP = N_ATTN_HEADS // N_KV_HEADS
WINDOW = 128
ATTN_BLOCK = 128
N_BUCKETS = 32
MAX_DISTANCE = 128
N_RET_HEADS = 8
RET_KDIM = 128
RET_VDIM = 256
RET_CHUNK = 128
ROPE_BASE = 10000.0
D_FF = 11008
CONV_W = 3
EPS = 1e-6

ATTN_WIDTH = N_ATTN_HEADS * HEAD_DIM
KV_WIDTH = N_KV_HEADS * HEAD_DIM
RET_QK_WIDTH = N_RET_HEADS * RET_KDIM
RET_V_WIDTH = N_RET_HEADS * RET_VDIM
IN_SIZES = (ATTN_WIDTH, KV_WIDTH, KV_WIDTH, RET_QK_WIDTH, RET_QK_WIDTH, RET_V_WIDTH, RET_V_WIDTH)
IN_WIDTH = sum(IN_SIZES)
IN_SPLITS = tuple(sum(IN_SIZES[:n + 1]) for n in range(len(IN_SIZES) - 1))
PAST_LEN = 8192

COL_AQ = 0
COL_RV = ATTN_WIDTH
COL_RG = COL_RV + RET_V_WIDTH
COL_RQ = COL_RG + RET_V_WIDTH
COL_RK = COL_RQ + RET_QK_WIDTH
COL_AK = COL_RK + RET_QK_WIDTH
COL_AV = COL_AK + KV_WIDTH

VMEM_LIMIT_BYTES = 56 * 1024 * 1024
FF_TILE = 256
N_FF_TILES = D_FF // FF_TILE
NORM_ROWS = 32
NORM_TILE = 512
OUT_COL_TILE = 1024
STAT_LANES = 128
IN_ROW_TILE = 2048
ROW_TILE = 1024
COL_TILE = 512
UP_ROW_TILE = 2048
CONV_CHUNK = 1024
DOWN_K_PARTS = 2


def _row_tile(m):
    return min(m, ROW_TILE)


def _in_src_blocks():
    start = dict(zip(("aq", "ak", "av", "rq", "rk", "rv", "rg"), (0,) + IN_SPLITS))
    order = (("aq", ATTN_WIDTH), ("rv", RET_V_WIDTH), ("rg", RET_V_WIDTH), ("rq", RET_QK_WIDTH),
             ("rk", RET_QK_WIDTH), ("ak", KV_WIDTH), ("av", KV_WIDTH))
    cols = [c for name, width in order for c in range(start[name], start[name] + width, KV_WIDTH)]
    per_tile = COL_TILE // KV_WIDTH
    blocks = []
    for b in range(0, len(cols), per_tile):
        group = cols[b:b + per_tile]
        assert group[0] % COL_TILE == 0 and all(c == group[0] + u * KV_WIDTH for u, c in enumerate(group))
        blocks.append(group[0] // COL_TILE)
    return tuple(blocks)


IN_SRC_BLOCKS = _in_src_blocks()


def _params(n_axes):
    return pltpu.CompilerParams(dimension_semantics=("arbitrary",) * n_axes,
                                vmem_limit_bytes=VMEM_LIMIT_BYTES)


def _rms(x, g):
    y = x * lax.rsqrt(jnp.mean(x * x, axis=-1, keepdims=True) + EPS)
    return y * g


def _norm_rows_to_bf16(x_ref, g_ref, h_ref):
    def body(c, carry):
        r0 = pl.multiple_of(c * NORM_ROWS, NORM_ROWS)
        h_ref[pl.ds(r0, NORM_ROWS), :] = _rms(x_ref[pl.ds(r0, NORM_ROWS), :], g_ref[...]).astype(h_ref.dtype)
        return carry
    lax.fori_loop(0, x_ref.shape[0] // NORM_ROWS, body, 0)


def _norm(x, g, tr):
    m, k = x.shape
    return pl.pallas_call(
        _norm_rows_to_bf16,
        out_shape=jax.ShapeDtypeStruct((m, k), BF16),
        grid=(m // tr,),
        in_specs=[pl.BlockSpec((tr, k), lambda i: (i, 0)),
                  pl.BlockSpec((1, k), lambda i: (0, 0))],
        out_specs=pl.BlockSpec((tr, k), lambda i: (i, 0)),
        compiler_params=_params(1),
        name="rmsnorm",
    )(x, g)


def _mxu_weights(w_ref, wb_ref):
    if not wb_ref:
        return w_ref[...]
    wb = w_ref[...].astype(BF16)
    wb_ref[0][...] = wb
    return wb


def _mm_kernel(h_ref, w_ref, o_ref, *wb_ref):
    o_ref[...] = jnp.dot(h_ref[...], _mxu_weights(w_ref, wb_ref), preferred_element_type=F32)


def _matmul(h, w, tm, tn, src_blocks=None):
    m, k = h.shape
    n = w.shape[1]
    emit = w.dtype != BF16
    if emit:
        w_map = lambda i, j: (0, sum(jnp.where(j == d, s, 0) for d, s in enumerate(src_blocks)))
    else:
        w_map = lambda i, j: (0, j)
    out_shape = [jax.ShapeDtypeStruct((m, n), F32)]
    out_specs = [pl.BlockSpec((tm, tn), lambda i, j: (i, j))]
    if emit:
        assert m == tm
        out_shape.append(jax.ShapeDtypeStruct((k, n), BF16))
        out_specs.append(pl.BlockSpec((k, tn), lambda i, j: (0, j)))
    return pl.pallas_call(
        _mm_kernel,
        out_shape=out_shape,
        grid=(m // tm, n // tn),
        in_specs=[pl.BlockSpec((tm, k), lambda i, j: (i, 0)),
                  pl.BlockSpec((k, tn), w_map)],
        out_specs=out_specs,
        compiler_params=_params(2),
        name="in_proj",
    )(h, w)


def _row_scale(ssq_ref, width, n_cols):
    r = lax.rsqrt(ssq_ref[...] / width + EPS)
    return jnp.concatenate([r] * (n_cols // STAT_LANES), axis=1)


def _residual_mm_kernel(*refs, n_lhs, emit_weights, norm_stats):
    lhs_refs = refs[:n_lhs]
    w_ref, x_ref = refs[n_lhs], refs[n_lhs + 1]
    rest = list(refs[n_lhs + 2:])
    g_ref = rest.pop(0) if norm_stats else None
    o_ref = rest.pop(0)
    wb_ref = (rest.pop(0),) if emit_weights else ()
    lhs = jnp.concatenate([r[...] for r in lhs_refs], axis=1) if n_lhs > 1 else lhs_refs[0][...]
    o = x_ref[...] + jnp.dot(lhs, _mxu_weights(w_ref, wb_ref), preferred_element_type=F32)
    o_ref[...] = o
    if norm_stats:
        hb_ref, ssq_ref = rest
        hb_ref[...] = (o * g_ref[...]).astype(hb_ref.dtype)
        part = jnp.broadcast_to(jnp.sum(o * o, axis=-1, keepdims=True), ssq_ref.shape)
        j = pl.program_id(1)

        @pl.when(j == 0)
        def _():
            ssq_ref[...] = part

        @pl.when(j > 0)
        def _():
            ssq_ref[...] += part


def _residual_matmul(lhs, lhs_specs, w, w_spec, wb_shape, x, g_next, tm, tn, name):
    m, n = x.shape
    emit = w.dtype != BF16
    stats = g_next is not None
    tile = pl.BlockSpec((tm, tn), lambda i, j: (i, j))
    operands = [*lhs, w, x]
    in_specs = [*lhs_specs, w_spec, tile]
    out_shape = [jax.ShapeDtypeStruct((m, n), F32)]
    out_specs = [tile]
    if stats:
        operands.append(g_next)
        in_specs.append(pl.BlockSpec((1, tn), lambda i, j: (0, j)))
    if emit:
        assert m == tm
        out_shape.append(jax.ShapeDtypeStruct(wb_shape, BF16))
        out_specs.append(pl.BlockSpec((wb_shape[0], tn), lambda i, j: (0, j)))
    if stats:
        out_shape += [jax.ShapeDtypeStruct((m, n), BF16), jax.ShapeDtypeStruct((m, STAT_LANES), F32)]
        out_specs += [tile, pl.BlockSpec((tm, STAT_LANES), lambda i, j: (i, 0))]
    return pl.pallas_call(
        functools.partial(_residual_mm_kernel, n_lhs=len(lhs), emit_weights=emit, norm_stats=stats),
        out_shape=out_shape,
        grid=(m // tm, n // tn),
        in_specs=in_specs,
        out_specs=out_specs,
        compiler_params=_params(2),
        name=name,
    )(*operands)


def _out_proj(a, r, w, x, tm, tn, g_next=None):
    ka, kr = a.shape[1], r.shape[1]
    lhs_specs = [pl.BlockSpec((tm, ka), lambda i, j: (i, 0)), pl.BlockSpec((tm, kr), lambda i, j: (i, 0))]
    w_spec = pl.BlockSpec((ka + kr, tn), lambda i, j: (0, j))
    return _residual_matmul([a, r], lhs_specs, w, w_spec, w.shape, x, g_next, tm, tn, "out_proj")


def _down_proj(a, w, x, tm, tn, k_parts, part, g_next=None):
    k = a.shape[1] // k_parts
    w_part = part if w.dtype != BF16 else 0
    lhs_specs = [pl.BlockSpec((tm, k), lambda i, j: (i, part))]
    w_spec = pl.BlockSpec((k, tn), lambda i, j: (w_part, j))
    return _residual_matmul([a], lhs_specs, w, w_spec, (k, w.shape[1]), x, g_next, tm, tn, "down_proj")


def _ple_kernel(*refs, emit_weights, row_scaled):
    h_ref, wg_ref, p_ref, wp_ref, x_ref = refs[:5]
    rest = list(refs[5:])
    ssq_ref = rest.pop(0) if row_scaled else None
    o_ref = rest.pop(0)
    wb_refs = tuple(rest) if emit_weights else ()
    z = jnp.dot(h_ref[...], _mxu_weights(wg_ref, wb_refs[:1]), preferred_element_type=F32)
    if row_scaled:
        z = z * _row_scale(ssq_ref, h_ref.shape[1], z.shape[1])
    pp = jnp.dot(p_ref[...].astype(BF16), _mxu_weights(wp_ref, wb_refs[1:]), preferred_element_type=F32)
    o_ref[...] = x_ref[...] + jax.nn.sigmoid(z) * pp


def _ple(x, h, ssq, wg, p, wp, tm, tn):
    m, k = h.shape
    n = wg.shape[1]
    kp = p.shape[1]
    emit = wg.dtype != BF16
    scaled = ssq is not None
    operands = [h, wg, p, wp, x]
    in_specs = [pl.BlockSpec((tm, k), lambda i, j: (i, 0)),
                pl.BlockSpec((k, tn), lambda i, j: (0, j)),
                pl.BlockSpec((tm, kp), lambda i, j: (i, 0)),
                pl.BlockSpec((kp, tn), lambda i, j: (0, j)),
                pl.BlockSpec((tm, tn), lambda i, j: (i, j))]
    if scaled:
        operands.append(ssq)
        in_specs.append(pl.BlockSpec((tm, STAT_LANES), lambda i, j: (i, 0)))
    out_shape = [jax.ShapeDtypeStruct((m, n), F32)]
    out_specs = [pl.BlockSpec((tm, tn), lambda i, j: (i, j))]
    if emit:
        assert m == tm
        out_shape += [jax.ShapeDtypeStruct(wg.shape, BF16), jax.ShapeDtypeStruct(wp.shape, BF16)]
        out_specs += [pl.BlockSpec((k, tn), lambda i, j: (0, j)), pl.BlockSpec((kp, tn), lambda i, j: (0, j))]
    return pl.pallas_call(
        functools.partial(_ple_kernel, emit_weights=emit, row_scaled=scaled),
        out_shape=out_shape,
        grid=(m // tm, n // tn),
        in_specs=in_specs,
        out_specs=out_specs,
        compiler_params=_params(2),
        name="ple",
    )(*operands)


def _gelu_erf(x):
    return 0.5 * x * (1.0 + lax.erf(x * math.sqrt(0.5)))


def _conv_taps(cb_ref, cw_ref, um2, um1, u):
    c = cb_ref[...] + cw_ref[0:1, :] * um2
    c = c + cw_ref[1:2, :] * um1
    return c + cw_ref[2:3, :] * u


def _convglu_prompt_kernel(h_ref, wg_ref, wv_ref, cwg_ref, cwv_ref, cbg_ref, cbv_ref,
                           a_ref, sg_ref, sv_ref, carry_g_ref, carry_v_ref, *, tiles_per_seq):
    i = pl.program_id(0)
    j = pl.program_id(1)
    tm = h_ref.shape[0]
    rows = min(tm, CONV_CHUNK)
    seq_start = (i % tiles_per_seq) == 0
    sub = 8
    row = lax.broadcasted_iota(jnp.int32, (sub, 1), 0)

    def half(h, w_ref, cw_ref, cb_ref, prev):
        u = jnp.dot(h, w_ref[...], preferred_element_type=F32)
        r1 = pltpu.roll(u, 1, axis=0)
        r2 = pltpu.roll(u, 2, axis=0)
        head1 = jnp.where(row == 0, prev[1:2, :], r1[:sub, :])
        head2 = jnp.where(row == 0, prev[0:1, :], jnp.where(row == 1, prev[1:2, :], r2[:sub, :]))
        um1 = jnp.concatenate([head1, r1[sub:, :]], axis=0)
        um2 = jnp.concatenate([head2, r2[sub:, :]], axis=0)
        return _conv_taps(cb_ref, cw_ref, um2, um1, u), u[rows - (CONV_W - 1):, :]

    prev_g = jnp.where(seq_start, 0.0, carry_g_ref[j])
    prev_v = jnp.where(seq_start, 0.0, carry_v_ref[j])
    for c in range(tm // rows):
        rs = slice(c * rows, (c + 1) * rows)
        h = h_ref[rs, :]
        cg, prev_g = half(h, wg_ref, cwg_ref, cbg_ref, prev_g)
        cv, prev_v = half(h, wv_ref, cwv_ref, cbv_ref, prev_v)
        a_ref[rs, :] = (_gelu_erf(cg) * cv).astype(a_ref.dtype)
    carry_g_ref[j] = prev_g
    carry_v_ref[j] = prev_v
    sg_ref[0] = prev_g
    sv_ref[0] = prev_v


def _convglu_prompt(h, w_gate, w_value, conv_w, conv_b, n_seq, tm):
    m, k = h.shape
    seq = m // n_seq
    tiles_per_seq = seq // tm
    nt = N_FF_TILES
    tn = FF_TILE
    a, sg, sv = pl.pallas_call(
        functools.partial(_convglu_prompt_kernel, tiles_per_seq=tiles_per_seq),
        out_shape=(jax.ShapeDtypeStruct((m, D_FF), BF16),
                   jax.ShapeDtypeStruct((m // tm, CONV_W - 1, D_FF), F32),
                   jax.ShapeDtypeStruct((m // tm, CONV_W - 1, D_FF), F32)),
        grid=(m // tm, nt),
        in_specs=[pl.BlockSpec((tm, k), lambda i, j: (i, 0)),
                  pl.BlockSpec((k, tn), lambda i, j: (0, j)),
                  pl.BlockSpec((k, tn), lambda i, j: (0, j)),
                  pl.BlockSpec((CONV_W, tn), lambda i, j: (0, j)),
                  pl.BlockSpec((CONV_W, tn), lambda i, j: (0, j + nt)),
                  pl.BlockSpec((1, tn), lambda i, j: (0, j)),
                  pl.BlockSpec((1, tn), lambda i, j: (0, j + nt))],
        out_specs=(pl.BlockSpec((tm, tn), lambda i, j: (i, j)),
                   pl.BlockSpec((1, CONV_W - 1, tn), lambda i, j: (i, 0, j)),
                   pl.BlockSpec((1, CONV_W - 1, tn), lambda i, j: (i, 0, j))),
        scratch_shapes=[pltpu.VMEM((nt, CONV_W - 1, tn), F32),
                        pltpu.VMEM((nt, CONV_W - 1, tn), F32)],
        compiler_params=_params(2),
        name="convglu_prompt",
    )(h, w_gate, w_value, conv_w, conv_w, conv_b, conv_b)
    tails = jnp.concatenate([sg, sv], axis=-1)
    return a, tails[tiles_per_seq - 1::tiles_per_seq]


def _convglu_sample_kernel(h_ref, wg_ref, wv_ref, cwg_ref, cwv_ref, cbg_ref, cbv_ref,
                           p0g_ref, p0v_ref, p1g_ref, p1v_ref, a_ref, ug_ref, uv_ref, wbg_ref, wbv_ref):
    def half(w_ref, wb_ref, cw_ref, cb_ref, p0_ref, p1_ref, u_ref):
        u = jnp.dot(h_ref[...], _mxu_weights(w_ref, (wb_ref,)), preferred_element_type=F32)
        u_ref[...] = u
        return _conv_taps(cb_ref, cw_ref, p0_ref[...], p1_ref[...], u)

    cg = half(wg_ref, wbg_ref, cwg_ref, cbg_ref, p0g_ref, p1g_ref, ug_ref)
    cv = half(wv_ref, wbv_ref, cwv_ref, cbv_ref, p0v_ref, p1v_ref, uv_ref)
    a_ref[...] = (_gelu_erf(cg) * cv).astype(a_ref.dtype)


def _convglu_sample(h, w_up, conv_w, conv_b, state_conv):
    m, k = h.shape
    nt = N_FF_TILES
    tn = FF_TILE
    prev = state_conv.reshape(m, (CONV_W - 1) * 2 * D_FF)
    col = lambda off: (lambda j: (0, j + off))
    a, ug, uv, w_gate, w_value = pl.pallas_call(
        _convglu_sample_kernel,
        out_shape=(jax.ShapeDtypeStruct((m, D_FF), BF16),
                   jax.ShapeDtypeStruct((m, D_FF), F32),
                   jax.ShapeDtypeStruct((m, D_FF), F32),
                   jax.ShapeDtypeStruct((k, D_FF), BF16),
                   jax.ShapeDtypeStruct((k, D_FF), BF16)),
        grid=(nt,),
        in_specs=[pl.BlockSpec((m, k), lambda j: (0, 0)),
                  pl.BlockSpec((k, tn), col(0)),
                  pl.BlockSpec((k, tn), col(nt)),
                  pl.BlockSpec((CONV_W, tn), col(0)),
                  pl.BlockSpec((CONV_W, tn), col(nt)),
                  pl.BlockSpec((1, tn), col(0)),
                  pl.BlockSpec((1, tn), col(nt)),
                  pl.BlockSpec((m, tn), col(0)),
                  pl.BlockSpec((m, tn), col(nt)),
                  pl.BlockSpec((m, tn), col(2 * nt)),
                  pl.BlockSpec((m, tn), col(3 * nt))],
        out_specs=(pl.BlockSpec((m, tn), col(0)),
                   pl.BlockSpec((m, tn), col(0)),
                   pl.BlockSpec((m, tn), col(0)),
                   pl.BlockSpec((k, tn), col(0)),
                   pl.BlockSpec((k, tn), col(0))),
        compiler_params=_params(1),
        name="convglu_sample",
    )(h, w_up, w_up, conv_w, conv_w, conv_b, conv_b, prev, prev, prev, prev)
    u = jnp.concatenate([ug, uv], axis=-1)
    conv_new = jnp.stack([state_conv[:, 1, :], u], axis=1)
    return a, conv_new, w_gate, w_value


def _segment_mean_square(x, seg_ones):
    sq = x * x
    hi = sq.astype(BF16)
    lo = (sq - hi.astype(F32)).astype(BF16)
    total = (jnp.dot(hi, seg_ones, preferred_element_type=F32)
             + jnp.dot(lo, seg_ones, preferred_element_type=F32))
    return total / HEAD_DIM


def _attn_prompt_kernel(q_ref, kc_ref, vc_ref, bucket_ref, seg_ones_ref, rb_ref, sinks_ref, gq_ref, gk_ref,
                        o_ref, kn_ref, bias_ref, kdup_ref, vdup_ref, *, blocks_per_seq):
    r = pl.program_id(0)
    blk = ATTN_BLOCK
    pair = 2 * HEAD_DIM
    n_bias_rows = N_ATTN_HEADS * blk
    seq_start = (r % blocks_per_seq) == 0

    @pl.when(r == 0)
    def _():
        bucket = bucket_ref[...]
        prev_cols = lax.broadcasted_iota(jnp.int32, bucket.shape, 1) < blk

        def per_head(h, carry):
            acc = jnp.full(bucket.shape, -jnp.inf, F32)
            for b in range(N_BUCKETS):
                acc = jnp.where(bucket == b, rb_ref[b, h], acc)
            row0 = pl.multiple_of(h * blk, blk)
            bias_ref[pl.ds(row0, blk), :] = acc
            bias_ref[pl.ds(n_bias_rows + row0, blk), :] = jnp.where(prev_cols, -jnp.inf, acc)
            return carry
        lax.fori_loop(0, N_ATTN_HEADS, per_head, 0)
        vdup_ref[:, :, pair:] = jnp.ones((N_KV_HEADS, 2 * blk, pair), BF16)

    @pl.when(seq_start)
    def _():
        kdup_ref[:, :blk, :] = jnp.zeros((N_KV_HEADS, blk, pair), BF16)
        vdup_ref[:, :blk, :pair] = jnp.zeros((N_KV_HEADS, blk, pair), BF16)

    @pl.when(jnp.logical_not(seq_start))
    def _():
        kdup_ref[:, :blk, :] = kdup_ref[:, blk:, :]
        vdup_ref[:, :blk, :pair] = vdup_ref[:, blk:, :pair]

    seg_ones = seg_ones_ref[...]
    low_half = lax.broadcasted_iota(jnp.int32, (blk, pair), 1) < HEAD_DIM

    def duplicate_half(x, odd):
        swapped = pltpu.roll(x, HEAD_DIM, axis=1)
        return jnp.where(low_half, swapped, x) if odd else jnp.where(low_half, x, swapped)

    kc = kc_ref[...]
    kn = kc * lax.rsqrt(_segment_mean_square(kc, seg_ones) + EPS) * gk_ref[...]
    kn_ref[...] = kn
    for j in range(N_KV_HEADS):
        col = slice((j // 2) * pair, (j // 2 + 1) * pair)
        kdup_ref[j, blk:, :] = duplicate_half(kn[:, col], j % 2).astype(BF16)
        vdup_ref[j, blk:, :pair] = duplicate_half(vc_ref[:, col], j % 2).astype(BF16)

    bias_base = jnp.where(seq_start, n_bias_rows, 0)
    chunk = 2 * pair
    heads_per_chunk = chunk // HEAD_DIM
    for j in range(N_KV_HEADS):
        heads = range(j * GQA_GROUP, (j + 1) * GQA_GROUP)
        lhs = []
        for c in range(j * GQA_GROUP // heads_per_chunk, (j + 1) * GQA_GROUP // heads_per_chunk):
            qc = q_ref[:, c * chunk:(c + 1) * chunk]
            qn = qc * lax.rsqrt(_segment_mean_square(qc, seg_ones) + EPS) * (gq_ref[...] * HEAD_DIM ** -0.5)
            for p in range(chunk // pair):
                qp = qn[:, p * pair:(p + 1) * pair]
                lhs.append(jnp.where(low_half, qp, 0.0).astype(BF16))
                lhs.append(jnp.where(low_half, 0.0, qp).astype(BF16))
        s = lax.dot_general(jnp.concatenate(lhs, axis=0), kdup_ref[j], (((1,), (1,)), ((), ())),
                            preferred_element_type=F32)
        exps, sink_terms = [], []
        for g, h in enumerate(heads):
            row0 = pl.multiple_of(bias_base + h * blk, blk)
            sg = s[g * blk:(g + 1) * blk, :] + bias_ref[pl.ds(row0, blk), :]
            sink = sinks_ref[h]
            m = jnp.maximum(jnp.max(sg, axis=-1, keepdims=True), sink)
            exps.append(jnp.exp(sg - m).astype(BF16))
            sink_terms.append(jnp.exp(sink - m))
        o = jnp.dot(jnp.concatenate(exps, axis=0), vdup_ref[j], preferred_element_type=F32)
        for g in range(0, GQA_GROUP, 2):
            even = o[g * blk:(g + 1) * blk, :]
            odd = o[(g + 1) * blk:(g + 2) * blk, :]
            num = jnp.where(low_half, even[:, :pair], odd[:, :pair])
            den = (jnp.where(low_half, even[:, pair:], odd[:, pair:])
                   + jnp.where(low_half, sink_terms[g], sink_terms[g + 1]))
            c0 = (j * GQA_GROUP + g) * HEAD_DIM
            o_ref[:, c0:c0 + pair] = (num / den).astype(o_ref.dtype)


def _attn_prompt(proj, bucket_tile, rel_bias, sinks, g_q, g_k, n_seq):
    m = proj.shape[0]
    blk = ATTN_BLOCK
    nb = m // blk
    blocks_per_seq = nb // n_seq
    lanes = 4 * HEAD_DIM
    seg = np.arange(lanes) // HEAD_DIM
    seg_ones = (seg[:, None] == seg[None, :]).astype(BF16)
    tile4 = lambda g: jnp.tile(g.reshape(-1), lanes // HEAD_DIM).reshape(1, lanes)
    smem = pl.BlockSpec(memory_space=pltpu.SMEM)
    return pl.pallas_call(
        functools.partial(_attn_prompt_kernel, blocks_per_seq=blocks_per_seq),
        out_shape=(jax.ShapeDtypeStruct((m, ATTN_WIDTH), BF16),
                   jax.ShapeDtypeStruct((m, KV_WIDTH), F32)),
        grid=(nb,),
        in_specs=[pl.BlockSpec((blk, ATTN_WIDTH), lambda r: (r, COL_AQ // ATTN_WIDTH)),
                  pl.BlockSpec((blk, KV_WIDTH), lambda r: (r, COL_AK // KV_WIDTH)),
                  pl.BlockSpec((blk, KV_WIDTH), lambda r: (r, COL_AV // KV_WIDTH)),
                  pl.BlockSpec((blk, 2 * blk), lambda r: (0, 0)),
                  pl.BlockSpec((lanes, lanes), lambda r: (0, 0)),
                  smem, smem,
                  pl.BlockSpec((1, lanes), lambda r: (0, 0)),
                  pl.BlockSpec((1, lanes), lambda r: (0, 0))],
        out_specs=(pl.BlockSpec((blk, ATTN_WIDTH), lambda r: (r, 0)),
                   pl.BlockSpec((blk, KV_WIDTH), lambda r: (r, 0))),
        scratch_shapes=[pltpu.VMEM((2 * N_ATTN_HEADS * blk, 2 * blk), F32),
                        pltpu.VMEM((N_KV_HEADS, 2 * blk, 2 * HEAD_DIM), BF16),
                        pltpu.VMEM((N_KV_HEADS, 2 * blk, 4 * HEAD_DIM), BF16)],
        compiler_params=_params(1),
        name="attn_prompt",
    )(proj, proj, proj, bucket_tile, seg_ones, rel_bias, sinks, tile4(g_q), tile4(g_k))


def _attn_sample_kernel(q_ref, knew_ref, vnew_ref, ck_ref, cv_ref, bucket_ref, rbt_ref, sinks_ref,
                        gq_ref, gk_ref, o_ref, cko_ref, cvo_ref):
    bt = q_ref.shape[0]
    win = ck_ref.shape[1]
    bucket = bucket_ref[...]
    bias = jnp.full((N_ATTN_HEADS, bucket.shape[1]), -jnp.inf, F32)
    for b in range(N_BUCKETS):
        bias = jnp.where(bucket == b, rbt_ref[:, b:b + 1], bias)
    bias_c = bias[:, :win]
    bias_n = bias[:, win:win + 1]
    sink = sinks_ref[...]
    head_group = jnp.right_shift(lax.broadcasted_iota(jnp.int32, (N_ATTN_HEADS, HEAD_DIM), 0),
                                 int(math.log2(GQA_GROUP)))
    lane_group = jnp.right_shift(lax.broadcasted_iota(jnp.int32, (1, KV_WIDTH), 1),
                                 int(math.log2(HEAD_DIM)))
    gk = gk_ref[...]
    scale = HEAD_DIM ** -0.5

    for b in range(bt):
        k_row = knew_ref[b]
        k_sq = k_row * k_row
        inv = jnp.zeros_like(k_row)
        for j in range(N_KV_HEADS):
            ms = jnp.sum(jnp.where(lane_group == j, k_sq, 0.0), axis=-1, keepdims=True) / HEAD_DIM
            inv = jnp.where(lane_group == j, lax.rsqrt(ms + EPS), inv)
        kn_row = k_row * inv * gk
        v_row = vnew_ref[b]

        qn = _rms(q_ref[b], gq_ref[...])
        q_bd = jnp.concatenate([jnp.where(head_group == j, qn, 0.0) for j in range(N_KV_HEADS)], axis=1)
        kc = ck_ref[b]
        vc = cv_ref[b]
        s_c = lax.dot_general(q_bd.astype(BF16), kc.astype(BF16), (((1,), (1,)), ((), ())),
                              preferred_element_type=F32) * scale + bias_c
        s_n = jnp.sum(q_bd * kn_row, axis=-1, keepdims=True) * scale + bias_n
        m = jnp.maximum(jnp.maximum(jnp.max(s_c, axis=-1, keepdims=True), s_n), sink)
        e_c = jnp.exp(s_c - m)
        e_n = jnp.exp(s_n - m)
        denom = jnp.sum(e_c, axis=-1, keepdims=True) + e_n + jnp.exp(sink - m)
        o_full = jnp.dot((e_c / denom).astype(BF16), vc.astype(BF16), preferred_element_type=F32)
        o_full = o_full + (e_n / denom) * v_row
        o = jnp.zeros((N_ATTN_HEADS, HEAD_DIM), F32)
        for j in range(N_KV_HEADS):
            o = jnp.where(head_group == j, o_full[:, j * HEAD_DIM:(j + 1) * HEAD_DIM], o)
        o_ref[b] = o.astype(o_ref.dtype)

        cko_ref[b, pl.ds(0, win - 1), :] = ck_ref[b, pl.ds(1, win - 1), :]
        cko_ref[b, pl.ds(win - 1, 1), :] = kn_row
        cvo_ref[b, pl.ds(0, win - 1), :] = cv_ref[b, pl.ds(1, win - 1), :]
        cvo_ref[b, pl.ds(win - 1, 1), :] = v_row


def _attn_sample(q, k_new, v_new, cache_k, cache_v, bucket_row, rel_bias_t, sinks_col, g_q, g_k_row, bt):
    nb = q.shape[0]
    win = cache_k.shape[1]
    full = lambda shape: pl.BlockSpec(shape, lambda i: (0,) * len(shape))
    return pl.pallas_call(
        _attn_sample_kernel,
        out_shape=(jax.ShapeDtypeStruct((nb, N_ATTN_HEADS, HEAD_DIM), BF16),
                   jax.ShapeDtypeStruct(cache_k.shape, F32),
                   jax.ShapeDtypeStruct(cache_v.shape, F32)),
        grid=(nb // bt,),
        in_specs=[pl.BlockSpec((bt, N_ATTN_HEADS, HEAD_DIM), lambda i: (i, 0, 0)),
                  pl.BlockSpec((bt, 1, KV_WIDTH), lambda i: (i, 0, 0)),
                  pl.BlockSpec((bt, 1, KV_WIDTH), lambda i: (i, 0, 0)),
                  pl.BlockSpec((bt, win, KV_WIDTH), lambda i: (i, 0, 0)),
                  pl.BlockSpec((bt, win, KV_WIDTH), lambda i: (i, 0, 0)),
                  full(bucket_row.shape), full(rel_bias_t.shape), full(sinks_col.shape),
                  full(g_q.shape), full(g_k_row.shape)],
        out_specs=(pl.BlockSpec((bt, N_ATTN_HEADS, HEAD_DIM), lambda i: (i, 0, 0)),
                   pl.BlockSpec((bt, win, KV_WIDTH), lambda i: (i, 0, 0)),
                   pl.BlockSpec((bt, win, KV_WIDTH), lambda i: (i, 0, 0))),
        compiler_params=_params(1),
        name="attn_sample",
    )(q, k_new, v_new, cache_k, cache_v, bucket_row, rel_bias_t, sinks_col, g_q, g_k_row)


def _ret_prompt_kernel(q_ref, k_ref, v_ref, gate_ref, cos_ref, sin_ref, dmask_ref, qdec_ref, kdec_ref,
                       cdec_ref, o_ref, state_ref, s_ref):
    c = pl.program_id(1)

    @pl.when(c == 0)
    def _():
        s_ref[...] = jnp.zeros_like(s_ref)

    cosf = cos_ref[...]
    sinf = sin_ref[...]
    half = RET_KDIM // 2
    nt = (((1,), (1,)), ((), ()))
    tn = (((0,), (0,)), ((), ()))

    def rotary(x):
        return x * cosf + pltpu.roll(x, half, axis=1) * sinf

    for h in range(N_RET_HEADS):
        kd = slice(h * RET_KDIM, (h + 1) * RET_KDIM)
        vd = slice(h * RET_VDIM, (h + 1) * RET_VDIM)
        q = rotary(q_ref[:, kd])
        k = rotary(k_ref[:, kd]) * RET_KDIM ** -0.5
        v = v_ref[:, vd].astype(BF16)
        s0 = s_ref[h]
        scores = lax.dot_general(q.astype(BF16), k.astype(BF16), nt,
                                 preferred_element_type=F32) * dmask_ref[h]
        o_intra = jnp.dot(scores.astype(BF16), v, preferred_element_type=F32)
        o_inter = jnp.dot((q * qdec_ref[h]).astype(BF16), s0.astype(BF16), preferred_element_type=F32)
        s_ref[h] = cdec_ref[h] * s0 + lax.dot_general((k * kdec_ref[h]).astype(BF16), v, tn,
                                                       preferred_element_type=F32)
        ro = o_intra + o_inter
        ro = ro * lax.rsqrt(jnp.mean(ro * ro, axis=-1, keepdims=True) + EPS)
        o_ref[:, vd] = (jax.nn.silu(gate_ref[:, vd]) * ro).astype(o_ref.dtype)

    @pl.when(c == pl.num_programs(1) - 1)
    def _():
        state_ref[0] = s_ref[...]


def _ret_prompt(proj, cosf, sinf, dmask, qdec, kdec, cdec, n_seq):
    m = proj.shape[0]
    ch = RET_CHUNK
    nc = m // n_seq // ch
    row = lambda b, c: b * nc + c
    const3 = pl.BlockSpec((N_RET_HEADS, ch, ch), lambda b, c: (0, 0, 0))
    return pl.pallas_call(
        _ret_prompt_kernel,
        out_shape=(jax.ShapeDtypeStruct((m, RET_V_WIDTH), BF16),
                   jax.ShapeDtypeStruct((n_seq, N_RET_HEADS, RET_KDIM, RET_VDIM), F32)),
        grid=(n_seq, nc),
        in_specs=[pl.BlockSpec((ch, RET_QK_WIDTH), lambda b, c: (row(b, c), COL_RQ // RET_QK_WIDTH)),
                  pl.BlockSpec((ch, RET_QK_WIDTH), lambda b, c: (row(b, c), COL_RK // RET_QK_WIDTH)),
                  pl.BlockSpec((ch, RET_V_WIDTH), lambda b, c: (row(b, c), COL_RV // RET_V_WIDTH)),
                  pl.BlockSpec((ch, RET_V_WIDTH), lambda b, c: (row(b, c), COL_RG // RET_V_WIDTH)),
                  pl.BlockSpec((ch, RET_KDIM), lambda b, c: (c, 0)),
                  pl.BlockSpec((ch, RET_KDIM), lambda b, c: (c, 0)),
                  const3, const3, const3,
                  pl.BlockSpec(memory_space=pltpu.SMEM)],
        out_specs=(pl.BlockSpec((ch, RET_V_WIDTH), lambda b, c: (row(b, c), 0)),
                   pl.BlockSpec((1, N_RET_HEADS, RET_KDIM, RET_VDIM), lambda b, c: (b, 0, 0, 0))),
        scratch_shapes=[pltpu.VMEM((N_RET_HEADS, RET_KDIM, RET_VDIM), F32)],
        compiler_params=_params(2),
        name="ret_prompt",
    )(proj, proj, proj, proj, cosf, sinf, dmask, qdec, kdec, cdec)


def _ret_sample_kernel(qt_ref, kt_ref, cos_ref, sin_ref, v_ref, gate_ref, s_ref, qdec_ref, kdec_ref,
                       cdec_ref, o_ref, so_ref):
    bt = s_ref.shape[0]
    half = RET_KDIM // 2
    cos = cos_ref[...]
    sin = sin_ref[...]

    def rotary(x):
        x1, x2 = x[:half, :], x[half:, :]
        return jnp.concatenate([x1 * cos - x2 * sin, x2 * cos + x1 * sin], axis=0)

    for h in range(N_RET_HEADS):
        vd = slice(h * RET_VDIM, (h + 1) * RET_VDIM)
        q_all = rotary(qt_ref[0, h])
        k_all = rotary(kt_ref[0, h]) * RET_KDIM ** -0.5
        for b in range(bt):
            q = q_all[:, b:b + 1]
            k = k_all[:, b:b + 1]
            v = v_ref[0, b:b + 1, vd]
            s0 = s_ref[b, h]
            o_intra = jnp.sum(q * k, axis=0, keepdims=True) * v
            o_inter = jnp.sum((q * qdec_ref[h]) * s0, axis=0, keepdims=True)
            so_ref[b, h] = cdec_ref[h] * s0 + (k * kdec_ref[h]) * v
            ro = o_intra + o_inter
            ro = ro * lax.rsqrt(jnp.mean(ro * ro, axis=-1, keepdims=True) + EPS)
            o_ref[0, b:b + 1, vd] = (jax.nn.silu(gate_ref[0, b:b + 1, vd]) * ro).astype(o_ref.dtype)


def _ret_sample(qt, kt, cos_col, sin_col, v, gate, state, qdec, kdec, cdec):
    nbt, _, _, bt = qt.shape
    smem = pl.BlockSpec(memory_space=pltpu.SMEM)
    st_spec = pl.BlockSpec((bt, N_RET_HEADS, RET_KDIM, RET_VDIM), lambda i: (i, 0, 0, 0))
    qk_spec = pl.BlockSpec((1, N_RET_HEADS, RET_KDIM, bt), lambda i: (i, 0, 0, 0))
    row_spec = pl.BlockSpec((1, bt, RET_V_WIDTH), lambda i: (i, 0, 0))
    col_spec = pl.BlockSpec(cos_col.shape, lambda i: (0, 0))
    return pl.pallas_call(
        _ret_sample_kernel,
        out_shape=(jax.ShapeDtypeStruct((nbt, bt, RET_V_WIDTH), BF16),
                   jax.ShapeDtypeStruct(state.shape, F32)),
        grid=(nbt,),
        in_specs=[qk_spec, qk_spec, col_spec, col_spec, row_spec, row_spec, st_spec, smem, smem, smem],
        out_specs=(row_spec, st_spec),
        compiler_params=_params(1),
        name="ret_sample",
    )(qt, kt, cos_col, sin_col, v, gate, state, qdec, kdec, cdec)


NPF = np.float32


def _t5_bucket(dist):
    n = np.maximum(dist, 0)
    max_exact = N_BUCKETS // 2
    nf = np.maximum(n, 1).astype(NPF)
    large = max_exact + (np.log(nf / NPF(max_exact)) / NPF(math.log(MAX_DISTANCE / max_exact))
                         * NPF(N_BUCKETS - max_exact)).astype(np.int32)
    return np.where(n < max_exact, n, np.minimum(large, N_BUCKETS - 1)).astype(np.int32)


def _rope_tables(pos):
    half = RET_KDIM // 2
    inv = NPF(ROPE_BASE) ** (-np.arange(half, dtype=NPF) / NPF(half))
    ang = pos.astype(NPF)[:, None] * inv[None]
    return np.cos(ang), np.sin(ang)


def _layer(xp, xs, pp, ps, cache_k, cache_v, state_ret, state_conv, rel_bias, lp, n_seq):
    (g_mix, w_in, g_q, g_k, sinks, w_out, g_ffn, w_up, conv_w, conv_b, w_down,
     g_ple, w_ple_gate, w_ple_proj) = lp
    mp = xp.shape[0]
    ms = xs.shape[0]
    seq = mp // n_seq
    win = cache_k.shape[1]

    row = lambda a: a.reshape(1, -1)

    log_decay = np.log(NPF(1.0) - NPF(2.0) ** (NPF(-5.0) - np.arange(N_RET_HEADS, dtype=NPF)))
    idx = np.arange(RET_CHUNK, dtype=NPF)
    diff = idx[:, None] - idx[None, :]
    dmask = np.where(diff[None] >= 0, np.exp(diff[None] * log_decay[:, None, None]), NPF(0.0))
    q_dec = np.exp((idx + NPF(1.0))[:, None] * log_decay[None])
    k_dec = np.exp((NPF(RET_CHUNK - 1.0) - idx)[:, None] * log_decay[None])
    qdec_t = np.broadcast_to(q_dec.T[:, :, None], (N_RET_HEADS, RET_CHUNK, RET_KDIM))
    kdec_t = np.broadcast_to(k_dec.T[:, :, None], (N_RET_HEADS, RET_CHUNK, RET_KDIM))
    cdec = np.exp(NPF(RET_CHUNK) * log_decay)
    cos_p, sin_p = _rope_tables(np.arange(seq, dtype=np.int32))
    cosf = np.concatenate([cos_p, cos_p], axis=1)
    sinf = np.concatenate([-sin_p, sin_p], axis=1)
    one = np.arange(1, dtype=NPF)
    qdec_s = np.exp((one + NPF(1.0))[:, None] * log_decay[None])[0]
    kdec_s = np.exp((NPF(1.0 - 1.0) - one)[:, None] * log_decay[None])[0]
    cdec_s = np.exp(NPF(1.0) * log_decay)
    cos_s, sin_s = _rope_tables(PAST_LEN + np.arange(1, dtype=np.int32))
    cos_col, sin_col = cos_s.reshape(-1, 1), sin_s.reshape(-1, 1)

    qi = np.arange(ATTN_BLOCK, dtype=np.int32)
    ki = np.arange(2 * ATTN_BLOCK, dtype=np.int32) - ATTN_BLOCK
    dist = qi[:, None] - ki[None, :]
    bucket_tile = np.where((dist >= 0) & (dist <= WINDOW), _t5_bucket(dist), -1).astype(np.int32)
    dist_s = win - np.arange(2 * ATTN_BLOCK, dtype=np.int32)
    bucket_row = np.where((dist_s >= 0) & (dist_s <= WINDOW), _t5_bucket(dist_s), -1).astype(np.int32).reshape(1, -1)

    tmp, tms = _row_tile(mp), _row_tile(ms)
    proj_s, w_in_b = _matmul(_norm(xs, row(g_mix), tms), w_in, tms, COL_TILE, IN_SRC_BLOCKS)
    proj_p, = _matmul(_norm(xp, row(g_mix), NORM_TILE), w_in_b, min(mp, IN_ROW_TILE), COL_TILE)

    attn_p, kn_p = _attn_prompt(proj_p, bucket_tile, rel_bias, sinks, row(g_q), row(g_k), n_seq)
    ret_p, ret_state_p = _ret_prompt(proj_p, cosf, sinf, dmask, qdec_t, kdec_t, cdec, n_seq)

    bt_a = 8
    attn_s, ck_new, cv_new = _attn_sample(
        proj_s[:, COL_AQ:COL_AQ + ATTN_WIDTH].reshape(ms, N_ATTN_HEADS, HEAD_DIM),
        proj_s[:, COL_AK:COL_AK + KV_WIDTH].reshape(ms, 1, KV_WIDTH),
        proj_s[:, COL_AV:COL_AV + KV_WIDTH].reshape(ms, 1, KV_WIDTH),
        cache_k.reshape(ms, win, KV_WIDTH), cache_v.reshape(ms, win, KV_WIDTH),
        bucket_row, rel_bias.T, sinks.reshape(-1, 1), row(g_q), jnp.tile(g_k, N_KV_HEADS).reshape(1, -1), bt_a)
    attn_s = attn_s.reshape(ms, ATTN_WIDTH)

    bt_r = 4
    to_cols = lambda a: a.reshape(ms // bt_r, bt_r, N_RET_HEADS, RET_KDIM).transpose(0, 2, 3, 1)
    ret_s, ret_state_s = _ret_sample(
        to_cols(proj_s[:, COL_RQ:COL_RQ + RET_QK_WIDTH]), to_cols(proj_s[:, COL_RK:COL_RK + RET_QK_WIDTH]),
        cos_col, sin_col,
        proj_s[:, COL_RV:COL_RV + RET_V_WIDTH].reshape(ms // bt_r, bt_r, RET_V_WIDTH),
        proj_s[:, COL_RG:COL_RG + RET_V_WIDTH].reshape(ms // bt_r, bt_r, RET_V_WIDTH),
        state_ret, qdec_s, kdec_s, cdec_s)
    ret_s = ret_s.reshape(ms, RET_V_WIDTH)

    xs, w_out_b = _out_proj(attn_s, ret_s, w_out, xs, tms, COL_TILE)
    xp, = _out_proj(attn_p, ret_p, w_out_b, xp, tmp, OUT_COL_TILE)

    act_s, conv_s, w_gate_b, w_value_b = _convglu_sample(_norm(xs, row(g_ffn), tms), w_up, conv_w, row(conv_b),
                                                         state_conv)
    act_p, conv_p = _convglu_prompt(_norm(xp, row(g_ffn), NORM_TILE), w_gate_b, w_value_b, conv_w, row(conv_b),
                                    n_seq, UP_ROW_TILE)
    for part in range(DOWN_K_PARTS):
        last = part == DOWN_K_PARTS - 1
        xs, w_down_b = _down_proj(act_s, w_down, xs, tms, COL_TILE, DOWN_K_PARTS, part)
        xp, *stats_p = _down_proj(act_p, w_down_b, xp, tmp, COL_TILE, DOWN_K_PARTS, part,
                                  row(g_ple) if last else None)
    hp, ssq_p = stats_p

    xs, w_pgate_b, w_pproj_b = _ple(xs, _norm(xs, row(g_ple), tms), None, w_ple_gate, ps, w_ple_proj, tms, COL_TILE)
    xp, = _ple(xp, hp, ssq_p, w_pgate_b, pp, w_pproj_b, tmp, COL_TILE)

    kp_new = kn_p.reshape(n_seq, seq, N_KV_HEADS, HEAD_DIM)[:, -WINDOW:]
    vp_new = proj_p[:, COL_AV:COL_AV + KV_WIDTH].reshape(n_seq, seq, N_KV_HEADS, HEAD_DIM)[:, -WINDOW:]
    ks_new = ck_new.reshape(ms, win, N_KV_HEADS, HEAD_DIM)
    vs_new = cv_new.reshape(ms, win, N_KV_HEADS, HEAD_DIM)
    return xp, xs, kp_new, vp_new, ret_state_p, conv_p, ks_new, vs_new, ret_state_s, conv_s


def kernel(x_prompt, x_sample, p_prompt, p_sample, cache_win_k, cache_win_v, state_ret, state_conv, rel_bias, g_mix, w_in, g_q, g_k, sinks, w_out, g_ffn, w_up, conv_w, conv_b, w_down, g_ple, w_ple_gate, w_ple_proj):
    depth = g_mix.shape[0]
    n_seq, seq, d = x_prompt.shape
    nb, dec_seq, _ = x_sample.shape
    assert dec_seq == 1 and seq % 512 == 0 and d == D_MODEL
    xp = x_prompt.reshape(n_seq * seq, d)
    xs = x_sample.reshape(nb, d)
    outs = [[] for _ in range(8)]
    for l in range(depth):
        lp = (g_mix[l], w_in[l], g_q[l], g_k[l], sinks[l], w_out[l], g_ffn[l], w_up[l], conv_w[l],
              conv_b[l], w_down[l], g_ple[l], w_ple_gate[l], w_ple_proj[l])
        res = _layer(xp, xs, p_prompt[l].reshape(n_seq * seq, -1), p_sample[l].reshape(nb, -1),
                     cache_win_k[l], cache_win_v[l], state_ret[l], state_conv[l], rel_bias, lp, n_seq)
        xp, xs = res[0], res[1]
        for o, r in zip(outs, res[2:]):
            o.append(r)
    stacked = [jnp.stack(o) for o in outs]
    return (xp.reshape(n_seq, seq, d), xs.reshape(nb, 1, d), *stacked)
```

```python
import functools
import math

import jax
import jax.numpy as jnp
import numpy as np
from jax import lax
from jax.experimental import pallas as pl
from jax.experimental.pallas import tpu as pltpu

F32 = jnp.float32
BF16 = jnp.bfloat16

D_MODEL = 4096
HEAD_DIM = 64
N_ATTN_HEADS = 32
N_KV_HEADS = 4
GQA_GROUP = N_ATTN_HEADS // N_KV_HEADS
WINDOW = 128
ATTN_BLOCK = 128
N_BUCKETS = 32
MAX_DISTANCE = 128
N_RET_HEADS = 8
RET_KDIM = 128
RET_VDIM = 256
RET_CHUNK = 128
ROPE_BASE = 10000.0
D_FF = 11008
CONV_W = 3
EPS = 1e-6

ATTN_WIDTH = N_ATTN_HEADS * HEAD_DIM
KV_WIDTH = N_KV_HEADS * HEAD_DIM
RET_QK_WIDTH = N_RET_HEADS * RET_KDIM
RET_V_WIDTH = N_RET_HEADS * RET_VDIM
IN_SIZES = (ATTN_WIDTH, KV_WIDTH, KV_WIDTH, RET_QK_WIDTH, RET_QK_WIDTH, RET_V_WIDTH, RET_V_WIDTH)
IN_WIDTH = sum(IN_SIZES)
IN_SPLITS = tuple(sum(IN_SIZES[:n + 1]) for n in range(len(IN_SIZES) - 1))
PAST_LEN = 8192

COL_AQ = 0
COL_RV = ATTN_WIDTH
COL_RG = COL_RV + RET_V_WIDTH
COL_RQ = COL_RG + RET_V_WIDTH
COL_RK = COL_RQ + RET_QK_WIDTH
COL_AK = COL_RK + RET_QK_WIDTH
COL_AV = COL_AK + KV_WIDTH

VMEM_LIMIT_BYTES = 56 * 1024 * 1024
FF_TILE = 256
N_FF_TILES = D_FF // FF_TILE
NORM_ROWS = 32
NORM_TILE = 512
OUT_COL_TILE = 1024
STAT_LANES = 128
IN_ROW_TILE = 2048
ROW_TILE = 1024
COL_TILE = 512
UP_ROW_TILE = 2048
CONV_CHUNK = 1024
DOWN_K_PARTS = 2


def _row_tile(m):
    return min(m, ROW_TILE)


def _in_src_blocks():
    start = dict(zip(("aq", "ak", "av", "rq", "rk", "rv", "rg"), (0,) + IN_SPLITS))
    order = (("aq", ATTN_WIDTH), ("rv", RET_V_WIDTH), ("rg", RET_V_WIDTH), ("rq", RET_QK_WIDTH),
             ("rk", RET_QK_WIDTH), ("ak", KV_WIDTH), ("av", KV_WIDTH))
    cols = [c for name, width in order for c in range(start[name], start[name] + width, KV_WIDTH)]
    per_tile = COL_TILE // KV_WIDTH
    blocks = []
    for b in range(0, len(cols), per_tile):
        group = cols[b:b + per_tile]
        assert group[0] % COL_TILE == 0 and all(c == group[0] + u * KV_WIDTH for u, c in enumerate(group))
        blocks.append(group[0] // COL_TILE)
    return tuple(blocks)


IN_SRC_BLOCKS = _in_src_blocks()


def _params(n_axes):
    return pltpu.CompilerParams(dimension_semantics=("arbitrary",) * n_axes,
                                vmem_limit_bytes=VMEM_LIMIT_BYTES)


def _rms(x, g):
    y = x * lax.rsqrt(jnp.mean(x * x, axis=-1, keepdims=True) + EPS)
    return y * g


def _norm_rows_to_bf16(x_ref, g_ref, h_ref):
    def body(c, carry):
        r0 = pl.multiple_of(c * NORM_ROWS, NORM_ROWS)
        h_ref[pl.ds(r0, NORM_ROWS), :] = _rms(x_ref[pl.ds(r0, NORM_ROWS), :], g_ref[...]).astype(h_ref.dtype)
        return carry
    lax.fori_loop(0, x_ref.shape[0] // NORM_ROWS, body, 0)


def _norm(x, g, tr):
    m, k = x.shape
    return pl.pallas_call(
        _norm_rows_to_bf16,
        out_shape=jax.ShapeDtypeStruct((m, k), BF16),
        grid=(m // tr,),
        in_specs=[pl.BlockSpec((tr, k), lambda i: (i, 0)),
                  pl.BlockSpec((1, k), lambda i: (0, 0))],
        out_specs=pl.BlockSpec((tr, k), lambda i: (i, 0)),
        compiler_params=_params(1),
        name="rmsnorm",
    )(x, g)


def _mxu_weights(w_ref, wb_ref):
    if not wb_ref:
        return w_ref[...]
    wb = w_ref[...].astype(BF16)
    wb_ref[0][...] = wb
    return wb


def _mm_kernel(h_ref, w_ref, o_ref, *wb_ref):
    o_ref[...] = jnp.dot(h_ref[...], _mxu_weights(w_ref, wb_ref), preferred_element_type=F32)


def _matmul(h, w, tm, tn, src_blocks=None):
    m, k = h.shape
    n = w.shape[1]
    emit = w.dtype != BF16
    if emit:
        w_map = lambda i, j: (0, sum(jnp.where(j == d, s, 0) for d, s in enumerate(src_blocks)))
    else:
        w_map = lambda i, j: (0, j)
    out_shape = [jax.ShapeDtypeStruct((m, n), F32)]
    out_specs = [pl.BlockSpec((tm, tn), lambda i, j: (i, j))]
    if emit:
        assert m == tm
        out_shape.append(jax.ShapeDtypeStruct((k, n), BF16))
        out_specs.append(pl.BlockSpec((k, tn), lambda i, j: (0, j)))
    return pl.pallas_call(
        _mm_kernel,
        out_shape=out_shape,
        grid=(m // tm, n // tn),
        in_specs=[pl.BlockSpec((tm, k), lambda i, j: (i, 0)),
                  pl.BlockSpec((k, tn), w_map)],
        out_specs=out_specs,
        compiler_params=_params(2),
        name="in_proj",
    )(h, w)


def _row_scale(ssq_ref, width, n_cols):
    r = lax.rsqrt(ssq_ref[...] / width + EPS)
    return jnp.concatenate([r] * (n_cols // STAT_LANES), axis=1)


def _residual_mm_kernel(*refs, n_lhs, emit_weights, norm_stats):
    lhs_refs = refs[:n_lhs]
    w_ref, x_ref = refs[n_lhs], refs[n_lhs + 1]
    rest = list(refs[n_lhs + 2:])
    g_ref = rest.pop(0) if norm_stats else None
    o_ref = rest.pop(0)
    wb_ref = (rest.pop(0),) if emit_weights else ()
    lhs = jnp.concatenate([r[...] for r in lhs_refs], axis=1) if n_lhs > 1 else lhs_refs[0][...]
    o = x_ref[...] + jnp.dot(lhs, _mxu_weights(w_ref, wb_ref), preferred_element_type=F32)
    o_ref[...] = o
    if norm_stats:
        hb_ref, ssq_ref = rest
        hb_ref[...] = (o * g_ref[...]).astype(hb_ref.dtype)
        part = jnp.broadcast_to(jnp.sum(o * o, axis=-1, keepdims=True), ssq_ref.shape)
        j = pl.program_id(1)

        @pl.when(j == 0)
        def _():
            ssq_ref[...] = part

        @pl.when(j > 0)
        def _():
            ssq_ref[...] += part


def _residual_matmul(lhs, lhs_specs, w, w_spec, wb_shape, x, g_next, tm, tn, name):
    m, n = x.shape
    emit = w.dtype != BF16
    stats = g_next is not None
    tile = pl.BlockSpec((tm, tn), lambda i, j: (i, j))
    operands = [*lhs, w, x]
    in_specs = [*lhs_specs, w_spec, tile]
    out_shape = [jax.ShapeDtypeStruct((m, n), F32)]
    out_specs = [tile]
    if stats:
        operands.append(g_next)
        in_specs.append(pl.BlockSpec((1, tn), lambda i, j: (0, j)))
    if emit:
        assert m == tm
        out_shape.append(jax.ShapeDtypeStruct(wb_shape, BF16))
        out_specs.append(pl.BlockSpec((wb_shape[0], tn), lambda i, j: (0, j)))
    if stats:
        out_shape += [jax.ShapeDtypeStruct((m, n), BF16), jax.ShapeDtypeStruct((m, STAT_LANES), F32)]
        out_specs += [tile, pl.BlockSpec((tm, STAT_LANES), lambda i, j: (i, 0))]
    return pl.pallas_call(
        functools.partial(_residual_mm_kernel, n_lhs=len(lhs), emit_weights=emit, norm_stats=stats),
        out_shape=out_shape,
        grid=(m // tm, n // tn),
        in_specs=in_specs,
        out_specs=out_specs,
        compiler_params=_params(2),
        name=name,
    )(*operands)


def _out_proj(a, r, w, x, tm, tn, g_next=None):
    ka, kr = a.shape[1], r.shape[1]
    lhs_specs = [pl.BlockSpec((tm, ka), lambda i, j: (i, 0)), pl.BlockSpec((tm, kr), lambda i, j: (i, 0))]
    w_spec = pl.BlockSpec((ka + kr, tn), lambda i, j: (0, j))
    return _residual_matmul([a, r], lhs_specs, w, w_spec, w.shape, x, g_next, tm, tn, "out_proj")


def _down_proj(a, w, x, tm, tn, k_parts, part, g_next=None):
    k = a.shape[1] // k_parts
    w_part = part if w.dtype != BF16 else 0
    lhs_specs = [pl.BlockSpec((tm, k), lambda i, j: (i, part))]
    w_spec = pl.BlockSpec((k, tn), lambda i, j: (w_part, j))
    return _residual_matmul([a], lhs_specs, w, w_spec, (k, w.shape[1]), x, g_next, tm, tn, "down_proj")


def _ple_kernel(*refs, emit_weights, row_scaled):
    h_ref, wg_ref, p_ref, wp_ref, x_ref = refs[:5]
    rest = list(refs[5:])
    ssq_ref = rest.pop(0) if row_scaled else None
    o_ref = rest.pop(0)
    wb_refs = tuple(rest) if emit_weights else ()
    z = jnp.dot(h_ref[...], _mxu_weights(wg_ref, wb_refs[:1]), preferred_element_type=F32)
    if row_scaled:
        z = z * _row_scale(ssq_ref, h_ref.shape[1], z.shape[1])
    pp = jnp.dot(p_ref[...].astype(BF16), _mxu_weights(wp_ref, wb_refs[1:]), preferred_element_type=F32)
    o_ref[...] = x_ref[...] + jax.nn.sigmoid(z) * pp


def _ple(x, h, ssq, wg, p, wp, tm, tn):
    m, k = h.shape
    n = wg.shape[1]
    kp = p.shape[1]
    emit = wg.dtype != BF16
    scaled = ssq is not None
    operands = [h, wg, p, wp, x]
    in_specs = [pl.BlockSpec((tm, k), lambda i, j: (i, 0)),
                pl.BlockSpec((k, tn), lambda i, j: (0, j)),
                pl.BlockSpec((tm, kp), lambda i, j: (i, 0)),
                pl.BlockSpec((kp, tn), lambda i, j: (0, j)),
                pl.BlockSpec((tm, tn), lambda i, j: (i, j))]
    if scaled:
        operands.append(ssq)
        in_specs.append(pl.BlockSpec((tm, STAT_LANES), lambda i, j: (i, 0)))
    out_shape = [jax.ShapeDtypeStruct((m, n), F32)]
    out_specs = [pl.BlockSpec((tm, tn), lambda i, j: (i, j))]
    if emit:
        assert m == tm
        out_shape += [jax.ShapeDtypeStruct(wg.shape, BF16), jax.ShapeDtypeStruct(wp.shape, BF16)]
        out_specs += [pl.BlockSpec((k, tn), lambda i, j: (0, j)), pl.BlockSpec((kp, tn), lambda i, j: (0, j))]
    return pl.pallas_call(
        functools.partial(_ple_kernel, emit_weights=emit, row_scaled=scaled),
        out_shape=out_shape,
        grid=(m // tm, n // tn),
        in_specs=in_specs,
        out_specs=out_specs,
        compiler_params=_params(2),
        name="ple",
    )(*operands)


def _gelu_erf(x):
    return 0.5 * x * (1.0 + lax.erf(x * math.sqrt(0.5)))


def _conv_taps(cb_ref, cw_ref, um2, um1, u):
    c = cb_ref[...] + cw_ref[0:1, :] * um2
    c = c + cw_ref[1:2, :] * um1
    return c + cw_ref[2:3, :] * u


def _convglu_prompt_kernel(h_ref, w_ref, cwg_ref, cwv_ref, cbg_ref, cbv_ref,
                           a_ref, sg_ref, sv_ref, carry_g_ref, carry_v_ref, *, tiles_per_seq):
    i = pl.program_id(0)
    j = pl.program_id(1)
    tm = h_ref.shape[0]
    rows = min(tm, CONV_CHUNK)
    seq_start = (i % tiles_per_seq) == 0
    sub = 8
    row = lax.broadcasted_iota(jnp.int32, (sub, 1), 0)

    def half(u, cw_ref, cb_ref, prev):
        r1 = pltpu.roll(u, 1, axis=0)
        r2 = pltpu.roll(u, 2, axis=0)
        head1 = jnp.where(row == 0, prev[1:2, :], r1[:sub, :])
        head2 = jnp.where(row == 0, prev[0:1, :], jnp.where(row == 1, prev[1:2, :], r2[:sub, :]))
        um1 = jnp.concatenate([head1, r1[sub:, :]], axis=0)
        um2 = jnp.concatenate([head2, r2[sub:, :]], axis=0)
        return _conv_taps(cb_ref, cw_ref, um2, um1, u), u[rows - (CONV_W - 1):, :]

    prev_g = jnp.where(seq_start, 0.0, carry_g_ref[j])
    prev_v = jnp.where(seq_start, 0.0, carry_v_ref[j])
    for c in range(tm // rows):
        rs = slice(c * rows, (c + 1) * rows)
        u = jnp.dot(h_ref[rs, :], w_ref[...], preferred_element_type=F32)
        cg, prev_g = half(u[:, :FF_TILE], cwg_ref, cbg_ref, prev_g)
        cv, prev_v = half(u[:, FF_TILE:], cwv_ref, cbv_ref, prev_v)
        a_ref[rs, :] = (_gelu_erf(cg) * cv).astype(a_ref.dtype)
    carry_g_ref[j] = prev_g
    carry_v_ref[j] = prev_v
    sg_ref[0] = prev_g
    sv_ref[0] = prev_v


def _convglu_prompt(h, w_pairs, conv_w, conv_b, n_seq, tm):
    m, k = h.shape
    seq = m // n_seq
    tiles_per_seq = seq // tm
    nt = N_FF_TILES
    tn = FF_TILE
    a, sg, sv = pl.pallas_call(
        functools.partial(_convglu_prompt_kernel, tiles_per_seq=tiles_per_seq),
        out_shape=(jax.ShapeDtypeStruct((m, D_FF), BF16),
                   jax.ShapeDtypeStruct((m // tm, CONV_W - 1, D_FF), F32),
                   jax.ShapeDtypeStruct((m // tm, CONV_W - 1, D_FF), F32)),
        grid=(m // tm, nt),
        in_specs=[pl.BlockSpec((tm, k), lambda i, j: (i, 0)),
                  pl.BlockSpec((k, 2 * tn), lambda i, j: (0, j)),
                  pl.BlockSpec((CONV_W, tn), lambda i, j: (0, j)),
                  pl.BlockSpec((CONV_W, tn), lambda i, j: (0, j + nt)),
                  pl.BlockSpec((1, tn), lambda i, j: (0, j)),
                  pl.BlockSpec((1, tn), lambda i, j: (0, j + nt))],
        out_specs=(pl.BlockSpec((tm, tn), lambda i, j: (i, j)),
                   pl.BlockSpec((1, CONV_W - 1, tn), lambda i, j: (i, 0, j)),
                   pl.BlockSpec((1, CONV_W - 1, tn), lambda i, j: (i, 0, j))),
        scratch_shapes=[pltpu.VMEM((nt, CONV_W - 1, tn), F32),
                        pltpu.VMEM((nt, CONV_W - 1, tn), F32)],
        compiler_params=_params(2),
        name="convglu_prompt",
    )(h, w_pairs, conv_w, conv_w, conv_b, conv_b)
    tails = jnp.concatenate([sg, sv], axis=-1)
    return a, tails[tiles_per_seq - 1::tiles_per_seq]


def _convglu_sample_kernel(h_ref, wg_ref, wv_ref, cwg_ref, cwv_ref, cbg_ref, cbv_ref,
                           p0g_ref, p0v_ref, p1g_ref, p1v_ref, a_ref, ug_ref, uv_ref, wb_ref):
    def half(w_ref, wb_cols, cw_ref, cb_ref, p0_ref, p1_ref, u_ref):
        u = jnp.dot(h_ref[...], _mxu_weights(w_ref, (wb_ref.at[:, wb_cols],)), preferred_element_type=F32)
        u_ref[...] = u
        return _conv_taps(cb_ref, cw_ref, p0_ref[...], p1_ref[...], u)

    cg = half(wg_ref, slice(0, FF_TILE), cwg_ref, cbg_ref, p0g_ref, p1g_ref, ug_ref)
    cv = half(wv_ref, slice(FF_TILE, 2 * FF_TILE), cwv_ref, cbv_ref, p0v_ref, p1v_ref, uv_ref)
    a_ref[...] = (_gelu_erf(cg) * cv).astype(a_ref.dtype)


def _convglu_sample(h, w_up, conv_w, conv_b, state_conv):
    m, k = h.shape
    nt = N_FF_TILES
    tn = FF_TILE
    prev = state_conv.reshape(m, (CONV_W - 1) * 2 * D_FF)
    col = lambda off: (lambda j: (0, j + off))
    a, ug, uv, w_pairs = pl.pallas_call(
        _convglu_sample_kernel,
        out_shape=(jax.ShapeDtypeStruct((m, D_FF), BF16),
                   jax.ShapeDtypeStruct((m, D_FF), F32),
                   jax.ShapeDtypeStruct((m, D_FF), F32),
                   jax.ShapeDtypeStruct((k, 2 * D_FF), BF16)),
        grid=(nt,),
        in_specs=[pl.BlockSpec((m, k), lambda j: (0, 0)),
                  pl.BlockSpec((k, tn), col(0)),
                  pl.BlockSpec((k, tn), col(nt)),
                  pl.BlockSpec((CONV_W, tn), col(0)),
                  pl.BlockSpec((CONV_W, tn), col(nt)),
                  pl.BlockSpec((1, tn), col(0)),
                  pl.BlockSpec((1, tn), col(nt)),
                  pl.BlockSpec((m, tn), col(0)),
                  pl.BlockSpec((m, tn), col(nt)),
                  pl.BlockSpec((m, tn), col(2 * nt)),
                  pl.BlockSpec((m, tn), col(3 * nt))],
        out_specs=(pl.BlockSpec((m, tn), col(0)),
                   pl.BlockSpec((m, tn), col(0)),
                   pl.BlockSpec((m, tn), col(0)),
                   pl.BlockSpec((k, 2 * tn), col(0))),
        compiler_params=_params(1),
        name="convglu_sample",
    )(h, w_up, w_up, conv_w, conv_w, conv_b, conv_b, prev, prev, prev, prev)
    u = jnp.concatenate([ug, uv], axis=-1)
    conv_new = jnp.stack([state_conv[:, 1, :], u], axis=1)
    return a, conv_new, w_pairs


def _segment_mean_square(x, seg_ones):
    sq = x * x
    hi = sq.astype(BF16)
    lo = (sq - hi.astype(F32)).astype(BF16)
    total = (jnp.dot(hi, seg_ones, preferred_element_type=F32)
             + jnp.dot(lo, seg_ones, preferred_element_type=F32))
    return total / HEAD_DIM


def _attn_prompt_kernel(q_ref, kc_ref, vc_ref, bucket_ref, seg_ones_ref, rb_ref, sinks_ref, gq_ref, gk_ref,
                        o_ref, kn_ref, bias_ref, kdup_ref, vdup_ref, *, blocks_per_seq):
    r = pl.program_id(0)
    blk = ATTN_BLOCK
    pair = 2 * HEAD_DIM
    n_bias_rows = N_ATTN_HEADS * blk
    seq_start = (r % blocks_per_seq) == 0

    @pl.when(r == 0)
    def _():
        bucket = bucket_ref[...]
        prev_cols = lax.broadcasted_iota(jnp.int32, bucket.shape, 1) < blk

        def per_head(h, carry):
            acc = jnp.full(bucket.shape, -jnp.inf, F32)
            for b in range(N_BUCKETS):
                acc = jnp.where(bucket == b, rb_ref[b, h], acc)
            row0 = pl.multiple_of(h * blk, blk)
            bias_ref[pl.ds(row0, blk), :] = acc
            bias_ref[pl.ds(n_bias_rows + row0, blk), :] = jnp.where(prev_cols, -jnp.inf, acc)
            return carry
        lax.fori_loop(0, N_ATTN_HEADS, per_head, 0)
        vdup_ref[:, :, pair:] = jnp.ones((N_KV_HEADS, 2 * blk, pair), BF16)

    @pl.when(seq_start)
    def _():
        kdup_ref[:, :blk, :] = jnp.zeros((N_KV_HEADS, blk, pair), BF16)
        vdup_ref[:, :blk, :pair] = jnp.zeros((N_KV_HEADS, blk, pair), BF16)

    @pl.when(jnp.logical_not(seq_start))
    def _():
        kdup_ref[:, :blk, :] = kdup_ref[:, blk:, :]
        vdup_ref[:, :blk, :pair] = vdup_ref[:, blk:, :pair]

    seg_ones = seg_ones_ref[...]
    low_half = lax.broadcasted_iota(jnp.int32, (blk, pair), 1) < HEAD_DIM

    def duplicate_half(x, odd):
        swapped = pltpu.roll(x, HEAD_DIM, axis=1)
        return jnp.where(low_half, swapped, x) if odd else jnp.where(low_half, x, swapped)

    kc = kc_ref[...]
    kn = kc * lax.rsqrt(_segment_mean_square(kc, seg_ones) + EPS) * gk_ref[...]
    kn_ref[...] = kn
    for j in range(N_KV_HEADS):
        col = slice((j // 2) * pair, (j // 2 + 1) * pair)
        kdup_ref[j, blk:, :] = duplicate_half(kn[:, col], j % 2).astype(BF16)
        vdup_ref[j, blk:, :pair] = duplicate_half(vc_ref[:, col], j % 2).astype(BF16)

    bias_base = jnp.where(seq_start, n_bias_rows, 0)
    chunk = 2 * pair
    heads_per_chunk = chunk // HEAD_DIM
    for j in range(N_KV_HEADS):
        heads = range(j * GQA_GROUP, (j + 1) * GQA_GROUP)
        lhs = []
        for c in range(j * GQA_GROUP // heads_per_chunk, (j + 1) * GQA_GROUP // heads_per_chunk):
            qc = q_ref[:, c * chunk:(c + 1) * chunk]
            qn = qc * lax.rsqrt(_segment_mean_square(qc, seg_ones) + EPS) * (gq_ref[...] * HEAD_DIM ** -0.5)
            for p in range(chunk // pair):
                qp = qn[:, p * pair:(p + 1) * pair]
                lhs.append(jnp.where(low_half, qp, 0.0).astype(BF16))
                lhs.append(jnp.where(low_half, 0.0, qp).astype(BF16))
        s = lax.dot_general(jnp.concatenate(lhs, axis=0), kdup_ref[j], (((1,), (1,)), ((), ())),
                            preferred_element_type=F32)
        exps, sink_terms = [], []
        for g, h in enumerate(heads):
            row0 = pl.multiple_of(bias_base + h * blk, blk)
            sg = s[g * blk:(g + 1) * blk, :] + bias_ref[pl.ds(row0, blk), :]
            sink = sinks_ref[h]
            m = jnp.maximum(jnp.max(sg, axis=-1, keepdims=True), sink)
            exps.append(jnp.exp(sg - m).astype(BF16))
            sink_terms.append(jnp.exp(sink - m))
        o = jnp.dot(jnp.concatenate(exps, axis=0), vdup_ref[j], preferred_element_type=F32)
        for g in range(0, GQA_GROUP, 2):
            even = o[g * blk:(g + 1) * blk, :]
            odd = o[(g + 1) * blk:(g + 2) * blk, :]
            num = jnp.where(low_half, even[:, :pair], odd[:, :pair])
            den = (jnp.where(low_half, even[:, pair:], odd[:, pair:])
                   + jnp.where(low_half, sink_terms[g], sink_terms[g + 1]))
            c0 = (j * GQA_GROUP + g) * HEAD_DIM
            o_ref[:, c0:c0 + pair] = (num / den).astype(o_ref.dtype)


def _attn_prompt(proj, bucket_tile, rel_bias, sinks, g_q, g_k, n_seq):
    m = proj.shape[0]
    blk = ATTN_BLOCK
    nb = m // blk
    blocks_per_seq = nb // n_seq
    lanes = 4 * HEAD_DIM
    seg = np.arange(lanes) // HEAD_DIM
    seg_ones = (seg[:, None] == seg[None, :]).astype(BF16)
    tile4 = lambda g: jnp.tile(g.reshape(-1), lanes // HEAD_DIM).reshape(1, lanes)
    smem = pl.BlockSpec(memory_space=pltpu.SMEM)
    return pl.pallas_call(
        functools.partial(_attn_prompt_kernel, blocks_per_seq=blocks_per_seq),
        out_shape=(jax.ShapeDtypeStruct((m, ATTN_WIDTH), BF16),
                   jax.ShapeDtypeStruct((m, KV_WIDTH), F32)),
        grid=(nb,),
        in_specs=[pl.BlockSpec((blk, ATTN_WIDTH), lambda r: (r, COL_AQ // ATTN_WIDTH)),
                  pl.BlockSpec((blk, KV_WIDTH), lambda r: (r, COL_AK // KV_WIDTH)),
                  pl.BlockSpec((blk, KV_WIDTH), lambda r: (r, COL_AV // KV_WIDTH)),
                  pl.BlockSpec((blk, 2 * blk), lambda r: (0, 0)),
                  pl.BlockSpec((lanes, lanes), lambda r: (0, 0)),
                  smem, smem,
                  pl.BlockSpec((1, lanes), lambda r: (0, 0)),
                  pl.BlockSpec((1, lanes), lambda r: (0, 0))],
        out_specs=(pl.BlockSpec((blk, ATTN_WIDTH), lambda r: (r, 0)),
                   pl.BlockSpec((blk, KV_WIDTH), lambda r: (r, 0))),
        scratch_shapes=[pltpu.VMEM((2 * N_ATTN_HEADS * blk, 2 * blk), F32),
                        pltpu.VMEM((N_KV_HEADS, 2 * blk, 2 * HEAD_DIM), BF16),
                        pltpu.VMEM((N_KV_HEADS, 2 * blk, 4 * HEAD_DIM), BF16)],
        compiler_params=_params(1),
        name="attn_prompt",
    )(proj, proj, proj, bucket_tile, seg_ones, rel_bias, sinks, tile4(g_q), tile4(g_k))


def _attn_sample_kernel(q_ref, knew_ref, vnew_ref, ck_ref, cv_ref, bucket_ref, rbt_ref, sinks_ref,
                        gq_ref, gk_ref, o_ref, cko_ref, cvo_ref):
    bt = q_ref.shape[0]
    win = ck_ref.shape[1]
    bucket = bucket_ref[...]
    bias = jnp.full((N_ATTN_HEADS, bucket.shape[1]), -jnp.inf, F32)
    for b in range(N_BUCKETS):
        bias = jnp.where(bucket == b, rbt_ref[:, b:b + 1], bias)
    bias_c = bias[:, :win]
    bias_n = bias[:, win:win + 1]
    sink = sinks_ref[...]
    head_group = jnp.right_shift(lax.broadcasted_iota(jnp.int32, (N_ATTN_HEADS, HEAD_DIM), 0),
                                 int(math.log2(GQA_GROUP)))
    lane_group = jnp.right_shift(lax.broadcasted_iota(jnp.int32, (1, KV_WIDTH), 1),
                                 int(math.log2(HEAD_DIM)))
    gk = gk_ref[...]
    scale = HEAD_DIM ** -0.5

    for b in range(bt):
        k_row = knew_ref[b]
        k_sq = k_row * k_row
        inv = jnp.zeros_like(k_row)
        for j in range(N_KV_HEADS):
            ms = jnp.sum(jnp.where(lane_group == j, k_sq, 0.0), axis=-1, keepdims=True) / HEAD_DIM
            inv = jnp.where(lane_group == j, lax.rsqrt(ms + EPS), inv)
        kn_row = k_row * inv * gk
        v_row = vnew_ref[b]

        qn = _rms(q_ref[b], gq_ref[...])
        q_bd = jnp.concatenate([jnp.where(head_group == j, qn, 0.0) for j in range(N_KV_HEADS)], axis=1)
        kc = ck_ref[b]
        vc = cv_ref[b]
        s_c = lax.dot_general(q_bd.astype(BF16), kc.astype(BF16), (((1,), (1,)), ((), ())),
                              preferred_element_type=F32) * scale + bias_c
        s_n = jnp.sum(q_bd * kn_row, axis=-1, keepdims=True) * scale + bias_n
        m = jnp.maximum(jnp.maximum(jnp.max(s_c, axis=-1, keepdims=True), s_n), sink)
        e_c = jnp.exp(s_c - m)
        e_n = jnp.exp(s_n - m)
        denom = jnp.sum(e_c, axis=-1, keepdims=True) + e_n + jnp.exp(sink - m)
        o_full = jnp.dot((e_c / denom).astype(BF16), vc.astype(BF16), preferred_element_type=F32)
        o_full = o_full + (e_n / denom) * v_row
        o = jnp.zeros((N_ATTN_HEADS, HEAD_DIM), F32)
        for j in range(N_KV_HEADS):
            o = jnp.where(head_group == j, o_full[:, j * HEAD_DIM:(j + 1) * HEAD_DIM], o)
        o_ref[b] = o.astype(o_ref.dtype)

        cko_ref[b, pl.ds(0, win - 1), :] = ck_ref[b, pl.ds(1, win - 1), :]
        cko_ref[b, pl.ds(win - 1, 1), :] = kn_row
        cvo_ref[b, pl.ds(0, win - 1), :] = cv_ref[b, pl.ds(1, win - 1), :]
        cvo_ref[b, pl.ds(win - 1, 1), :] = v_row


def _attn_sample(q, k_new, v_new, cache_k, cache_v, bucket_row, rel_bias_t, sinks_col, g_q, g_k_row, bt):
    nb = q.shape[0]
    win = cache_k.shape[1]
    full = lambda shape: pl.BlockSpec(shape, lambda i: (0,) * len(shape))
    return pl.pallas_call(
        _attn_sample_kernel,
        out_shape=(jax.ShapeDtypeStruct((nb, N_ATTN_HEADS, HEAD_DIM), BF16),
                   jax.ShapeDtypeStruct(cache_k.shape, F32),
                   jax.ShapeDtypeStruct(cache_v.shape, F32)),
        grid=(nb // bt,),
        in_specs=[pl.BlockSpec((bt, N_ATTN_HEADS, HEAD_DIM), lambda i: (i, 0, 0)),
                  pl.BlockSpec((bt, 1, KV_WIDTH), lambda i: (i, 0, 0)),
                  pl.BlockSpec((bt, 1, KV_WIDTH), lambda i: (i, 0, 0)),
                  pl.BlockSpec((bt, win, KV_WIDTH), lambda i: (i, 0, 0)),
                  pl.BlockSpec((bt, win, KV_WIDTH), lambda i: (i, 0, 0)),
                  full(bucket_row.shape), full(rel_bias_t.shape), full(sinks_col.shape),
                  full(g_q.shape), full(g_k_row.shape)],
        out_specs=(pl.BlockSpec((bt, N_ATTN_HEADS, HEAD_DIM), lambda i: (i, 0, 0)),
                   pl.BlockSpec((bt, win, KV_WIDTH), lambda i: (i, 0, 0)),
                   pl.BlockSpec((bt, win, KV_WIDTH), lambda i: (i, 0, 0))),
        compiler_params=_params(1),
        name="attn_sample",
    )(q, k_new, v_new, cache_k, cache_v, bucket_row, rel_bias_t, sinks_col, g_q, g_k_row)


def _ret_prompt_kernel(q_ref, k_ref, v_ref, gate_ref, cos_ref, sin_ref, dmask_ref, qdec_ref, kdec_ref,
                       cdec_ref, o_ref, state_ref, s_ref):
    c = pl.program_id(0)

    @pl.when(c == 0)
    def _():
        s_ref[...] = jnp.zeros_like(s_ref)

    cosf = cos_ref[...]
    sinf = sin_ref[...]
    half = RET_KDIM // 2
    nt = (((1,), (1,)), ((), ()))
    tn = (((0,), (0,)), ((), ()))

    def rotary(x):
        return x * cosf + pltpu.roll(x, half, axis=1) * sinf

    for h in range(N_RET_HEADS):
        kd = slice(h * RET_KDIM, (h + 1) * RET_KDIM)
        vd = slice(h * RET_VDIM, (h + 1) * RET_VDIM)
        for b in range(q_ref.shape[0]):
            q = rotary(q_ref[b, :, kd])
            k = rotary(k_ref[b, :, kd]) * RET_KDIM ** -0.5
            v = v_ref[b, :, vd].astype(BF16)
            s0 = s_ref[b, h]
            scores = lax.dot_general(q.astype(BF16), k.astype(BF16), nt,
                                     preferred_element_type=F32) * dmask_ref[h]
            o_intra = jnp.dot(scores.astype(BF16), v, preferred_element_type=F32)
            o_inter = jnp.dot((q * qdec_ref[h]).astype(BF16), s0.astype(BF16), preferred_element_type=F32)
            s_ref[b, h] = cdec_ref[h] * s0 + lax.dot_general((k * kdec_ref[h]).astype(BF16), v, tn,
                                                              preferred_element_type=F32)
            ro = o_intra + o_inter
            ro = ro * lax.rsqrt(jnp.mean(ro * ro, axis=-1, keepdims=True) + EPS)
            o_ref[b, :, vd] = (jax.nn.silu(gate_ref[b, :, vd]) * ro).astype(o_ref.dtype)

    @pl.when(c == pl.num_programs(0) - 1)
    def _():
        state_ref[...] = s_ref[...]


def _ret_prompt(proj, cosf, sinf, dmask, qdec, kdec, cdec, n_seq):
    m, width = proj.shape
    seq = m // n_seq
    ch = RET_CHUNK
    proj3 = proj.reshape(n_seq, seq, width)
    const3 = pl.BlockSpec((N_RET_HEADS, ch, ch), lambda c: (0, 0, 0))
    state_shape = (n_seq, N_RET_HEADS, RET_KDIM, RET_VDIM)
    o, state = pl.pallas_call(
        _ret_prompt_kernel,
        out_shape=(jax.ShapeDtypeStruct((n_seq, seq, RET_V_WIDTH), BF16),
                   jax.ShapeDtypeStruct(state_shape, F32)),
        grid=(seq // ch,),
        in_specs=[pl.BlockSpec((n_seq, ch, RET_QK_WIDTH), lambda c: (0, c, COL_RQ // RET_QK_WIDTH)),
                  pl.BlockSpec((n_seq, ch, RET_QK_WIDTH), lambda c: (0, c, COL_RK // RET_QK_WIDTH)),
                  pl.BlockSpec((n_seq, ch, RET_V_WIDTH), lambda c: (0, c, COL_RV // RET_V_WIDTH)),
                  pl.BlockSpec((n_seq, ch, RET_V_WIDTH), lambda c: (0, c, COL_RG // RET_V_WIDTH)),
                  pl.BlockSpec((ch, RET_KDIM), lambda c: (c, 0)),
                  pl.BlockSpec((ch, RET_KDIM), lambda c: (c, 0)),
                  const3, const3, const3,
                  pl.BlockSpec(memory_space=pltpu.SMEM)],
        out_specs=(pl.BlockSpec((n_seq, ch, RET_V_WIDTH), lambda c: (0, c, 0)),
                   pl.BlockSpec(state_shape, lambda c: (0, 0, 0, 0))),
        scratch_shapes=[pltpu.VMEM(state_shape, F32)],
        compiler_params=_params(1),
        name="ret_prompt",
    )(proj3, proj3, proj3, proj3, cosf, sinf, dmask, qdec, kdec, cdec)
    return o.reshape(m, RET_V_WIDTH), state


def _ret_sample_kernel(qt_ref, kt_ref, cos_ref, sin_ref, v_ref, gate_ref, s_ref, qdec_ref, kdec_ref,
                       cdec_ref, o_ref, so_ref):
    bt = s_ref.shape[0]
    half = RET_KDIM // 2
    cos = cos_ref[...]
    sin = sin_ref[...]

    def rotary(x):
        x1, x2 = x[:half, :], x[half:, :]
        return jnp.concatenate([x1 * cos - x2 * sin, x2 * cos + x1 * sin], axis=0)

    for h in range(N_RET_HEADS):
        vd = slice(h * RET_VDIM, (h + 1) * RET_VDIM)
        q_all = rotary(qt_ref[0, h])
        k_all = rotary(kt_ref[0, h]) * RET_KDIM ** -0.5
        qk_all = jnp.sum(q_all * k_all, axis=0, keepdims=True)
        qd_all = q_all * qdec_ref[h]
        kd_all = k_all * kdec_ref[h]
        for b in range(bt):
            v = v_ref[0, b:b + 1, vd]
            s0 = s_ref[b, h]
            o_intra = qk_all[:, b:b + 1] * v
            o_inter = jnp.sum(qd_all[:, b:b + 1] * s0, axis=0, keepdims=True)
            so_ref[b, h] = cdec_ref[h] * s0 + kd_all[:, b:b + 1] * v
            ro = o_intra + o_inter
            ro = ro * lax.rsqrt(jnp.mean(ro * ro, axis=-1, keepdims=True) + EPS)
            o_ref[0, b:b + 1, vd] = (jax.nn.silu(gate_ref[0, b:b + 1, vd]) * ro).astype(o_ref.dtype)


def _ret_sample(qt, kt, cos_col, sin_col, v, gate, state, qdec, kdec, cdec):
    nbt, _, _, bt = qt.shape
    smem = pl.BlockSpec(memory_space=pltpu.SMEM)
    st_spec = pl.BlockSpec((bt, N_RET_HEADS, RET_KDIM, RET_VDIM), lambda i: (i, 0, 0, 0))
    qk_spec = pl.BlockSpec((1, N_RET_HEADS, RET_KDIM, bt), lambda i: (i, 0, 0, 0))
    row_spec = pl.BlockSpec((1, bt, RET_V_WIDTH), lambda i: (i, 0, 0))
    col_spec = pl.BlockSpec(cos_col.shape, lambda i: (0, 0))
    return pl.pallas_call(
        _ret_sample_kernel,
        out_shape=(jax.ShapeDtypeStruct((nbt, bt, RET_V_WIDTH), BF16),
                   jax.ShapeDtypeStruct(state.shape, F32)),
        grid=(nbt,),
        in_specs=[qk_spec, qk_spec, col_spec, col_spec, row_spec, row_spec, st_spec, smem, smem, smem],
        out_specs=(row_spec, st_spec),
        compiler_params=_params(1),
        name="ret_sample",
    )(qt, kt, cos_col, sin_col, v, gate, state, qdec, kdec, cdec)


NPF = np.float32


def _t5_bucket(dist):
    n = np.maximum(dist, 0)
    max_exact = N_BUCKETS // 2
    nf = np.maximum(n, 1).astype(NPF)
    large = max_exact + (np.log(nf / NPF(max_exact)) / NPF(math.log(MAX_DISTANCE / max_exact))
                         * NPF(N_BUCKETS - max_exact)).astype(np.int32)
    return np.where(n < max_exact, n, np.minimum(large, N_BUCKETS - 1)).astype(np.int32)


def _rope_tables(pos):
    half = RET_KDIM // 2
    inv = NPF(ROPE_BASE) ** (-np.arange(half, dtype=NPF) / NPF(half))
    ang = pos.astype(NPF)[:, None] * inv[None]
    return np.cos(ang), np.sin(ang)


def _layer(xp, xs, pp, ps, cache_k, cache_v, state_ret, state_conv, rel_bias, lp, n_seq):
    (g_mix, w_in, g_q, g_k, sinks, w_out, g_ffn, w_up, conv_w, conv_b, w_down,
     g_ple, w_ple_gate, w_ple_proj) = lp
    mp = xp.shape[0]
    ms = xs.shape[0]
    seq = mp // n_seq
    win = cache_k.shape[1]

    row = lambda a: a.reshape(1, -1)

    log_decay = np.log(NPF(1.0) - NPF(2.0) ** (NPF(-5.0) - np.arange(N_RET_HEADS, dtype=NPF)))
    idx = np.arange(RET_CHUNK, dtype=NPF)
    diff = idx[:, None] - idx[None, :]
    dmask = np.where(diff[None] >= 0, np.exp(diff[None] * log_decay[:, None, None]), NPF(0.0))
    q_dec = np.exp((idx + NPF(1.0))[:, None] * log_decay[None])
    k_dec = np.exp((NPF(RET_CHUNK - 1.0) - idx)[:, None] * log_decay[None])
    qdec_t = np.broadcast_to(q_dec.T[:, :, None], (N_RET_HEADS, RET_CHUNK, RET_KDIM))
    kdec_t = np.broadcast_to(k_dec.T[:, :, None], (N_RET_HEADS, RET_CHUNK, RET_KDIM))
    cdec = np.exp(NPF(RET_CHUNK) * log_decay)
    cos_p, sin_p = _rope_tables(np.arange(seq, dtype=np.int32))
    cosf = np.concatenate([cos_p, cos_p], axis=1)
    sinf = np.concatenate([-sin_p, sin_p], axis=1)
    one = np.arange(1, dtype=NPF)
    qdec_s = np.exp((one + NPF(1.0))[:, None] * log_decay[None])[0]
    kdec_s = np.exp((NPF(1.0 - 1.0) - one)[:, None] * log_decay[None])[0]
    cdec_s = np.exp(NPF(1.0) * log_decay)
    cos_s, sin_s = _rope_tables(PAST_LEN + np.arange(1, dtype=np.int32))
    cos_col, sin_col = cos_s.reshape(-1, 1), sin_s.reshape(-1, 1)

    qi = np.arange(ATTN_BLOCK, dtype=np.int32)
    ki = np.arange(2 * ATTN_BLOCK, dtype=np.int32) - ATTN_BLOCK
    dist = qi[:, None] - ki[None, :]
    bucket_tile = np.where((dist >= 0) & (dist <= WINDOW), _t5_bucket(dist), -1).astype(np.int32)
    dist_s = win - np.arange(2 * ATTN_BLOCK, dtype=np.int32)
    bucket_row = np.where((dist_s >= 0) & (dist_s <= WINDOW), _t5_bucket(dist_s), -1).astype(np.int32).reshape(1, -1)

    tmp, tms = _row_tile(mp), _row_tile(ms)
    proj_s, w_in_b = _matmul(_norm(xs, row(g_mix), tms), w_in, tms, COL_TILE, IN_SRC_BLOCKS)
    proj_p, = _matmul(_norm(xp, row(g_mix), NORM_TILE), w_in_b, min(mp, IN_ROW_TILE), COL_TILE)

    attn_p, kn_p = _attn_prompt(proj_p, bucket_tile, rel_bias, sinks, row(g_q), row(g_k), n_seq)
    ret_p, ret_state_p = _ret_prompt(proj_p, cosf, sinf, dmask, qdec_t, kdec_t, cdec, n_seq)

    bt_a = 8
    attn_s, ck_new, cv_new = _attn_sample(
        proj_s[:, COL_AQ:COL_AQ + ATTN_WIDTH].reshape(ms, N_ATTN_HEADS, HEAD_DIM),
        proj_s[:, COL_AK:COL_AK + KV_WIDTH].reshape(ms, 1, KV_WIDTH),
        proj_s[:, COL_AV:COL_AV + KV_WIDTH].reshape(ms, 1, KV_WIDTH),
        cache_k.reshape(ms, win, KV_WIDTH), cache_v.reshape(ms, win, KV_WIDTH),
        bucket_row, rel_bias.T, sinks.reshape(-1, 1), row(g_q), jnp.tile(g_k, N_KV_HEADS).reshape(1, -1), bt_a)
    attn_s = attn_s.reshape(ms, ATTN_WIDTH)

    bt_r = 4
    to_cols = lambda a: a.reshape(ms // bt_r, bt_r, N_RET_HEADS, RET_KDIM).transpose(0, 2, 3, 1)
    ret_s, ret_state_s = _ret_sample(
        to_cols(proj_s[:, COL_RQ:COL_RQ + RET_QK_WIDTH]), to_cols(proj_s[:, COL_RK:COL_RK + RET_QK_WIDTH]),
        cos_col, sin_col,
        proj_s[:, COL_RV:COL_RV + RET_V_WIDTH].reshape(ms // bt_r, bt_r, RET_V_WIDTH),
        proj_s[:, COL_RG:COL_RG + RET_V_WIDTH].reshape(ms // bt_r, bt_r, RET_V_WIDTH),
        state_ret, qdec_s, kdec_s, cdec_s)
    ret_s = ret_s.reshape(ms, RET_V_WIDTH)

    xs, w_out_b = _out_proj(attn_s, ret_s, w_out, xs, tms, COL_TILE)
    xp, = _out_proj(attn_p, ret_p, w_out_b, xp, tmp, OUT_COL_TILE)

    act_s, conv_s, w_up_b = _convglu_sample(_norm(xs, row(g_ffn), tms), w_up, conv_w, row(conv_b), state_conv)
    act_p, conv_p = _convglu_prompt(_norm(xp, row(g_ffn), NORM_TILE), w_up_b, conv_w, row(conv_b), n_seq, UP_ROW_TILE)
    for part in range(DOWN_K_PARTS):
        last = part == DOWN_K_PARTS - 1
        xs, w_down_b = _down_proj(act_s, w_down, xs, tms, COL_TILE, DOWN_K_PARTS, part)
        xp, *stats_p = _down_proj(act_p, w_down_b, xp, tmp, COL_TILE, DOWN_K_PARTS, part,
                                  row(g_ple) if last else None)
    hp, ssq_p = stats_p

    xs, w_pgate_b, w_pproj_b = _ple(xs, _norm(xs, row(g_ple), tms), None, w_ple_gate, ps, w_ple_proj, tms, COL_TILE)
    xp, = _ple(xp, hp, ssq_p, w_pgate_b, pp, w_pproj_b, tmp, COL_TILE)

    kp_new = kn_p.reshape(n_seq, seq, N_KV_HEADS, HEAD_DIM)[:, -WINDOW:]
    vp_new = proj_p[:, COL_AV:COL_AV + KV_WIDTH].reshape(n_seq, seq, N_KV_HEADS, HEAD_DIM)[:, -WINDOW:]
    ks_new = ck_new.reshape(ms, win, N_KV_HEADS, HEAD_DIM)
    vs_new = cv_new.reshape(ms, win, N_KV_HEADS, HEAD_DIM)
    return xp, xs, kp_new, vp_new, ret_state_p, conv_p, ks_new, vs_new, ret_state_s, conv_s


def kernel(x_prompt, x_sample, p_prompt, p_sample, cache_win_k, cache_win_v, state_ret, state_conv, rel_bias, g_mix, w_in, g_q, g_k, sinks, w_out, g_ffn, w_up, conv_w, conv_b, w_down, g_ple, w_ple_gate, w_ple_proj):
    depth = g_mix.shape[0]
    n_seq, seq, d = x_prompt.shape
    nb, dec_seq, _ = x_sample.shape
    assert dec_seq == 1 and seq % 512 == 0 and d == D_MODEL
    xp = x_prompt.reshape(n_seq * seq, d)
    xs = x_sample.reshape(nb, d)
    outs = [[] for _ in range(8)]
    for l in range(depth):
        lp = (g_mix[l], w_in[l], g_q[l], g_k[l], sinks[l], w_out[l], g_ffn[l], w_up[l], conv_w[l],
              conv_b[l], w_down[l], g_ple[l], w_ple_gate[l], w_ple_proj[l])
        res = _layer(xp, xs, p_prompt[l].reshape(n_seq * seq, -1), p_sample[l].reshape(nb, -1),
                     cache_win_k[l], cache_win_v[l], state_ret[l], state_conv[l], rel_bias, lp, n_seq)
        xp, xs = res[0], res[1]
        for o, r in zip(outs, res[2:]):
            o.append(r)
    stacked = [jnp.stack(o) for o in outs]
    return (xp.reshape(n_seq, seq, d), xs.reshape(nb, 1, d), *stacked)
```

```python
import functools
import math

import jax
import jax.numpy as jnp
import numpy as np
from jax import lax
from jax.experimental import pallas as pl
from jax.experimental.pallas import tpu as pltpu

F32 = jnp.float32
BF16 = jnp.bfloat16

D_MODEL = 4096
HEAD_DIM = 64
N_ATTN_HEADS = 32
N_KV_HEADS = 4
GQA_GROUP = N_ATTN_HEADS // N_KV_HEADS
WINDOW = 128
ATTN_BLOCK = 128
N_BUCKETS = 32
MAX_DISTANCE = 128
N_RET_HEADS = 8
RET_KDIM = 128
RET_VDIM = 256
RET_CHUNK = 128
ROPE_BASE = 10000.0
D_FF = 11008
CONV_W = 3
EPS = 1e-6

ATTN_WIDTH = N_ATTN_HEADS * HEAD_DIM
KV_WIDTH = N_KV_HEADS * HEAD_DIM
RET_QK_WIDTH = N_RET_HEADS * RET_KDIM
RET_V_WIDTH = N_RET_HEADS * RET_VDIM
IN_SIZES = (ATTN_WIDTH, KV_WIDTH, KV_WIDTH, RET_QK_WIDTH, RET_QK_WIDTH, RET_V_WIDTH, RET_V_WIDTH)
IN_WIDTH = sum(IN_SIZES)
IN_SPLITS = tuple(sum(IN_SIZES[:n + 1]) for n in range(len(IN_SIZES) - 1))
PAST_LEN = 8192

COL_AQ = 0
COL_RV = ATTN_WIDTH
COL_RG = COL_RV + RET_V_WIDTH
COL_RQ = COL_RG + RET_V_WIDTH
COL_RK = COL_RQ + RET_QK_WIDTH
COL_AK = COL_RK + RET_QK_WIDTH
COL_AV = COL_AK + KV_WIDTH

VMEM_LIMIT_BYTES = 56 * 1024 * 1024
FF_TILE = 256
N_FF_TILES = D_FF // FF_TILE
NORM_ROWS = 32
NORM_TILE = 512
OUT_COL_TILE = 1024
STAT_LANES = 128
IN_ROW_TILE = 2048
ROW_TILE = 1024
COL_TILE = 512
UP_ROW_TILE = 2048
CONV_CHUNK = 1024
DOWN_K_PARTS = 2


def _row_tile(m):
    return min(m, ROW_TILE)


def _in_src_blocks():
    start = dict(zip(("aq", "ak", "av", "rq", "rk", "rv", "rg"), (0,) + IN_SPLITS))
    order = (("aq", ATTN_WIDTH), ("rv", RET_V_WIDTH), ("rg", RET_V_WIDTH), ("rq", RET_QK_WIDTH),
             ("rk", RET_QK_WIDTH), ("ak", KV_WIDTH), ("av", KV_WIDTH))
    cols = [c for name, width in order for c in range(start[name], start[name] + width, KV_WIDTH)]
    per_tile = COL_TILE // KV_WIDTH
    blocks = []
    for b in range(0, len(cols), per_tile):
        group = cols[b:b + per_tile]
        assert group[0] % COL_TILE == 0 and all(c == group[0] + u * KV_WIDTH for u, c in enumerate(group))
        blocks.append(group[0] // COL_TILE)
    return tuple(blocks)


IN_SRC_BLOCKS = _in_src_blocks()


def _params(n_axes):
    return pltpu.CompilerParams(dimension_semantics=("arbitrary",) * n_axes,
                                vmem_limit_bytes=VMEM_LIMIT_BYTES)


def _rms(x, g):
    y = x * lax.rsqrt(jnp.mean(x * x, axis=-1, keepdims=True) + EPS)
    return y * g


def _norm_rows_to_bf16(x_ref, g_ref, h_ref):
    def body(c, carry):
        r0 = pl.multiple_of(c * NORM_ROWS, NORM_ROWS)
        h_ref[pl.ds(r0, NORM_ROWS), :] = _rms(x_ref[pl.ds(r0, NORM_ROWS), :], g_ref[...]).astype(h_ref.dtype)
        return carry
    lax.fori_loop(0, x_ref.shape[0] // NORM_ROWS, body, 0)


def _norm(x, g, tr):
    m, k = x.shape
    return pl.pallas_call(
        _norm_rows_to_bf16,
        out_shape=jax.ShapeDtypeStruct((m, k), BF16),
        grid=(m // tr,),
        in_specs=[pl.BlockSpec((tr, k), lambda i: (i, 0)),
                  pl.BlockSpec((1, k), lambda i: (0, 0))],
        out_specs=pl.BlockSpec((tr, k), lambda i: (i, 0)),
        compiler_params=_params(1),
        name="rmsnorm",
    )(x, g)


def _mxu_weights(w_ref, wb_ref):
    if not wb_ref:
        return w_ref[...]
    wb = w_ref[...].astype(BF16)
    wb_ref[0][...] = wb
    return wb


def _mm_kernel(h_ref, w_ref, o_ref, *wb_ref):
    o_ref[...] = jnp.dot(h_ref[...], _mxu_weights(w_ref, wb_ref), preferred_element_type=F32)


def _matmul(h, w, tm, tn, src_blocks=None):
    m, k = h.shape
    n = w.shape[1]
    emit = w.dtype != BF16
    if emit:
        w_map = lambda i, j: (0, sum(jnp.where(j == d, s, 0) for d, s in enumerate(src_blocks)))
    else:
        w_map = lambda i, j: (0, j)
    out_shape = [jax.ShapeDtypeStruct((m, n), F32)]
    out_specs = [pl.BlockSpec((tm, tn), lambda i, j: (i, j))]
    if emit:
        assert m == tm
        out_shape.append(jax.ShapeDtypeStruct((k, n), BF16))
        out_specs.append(pl.BlockSpec((k, tn), lambda i, j: (0, j)))
    return pl.pallas_call(
        _mm_kernel,
        out_shape=out_shape,
        grid=(m // tm, n // tn),
        in_specs=[pl.BlockSpec((tm, k), lambda i, j: (i, 0)),
                  pl.BlockSpec((k, tn), w_map)],
        out_specs=out_specs,
        compiler_params=_params(2),
        name="in_proj",
    )(h, w)


def _row_scale(ssq_ref, width, n_cols):
    r = lax.rsqrt(ssq_ref[...] / width + EPS)
    return jnp.concatenate([r] * (n_cols // STAT_LANES), axis=1)


def _residual_mm_kernel(*refs, n_lhs, emit_weights, norm_stats):
    lhs_refs = refs[:n_lhs]
    w_ref, x_ref = refs[n_lhs], refs[n_lhs + 1]
    rest = list(refs[n_lhs + 2:])
    g_ref = rest.pop(0) if norm_stats else None
    o_ref = rest.pop(0)
    wb_ref = (rest.pop(0),) if emit_weights else ()
    lhs = jnp.concatenate([r[...] for r in lhs_refs], axis=1) if n_lhs > 1 else lhs_refs[0][...]
    o = x_ref[...] + jnp.dot(lhs, _mxu_weights(w_ref, wb_ref), preferred_element_type=F32)
    o_ref[...] = o
    if norm_stats:
        hb_ref, ssq_ref = rest
        hb_ref[...] = (o * g_ref[...]).astype(hb_ref.dtype)
        part = jnp.broadcast_to(jnp.sum(o * o, axis=-1, keepdims=True), ssq_ref.shape)
        j = pl.program_id(1)

        @pl.when(j == 0)
        def _():
            ssq_ref[...] = part

        @pl.when(j > 0)
        def _():
            ssq_ref[...] += part


def _residual_matmul(lhs, lhs_specs, w, w_spec, wb_shape, x, g_next, tm, tn, name):
    m, n = x.shape
    emit = w.dtype != BF16
    stats = g_next is not None
    tile = pl.BlockSpec((tm, tn), lambda i, j: (i, j))
    operands = [*lhs, w, x]
    in_specs = [*lhs_specs, w_spec, tile]
    out_shape = [jax.ShapeDtypeStruct((m, n), F32)]
    out_specs = [tile]
    if stats:
        operands.append(g_next)
        in_specs.append(pl.BlockSpec((1, tn), lambda i, j: (0, j)))
    if emit:
        assert m == tm
        out_shape.append(jax.ShapeDtypeStruct(wb_shape, BF16))
        out_specs.append(pl.BlockSpec((wb_shape[0], tn), lambda i, j: (0, j)))
    if stats:
        out_shape += [jax.ShapeDtypeStruct((m, n), BF16), jax.ShapeDtypeStruct((m, STAT_LANES), F32)]
        out_specs += [tile, pl.BlockSpec((tm, STAT_LANES), lambda i, j: (i, 0))]
    return pl.pallas_call(
        functools.partial(_residual_mm_kernel, n_lhs=len(lhs), emit_weights=emit, norm_stats=stats),
        out_shape=out_shape,
        grid=(m // tm, n // tn),
        in_specs=in_specs,
        out_specs=out_specs,
        compiler_params=_params(2),
        name=name,
    )(*operands)


def _out_proj(a, r, w, x, tm, tn, g_next=None):
    ka, kr = a.shape[1], r.shape[1]
    lhs_specs = [pl.BlockSpec((tm, ka), lambda i, j: (i, 0)), pl.BlockSpec((tm, kr), lambda i, j: (i, 0))]
    w_spec = pl.BlockSpec((ka + kr, tn), lambda i, j: (0, j))
    return _residual_matmul([a, r], lhs_specs, w, w_spec, w.shape, x, g_next, tm, tn, "out_proj")


def _down_proj(a, w, x, tm, tn, k_parts, part, g_next=None):
    k = a.shape[1] // k_parts
    w_part = part if w.dtype != BF16 else 0
    lhs_specs = [pl.BlockSpec((tm, k), lambda i, j: (i, part))]
    w_spec = pl.BlockSpec((k, tn), lambda i, j: (w_part, j))
    return _residual_matmul([a], lhs_specs, w, w_spec, (k, w.shape[1]), x, g_next, tm, tn, "down_proj")


def _ple_kernel(*refs, emit_weights, row_scaled):
    h_ref, wg_ref, p_ref, wp_ref, x_ref = refs[:5]
    rest = list(refs[5:])
    ssq_ref = rest.pop(0) if row_scaled else None
    o_ref = rest.pop(0)
    wb_refs = tuple(rest) if emit_weights else ()
    z = jnp.dot(h_ref[...], _mxu_weights(wg_ref, wb_refs[:1]), preferred_element_type=F32)
    if row_scaled:
        z = z * _row_scale(ssq_ref, h_ref.shape[1], z.shape[1])
    pp = jnp.dot(p_ref[...].astype(BF16), _mxu_weights(wp_ref, wb_refs[1:]), preferred_element_type=F32)
    o_ref[...] = x_ref[...] + jax.nn.sigmoid(z) * pp


def _ple(x, h, ssq, wg, p, wp, tm, tn):
    m, k = h.shape
    n = wg.shape[1]
    kp = p.shape[1]
    emit = wg.dtype != BF16
    scaled = ssq is not None
    operands = [h, wg, p, wp, x]
    in_specs = [pl.BlockSpec((tm, k), lambda i, j: (i, 0)),
                pl.BlockSpec((k, tn), lambda i, j: (0, j)),
                pl.BlockSpec((tm, kp), lambda i, j: (i, 0)),
                pl.BlockSpec((kp, tn), lambda i, j: (0, j)),
                pl.BlockSpec((tm, tn), lambda i, j: (i, j))]
    if scaled:
        operands.append(ssq)
        in_specs.append(pl.BlockSpec((tm, STAT_LANES), lambda i, j: (i, 0)))
    out_shape = [jax.ShapeDtypeStruct((m, n), F32)]
    out_specs = [pl.BlockSpec((tm, tn), lambda i, j: (i, j))]
    if emit:
        assert m == tm
        out_shape += [jax.ShapeDtypeStruct(wg.shape, BF16), jax.ShapeDtypeStruct(wp.shape, BF16)]
        out_specs += [pl.BlockSpec((k, tn), lambda i, j: (0, j)), pl.BlockSpec((kp, tn), lambda i, j: (0, j))]
    return pl.pallas_call(
        functools.partial(_ple_kernel, emit_weights=emit, row_scaled=scaled),
        out_shape=out_shape,
        grid=(m // tm, n // tn),
        in_specs=in_specs,
        out_specs=out_specs,
        compiler_params=_params(2),
        name="ple",
    )(*operands)


def _gelu_erf(x):
    return 0.5 * x * (1.0 + lax.erf(x * math.sqrt(0.5)))


def _conv_taps(cb_ref, cw_ref, um2, um1, u):
    c = cb_ref[...] + cw_ref[0:1, :] * um2
    c = c + cw_ref[1:2, :] * um1
    return c + cw_ref[2:3, :] * u


def _convglu_prompt_kernel(h_ref, w_ref, cwg_ref, cwv_ref, cbg_ref, cbv_ref,
                           a_ref, sg_ref, sv_ref, carry_g_ref, carry_v_ref, *, tiles_per_seq):
    i = pl.program_id(0)
    j = pl.program_id(1)
    tm = h_ref.shape[0]
    rows = min(tm, CONV_CHUNK)
    seq_start = (i % tiles_per_seq) == 0
    sub = 8
    row = lax.broadcasted_iota(jnp.int32, (sub, 1), 0)

    def half(u, cw_ref, cb_ref, prev):
        r1 = pltpu.roll(u, 1, axis=0)
        r2 = pltpu.roll(u, 2, axis=0)
        head1 = jnp.where(row == 0, prev[1:2, :], r1[:sub, :])
        head2 = jnp.where(row == 0, prev[0:1, :], jnp.where(row == 1, prev[1:2, :], r2[:sub, :]))
        um1 = jnp.concatenate([head1, r1[sub:, :]], axis=0)
        um2 = jnp.concatenate([head2, r2[sub:, :]], axis=0)
        return _conv_taps(cb_ref, cw_ref, um2, um1, u), u[rows - (CONV_W - 1):, :]

    prev_g = jnp.where(seq_start, 0.0, carry_g_ref[j])
    prev_v = jnp.where(seq_start, 0.0, carry_v_ref[j])
    for c in range(tm // rows):
        rs = slice(c * rows, (c + 1) * rows)
        u = jnp.dot(h_ref[rs, :], w_ref[...], preferred_element_type=F32)
        cg, prev_g = half(u[:, :FF_TILE], cwg_ref, cbg_ref, prev_g)
        cv, prev_v = half(u[:, FF_TILE:], cwv_ref, cbv_ref, prev_v)
        a_ref[rs, :] = (_gelu_erf(cg) * cv).astype(a_ref.dtype)
    carry_g_ref[j] = prev_g
    carry_v_ref[j] = prev_v
    sg_ref[0] = prev_g
    sv_ref[0] = prev_v


def _convglu_prompt(h, w_pairs, conv_w, conv_b, n_seq, tm):
    m, k = h.shape
    seq = m // n_seq
    tiles_per_seq = seq // tm
    nt = N_FF_TILES
    tn = FF_TILE
    a, sg, sv = pl.pallas_call(
        functools.partial(_convglu_prompt_kernel, tiles_per_seq=tiles_per_seq),
        out_shape=(jax.ShapeDtypeStruct((m, D_FF), BF16),
                   jax.ShapeDtypeStruct((m // tm, CONV_W - 1, D_FF), F32),
                   jax.ShapeDtypeStruct((m // tm, CONV_W - 1, D_FF), F32)),
        grid=(m // tm, nt),
        in_specs=[pl.BlockSpec((tm, k), lambda i, j: (i, 0)),
                  pl.BlockSpec((k, 2 * tn), lambda i, j: (0, j)),
                  pl.BlockSpec((CONV_W, tn), lambda i, j: (0, j)),
                  pl.BlockSpec((CONV_W, tn), lambda i, j: (0, j + nt)),
                  pl.BlockSpec((1, tn), lambda i, j: (0, j)),
                  pl.BlockSpec((1, tn), lambda i, j: (0, j + nt))],
        out_specs=(pl.BlockSpec((tm, tn), lambda i, j: (i, j)),
                   pl.BlockSpec((1, CONV_W - 1, tn), lambda i, j: (i, 0, j)),
                   pl.BlockSpec((1, CONV_W - 1, tn), lambda i, j: (i, 0, j))),
        scratch_shapes=[pltpu.VMEM((nt, CONV_W - 1, tn), F32),
                        pltpu.VMEM((nt, CONV_W - 1, tn), F32)],
        compiler_params=_params(2),
        name="convglu_prompt",
    )(h, w_pairs, conv_w, conv_w, conv_b, conv_b)
    tails = jnp.concatenate([sg, sv], axis=-1)
    return a, tails[tiles_per_seq - 1::tiles_per_seq]


def _convglu_sample_kernel(h_ref, wg_ref, wv_ref, cwg_ref, cwv_ref, cbg_ref, cbv_ref,
                           p0g_ref, p0v_ref, p1g_ref, p1v_ref, a_ref, ug_ref, uv_ref, wb_ref):
    def half(w_ref, wb_cols, cw_ref, cb_ref, p0_ref, p1_ref, u_ref):
        u = jnp.dot(h_ref[...], _mxu_weights(w_ref, (wb_ref.at[:, wb_cols],)), preferred_element_type=F32)
        u_ref[...] = u
        return _conv_taps(cb_ref, cw_ref, p0_ref[...], p1_ref[...], u)

    cg = half(wg_ref, slice(0, FF_TILE), cwg_ref, cbg_ref, p0g_ref, p1g_ref, ug_ref)
    cv = half(wv_ref, slice(FF_TILE, 2 * FF_TILE), cwv_ref, cbv_ref, p0v_ref, p1v_ref, uv_ref)
    a_ref[...] = (_gelu_erf(cg) * cv).astype(a_ref.dtype)


def _convglu_sample(h, w_up, conv_w, conv_b, state_conv):
    m, k = h.shape
    nt = N_FF_TILES
    tn = FF_TILE
    prev = state_conv.reshape(m, (CONV_W - 1) * 2 * D_FF)
    col = lambda off: (lambda j: (0, j + off))
    a, ug, uv, w_pairs = pl.pallas_call(
        _convglu_sample_kernel,
        out_shape=(jax.ShapeDtypeStruct((m, D_FF), BF16),
                   jax.ShapeDtypeStruct((m, D_FF), F32),
                   jax.ShapeDtypeStruct((m, D_FF), F32),
                   jax.ShapeDtypeStruct((k, 2 * D_FF), BF16)),
        grid=(nt,),
        in_specs=[pl.BlockSpec((m, k), lambda j: (0, 0)),
                  pl.BlockSpec((k, tn), col(0)),
                  pl.BlockSpec((k, tn), col(nt)),
                  pl.BlockSpec((CONV_W, tn), col(0)),
                  pl.BlockSpec((CONV_W, tn), col(nt)),
                  pl.BlockSpec((1, tn), col(0)),
                  pl.BlockSpec((1, tn), col(nt)),
                  pl.BlockSpec((m, tn), col(0)),
                  pl.BlockSpec((m, tn), col(nt)),
                  pl.BlockSpec((m, tn), col(2 * nt)),
                  pl.BlockSpec((m, tn), col(3 * nt))],
        out_specs=(pl.BlockSpec((m, tn), col(0)),
                   pl.BlockSpec((m, tn), col(0)),
                   pl.BlockSpec((m, tn), col(0)),
                   pl.BlockSpec((k, 2 * tn), col(0))),
        compiler_params=_params(1),
        name="convglu_sample",
    )(h, w_up, w_up, conv_w, conv_w, conv_b, conv_b, prev, prev, prev, prev)
    u = jnp.concatenate([ug, uv], axis=-1)
    conv_new = jnp.stack([state_conv[:, 1, :], u], axis=1)
    return a, conv_new, w_pairs


def _segment_mean_square(x, seg_ones):
    sq = x * x
    hi = sq.astype(BF16)
    lo = (sq - hi.astype(F32)).astype(BF16)
    total = (jnp.dot(hi, seg_ones, preferred_element_type=F32)
             + jnp.dot(lo, seg_ones, preferred_element_type=F32))
    return total / HEAD_DIM


def _attn_prompt_kernel(q_ref, kc_ref, vc_ref, bucket_ref, seg_ones_ref, rb_ref, sinks_ref, gq_ref, gk_ref,
                        o_ref, kn_ref, bias_ref, kdup_ref, vdup_ref, *, blocks_per_seq):
    r = pl.program_id(0)
    blk = ATTN_BLOCK
    pair = 2 * HEAD_DIM
    n_bias_rows = N_ATTN_HEADS * blk
    seq_start = (r % blocks_per_seq) == 0

    @pl.when(r == 0)
    def _():
        bucket = bucket_ref[...]
        prev_cols = lax.broadcasted_iota(jnp.int32, bucket.shape, 1) < blk

        def per_head(h, carry):
            acc = jnp.full(bucket.shape, -jnp.inf, F32)
            for b in range(N_BUCKETS):
                acc = jnp.where(bucket == b, rb_ref[b, h], acc)
            row0 = pl.multiple_of(h * blk, blk)
            bias_ref[pl.ds(row0, blk), :] = acc
            bias_ref[pl.ds(n_bias_rows + row0, blk), :] = jnp.where(prev_cols, -jnp.inf, acc)
            return carry
        lax.fori_loop(0, N_ATTN_HEADS, per_head, 0)
        vdup_ref[:, :, pair:] = jnp.ones((N_KV_HEADS, 2 * blk, pair), BF16)

    @pl.when(seq_start)
    def _():
        kdup_ref[:, :blk, :] = jnp.zeros((N_KV_HEADS, blk, pair), BF16)
        vdup_ref[:, :blk, :pair] = jnp.zeros((N_KV_HEADS, blk, pair), BF16)

    @pl.when(jnp.logical_not(seq_start))
    def _():
        kdup_ref[:, :blk, :] = kdup_ref[:, blk:, :]
        vdup_ref[:, :blk, :pair] = vdup_ref[:, blk:, :pair]

    seg_ones = seg_ones_ref[...]
    low_half = lax.broadcasted_iota(jnp.int32, (blk, pair), 1) < HEAD_DIM

    def duplicate_half(x, odd):
        swapped = pltpu.roll(x, HEAD_DIM, axis=1)
        return jnp.where(low_half, swapped, x) if odd else jnp.where(low_half, x, swapped)

    kc = kc_ref[...]
    kn = kc * lax.rsqrt(_segment_mean_square(kc, seg_ones) + EPS) * gk_ref[...]
    kn_ref[...] = kn
    for j in range(N_KV_HEADS):
        col = slice((j // 2) * pair, (j // 2 + 1) * pair)
        kdup_ref[j, blk:, :] = duplicate_half(kn[:, col], j % 2).astype(BF16)
        vdup_ref[j, blk:, :pair] = duplicate_half(vc_ref[:, col], j % 2).astype(BF16)

    bias_base = jnp.where(seq_start, n_bias_rows, 0)
    chunk = 2 * pair
    heads_per_chunk = chunk // HEAD_DIM
    for j in range(N_KV_HEADS):
        heads = range(j * GQA_GROUP, (j + 1) * GQA_GROUP)
        lhs = []
        for c in range(j * GQA_GROUP // heads_per_chunk, (j + 1) * GQA_GROUP // heads_per_chunk):
            qc = q_ref[:, c * chunk:(c + 1) * chunk]
            qn = qc * lax.rsqrt(_segment_mean_square(qc, seg_ones) + EPS) * (gq_ref[...] * HEAD_DIM ** -0.5)
            for p in range(chunk // pair):
                qp = qn[:, p * pair:(p + 1) * pair]
                lhs.append(jnp.where(low_half, qp, 0.0).astype(BF16))
                lhs.append(jnp.where(low_half, 0.0, qp).astype(BF16))
        s = lax.dot_general(jnp.concatenate(lhs, axis=0), kdup_ref[j], (((1,), (1,)), ((), ())),
                            preferred_element_type=F32)
        exps, sink_terms = [], []
        for g, h in enumerate(heads):
            row0 = pl.multiple_of(bias_base + h * blk, blk)
            sg = s[g * blk:(g + 1) * blk, :] + bias_ref[pl.ds(row0, blk), :]
            sink = sinks_ref[h]
            m = jnp.maximum(jnp.max(sg, axis=-1, keepdims=True), sink)
            exps.append(jnp.exp(sg - m).astype(BF16))
            sink_terms.append(jnp.exp(sink - m))
        o = jnp.dot(jnp.concatenate(exps, axis=0), vdup_ref[j], preferred_element_type=F32)
        for g in range(0, GQA_GROUP, 2):
            even = o[g * blk:(g + 1) * blk, :]
            odd = o[(g + 1) * blk:(g + 2) * blk, :]
            num = jnp.where(low_half, even[:, :pair], odd[:, :pair])
            den = (jnp.where(low_half, even[:, pair:], odd[:, pair:])
                   + jnp.where(low_half, sink_terms[g], sink_terms[g + 1]))
            c0 = (j * GQA_GROUP + g) * HEAD_DIM
            o_ref[:, c0:c0 + pair] = (num / den).astype(o_ref.dtype)


def _attn_prompt(proj, bucket_tile, rel_bias, sinks, g_q, g_k, n_seq):
    m = proj.shape[0]
    blk = ATTN_BLOCK
    nb = m // blk
    blocks_per_seq = nb // n_seq
    lanes = 4 * HEAD_DIM
    seg = np.arange(lanes) // HEAD_DIM
    seg_ones = (seg[:, None] == seg[None, :]).astype(BF16)
    tile4 = lambda g: jnp.tile(g.reshape(-1), lanes // HEAD_DIM).reshape(1, lanes)
    smem = pl.BlockSpec(memory_space=pltpu.SMEM)
    return pl.pallas_call(
        functools.partial(_attn_prompt_kernel, blocks_per_seq=blocks_per_seq),
        out_shape=(jax.ShapeDtypeStruct((m, ATTN_WIDTH), BF16),
                   jax.ShapeDtypeStruct((m, KV_WIDTH), F32)),
        grid=(nb,),
        in_specs=[pl.BlockSpec((blk, ATTN_WIDTH), lambda r: (r, COL_AQ // ATTN_WIDTH)),
                  pl.BlockSpec((blk, KV_WIDTH), lambda r: (r, COL_AK // KV_WIDTH)),
                  pl.BlockSpec((blk, KV_WIDTH), lambda r: (r, COL_AV // KV_WIDTH)),
                  pl.BlockSpec((blk, 2 * blk), lambda r: (0, 0)),
                  pl.BlockSpec((lanes, lanes), lambda r: (0, 0)),
                  smem, smem,
                  pl.BlockSpec((1, lanes), lambda r: (0, 0)),
                  pl.BlockSpec((1, lanes), lambda r: (0, 0))],
        out_specs=(pl.BlockSpec((blk, ATTN_WIDTH), lambda r: (r, 0)),
                   pl.BlockSpec((blk, KV_WIDTH), lambda r: (r, 0))),
        scratch_shapes=[pltpu.VMEM((2 * N_ATTN_HEADS * blk, 2 * blk), F32),
                        pltpu.VMEM((N_KV_HEADS, 2 * blk, 2 * HEAD_DIM), BF16),
                        pltpu.VMEM((N_KV_HEADS, 2 * blk, 4 * HEAD_DIM), BF16)],
        compiler_params=_params(1),
        name="attn_prompt",
    )(proj, proj, proj, bucket_tile, seg_ones, rel_bias, sinks, tile4(g_q), tile4(g_k))


def _attn_sample_kernel(q_ref, knew_ref, vnew_ref, ck_ref, cv_ref, bucket_ref, rbt_ref, sinks_ref,
                        gq_ref, gk_ref, o_ref, cko_ref, cvo_ref):
    win = ck_ref.shape[1]
    bucket = bucket_ref[...]
    bias = jnp.full((N_ATTN_HEADS, bucket.shape[1]), -jnp.inf, F32)
    for b in range(N_BUCKETS):
        bias = jnp.where(bucket == b, rbt_ref[:, b:b + 1], bias)
    bias_c = bias[:, :win]
    bias_n = bias[:, win:win + 1]
    sink = sinks_ref[...]
    head_group = jnp.right_shift(lax.broadcasted_iota(jnp.int32, (1, N_ATTN_HEADS, HEAD_DIM), 1),
                                 int(math.log2(GQA_GROUP)))
    lane_group = jnp.right_shift(lax.broadcasted_iota(jnp.int32, (1, 1, KV_WIDTH), 2),
                                 int(math.log2(HEAD_DIM)))
    scale = HEAD_DIM ** -0.5

    k_rows = knew_ref[...]
    k_sq = k_rows * k_rows
    inv = jnp.zeros_like(k_rows)
    for j in range(N_KV_HEADS):
        ms = jnp.sum(jnp.where(lane_group == j, k_sq, 0.0), axis=-1, keepdims=True) / HEAD_DIM
        inv = jnp.where(lane_group == j, lax.rsqrt(ms + EPS), inv)
    kn_rows = k_rows * inv * gk_ref[...]
    v_rows = vnew_ref[...]

    qn = _rms(q_ref[...], gq_ref[...])
    q_bd = jnp.concatenate([jnp.where(head_group == j, qn, 0.0) for j in range(N_KV_HEADS)], axis=2)
    s_c = jnp.einsum('bhd,bkd->bhk', q_bd.astype(BF16), ck_ref[...].astype(BF16),
                     preferred_element_type=F32) * scale + bias_c
    s_n = jnp.sum(q_bd * kn_rows, axis=-1, keepdims=True) * scale + bias_n
    m = jnp.maximum(jnp.maximum(jnp.max(s_c, axis=-1, keepdims=True), s_n), sink)
    e_c = jnp.exp(s_c - m)
    e_n = jnp.exp(s_n - m)
    denom = jnp.sum(e_c, axis=-1, keepdims=True) + e_n + jnp.exp(sink - m)
    o_full = jnp.einsum('bhk,bkd->bhd', (e_c / denom).astype(BF16), cv_ref[...].astype(BF16),
                        preferred_element_type=F32)
    o_full = o_full + (e_n / denom) * v_rows
    o = jnp.zeros(q_ref.shape, F32)
    for j in range(N_KV_HEADS):
        o = jnp.where(head_group == j, o_full[:, :, j * HEAD_DIM:(j + 1) * HEAD_DIM], o)
    o_ref[...] = o.astype(o_ref.dtype)

    cko_ref[:, pl.ds(0, win - 1), :] = ck_ref[:, pl.ds(1, win - 1), :]
    cko_ref[:, pl.ds(win - 1, 1), :] = kn_rows
    cvo_ref[:, pl.ds(0, win - 1), :] = cv_ref[:, pl.ds(1, win - 1), :]
    cvo_ref[:, pl.ds(win - 1, 1), :] = v_rows


def _attn_sample(q, k_new, v_new, cache_k, cache_v, bucket_row, rel_bias_t, sinks_col, g_q, g_k_row, bt):
    nb = q.shape[0]
    win = cache_k.shape[1]
    full = lambda shape: pl.BlockSpec(shape, lambda i: (0,) * len(shape))
    return pl.pallas_call(
        _attn_sample_kernel,
        out_shape=(jax.ShapeDtypeStruct((nb, N_ATTN_HEADS, HEAD_DIM), BF16),
                   jax.ShapeDtypeStruct(cache_k.shape, F32),
                   jax.ShapeDtypeStruct(cache_v.shape, F32)),
        grid=(nb // bt,),
        in_specs=[pl.BlockSpec((bt, N_ATTN_HEADS, HEAD_DIM), lambda i: (i, 0, 0)),
                  pl.BlockSpec((bt, 1, KV_WIDTH), lambda i: (i, 0, 0)),
                  pl.BlockSpec((bt, 1, KV_WIDTH), lambda i: (i, 0, 0)),
                  pl.BlockSpec((bt, win, KV_WIDTH), lambda i: (i, 0, 0)),
                  pl.BlockSpec((bt, win, KV_WIDTH), lambda i: (i, 0, 0)),
                  full(bucket_row.shape), full(rel_bias_t.shape), full(sinks_col.shape),
                  full(g_q.shape), full(g_k_row.shape)],
        out_specs=(pl.BlockSpec((bt, N_ATTN_HEADS, HEAD_DIM), lambda i: (i, 0, 0)),
                   pl.BlockSpec((bt, win, KV_WIDTH), lambda i: (i, 0, 0)),
                   pl.BlockSpec((bt, win, KV_WIDTH), lambda i: (i, 0, 0))),
        compiler_params=_params(1),
        name="attn_sample",
    )(q, k_new, v_new, cache_k, cache_v, bucket_row, rel_bias_t, sinks_col, g_q, g_k_row)


def _ret_prompt_kernel(q_ref, k_ref, v_ref, gate_ref, cos_ref, sin_ref, dmask_ref, qdec_ref, kdec_ref,
                       cdec_ref, o_ref, state_ref, s_ref):
    c = pl.program_id(0)

    @pl.when(c == 0)
    def _():
        s_ref[...] = jnp.zeros_like(s_ref)

    cosf = cos_ref[...]
    sinf = sin_ref[...]
    half = RET_KDIM // 2
    nt = (((1,), (1,)), ((), ()))
    tn = (((0,), (0,)), ((), ()))

    def rotary(x):
        return x * cosf + pltpu.roll(x, half, axis=1) * sinf

    for h in range(N_RET_HEADS):
        kd = slice(h * RET_KDIM, (h + 1) * RET_KDIM)
        vd = slice(h * RET_VDIM, (h + 1) * RET_VDIM)
        for b in range(q_ref.shape[0]):
            q = rotary(q_ref[b, :, kd])
            k = rotary(k_ref[b, :, kd]) * RET_KDIM ** -0.5
            v = v_ref[b, :, vd].astype(BF16)
            s0 = s_ref[b, h]
            scores = lax.dot_general(q.astype(BF16), k.astype(BF16), nt,
                                     preferred_element_type=F32) * dmask_ref[h]
            o_intra = jnp.dot(scores.astype(BF16), v, preferred_element_type=F32)
            o_inter = jnp.dot((q * qdec_ref[h]).astype(BF16), s0.astype(BF16), preferred_element_type=F32)
            s_ref[b, h] = cdec_ref[h] * s0 + lax.dot_general((k * kdec_ref[h]).astype(BF16), v, tn,
                                                              preferred_element_type=F32)
            ro = o_intra + o_inter
            ro = ro * lax.rsqrt(jnp.mean(ro * ro, axis=-1, keepdims=True) + EPS)
            o_ref[b, :, vd] = (jax.nn.silu(gate_ref[b, :, vd]) * ro).astype(o_ref.dtype)

    @pl.when(c == pl.num_programs(0) - 1)
    def _():
        state_ref[...] = s_ref[...]


def _ret_prompt(proj, cosf, sinf, dmask, qdec, kdec, cdec, n_seq):
    m, width = proj.shape
    seq = m // n_seq
    ch = RET_CHUNK
    proj3 = proj.reshape(n_seq, seq, width)
    const3 = pl.BlockSpec((N_RET_HEADS, ch, ch), lambda c: (0, 0, 0))
    state_shape = (n_seq, N_RET_HEADS, RET_KDIM, RET_VDIM)
    o, state = pl.pallas_call(
        _ret_prompt_kernel,
        out_shape=(jax.ShapeDtypeStruct((n_seq, seq, RET_V_WIDTH), BF16),
                   jax.ShapeDtypeStruct(state_shape, F32)),
        grid=(seq // ch,),
        in_specs=[pl.BlockSpec((n_seq, ch, RET_QK_WIDTH), lambda c: (0, c, COL_RQ // RET_QK_WIDTH)),
                  pl.BlockSpec((n_seq, ch, RET_QK_WIDTH), lambda c: (0, c, COL_RK // RET_QK_WIDTH)),
                  pl.BlockSpec((n_seq, ch, RET_V_WIDTH), lambda c: (0, c, COL_RV // RET_V_WIDTH)),
                  pl.BlockSpec((n_seq, ch, RET_V_WIDTH), lambda c: (0, c, COL_RG // RET_V_WIDTH)),
                  pl.BlockSpec((ch, RET_KDIM), lambda c: (c, 0)),
                  pl.BlockSpec((ch, RET_KDIM), lambda c: (c, 0)),
                  const3, const3, const3,
                  pl.BlockSpec(memory_space=pltpu.SMEM)],
        out_specs=(pl.BlockSpec((n_seq, ch, RET_V_WIDTH), lambda c: (0, c, 0)),
                   pl.BlockSpec(state_shape, lambda c: (0, 0, 0, 0))),
        scratch_shapes=[pltpu.VMEM(state_shape, F32)],
        compiler_params=_params(1),
        name="ret_prompt",
    )(proj3, proj3, proj3, proj3, cosf, sinf, dmask, qdec, kdec, cdec)
    return o.reshape(m, RET_V_WIDTH), state


def _ret_sample_kernel(qr_ref, qt_ref, kt_ref, cosr_ref, sinr_ref, cos_ref, sin_ref, v_ref, gate_ref, s_ref,
                       qdec_ref, kdec_ref, cdec_ref, o_ref, so_ref):
    bt = s_ref.shape[0]
    half = RET_KDIM // 2
    cos = cos_ref[...]
    sin = sin_ref[...]
    cosr = cosr_ref[...]
    sinr = sinr_ref[...]

    def rotary(x):
        x1, x2 = x[:half, :], x[half:, :]
        return jnp.concatenate([x1 * cos - x2 * sin, x2 * cos + x1 * sin], axis=0)

    for h in range(N_RET_HEADS):
        vd = slice(h * RET_VDIM, (h + 1) * RET_VDIM)
        q_all = rotary(qt_ref[0, h])
        k_all = rotary(kt_ref[0, h]) * RET_KDIM ** -0.5
        qk_all = jnp.sum(q_all * k_all, axis=0, keepdims=True)
        kd_all = k_all * kdec_ref[h]
        q_rows = qr_ref[0, :, h * RET_KDIM:(h + 1) * RET_KDIM]
        q_rows = q_rows * cosr + pltpu.roll(q_rows, half, axis=1) * sinr
        qd_rows = (q_rows * qdec_ref[h]).astype(BF16)
        for b in range(bt):
            v = v_ref[0, b:b + 1, vd]
            s0 = s_ref[b, h]
            o_intra = qk_all[:, b:b + 1] * v
            o_inter = jnp.dot(qd_rows, s0.astype(BF16), preferred_element_type=F32)[b:b + 1, :]
            so_ref[b, h] = cdec_ref[h] * s0 + kd_all[:, b:b + 1] * v
            ro = o_intra + o_inter
            ro = ro * lax.rsqrt(jnp.mean(ro * ro, axis=-1, keepdims=True) + EPS)
            o_ref[0, b:b + 1, vd] = (jax.nn.silu(gate_ref[0, b:b + 1, vd]) * ro).astype(o_ref.dtype)


def _ret_sample(qr, qt, kt, cos_row, sin_row, cos_col, sin_col, v, gate, state, qdec, kdec, cdec):
    nbt, _, _, bt = qt.shape
    smem = pl.BlockSpec(memory_space=pltpu.SMEM)
    st_spec = pl.BlockSpec((bt, N_RET_HEADS, RET_KDIM, RET_VDIM), lambda i: (i, 0, 0, 0))
    qk_spec = pl.BlockSpec((1, N_RET_HEADS, RET_KDIM, bt), lambda i: (i, 0, 0, 0))
    row_spec = pl.BlockSpec((1, bt, RET_V_WIDTH), lambda i: (i, 0, 0))
    col_spec = pl.BlockSpec(cos_col.shape, lambda i: (0, 0))
    trig_row_spec = pl.BlockSpec(cos_row.shape, lambda i: (0, 0))
    q_row_spec = pl.BlockSpec((1, bt, RET_QK_WIDTH), lambda i: (i, 0, 0))
    return pl.pallas_call(
        _ret_sample_kernel,
        out_shape=(jax.ShapeDtypeStruct((nbt, bt, RET_V_WIDTH), BF16),
                   jax.ShapeDtypeStruct(state.shape, F32)),
        grid=(nbt,),
        in_specs=[q_row_spec, qk_spec, qk_spec, trig_row_spec, trig_row_spec, col_spec, col_spec,
                  row_spec, row_spec, st_spec, smem, smem, smem],
        out_specs=(row_spec, st_spec),
        compiler_params=_params(1),
        name="ret_sample",
    )(qr, qt, kt, cos_row, sin_row, cos_col, sin_col, v, gate, state, qdec, kdec, cdec)


NPF = np.float32


def _t5_bucket(dist):
    n = np.maximum(dist, 0)
    max_exact = N_BUCKETS // 2
    nf = np.maximum(n, 1).astype(NPF)
    large = max_exact + (np.log(nf / NPF(max_exact)) / NPF(math.log(MAX_DISTANCE / max_exact))
                         * NPF(N_BUCKETS - max_exact)).astype(np.int32)
    return np.where(n < max_exact, n, np.minimum(large, N_BUCKETS - 1)).astype(np.int32)


def _rope_tables(pos):
    half = RET_KDIM // 2
    inv = NPF(ROPE_BASE) ** (-np.arange(half, dtype=NPF) / NPF(half))
    ang = pos.astype(NPF)[:, None] * inv[None]
    return np.cos(ang), np.sin(ang)


def _layer(xp, xs, pp, ps, cache_k, cache_v, state_ret, state_conv, rel_bias, lp, n_seq):
    (g_mix, w_in, g_q, g_k, sinks, w_out, g_ffn, w_up, conv_w, conv_b, w_down,
     g_ple, w_ple_gate, w_ple_proj) = lp
    mp = xp.shape[0]
    ms = xs.shape[0]
    seq = mp // n_seq
    win = cache_k.shape[1]

    row = lambda a: a.reshape(1, -1)

    log_decay = np.log(NPF(1.0) - NPF(2.0) ** (NPF(-5.0) - np.arange(N_RET_HEADS, dtype=NPF)))
    idx = np.arange(RET_CHUNK, dtype=NPF)
    diff = idx[:, None] - idx[None, :]
    dmask = np.where(diff[None] >= 0, np.exp(diff[None] * log_decay[:, None, None]), NPF(0.0))
    q_dec = np.exp((idx + NPF(1.0))[:, None] * log_decay[None])
    k_dec = np.exp((NPF(RET_CHUNK - 1.0) - idx)[:, None] * log_decay[None])
    qdec_t = np.broadcast_to(q_dec.T[:, :, None], (N_RET_HEADS, RET_CHUNK, RET_KDIM))
    kdec_t = np.broadcast_to(k_dec.T[:, :, None], (N_RET_HEADS, RET_CHUNK, RET_KDIM))
    cdec = np.exp(NPF(RET_CHUNK) * log_decay)
    cos_p, sin_p = _rope_tables(np.arange(seq, dtype=np.int32))
    cosf = np.concatenate([cos_p, cos_p], axis=1)
    sinf = np.concatenate([-sin_p, sin_p], axis=1)
    one = np.arange(1, dtype=NPF)
    qdec_s = np.exp((one + NPF(1.0))[:, None] * log_decay[None])[0]
    kdec_s = np.exp((NPF(1.0 - 1.0) - one)[:, None] * log_decay[None])[0]
    cdec_s = np.exp(NPF(1.0) * log_decay)
    cos_s, sin_s = _rope_tables(PAST_LEN + np.arange(1, dtype=np.int32))
    cos_col, sin_col = cos_s.reshape(-1, 1), sin_s.reshape(-1, 1)

    qi = np.arange(ATTN_BLOCK, dtype=np.int32)
    ki = np.arange(2 * ATTN_BLOCK, dtype=np.int32) - ATTN_BLOCK
    dist = qi[:, None] - ki[None, :]
    bucket_tile = np.where((dist >= 0) & (dist <= WINDOW), _t5_bucket(dist), -1).astype(np.int32)
    dist_s = win - np.arange(2 * ATTN_BLOCK, dtype=np.int32)
    bucket_row = np.where((dist_s >= 0) & (dist_s <= WINDOW), _t5_bucket(dist_s), -1).astype(np.int32).reshape(1, -1)

    tmp, tms = _row_tile(mp), _row_tile(ms)
    proj_s, w_in_b = _matmul(_norm(xs, row(g_mix), tms), w_in, tms, COL_TILE, IN_SRC_BLOCKS)
    proj_p, = _matmul(_norm(xp, row(g_mix), NORM_TILE), w_in_b, min(mp, IN_ROW_TILE), COL_TILE)

    attn_p, kn_p = _attn_prompt(proj_p, bucket_tile, rel_bias, sinks, row(g_q), row(g_k), n_seq)
    ret_p, ret_state_p = _ret_prompt(proj_p, cosf, sinf, dmask, qdec_t, kdec_t, cdec, n_seq)

    bt_a = 8
    attn_s, ck_new, cv_new = _attn_sample(
        proj_s[:, COL_AQ:COL_AQ + ATTN_WIDTH].reshape(ms, N_ATTN_HEADS, HEAD_DIM),
        proj_s[:, COL_AK:COL_AK + KV_WIDTH].reshape(ms, 1, KV_WIDTH),
        proj_s[:, COL_AV:COL_AV + KV_WIDTH].reshape(ms, 1, KV_WIDTH),
        cache_k.reshape(ms, win, KV_WIDTH), cache_v.reshape(ms, win, KV_WIDTH),
        bucket_row, rel_bias.T, sinks.reshape(-1, 1), row(g_q), jnp.tile(g_k, N_KV_HEADS).reshape(1, -1), bt_a)
    attn_s = attn_s.reshape(ms, ATTN_WIDTH)

    bt_r = 8
    to_cols = lambda a: a.reshape(ms // bt_r, bt_r, N_RET_HEADS, RET_KDIM).transpose(0, 2, 3, 1)
    rq_s = proj_s[:, COL_RQ:COL_RQ + RET_QK_WIDTH]
    ret_s, ret_state_s = _ret_sample(
        rq_s.reshape(ms // bt_r, bt_r, RET_QK_WIDTH), to_cols(rq_s), to_cols(proj_s[:, COL_RK:COL_RK + RET_QK_WIDTH]),
        np.concatenate([cos_s, cos_s], axis=1), np.concatenate([-sin_s, sin_s], axis=1), cos_col, sin_col,
        proj_s[:, COL_RV:COL_RV + RET_V_WIDTH].reshape(ms // bt_r, bt_r, RET_V_WIDTH),
        proj_s[:, COL_RG:COL_RG + RET_V_WIDTH].reshape(ms // bt_r, bt_r, RET_V_WIDTH),
        state_ret, qdec_s, kdec_s, cdec_s)
    ret_s = ret_s.reshape(ms, RET_V_WIDTH)

    xs, w_out_b = _out_proj(attn_s, ret_s, w_out, xs, tms, COL_TILE)
    xp, = _out_proj(attn_p, ret_p, w_out_b, xp, tmp, OUT_COL_TILE)

    act_s, conv_s, w_up_b = _convglu_sample(_norm(xs, row(g_ffn), tms), w_up, conv_w, row(conv_b), state_conv)
    act_p, conv_p = _convglu_prompt(_norm(xp, row(g_ffn), NORM_TILE), w_up_b, conv_w, row(conv_b), n_seq, UP_ROW_TILE)
    for part in range(DOWN_K_PARTS):
        last = part == DOWN_K_PARTS - 1
        xs, w_down_b = _down_proj(act_s, w_down, xs, tms, COL_TILE, DOWN_K_PARTS, part)
        xp, *stats_p = _down_proj(act_p, w_down_b, xp, tmp, COL_TILE, DOWN_K_PARTS, part,
                                  row(g_ple) if last else None)
    hp, ssq_p = stats_p

    xs, w_pgate_b, w_pproj_b = _ple(xs, _norm(xs, row(g_ple), tms), None, w_ple_gate, ps, w_ple_proj, tms, COL_TILE)
    xp, = _ple(xp, hp, ssq_p, w_pgate_b, pp, w_pproj_b, tmp, COL_TILE)

    kp_new = kn_p.reshape(n_seq, seq, N_KV_HEADS, HEAD_DIM)[:, -WINDOW:]
    vp_new = proj_p[:, COL_AV:COL_AV + KV_WIDTH].reshape(n_seq, seq, N_KV_HEADS, HEAD_DIM)[:, -WINDOW:]
    ks_new = ck_new.reshape(ms, win, N_KV_HEADS, HEAD_DIM)
    vs_new = cv_new.reshape(ms, win, N_KV_HEADS, HEAD_DIM)
    return xp, xs, kp_new, vp_new, ret_state_p, conv_p, ks_new, vs_new, ret_state_s, conv_s


def kernel(x_prompt, x_sample, p_prompt, p_sample, cache_win_k, cache_win_v, state_ret, state_conv, rel_bias, g_mix, w_in, g_q, g_k, sinks, w_out, g_ffn, w_up, conv_w, conv_b, w_down, g_ple, w_ple_gate, w_ple_proj):
    depth = g_mix.shape[0]
    n_seq, seq, d = x_prompt.shape
    nb, dec_seq, _ = x_sample.shape
    assert dec_seq == 1 and seq % 512 == 0 and d == D_MODEL
    xp = x_prompt.reshape(n_seq * seq, d)
    xs = x_sample.reshape(nb, d)
    outs = [[] for _ in range(8)]
    for l in range(depth):
        lp = (g_mix[l], w_in[l], g_q[l], g_k[l], sinks[l], w_out[l], g_ffn[l], w_up[l], conv_w[l],
              conv_b[l], w_down[l], g_ple[l], w_ple_gate[l], w_ple_proj[l])
        res = _layer(xp, xs, p_prompt[l].reshape(n_seq * seq, -1), p_sample[l].reshape(nb, -1),
                     cache_win_k[l], cache_win_v[l], state_ret[l], state_conv[l], rel_bias, lp, n_seq)
        xp, xs = res[0], res[1]
        for o, r in zip(outs, res[2:]):
            o.append(r)
    stacked = [jnp.stack(o) for o in outs]
    return (xp.reshape(n_seq, seq, d), xs.reshape(nb, 1, d), *stacked)
```

```python
import functools
import math

import jax
import jax.numpy as jnp
import numpy as np
from jax import lax
from jax.experimental import pallas as pl
from jax.experimental.pallas import tpu as pltpu

F32 = jnp.float32
BF16 = jnp.bfloat16

D_MODEL = 4096
HEAD_DIM = 64
N_ATTN_HEADS = 32
N_KV_HEADS = 4
GQA_GROUP = N_ATTN_HEADS // N_KV_HEADS
WINDOW = 128
ATTN_BLOCK = 128
N_BUCKETS = 32
MAX_DISTANCE = 128
N_RET_HEADS = 8
RET_KDIM = 128
RET_VDIM = 256
RET_CHUNK = 128
ROPE_BASE = 10000.0
D_FF = 11008
CONV_W = 3
EPS = 1e-6

ATTN_WIDTH = N_ATTN_HEADS * HEAD_DIM
KV_WIDTH = N_KV_HEADS * HEAD_DIM
RET_QK_WIDTH = N_RET_HEADS * RET_KDIM
RET_V_WIDTH = N_RET_HEADS * RET_VDIM
IN_SIZES = (ATTN_WIDTH, KV_WIDTH, KV_WIDTH, RET_QK_WIDTH, RET_QK_WIDTH, RET_V_WIDTH, RET_V_WIDTH)
IN_WIDTH = sum(IN_SIZES)
IN_SPLITS = tuple(sum(IN_SIZES[:n + 1]) for n in range(len(IN_SIZES) - 1))
PAST_LEN = 8192

COL_AQ = 0
COL_RV = ATTN_WIDTH
COL_RG = COL_RV + RET_V_WIDTH
COL_RQ = COL_RG + RET_V_WIDTH
COL_RK = COL_RQ + RET_QK_WIDTH
COL_AK = COL_RK + RET_QK_WIDTH
COL_AV = COL_AK + KV_WIDTH

VMEM_LIMIT_BYTES = 56 * 1024 * 1024
FF_TILE = 256
N_FF_TILES = D_FF // FF_TILE
NORM_ROWS = 128
NORM_TILE = 512
OUT_COL_TILE = 1024
STAT_LANES = 128
IN_ROW_TILE = 2048
ROW_TILE = 1024
COL_TILE = 512
UP_ROW_TILE = 2048
CONV_CHUNK = 1024
DOWN_K_PARTS = 2


def _row_tile(m):
    return min(m, ROW_TILE)


def _in_src_blocks():
    start = dict(zip(("aq", "ak", "av", "rq", "rk", "rv", "rg"), (0,) + IN_SPLITS))
    order = (("aq", ATTN_WIDTH), ("rv", RET_V_WIDTH), ("rg", RET_V_WIDTH), ("rq", RET_QK_WIDTH),
             ("rk", RET_QK_WIDTH), ("ak", KV_WIDTH), ("av", KV_WIDTH))
    cols = [c for name, width in order for c in range(start[name], start[name] + width, KV_WIDTH)]
    per_tile = COL_TILE // KV_WIDTH
    blocks = []
    for b in range(0, len(cols), per_tile):
        group = cols[b:b + per_tile]
        assert group[0] % COL_TILE == 0 and all(c == group[0] + u * KV_WIDTH for u, c in enumerate(group))
        blocks.append(group[0] // COL_TILE)
    return tuple(blocks)


IN_SRC_BLOCKS = _in_src_blocks()


def _params(n_axes):
    return pltpu.CompilerParams(dimension_semantics=("arbitrary",) * n_axes,
                                vmem_limit_bytes=VMEM_LIMIT_BYTES)


def _rms(x, g):
    y = x * lax.rsqrt(jnp.mean(x * x, axis=-1, keepdims=True) + EPS)
    return y * g


def _norm_rows_to_bf16(x_ref, g_ref, h_ref):
    def body(c, carry):
        r0 = pl.multiple_of(c * NORM_ROWS, NORM_ROWS)
        h_ref[pl.ds(r0, NORM_ROWS), :] = _rms(x_ref[pl.ds(r0, NORM_ROWS), :], g_ref[...]).astype(h_ref.dtype)
        return carry
    lax.fori_loop(0, x_ref.shape[0] // NORM_ROWS, body, 0)


def _norm(x, g, tr):
    m, k = x.shape
    return pl.pallas_call(
        _norm_rows_to_bf16,
        out_shape=jax.ShapeDtypeStruct((m, k), BF16),
        grid=(m // tr,),
        in_specs=[pl.BlockSpec((tr, k), lambda i: (i, 0)),
                  pl.BlockSpec((1, k), lambda i: (0, 0))],
        out_specs=pl.BlockSpec((tr, k), lambda i: (i, 0)),
        compiler_params=_params(1),
        name="rmsnorm",
    )(x, g)


def _mxu_weights(w_ref, wb_ref):
    if not wb_ref:
        return w_ref[...]
    wb = w_ref[...].astype(BF16)
    wb_ref[0][...] = wb
    return wb


def _mm_kernel(h_ref, w_ref, o_ref, *wb_ref):
    o_ref[...] = jnp.dot(h_ref[...], _mxu_weights(w_ref, wb_ref), preferred_element_type=F32)


def _matmul(h, w, tm, tn, src_blocks=None):
    m, k = h.shape
    n = w.shape[1]
    emit = w.dtype != BF16
    if emit:
        w_map = lambda i, j: (0, sum(jnp.where(j == d, s, 0) for d, s in enumerate(src_blocks)))
    else:
        w_map = lambda i, j: (0, j)
    out_shape = [jax.ShapeDtypeStruct((m, n), F32)]
    out_specs = [pl.BlockSpec((tm, tn), lambda i, j: (i, j))]
    if emit:
        assert m == tm
        out_shape.append(jax.ShapeDtypeStruct((k, n), BF16))
        out_specs.append(pl.BlockSpec((k, tn), lambda i, j: (0, j)))
    return pl.pallas_call(
        _mm_kernel,
        out_shape=out_shape,
        grid=(m // tm, n // tn),
        in_specs=[pl.BlockSpec((tm, k), lambda i, j: (i, 0)),
                  pl.BlockSpec((k, tn), w_map)],
        out_specs=out_specs,
        compiler_params=_params(2),
        name="in_proj",
    )(h, w)


def _row_scale(ssq_ref, width, n_cols):
    r = lax.rsqrt(ssq_ref[...] / width + EPS)
    return jnp.concatenate([r] * (n_cols // STAT_LANES), axis=1)


def _residual_mm_kernel(*refs, n_lhs, emit_weights, norm_stats):
    lhs_refs = refs[:n_lhs]
    w_ref, x_ref = refs[n_lhs], refs[n_lhs + 1]
    rest = list(refs[n_lhs + 2:])
    g_ref = rest.pop(0) if norm_stats else None
    o_ref = rest.pop(0)
    wb_ref = (rest.pop(0),) if emit_weights else ()
    lhs = jnp.concatenate([r[...] for r in lhs_refs], axis=1) if n_lhs > 1 else lhs_refs[0][...]
    o = x_ref[...] + jnp.dot(lhs, _mxu_weights(w_ref, wb_ref), preferred_element_type=F32)
    o_ref[...] = o
    if norm_stats:
        hb_ref, ssq_ref = rest
        hb_ref[...] = (o * g_ref[...]).astype(hb_ref.dtype)
        part = jnp.broadcast_to(jnp.sum(o * o, axis=-1, keepdims=True), ssq_ref.shape)
        j = pl.program_id(1)

        @pl.when(j == 0)
        def _():
            ssq_ref[...] = part

        @pl.when(j > 0)
        def _():
            ssq_ref[...] += part


def _residual_matmul(lhs, lhs_specs, w, w_spec, wb_shape, x, g_next, tm, tn, name):
    m, n = x.shape
    emit = w.dtype != BF16
    stats = g_next is not None
    tile = pl.BlockSpec((tm, tn), lambda i, j: (i, j))
    operands = [*lhs, w, x]
    in_specs = [*lhs_specs, w_spec, tile]
    out_shape = [jax.ShapeDtypeStruct((m, n), F32)]
    out_specs = [tile]
    if stats:
        operands.append(g_next)
        in_specs.append(pl.BlockSpec((1, tn), lambda i, j: (0, j)))
    if emit:
        assert m == tm
        out_shape.append(jax.ShapeDtypeStruct(wb_shape, BF16))
        out_specs.append(pl.BlockSpec((wb_shape[0], tn), lambda i, j: (0, j)))
    if stats:
        out_shape += [jax.ShapeDtypeStruct((m, n), BF16), jax.ShapeDtypeStruct((m, STAT_LANES), F32)]
        out_specs += [tile, pl.BlockSpec((tm, STAT_LANES), lambda i, j: (i, 0))]
    return pl.pallas_call(
        functools.partial(_residual_mm_kernel, n_lhs=len(lhs), emit_weights=emit, norm_stats=stats),
        out_shape=out_shape,
        grid=(m // tm, n // tn),
        in_specs=in_specs,
        out_specs=out_specs,
        compiler_params=_params(2),
        name=name,
    )(*operands)


def _out_proj(a, r, w, x, tm, tn, g_next=None):
    ka, kr = a.shape[1], r.shape[1]
    lhs_specs = [pl.BlockSpec((tm, ka), lambda i, j: (i, 0)), pl.BlockSpec((tm, kr), lambda i, j: (i, 0))]
    w_spec = pl.BlockSpec((ka + kr, tn), lambda i, j: (0, j))
    return _residual_matmul([a, r], lhs_specs, w, w_spec, w.shape, x, g_next, tm, tn, "out_proj")


def _down_proj(a, w, x, tm, tn, k_parts, part, g_next=None):
    k = a.shape[1] // k_parts
    w_part = part if w.dtype != BF16 else 0
    lhs_specs = [pl.BlockSpec((tm, k), lambda i, j: (i, part))]
    w_spec = pl.BlockSpec((k, tn), lambda i, j: (w_part, j))
    return _residual_matmul([a], lhs_specs, w, w_spec, (k, w.shape[1]), x, g_next, tm, tn, "down_proj")


def _ple_kernel(*refs, emit_weights, row_scaled):
    h_ref, wg_ref, p_ref, wp_ref, x_ref = refs[:5]
    rest = list(refs[5:])
    ssq_ref = rest.pop(0) if row_scaled else None
    o_ref = rest.pop(0)
    wb_refs = tuple(rest) if emit_weights else ()
    z = jnp.dot(h_ref[...], _mxu_weights(wg_ref, wb_refs[:1]), preferred_element_type=F32)
    if row_scaled:
        z = z * _row_scale(ssq_ref, h_ref.shape[1], z.shape[1])
    pp = jnp.dot(p_ref[...].astype(BF16), _mxu_weights(wp_ref, wb_refs[1:]), preferred_element_type=F32)
    o_ref[...] = x_ref[...] + jax.nn.sigmoid(z) * pp


def _ple(x, h, ssq, wg, p, wp, tm, tn):
    m, k = h.shape
    n = wg.shape[1]
    kp = p.shape[1]
    emit = wg.dtype != BF16
    scaled = ssq is not None
    operands = [h, wg, p, wp, x]
    in_specs = [pl.BlockSpec((tm, k), lambda i, j: (i, 0)),
                pl.BlockSpec((k, tn), lambda i, j: (0, j)),
                pl.BlockSpec((tm, kp), lambda i, j: (i, 0)),
                pl.BlockSpec((kp, tn), lambda i, j: (0, j)),
                pl.BlockSpec((tm, tn), lambda i, j: (i, j))]
    if scaled:
        operands.append(ssq)
        in_specs.append(pl.BlockSpec((tm, STAT_LANES), lambda i, j: (i, 0)))
    out_shape = [jax.ShapeDtypeStruct((m, n), F32)]
    out_specs = [pl.BlockSpec((tm, tn), lambda i, j: (i, j))]
    if emit:
        assert m == tm
        out_shape += [jax.ShapeDtypeStruct(wg.shape, BF16), jax.ShapeDtypeStruct(wp.shape, BF16)]
        out_specs += [pl.BlockSpec((k, tn), lambda i, j: (0, j)), pl.BlockSpec((kp, tn), lambda i, j: (0, j))]
    return pl.pallas_call(
        functools.partial(_ple_kernel, emit_weights=emit, row_scaled=scaled),
        out_shape=out_shape,
        grid=(m // tm, n // tn),
        in_specs=in_specs,
        out_specs=out_specs,
        compiler_params=_params(2),
        name="ple",
    )(*operands)


def _gelu_erf(x):
    return 0.5 * x * (1.0 + lax.erf(x * math.sqrt(0.5)))


def _conv_taps(cb_ref, cw_ref, um2, um1, u):
    c = cb_ref[...] + cw_ref[0:1, :] * um2
    c = c + cw_ref[1:2, :] * um1
    return c + cw_ref[2:3, :] * u


def _convglu_prompt_kernel(h_ref, w_ref, cwg_ref, cwv_ref, cbg_ref, cbv_ref,
                           a_ref, sg_ref, sv_ref, carry_g_ref, carry_v_ref, *, tiles_per_seq):
    i = pl.program_id(0)
    j = pl.program_id(1)
    tm = h_ref.shape[0]
    rows = min(tm, CONV_CHUNK)
    seq_start = (i % tiles_per_seq) == 0
    sub = 8
    row = lax.broadcasted_iota(jnp.int32, (sub, 1), 0)

    def half(u, cw_ref, cb_ref, prev):
        r1 = pltpu.roll(u, 1, axis=0)
        r2 = pltpu.roll(u, 2, axis=0)
        head1 = jnp.where(row == 0, prev[1:2, :], r1[:sub, :])
        head2 = jnp.where(row == 0, prev[0:1, :], jnp.where(row == 1, prev[1:2, :], r2[:sub, :]))
        um1 = jnp.concatenate([head1, r1[sub:, :]], axis=0)
        um2 = jnp.concatenate([head2, r2[sub:, :]], axis=0)
        return _conv_taps(cb_ref, cw_ref, um2, um1, u), u[rows - (CONV_W - 1):, :]

    prev_g = jnp.where(seq_start, 0.0, carry_g_ref[j])
    prev_v = jnp.where(seq_start, 0.0, carry_v_ref[j])
    for c in range(tm // rows):
        rs = slice(c * rows, (c + 1) * rows)
        u = jnp.dot(h_ref[rs, :], w_ref[...], preferred_element_type=F32)
        cg, prev_g = half(u[:, :FF_TILE], cwg_ref, cbg_ref, prev_g)
        cv, prev_v = half(u[:, FF_TILE:], cwv_ref, cbv_ref, prev_v)
        a_ref[rs, :] = (_gelu_erf(cg) * cv).astype(a_ref.dtype)
    carry_g_ref[j] = prev_g
    carry_v_ref[j] = prev_v
    sg_ref[0] = prev_g
    sv_ref[0] = prev_v


def _convglu_prompt(h, w_pairs, conv_w, conv_b, n_seq, tm):
    m, k = h.shape
    seq = m // n_seq
    tiles_per_seq = seq // tm
    nt = N_FF_TILES
    tn = FF_TILE
    a, sg, sv = pl.pallas_call(
        functools.partial(_convglu_prompt_kernel, tiles_per_seq=tiles_per_seq),
        out_shape=(jax.ShapeDtypeStruct((m, D_FF), BF16),
                   jax.ShapeDtypeStruct((m // tm, CONV_W - 1, D_FF), F32),
                   jax.ShapeDtypeStruct((m // tm, CONV_W - 1, D_FF), F32)),
        grid=(m // tm, nt),
        in_specs=[pl.BlockSpec((tm, k), lambda i, j: (i, 0)),
                  pl.BlockSpec((k, 2 * tn), lambda i, j: (0, j)),
                  pl.BlockSpec((CONV_W, tn), lambda i, j: (0, j)),
                  pl.BlockSpec((CONV_W, tn), lambda i, j: (0, j + nt)),
                  pl.BlockSpec((1, tn), lambda i, j: (0, j)),
                  pl.BlockSpec((1, tn), lambda i, j: (0, j + nt))],
        out_specs=(pl.BlockSpec((tm, tn), lambda i, j: (i, j)),
                   pl.BlockSpec((1, CONV_W - 1, tn), lambda i, j: (i, 0, j)),
                   pl.BlockSpec((1, CONV_W - 1, tn), lambda i, j: (i, 0, j))),
        scratch_shapes=[pltpu.VMEM((nt, CONV_W - 1, tn), F32),
                        pltpu.VMEM((nt, CONV_W - 1, tn), F32)],
        compiler_params=_params(2),
        name="convglu_prompt",
    )(h, w_pairs, conv_w, conv_w, conv_b, conv_b)
    tails = jnp.concatenate([sg, sv], axis=-1)
    return a, tails[tiles_per_seq - 1::tiles_per_seq]


def _convglu_sample_kernel(h_ref, wg_ref, wv_ref, cwg_ref, cwv_ref, cbg_ref, cbv_ref,
                           p0g_ref, p0v_ref, p1g_ref, p1v_ref, a_ref, ug_ref, uv_ref, wb_ref):
    def half(w_ref, wb_cols, cw_ref, cb_ref, p0_ref, p1_ref, u_ref):
        u = jnp.dot(h_ref[...], _mxu_weights(w_ref, (wb_ref.at[:, wb_cols],)), preferred_element_type=F32)
        u_ref[...] = u
        return _conv_taps(cb_ref, cw_ref, p0_ref[...], p1_ref[...], u)

    cg = half(wg_ref, slice(0, FF_TILE), cwg_ref, cbg_ref, p0g_ref, p1g_ref, ug_ref)
    cv = half(wv_ref, slice(FF_TILE, 2 * FF_TILE), cwv_ref, cbv_ref, p0v_ref, p1v_ref, uv_ref)
    a_ref[...] = (_gelu_erf(cg) * cv).astype(a_ref.dtype)


def _convglu_sample(h, w_up, conv_w, conv_b, state_conv):
    m, k = h.shape
    nt = N_FF_TILES
    tn = FF_TILE
    prev = state_conv.reshape(m, (CONV_W - 1) * 2 * D_FF)
    col = lambda off: (lambda j: (0, j + off))
    a, ug, uv, w_pairs = pl.pallas_call(
        _convglu_sample_kernel,
        out_shape=(jax.ShapeDtypeStruct((m, D_FF), BF16),
                   jax.ShapeDtypeStruct((m, D_FF), F32),
                   jax.ShapeDtypeStruct((m, D_FF), F32),
                   jax.ShapeDtypeStruct((k, 2 * D_FF), BF16)),
        grid=(nt,),
        in_specs=[pl.BlockSpec((m, k), lambda j: (0, 0)),
                  pl.BlockSpec((k, tn), col(0)),
                  pl.BlockSpec((k, tn), col(nt)),
                  pl.BlockSpec((CONV_W, tn), col(0)),
                  pl.BlockSpec((CONV_W, tn), col(nt)),
                  pl.BlockSpec((1, tn), col(0)),
                  pl.BlockSpec((1, tn), col(nt)),
                  pl.BlockSpec((m, tn), col(0)),
                  pl.BlockSpec((m, tn), col(nt)),
                  pl.BlockSpec((m, tn), col(2 * nt)),
                  pl.BlockSpec((m, tn), col(3 * nt))],
        out_specs=(pl.BlockSpec((m, tn), col(0)),
                   pl.BlockSpec((m, tn), col(0)),
                   pl.BlockSpec((m, tn), col(0)),
                   pl.BlockSpec((k, 2 * tn), col(0))),
        compiler_params=_params(1),
        name="convglu_sample",
    )(h, w_up, w_up, conv_w, conv_w, conv_b, conv_b, prev, prev, prev, prev)
    u = jnp.concatenate([ug, uv], axis=-1)
    conv_new = jnp.stack([state_conv[:, 1, :], u], axis=1)
    return a, conv_new, w_pairs


def _segment_mean_square(x, seg_ones):
    sq = x * x
    hi = sq.astype(BF16)
    lo = (sq - hi.astype(F32)).astype(BF16)
    total = (jnp.dot(hi, seg_ones, preferred_element_type=F32)
             + jnp.dot(lo, seg_ones, preferred_element_type=F32))
    return total / HEAD_DIM


def _attn_prompt_kernel(q_ref, kc_ref, vc_ref, bucket_ref, seg_ones_ref, rb_ref, sinks_ref, gq_ref, gk_ref,
                        o_ref, kn_ref, bias_ref, kdup_ref, vdup_ref, *, blocks_per_seq):
    r = pl.program_id(0)
    blk = ATTN_BLOCK
    pair = 2 * HEAD_DIM
    n_bias_rows = N_ATTN_HEADS * blk
    seq_start = (r % blocks_per_seq) == 0

    @pl.when(r == 0)
    def _():
        bucket = bucket_ref[...]
        prev_cols = lax.broadcasted_iota(jnp.int32, bucket.shape, 1) < blk

        def per_head(h, carry):
            acc = jnp.full(bucket.shape, -jnp.inf, F32)
            for b in range(N_BUCKETS):
                acc = jnp.where(bucket == b, rb_ref[b, h], acc)
            row0 = pl.multiple_of(h * blk, blk)
            bias_ref[pl.ds(row0, blk), :] = acc
            bias_ref[pl.ds(n_bias_rows + row0, blk), :] = jnp.where(prev_cols, -jnp.inf, acc)
            return carry
        lax.fori_loop(0, N_ATTN_HEADS, per_head, 0)
        vdup_ref[:, :, pair:] = jnp.ones((N_KV_HEADS, 2 * blk, pair), BF16)

    @pl.when(seq_start)
    def _():
        kdup_ref[:, :blk, :] = jnp.zeros((N_KV_HEADS, blk, pair), BF16)
        vdup_ref[:, :blk, :pair] = jnp.zeros((N_KV_HEADS, blk, pair), BF16)

    @pl.when(jnp.logical_not(seq_start))
    def _():
        kdup_ref[:, :blk, :] = kdup_ref[:, blk:, :]
        vdup_ref[:, :blk, :pair] = vdup_ref[:, blk:, :pair]

    seg_ones = seg_ones_ref[...]
    low_half = lax.broadcasted_iota(jnp.int32, (blk, pair), 1) < HEAD_DIM

    def duplicate_half(x, odd):
        swapped = pltpu.roll(x, HEAD_DIM, axis=1)
        return jnp.where(low_half, swapped, x) if odd else jnp.where(low_half, x, swapped)

    kc = kc_ref[...]
    kn = kc * lax.rsqrt(_segment_mean_square(kc, seg_ones) + EPS) * gk_ref[...]
    kn_ref[...] = kn
    for j in range(N_KV_HEADS):
        col = slice((j // 2) * pair, (j // 2 + 1) * pair)
        kdup_ref[j, blk:, :] = duplicate_half(kn[:, col], j % 2).astype(BF16)
        vdup_ref[j, blk:, :pair] = duplicate_half(vc_ref[:, col], j % 2).astype(BF16)

    bias_base = jnp.where(seq_start, n_bias_rows, 0)
    chunk = 2 * pair
    n_heads = N_ATTN_HEADS
    q_gain = gq_ref[...] * HEAD_DIM ** -0.5
    q_chunks = [q_ref[:, c * chunk:(c + 1) * chunk] for c in range(ATTN_WIDTH // chunk)]
    q_ms = [_segment_mean_square(qc, seg_ones) for qc in q_chunks]
    q_norm = [qc * lax.rsqrt(ms + EPS) * q_gain for qc, ms in zip(q_chunks, q_ms)]
    lhs = []
    for qn in q_norm:
        for p in range(chunk // pair):
            qp = qn[:, p * pair:(p + 1) * pair]
            lhs.append(jnp.where(low_half, qp, 0.0).astype(BF16))
            lhs.append(jnp.where(low_half, 0.0, qp).astype(BF16))
    scores = [lax.dot_general(jnp.concatenate(lhs[j * GQA_GROUP:(j + 1) * GQA_GROUP], axis=0), kdup_ref[j],
                              (((1,), (1,)), ((), ())), preferred_element_type=F32)
              for j in range(N_KV_HEADS)]
    sg = [scores[h // GQA_GROUP][(h % GQA_GROUP) * blk:(h % GQA_GROUP + 1) * blk, :]
          + bias_ref[pl.ds(pl.multiple_of(bias_base + h * blk, blk), blk), :] for h in range(n_heads)]
    m = [jnp.maximum(jnp.max(sg[h], axis=-1, keepdims=True), sinks_ref[h]) for h in range(n_heads)]
    exps = [jnp.exp(sg[h] - m[h]).astype(BF16) for h in range(n_heads)]
    sink_terms = [jnp.exp(sinks_ref[h] - m[h]) for h in range(n_heads)]
    outs = [jnp.dot(jnp.concatenate(exps[j * GQA_GROUP:(j + 1) * GQA_GROUP], axis=0), vdup_ref[j],
                    preferred_element_type=F32) for j in range(N_KV_HEADS)]
    for h in range(0, n_heads, 2):
        o = outs[h // GQA_GROUP]
        g = h % GQA_GROUP
        even = o[g * blk:(g + 1) * blk, :]
        odd = o[(g + 1) * blk:(g + 2) * blk, :]
        num = jnp.where(low_half, even[:, :pair], odd[:, :pair])
        den = (jnp.where(low_half, even[:, pair:], odd[:, pair:])
               + jnp.where(low_half, sink_terms[h], sink_terms[h + 1]))
        o_ref[:, h * HEAD_DIM:h * HEAD_DIM + pair] = (num / den).astype(o_ref.dtype)


def _attn_prompt(proj, bucket_tile, rel_bias, sinks, g_q, g_k, n_seq):
    m = proj.shape[0]
    blk = ATTN_BLOCK
    nb = m // blk
    blocks_per_seq = nb // n_seq
    lanes = 4 * HEAD_DIM
    seg = np.arange(lanes) // HEAD_DIM
    seg_ones = (seg[:, None] == seg[None, :]).astype(BF16)
    tile4 = lambda g: jnp.tile(g.reshape(-1), lanes // HEAD_DIM).reshape(1, lanes)
    smem = pl.BlockSpec(memory_space=pltpu.SMEM)
    return pl.pallas_call(
        functools.partial(_attn_prompt_kernel, blocks_per_seq=blocks_per_seq),
        out_shape=(jax.ShapeDtypeStruct((m, ATTN_WIDTH), BF16),
                   jax.ShapeDtypeStruct((m, KV_WIDTH), F32)),
        grid=(nb,),
        in_specs=[pl.BlockSpec((blk, ATTN_WIDTH), lambda r: (r, COL_AQ // ATTN_WIDTH)),
                  pl.BlockSpec((blk, KV_WIDTH), lambda r: (r, COL_AK // KV_WIDTH)),
                  pl.BlockSpec((blk, KV_WIDTH), lambda r: (r, COL_AV // KV_WIDTH)),
                  pl.BlockSpec((blk, 2 * blk), lambda r: (0, 0)),
                  pl.BlockSpec((lanes, lanes), lambda r: (0, 0)),
                  smem, smem,
                  pl.BlockSpec((1, lanes), lambda r: (0, 0)),
                  pl.BlockSpec((1, lanes), lambda r: (0, 0))],
        out_specs=(pl.BlockSpec((blk, ATTN_WIDTH), lambda r: (r, 0)),
                   pl.BlockSpec((blk, KV_WIDTH), lambda r: (r, 0))),
        scratch_shapes=[pltpu.VMEM((2 * N_ATTN_HEADS * blk, 2 * blk), F32),
                        pltpu.VMEM((N_KV_HEADS, 2 * blk, 2 * HEAD_DIM), BF16),
                        pltpu.VMEM((N_KV_HEADS, 2 * blk, 4 * HEAD_DIM), BF16)],
        compiler_params=_params(1),
        name="attn_prompt",
    )(proj, proj, proj, bucket_tile, seg_ones, rel_bias, sinks, tile4(g_q), tile4(g_k))


def _attn_sample_kernel(q_ref, knew_ref, vnew_ref, ck_ref, cv_ref, bucket_ref, rbt_ref, sinks_ref,
                        gq_ref, gk_ref, o_ref, cko_ref, cvo_ref):
    win = ck_ref.shape[1]
    bucket = bucket_ref[...]
    bias = jnp.full((N_ATTN_HEADS, bucket.shape[1]), -jnp.inf, F32)
    for b in range(N_BUCKETS):
        bias = jnp.where(bucket == b, rbt_ref[:, b:b + 1], bias)
    bias_c = bias[:, :win]
    bias_n = bias[:, win:win + 1]
    sink = sinks_ref[...]
    head_group = jnp.right_shift(lax.broadcasted_iota(jnp.int32, (1, N_ATTN_HEADS, HEAD_DIM), 1),
                                 int(math.log2(GQA_GROUP)))
    lane_group = jnp.right_shift(lax.broadcasted_iota(jnp.int32, (1, 1, KV_WIDTH), 2),
                                 int(math.log2(HEAD_DIM)))
    scale = HEAD_DIM ** -0.5

    k_rows = knew_ref[...]
    k_sq = k_rows * k_rows
    inv = jnp.zeros_like(k_rows)
    for j in range(N_KV_HEADS):
        ms = jnp.sum(jnp.where(lane_group == j, k_sq, 0.0), axis=-1, keepdims=True) / HEAD_DIM
        inv = jnp.where(lane_group == j, lax.rsqrt(ms + EPS), inv)
    kn_rows = k_rows * inv * gk_ref[...]
    v_rows = vnew_ref[...]

    qn = _rms(q_ref[...], gq_ref[...])
    q_bd = jnp.concatenate([jnp.where(head_group == j, qn, 0.0) for j in range(N_KV_HEADS)], axis=2)
    s_c = jnp.einsum('bhd,bkd->bhk', q_bd.astype(BF16), ck_ref[...].astype(BF16),
                     preferred_element_type=F32) * scale + bias_c
    s_n = jnp.sum(q_bd * kn_rows, axis=-1, keepdims=True) * scale + bias_n
    m = jnp.maximum(jnp.maximum(jnp.max(s_c, axis=-1, keepdims=True), s_n), sink)
    e_c = jnp.exp(s_c - m)
    e_n = jnp.exp(s_n - m)
    denom = jnp.sum(e_c, axis=-1, keepdims=True) + e_n + jnp.exp(sink - m)
    o_full = jnp.einsum('bhk,bkd->bhd', (e_c / denom).astype(BF16), cv_ref[...].astype(BF16),
                        preferred_element_type=F32)
    o_full = o_full + (e_n / denom) * v_rows
    o = jnp.zeros(q_ref.shape, F32)
    for j in range(N_KV_HEADS):
        o = jnp.where(head_group == j, o_full[:, :, j * HEAD_DIM:(j + 1) * HEAD_DIM], o)
    o_ref[...] = o.astype(o_ref.dtype)

    cko_ref[:, pl.ds(0, win - 1), :] = ck_ref[:, pl.ds(1, win - 1), :]
    cko_ref[:, pl.ds(win - 1, 1), :] = kn_rows
    cvo_ref[:, pl.ds(0, win - 1), :] = cv_ref[:, pl.ds(1, win - 1), :]
    cvo_ref[:, pl.ds(win - 1, 1), :] = v_rows


def _attn_sample(q, k_new, v_new, cache_k, cache_v, bucket_row, rel_bias_t, sinks_col, g_q, g_k_row, bt):
    nb = q.shape[0]
    win = cache_k.shape[1]
    full = lambda shape: pl.BlockSpec(shape, lambda i: (0,) * len(shape))
    return pl.pallas_call(
        _attn_sample_kernel,
        out_shape=(jax.ShapeDtypeStruct((nb, N_ATTN_HEADS, HEAD_DIM), BF16),
                   jax.ShapeDtypeStruct(cache_k.shape, F32),
                   jax.ShapeDtypeStruct(cache_v.shape, F32)),
        grid=(nb // bt,),
        in_specs=[pl.BlockSpec((bt, N_ATTN_HEADS, HEAD_DIM), lambda i: (i, 0, 0)),
                  pl.BlockSpec((bt, 1, KV_WIDTH), lambda i: (i, 0, 0)),
                  pl.BlockSpec((bt, 1, KV_WIDTH), lambda i: (i, 0, 0)),
                  pl.BlockSpec((bt, win, KV_WIDTH), lambda i: (i, 0, 0)),
                  pl.BlockSpec((bt, win, KV_WIDTH), lambda i: (i, 0, 0)),
                  full(bucket_row.shape), full(rel_bias_t.shape), full(sinks_col.shape),
                  full(g_q.shape), full(g_k_row.shape)],
        out_specs=(pl.BlockSpec((bt, N_ATTN_HEADS, HEAD_DIM), lambda i: (i, 0, 0)),
                   pl.BlockSpec((bt, win, KV_WIDTH), lambda i: (i, 0, 0)),
                   pl.BlockSpec((bt, win, KV_WIDTH), lambda i: (i, 0, 0))),
        compiler_params=_params(1),
        name="attn_sample",
    )(q, k_new, v_new, cache_k, cache_v, bucket_row, rel_bias_t, sinks_col, g_q, g_k_row)


def _ret_prompt_kernel(q_ref, k_ref, v_ref, gate_ref, cos_ref, sin_ref, dmask_ref, qdec_ref, kdec_ref,
                       cdec_ref, o_ref, state_ref, s_ref):
    c = pl.program_id(0)

    @pl.when(c == 0)
    def _():
        s_ref[...] = jnp.zeros_like(s_ref)

    cosf = cos_ref[...]
    sinf = sin_ref[...]
    half = RET_KDIM // 2
    nt = (((1,), (1,)), ((), ()))
    tn = (((0,), (0,)), ((), ()))

    def rotary(x):
        return x * cosf + pltpu.roll(x, half, axis=1) * sinf

    for h in range(N_RET_HEADS):
        kd = slice(h * RET_KDIM, (h + 1) * RET_KDIM)
        vd = slice(h * RET_VDIM, (h + 1) * RET_VDIM)
        for b in range(q_ref.shape[0]):
            q = rotary(q_ref[b, :, kd])
            k = rotary(k_ref[b, :, kd]) * RET_KDIM ** -0.5
            v = v_ref[b, :, vd].astype(BF16)
            s0 = s_ref[b, h]
            scores = lax.dot_general(q.astype(BF16), k.astype(BF16), nt,
                                     preferred_element_type=F32) * dmask_ref[h]
            o_intra = jnp.dot(scores.astype(BF16), v, preferred_element_type=F32)
            o_inter = jnp.dot((q * qdec_ref[h]).astype(BF16), s0.astype(BF16), preferred_element_type=F32)
            s_ref[b, h] = cdec_ref[h] * s0 + lax.dot_general((k * kdec_ref[h]).astype(BF16), v, tn,
                                                              preferred_element_type=F32)
            ro = o_intra + o_inter
            ro = ro * lax.rsqrt(jnp.mean(ro * ro, axis=-1, keepdims=True) + EPS)
            o_ref[b, :, vd] = (jax.nn.silu(gate_ref[b, :, vd]) * ro).astype(o_ref.dtype)

    @pl.when(c == pl.num_programs(0) - 1)
    def _():
        state_ref[...] = s_ref[...]


def _ret_prompt(proj, cosf, sinf, dmask, qdec, kdec, cdec, n_seq):
    m, width = proj.shape
    seq = m // n_seq
    ch = RET_CHUNK
    proj3 = proj.reshape(n_seq, seq, width)
    const3 = pl.BlockSpec((N_RET_HEADS, ch, ch), lambda c: (0, 0, 0))
    state_shape = (n_seq, N_RET_HEADS, RET_KDIM, RET_VDIM)
    o, state = pl.pallas_call(
        _ret_prompt_kernel,
        out_shape=(jax.ShapeDtypeStruct((n_seq, seq, RET_V_WIDTH), BF16),
                   jax.ShapeDtypeStruct(state_shape, F32)),
        grid=(seq // ch,),
        in_specs=[pl.BlockSpec((n_seq, ch, RET_QK_WIDTH), lambda c: (0, c, COL_RQ // RET_QK_WIDTH)),
                  pl.BlockSpec((n_seq, ch, RET_QK_WIDTH), lambda c: (0, c, COL_RK // RET_QK_WIDTH)),
                  pl.BlockSpec((n_seq, ch, RET_V_WIDTH), lambda c: (0, c, COL_RV // RET_V_WIDTH)),
                  pl.BlockSpec((n_seq, ch, RET_V_WIDTH), lambda c: (0, c, COL_RG // RET_V_WIDTH)),
                  pl.BlockSpec((ch, RET_KDIM), lambda c: (c, 0)),
                  pl.BlockSpec((ch, RET_KDIM), lambda c: (c, 0)),
                  const3, const3, const3,
                  pl.BlockSpec(memory_space=pltpu.SMEM)],
        out_specs=(pl.BlockSpec((n_seq, ch, RET_V_WIDTH), lambda c: (0, c, 0)),
                   pl.BlockSpec(state_shape, lambda c: (0, 0, 0, 0))),
        scratch_shapes=[pltpu.VMEM(state_shape, F32)],
        compiler_params=_params(1),
        name="ret_prompt",
    )(proj3, proj3, proj3, proj3, cosf, sinf, dmask, qdec, kdec, cdec)
    return o.reshape(m, RET_V_WIDTH), state


def _ret_sample_kernel(qr_ref, qt_ref, kt_ref, cosr_ref, sinr_ref, cos_ref, sin_ref, v_ref, gate_ref, s_ref,
                       qdec_ref, kdec_ref, cdec_ref, o_ref, so_ref):
    bt = s_ref.shape[0]
    half = RET_KDIM // 2
    cos = cos_ref[...]
    sin = sin_ref[...]
    cosr = cosr_ref[...]
    sinr = sinr_ref[...]

    def rotary(x):
        x1, x2 = x[:half, :], x[half:, :]
        return jnp.concatenate([x1 * cos - x2 * sin, x2 * cos + x1 * sin], axis=0)

    for h in range(N_RET_HEADS):
        vd = slice(h * RET_VDIM, (h + 1) * RET_VDIM)
        q_all = rotary(qt_ref[0, h])
        k_all = rotary(kt_ref[0, h]) * RET_KDIM ** -0.5
        qk_all = jnp.sum(q_all * k_all, axis=0, keepdims=True)
        kd_all = k_all * kdec_ref[h]
        q_rows = qr_ref[0, :, h * RET_KDIM:(h + 1) * RET_KDIM]
        q_rows = q_rows * cosr + pltpu.roll(q_rows, half, axis=1) * sinr
        qd_rows = (q_rows * qdec_ref[h]).astype(BF16)
        for b in range(bt):
            v = v_ref[0, b:b + 1, vd]
            s0 = s_ref[b, h]
            o_intra = qk_all[:, b:b + 1] * v
            o_inter = jnp.dot(qd_rows, s0.astype(BF16), preferred_element_type=F32)[b:b + 1, :]
            so_ref[b, h] = cdec_ref[h] * s0 + kd_all[:, b:b + 1] * v
            ro = o_intra + o_inter
            ro = ro * lax.rsqrt(jnp.mean(ro * ro, axis=-1, keepdims=True) + EPS)
            o_ref[0, b:b + 1, vd] = (jax.nn.silu(gate_ref[0, b:b + 1, vd]) * ro).astype(o_ref.dtype)


def _ret_sample(qr, qt, kt, cos_row, sin_row, cos_col, sin_col, v, gate, state, qdec, kdec, cdec):
    nbt, _, _, bt = qt.shape
    smem = pl.BlockSpec(memory_space=pltpu.SMEM)
    st_spec = pl.BlockSpec((bt, N_RET_HEADS, RET_KDIM, RET_VDIM), lambda i: (i, 0, 0, 0))
    qk_spec = pl.BlockSpec((1, N_RET_HEADS, RET_KDIM, bt), lambda i: (i, 0, 0, 0))
    row_spec = pl.BlockSpec((1, bt, RET_V_WIDTH), lambda i: (i, 0, 0))
    col_spec = pl.BlockSpec(cos_col.shape, lambda i: (0, 0))
    trig_row_spec = pl.BlockSpec(cos_row.shape, lambda i: (0, 0))
    q_row_spec = pl.BlockSpec((1, bt, RET_QK_WIDTH), lambda i: (i, 0, 0))
    return pl.pallas_call(
        _ret_sample_kernel,
        out_shape=(jax.ShapeDtypeStruct((nbt, bt, RET_V_WIDTH), BF16),
                   jax.ShapeDtypeStruct(state.shape, F32)),
        grid=(nbt,),
        in_specs=[q_row_spec, qk_spec, qk_spec, trig_row_spec, trig_row_spec, col_spec, col_spec,
                  row_spec, row_spec, st_spec, smem, smem, smem],
        out_specs=(row_spec, st_spec),
        compiler_params=_params(1),
        name="ret_sample",
    )(qr, qt, kt, cos_row, sin_row, cos_col, sin_col, v, gate, state, qdec, kdec, cdec)


NPF = np.float32


def _t5_bucket(dist):
    n = np.maximum(dist, 0)
    max_exact = N_BUCKETS // 2
    nf = np.maximum(n, 1).astype(NPF)
    large = max_exact + (np.log(nf / NPF(max_exact)) / NPF(math.log(MAX_DISTANCE / max_exact))
                         * NPF(N_BUCKETS - max_exact)).astype(np.int32)
    return np.where(n < max_exact, n, np.minimum(large, N_BUCKETS - 1)).astype(np.int32)


def _rope_tables(pos):
    half = RET_KDIM // 2
    inv = NPF(ROPE_BASE) ** (-np.arange(half, dtype=NPF) / NPF(half))
    ang = pos.astype(NPF)[:, None] * inv[None]
    return np.cos(ang), np.sin(ang)


def _layer(xp, xs, pp, ps, cache_k, cache_v, state_ret, state_conv, rel_bias, lp, n_seq):
    (g_mix, w_in, g_q, g_k, sinks, w_out, g_ffn, w_up, conv_w, conv_b, w_down,
     g_ple, w_ple_gate, w_ple_proj) = lp
    mp = xp.shape[0]
    ms = xs.shape[0]
    seq = mp // n_seq
    win = cache_k.shape[1]

    row = lambda a: a.reshape(1, -1)

    log_decay = np.log(NPF(1.0) - NPF(2.0) ** (NPF(-5.0) - np.arange(N_RET_HEADS, dtype=NPF)))
    idx = np.arange(RET_CHUNK, dtype=NPF)
    diff = idx[:, None] - idx[None, :]
    dmask = np.where(diff[None] >= 0, np.exp(diff[None] * log_decay[:, None, None]), NPF(0.0))
    q_dec = np.exp((idx + NPF(1.0))[:, None] * log_decay[None])
    k_dec = np.exp((NPF(RET_CHUNK - 1.0) - idx)[:, None] * log_decay[None])
    qdec_t = np.broadcast_to(q_dec.T[:, :, None], (N_RET_HEADS, RET_CHUNK, RET_KDIM))
    kdec_t = np.broadcast_to(k_dec.T[:, :, None], (N_RET_HEADS, RET_CHUNK, RET_KDIM))
    cdec = np.exp(NPF(RET_CHUNK) * log_decay)
    cos_p, sin_p = _rope_tables(np.arange(seq, dtype=np.int32))
    cosf = np.concatenate([cos_p, cos_p], axis=1)
    sinf = np.concatenate([-sin_p, sin_p], axis=1)
    one = np.arange(1, dtype=NPF)
    qdec_s = np.exp((one + NPF(1.0))[:, None] * log_decay[None])[0]
    kdec_s = np.exp((NPF(1.0 - 1.0) - one)[:, None] * log_decay[None])[0]
    cdec_s = np.exp(NPF(1.0) * log_decay)
    cos_s, sin_s = _rope_tables(PAST_LEN + np.arange(1, dtype=np.int32))
    cos_col, sin_col = cos_s.reshape(-1, 1), sin_s.reshape(-1, 1)

    qi = np.arange(ATTN_BLOCK, dtype=np.int32)
    ki = np.arange(2 * ATTN_BLOCK, dtype=np.int32) - ATTN_BLOCK
    dist = qi[:, None] - ki[None, :]
    bucket_tile = np.where((dist >= 0) & (dist <= WINDOW), _t5_bucket(dist), -1).astype(np.int32)
    dist_s = win - np.arange(2 * ATTN_BLOCK, dtype=np.int32)
    bucket_row = np.where((dist_s >= 0) & (dist_s <= WINDOW), _t5_bucket(dist_s), -1).astype(np.int32).reshape(1, -1)

    tmp, tms = _row_tile(mp), _row_tile(ms)
    proj_s, w_in_b = _matmul(_norm(xs, row(g_mix), tms), w_in, tms, COL_TILE, IN_SRC_BLOCKS)
    proj_p, = _matmul(_norm(xp, row(g_mix), NORM_TILE), w_in_b, min(mp, IN_ROW_TILE), COL_TILE)

    attn_p, kn_p = _attn_prompt(proj_p, bucket_tile, rel_bias, sinks, row(g_q), row(g_k), n_seq)
    ret_p, ret_state_p = _ret_prompt(proj_p, cosf, sinf, dmask, qdec_t, kdec_t, cdec, n_seq)

    bt_a = 8
    attn_s, ck_new, cv_new = _attn_sample(
        proj_s[:, COL_AQ:COL_AQ + ATTN_WIDTH].reshape(ms, N_ATTN_HEADS, HEAD_DIM),
        proj_s[:, COL_AK:COL_AK + KV_WIDTH].reshape(ms, 1, KV_WIDTH),
        proj_s[:, COL_AV:COL_AV + KV_WIDTH].reshape(ms, 1, KV_WIDTH),
        cache_k.reshape(ms, win, KV_WIDTH), cache_v.reshape(ms, win, KV_WIDTH),
        bucket_row, rel_bias.T, sinks.reshape(-1, 1), row(g_q), jnp.tile(g_k, N_KV_HEADS).reshape(1, -1), bt_a)
    attn_s = attn_s.reshape(ms, ATTN_WIDTH)

    bt_r = 8
    to_cols = lambda a: a.reshape(ms // bt_r, bt_r, N_RET_HEADS, RET_KDIM).transpose(0, 2, 3, 1)
    rq_s = proj_s[:, COL_RQ:COL_RQ + RET_QK_WIDTH]
    ret_s, ret_state_s = _ret_sample(
        rq_s.reshape(ms // bt_r, bt_r, RET_QK_WIDTH), to_cols(rq_s), to_cols(proj_s[:, COL_RK:COL_RK + RET_QK_WIDTH]),
        np.concatenate([cos_s, cos_s], axis=1), np.concatenate([-sin_s, sin_s], axis=1), cos_col, sin_col,
        proj_s[:, COL_RV:COL_RV + RET_V_WIDTH].reshape(ms // bt_r, bt_r, RET_V_WIDTH),
        proj_s[:, COL_RG:COL_RG + RET_V_WIDTH].reshape(ms // bt_r, bt_r, RET_V_WIDTH),
        state_ret, qdec_s, kdec_s, cdec_s)
    ret_s = ret_s.reshape(ms, RET_V_WIDTH)

    xs, w_out_b = _out_proj(attn_s, ret_s, w_out, xs, tms, COL_TILE)
    xp, = _out_proj(attn_p, ret_p, w_out_b, xp, tmp, OUT_COL_TILE)

    act_s, conv_s, w_up_b = _convglu_sample(_norm(xs, row(g_ffn), tms), w_up, conv_w, row(conv_b), state_conv)
    act_p, conv_p = _convglu_prompt(_norm(xp, row(g_ffn), NORM_TILE), w_up_b, conv_w, row(conv_b), n_seq, UP_ROW_TILE)
    for part in range(DOWN_K_PARTS):
        last = part == DOWN_K_PARTS - 1
        xs, w_down_b = _down_proj(act_s, w_down, xs, tms, COL_TILE, DOWN_K_PARTS, part)
        xp, *stats_p = _down_proj(act_p, w_down_b, xp, tmp, COL_TILE, DOWN_K_PARTS, part,
                                  row(g_ple) if last else None)
    hp, ssq_p = stats_p

    xs, w_pgate_b, w_pproj_b = _ple(xs, _norm(xs, row(g_ple), tms), None, w_ple_gate, ps, w_ple_proj, tms, COL_TILE)
    xp, = _ple(xp, hp, ssq_p, w_pgate_b, pp, w_pproj_b, tmp, COL_TILE)

    kp_new = kn_p.reshape(n_seq, seq, N_KV_HEADS, HEAD_DIM)[:, -WINDOW:]
    vp_new = proj_p[:, COL_AV:COL_AV + KV_WIDTH].reshape(n_seq, seq, N_KV_HEADS, HEAD_DIM)[:, -WINDOW:]
    ks_new = ck_new.reshape(ms, win, N_KV_HEADS, HEAD_DIM)
    vs_new = cv_new.reshape(ms, win, N_KV_HEADS, HEAD_DIM)
    return xp, xs, kp_new, vp_new, ret_state_p, conv_p, ks_new, vs_new, ret_state_s, conv_s


def kernel(x_prompt, x_sample, p_prompt, p_sample, cache_win_k, cache_win_v, state_ret, state_conv, rel_bias, g_mix, w_in, g_q, g_k, sinks, w_out, g_ffn, w_up, conv_w, conv_b, w_down, g_ple, w_ple_gate, w_ple_proj):
    depth = g_mix.shape[0]
    n_seq, seq, d = x_prompt.shape
    nb, dec_seq, _ = x_sample.shape
    assert dec_seq == 1 and seq % 512 == 0 and d == D_MODEL
    xp = x_prompt.reshape(n_seq * seq, d)
    xs = x_sample.reshape(nb, d)
    outs = [[] for _ in range(8)]
    for l in range(depth):
        lp = (g_mix[l], w_in[l], g_q[l], g_k[l], sinks[l], w_out[l], g_ffn[l], w_up[l], conv_w[l],
              conv_b[l], w_down[l], g_ple[l], w_ple_gate[l], w_ple_proj[l])
        res = _layer(xp, xs, p_prompt[l].reshape(n_seq * seq, -1), p_sample[l].reshape(nb, -1),
                     cache_win_k[l], cache_win_v[l], state_ret[l], state_conv[l], rel_bias, lp, n_seq)
        xp, xs = res[0], res[1]
        for o, r in zip(outs, res[2:]):
            o.append(r)
    stacked = [jnp.stack(o) for o in outs]
    return (xp.reshape(n_seq, seq, d), xs.reshape(nb, 1, d), *stacked)
```

```python
import functools
import math

import jax
import jax.numpy as jnp
import numpy as np
from jax import lax
from jax.experimental import pallas as pl
from jax.experimental.pallas import tpu as pltpu

F32 = jnp.float32
BF16 = jnp.bfloat16

D_MODEL = 4096
HEAD_DIM = 64
N_ATTN_HEADS = 32
N_KV_HEADS = 4
GQA_GROUP = N_ATTN_HEADS // N_KV_HEADS
WINDOW = 128
ATTN_BLOCK = 128
N_BUCKETS = 32
MAX_DISTANCE = 128
N_RET_HEADS = 8
RET_KDIM = 128
RET_VDIM = 256
RET_CHUNK = 128
ROPE_BASE = 10000.0
D_FF = 11008
CONV_W = 3
EPS = 1e-6

ATTN_WIDTH = N_ATTN_HEADS * HEAD_DIM
KV_WIDTH = N_KV_HEADS * HEAD_DIM
RET_QK_WIDTH = N_RET_HEADS * RET_KDIM
RET_V_WIDTH = N_RET_HEADS * RET_VDIM
IN_SIZES = (ATTN_WIDTH, KV_WIDTH, KV_WIDTH, RET_QK_WIDTH, RET_QK_WIDTH, RET_V_WIDTH, RET_V_WIDTH)
IN_WIDTH = sum(IN_SIZES)
IN_SPLITS = tuple(sum(IN_SIZES[:n + 1]) for n in range(len(IN_SIZES) - 1))
PAST_LEN = 8192

COL_AQ = 0
COL_RV = ATTN_WIDTH
COL_RG = COL_RV + RET_V_WIDTH
COL_RQ = COL_RG + RET_V_WIDTH
COL_RK = COL_RQ + RET_QK_WIDTH
COL_AK = COL_RK + RET_QK_WIDTH
COL_AV = COL_AK + KV_WIDTH

VMEM_LIMIT_BYTES = 56 * 1024 * 1024
FF_TILE = 256
N_FF_TILES = D_FF // FF_TILE
NORM_ROWS = 128
NORM_TILE = 512
OUT_COL_TILE = 1024
STAT_LANES = 128
IN_ROW_TILE = 2048
ROW_TILE = 1024
COL_TILE = 512
UP_ROW_TILE = 2048
CONV_CHUNK = 1024
DOWN_K_PARTS = 2


def _row_tile(m):
    return min(m, ROW_TILE)


def _in_src_blocks():
    start = dict(zip(("aq", "ak", "av", "rq", "rk", "rv", "rg"), (0,) + IN_SPLITS))
    order = (("aq", ATTN_WIDTH), ("rv", RET_V_WIDTH), ("rg", RET_V_WIDTH), ("rq", RET_QK_WIDTH),
             ("rk", RET_QK_WIDTH), ("ak", KV_WIDTH), ("av", KV_WIDTH))
    cols = [c for name, width in order for c in range(start[name], start[name] + width, KV_WIDTH)]
    per_tile = COL_TILE // KV_WIDTH
    blocks = []
    for b in range(0, len(cols), per_tile):
        group = cols[b:b + per_tile]
        assert group[0] % COL_TILE == 0 and all(c == group[0] + u * KV_WIDTH for u, c in enumerate(group))
        blocks.append(group[0] // COL_TILE)
    return tuple(blocks)


IN_SRC_BLOCKS = _in_src_blocks()


def _params(n_axes):
    return pltpu.CompilerParams(dimension_semantics=("arbitrary",) * n_axes,
                                vmem_limit_bytes=VMEM_LIMIT_BYTES)


def _rms(x, g):
    y = x * lax.rsqrt(jnp.mean(x * x, axis=-1, keepdims=True) + EPS)
    return y * g


def _norm_rows_to_bf16(x_ref, g_ref, h_ref):
    def body(c, carry):
        r0 = pl.multiple_of(c * NORM_ROWS, NORM_ROWS)
        h_ref[pl.ds(r0, NORM_ROWS), :] = _rms(x_ref[pl.ds(r0, NORM_ROWS), :], g_ref[...]).astype(h_ref.dtype)
        return carry
    lax.fori_loop(0, x_ref.shape[0] // NORM_ROWS, body, 0)


def _norm(x, g, tr):
    m, k = x.shape
    return pl.pallas_call(
        _norm_rows_to_bf16,
        out_shape=jax.ShapeDtypeStruct((m, k), BF16),
        grid=(m // tr,),
        in_specs=[pl.BlockSpec((tr, k), lambda i: (i, 0)),
                  pl.BlockSpec((1, k), lambda i: (0, 0))],
        out_specs=pl.BlockSpec((tr, k), lambda i: (i, 0)),
        compiler_params=_params(1),
        name="rmsnorm",
    )(x, g)


def _mxu_weights(w_ref, wb_ref):
    if not wb_ref:
        return w_ref[...]
    wb = w_ref[...].astype(BF16)
    wb_ref[0][...] = wb
    return wb


def _mm_kernel(h_ref, w_ref, o_ref, *wb_ref):
    o_ref[...] = jnp.dot(h_ref[...], _mxu_weights(w_ref, wb_ref), preferred_element_type=F32)


def _matmul(h, w, tm, tn, src_blocks=None):
    m, k = h.shape
    n = w.shape[1]
    emit = w.dtype != BF16
    if emit:
        w_map = lambda i, j: (0, sum(jnp.where(j == d, s, 0) for d, s in enumerate(src_blocks)))
    else:
        w_map = lambda i, j: (0, j)
    out_shape = [jax.ShapeDtypeStruct((m, n), F32)]
    out_specs = [pl.BlockSpec((tm, tn), lambda i, j: (i, j))]
    if emit:
        assert m == tm
        out_shape.append(jax.ShapeDtypeStruct((k, n), BF16))
        out_specs.append(pl.BlockSpec((k, tn), lambda i, j: (0, j)))
    return pl.pallas_call(
        _mm_kernel,
        out_shape=out_shape,
        grid=(m // tm, n // tn),
        in_specs=[pl.BlockSpec((tm, k), lambda i, j: (i, 0)),
                  pl.BlockSpec((k, tn), w_map)],
        out_specs=out_specs,
        compiler_params=_params(2),
        name="in_proj",
    )(h, w)


def _row_scale(ssq_ref, width, n_cols):
    r = lax.rsqrt(ssq_ref[...] / width + EPS)
    return jnp.concatenate([r] * (n_cols // STAT_LANES), axis=1)


def _residual_mm_kernel(*refs, n_lhs, emit_weights, norm_stats):
    lhs_refs = refs[:n_lhs]
    w_ref, x_ref = refs[n_lhs], refs[n_lhs + 1]
    rest = list(refs[n_lhs + 2:])
    g_ref = rest.pop(0) if norm_stats else None
    o_ref = rest.pop(0)
    wb_ref = (rest.pop(0),) if emit_weights else ()
    lhs = jnp.concatenate([r[...] for r in lhs_refs], axis=1) if n_lhs > 1 else lhs_refs[0][...]
    o = x_ref[...] + jnp.dot(lhs, _mxu_weights(w_ref, wb_ref), preferred_element_type=F32)
    o_ref[...] = o
    if norm_stats:
        hb_ref, ssq_ref = rest
        hb_ref[...] = (o * g_ref[...]).astype(hb_ref.dtype)
        part = jnp.broadcast_to(jnp.sum(o * o, axis=-1, keepdims=True), ssq_ref.shape)
        j = pl.program_id(1)

        @pl.when(j == 0)
        def _():
            ssq_ref[...] = part

        @pl.when(j > 0)
        def _():
            ssq_ref[...] += part


def _residual_matmul(lhs, lhs_specs, w, w_spec, wb_shape, x, g_next, tm, tn, name):
    m, n = x.shape
    emit = w.dtype != BF16
    stats = g_next is not None
    tile = pl.BlockSpec((tm, tn), lambda i, j: (i, j))
    operands = [*lhs, w, x]
    in_specs = [*lhs_specs, w_spec, tile]
    out_shape = [jax.ShapeDtypeStruct((m, n), F32)]
    out_specs = [tile]
    if stats:
        operands.append(g_next)
        in_specs.append(pl.BlockSpec((1, tn), lambda i, j: (0, j)))
    if emit:
        assert m == tm
        out_shape.append(jax.ShapeDtypeStruct(wb_shape, BF16))
        out_specs.append(pl.BlockSpec((wb_shape[0], tn), lambda i, j: (0, j)))
    if stats:
        out_shape += [jax.ShapeDtypeStruct((m, n), BF16), jax.ShapeDtypeStruct((m, STAT_LANES), F32)]
        out_specs += [tile, pl.BlockSpec((tm, STAT_LANES), lambda i, j: (i, 0))]
    return pl.pallas_call(
        functools.partial(_residual_mm_kernel, n_lhs=len(lhs), emit_weights=emit, norm_stats=stats),
        out_shape=out_shape,
        grid=(m // tm, n // tn),
        in_specs=in_specs,
        out_specs=out_specs,
        compiler_params=_params(2),
        name=name,
    )(*operands)


def _out_proj(a, r, w, x, tm, tn, g_next=None):
    ka, kr = a.shape[1], r.shape[1]
    lhs_specs = [pl.BlockSpec((tm, ka), lambda i, j: (i, 0)), pl.BlockSpec((tm, kr), lambda i, j: (i, 0))]
    w_spec = pl.BlockSpec((ka + kr, tn), lambda i, j: (0, j))
    return _residual_matmul([a, r], lhs_specs, w, w_spec, w.shape, x, g_next, tm, tn, "out_proj")


def _down_proj(a, w, x, tm, tn, k_parts, part, g_next=None):
    k = a.shape[1] // k_parts
    w_part = part if w.dtype != BF16 else 0
    lhs_specs = [pl.BlockSpec((tm, k), lambda i, j: (i, part))]
    w_spec = pl.BlockSpec((k, tn), lambda i, j: (w_part, j))
    return _residual_matmul([a], lhs_specs, w, w_spec, (k, w.shape[1]), x, g_next, tm, tn, "down_proj")


def _ple_kernel(*refs, emit_weights, row_scaled):
    h_ref, wg_ref, p_ref, wp_ref, x_ref = refs[:5]
    rest = list(refs[5:])
    ssq_ref = rest.pop(0) if row_scaled else None
    o_ref = rest.pop(0)
    wb_refs = tuple(rest) if emit_weights else ()
    z = jnp.dot(h_ref[...], _mxu_weights(wg_ref, wb_refs[:1]), preferred_element_type=F32)
    if row_scaled:
        z = z * _row_scale(ssq_ref, h_ref.shape[1], z.shape[1])
    pp = jnp.dot(p_ref[...].astype(BF16), _mxu_weights(wp_ref, wb_refs[1:]), preferred_element_type=F32)
    o_ref[...] = x_ref[...] + jax.nn.sigmoid(z) * pp


def _ple(x, h, ssq, wg, p, wp, tm, tn):
    m, k = h.shape
    n = wg.shape[1]
    kp = p.shape[1]
    emit = wg.dtype != BF16
    scaled = ssq is not None
    operands = [h, wg, p, wp, x]
    in_specs = [pl.BlockSpec((tm, k), lambda i, j: (i, 0)),
                pl.BlockSpec((k, tn), lambda i, j: (0, j)),
                pl.BlockSpec((tm, kp), lambda i, j: (i, 0)),
                pl.BlockSpec((kp, tn), lambda i, j: (0, j)),
                pl.BlockSpec((tm, tn), lambda i, j: (i, j))]
    if scaled:
        operands.append(ssq)
        in_specs.append(pl.BlockSpec((tm, STAT_LANES), lambda i, j: (i, 0)))
    out_shape = [jax.ShapeDtypeStruct((m, n), F32)]
    out_specs = [pl.BlockSpec((tm, tn), lambda i, j: (i, j))]
    if emit:
        assert m == tm
        out_shape += [jax.ShapeDtypeStruct(wg.shape, BF16), jax.ShapeDtypeStruct(wp.shape, BF16)]
        out_specs += [pl.BlockSpec((k, tn), lambda i, j: (0, j)), pl.BlockSpec((kp, tn), lambda i, j: (0, j))]
    return pl.pallas_call(
        functools.partial(_ple_kernel, emit_weights=emit, row_scaled=scaled),
        out_shape=out_shape,
        grid=(m // tm, n // tn),
        in_specs=in_specs,
        out_specs=out_specs,
        compiler_params=_params(2),
        name="ple",
    )(*operands)


def _gelu_erf(x):
    return 0.5 * x * (1.0 + lax.erf(x * math.sqrt(0.5)))


def _conv_taps(cb_ref, cw_ref, um2, um1, u):
    c = cb_ref[...] + cw_ref[0:1, :] * um2
    c = c + cw_ref[1:2, :] * um1
    return c + cw_ref[2:3, :] * u


def _convglu_prompt_kernel(h_ref, w_ref, cwg_ref, cwv_ref, cbg_ref, cbv_ref,
                           a_ref, sg_ref, sv_ref, carry_g_ref, carry_v_ref, *, tiles_per_seq):
    i = pl.program_id(0)
    j = pl.program_id(1)
    tm = h_ref.shape[0]
    rows = min(tm, CONV_CHUNK)
    seq_start = (i % tiles_per_seq) == 0
    sub = 8
    row = lax.broadcasted_iota(jnp.int32, (sub, 1), 0)

    def half(u, cw_ref, cb_ref, prev):
        r1 = pltpu.roll(u, 1, axis=0)
        r2 = pltpu.roll(u, 2, axis=0)
        head1 = jnp.where(row == 0, prev[1:2, :], r1[:sub, :])
        head2 = jnp.where(row == 0, prev[0:1, :], jnp.where(row == 1, prev[1:2, :], r2[:sub, :]))
        um1 = jnp.concatenate([head1, r1[sub:, :]], axis=0)
        um2 = jnp.concatenate([head2, r2[sub:, :]], axis=0)
        return _conv_taps(cb_ref, cw_ref, um2, um1, u), u[rows - (CONV_W - 1):, :]

    prev_g = jnp.where(seq_start, 0.0, carry_g_ref[j])
    prev_v = jnp.where(seq_start, 0.0, carry_v_ref[j])
    for c in range(tm // rows):
        rs = slice(c * rows, (c + 1) * rows)
        u = jnp.dot(h_ref[rs, :], w_ref[...], preferred_element_type=F32)
        cg, prev_g = half(u[:, :FF_TILE], cwg_ref, cbg_ref, prev_g)
        cv, prev_v = half(u[:, FF_TILE:], cwv_ref, cbv_ref, prev_v)
        a_ref[rs, :] = (_gelu_erf(cg) * cv).astype(a_ref.dtype)
    carry_g_ref[j] = prev_g
    carry_v_ref[j] = prev_v
    sg_ref[0] = prev_g
    sv_ref[0] = prev_v


def _convglu_prompt(h, w_pairs, conv_w, conv_b, n_seq, tm):
    m, k = h.shape
    seq = m // n_seq
    tiles_per_seq = seq // tm
    nt = N_FF_TILES
    tn = FF_TILE
    a, sg, sv = pl.pallas_call(
        functools.partial(_convglu_prompt_kernel, tiles_per_seq=tiles_per_seq),
        out_shape=(jax.ShapeDtypeStruct((m, D_FF), BF16),
                   jax.ShapeDtypeStruct((m // tm, CONV_W - 1, D_FF), F32),
                   jax.ShapeDtypeStruct((m // tm, CONV_W - 1, D_FF), F32)),
        grid=(m // tm, nt),
        in_specs=[pl.BlockSpec((tm, k), lambda i, j: (i, 0)),
                  pl.BlockSpec((k, 2 * tn), lambda i, j: (0, j)),
                  pl.BlockSpec((CONV_W, tn), lambda i, j: (0, j)),
                  pl.BlockSpec((CONV_W, tn), lambda i, j: (0, j + nt)),
                  pl.BlockSpec((1, tn), lambda i, j: (0, j)),
                  pl.BlockSpec((1, tn), lambda i, j: (0, j + nt))],
        out_specs=(pl.BlockSpec((tm, tn), lambda i, j: (i, j)),
                   pl.BlockSpec((1, CONV_W - 1, tn), lambda i, j: (i, 0, j)),
                   pl.BlockSpec((1, CONV_W - 1, tn), lambda i, j: (i, 0, j))),
        scratch_shapes=[pltpu.VMEM((nt, CONV_W - 1, tn), F32),
                        pltpu.VMEM((nt, CONV_W - 1, tn), F32)],
        compiler_params=_params(2),
        name="convglu_prompt",
    )(h, w_pairs, conv_w, conv_w, conv_b, conv_b)
    tails = jnp.concatenate([sg, sv], axis=-1)
    return a, tails[tiles_per_seq - 1::tiles_per_seq]


def _convglu_sample_kernel(h_ref, wg_ref, wv_ref, cwg_ref, cwv_ref, cbg_ref, cbv_ref,
                           p0g_ref, p0v_ref, p1g_ref, p1v_ref, a_ref, ug_ref, uv_ref, wb_ref):
    def half(w_ref, wb_cols, cw_ref, cb_ref, p0_ref, p1_ref, u_ref):
        u = jnp.dot(h_ref[...], _mxu_weights(w_ref, (wb_ref.at[:, wb_cols],)), preferred_element_type=F32)
        u_ref[...] = u
        return _conv_taps(cb_ref, cw_ref, p0_ref[...], p1_ref[...], u)

    cg = half(wg_ref, slice(0, FF_TILE), cwg_ref, cbg_ref, p0g_ref, p1g_ref, ug_ref)
    cv = half(wv_ref, slice(FF_TILE, 2 * FF_TILE), cwv_ref, cbv_ref, p0v_ref, p1v_ref, uv_ref)
    a_ref[...] = (_gelu_erf(cg) * cv).astype(a_ref.dtype)


def _convglu_sample(h, w_up, conv_w, conv_b, state_conv):
    m, k = h.shape
    nt = N_FF_TILES
    tn = FF_TILE
    prev = state_conv.reshape(m, (CONV_W - 1) * 2 * D_FF)
    col = lambda off: (lambda j: (0, j + off))
    a, ug, uv, w_pairs = pl.pallas_call(
        _convglu_sample_kernel,
        out_shape=(jax.ShapeDtypeStruct((m, D_FF), BF16),
                   jax.ShapeDtypeStruct((m, D_FF), F32),
                   jax.ShapeDtypeStruct((m, D_FF), F32),
                   jax.ShapeDtypeStruct((k, 2 * D_FF), BF16)),
        grid=(nt,),
        in_specs=[pl.BlockSpec((m, k), lambda j: (0, 0)),
                  pl.BlockSpec((k, tn), col(0)),
                  pl.BlockSpec((k, tn), col(nt)),
                  pl.BlockSpec((CONV_W, tn), col(0)),
                  pl.BlockSpec((CONV_W, tn), col(nt)),
                  pl.BlockSpec((1, tn), col(0)),
                  pl.BlockSpec((1, tn), col(nt)),
                  pl.BlockSpec((m, tn), col(0)),
                  pl.BlockSpec((m, tn), col(nt)),
                  pl.BlockSpec((m, tn), col(2 * nt)),
                  pl.BlockSpec((m, tn), col(3 * nt))],
        out_specs=(pl.BlockSpec((m, tn), col(0)),
                   pl.BlockSpec((m, tn), col(0)),
                   pl.BlockSpec((m, tn), col(0)),
                   pl.BlockSpec((k, 2 * tn), col(0))),
        compiler_params=_params(1),
        name="convglu_sample",
    )(h, w_up, w_up, conv_w, conv_w, conv_b, conv_b, prev, prev, prev, prev)
    u = jnp.concatenate([ug, uv], axis=-1)
    conv_new = jnp.stack([state_conv[:, 1, :], u], axis=1)
    return a, conv_new, w_pairs


def _segment_mean_square(x, seg_ones):
    sq = x * x
    hi = sq.astype(BF16)
    lo = (sq - hi.astype(F32)).astype(BF16)
    total = (jnp.dot(hi, seg_ones, preferred_element_type=F32)
             + jnp.dot(lo, seg_ones, preferred_element_type=F32))
    return total / HEAD_DIM


def _attn_prompt_kernel(q_ref, kc_ref, vc_ref, bucket_ref, seg_ones_ref, rb_ref, sinks_ref, gq_ref, gk_ref,
                        o_ref, kn_ref, bias_ref, kdup_ref, vdup_ref, *, blocks_per_seq):
    r = pl.program_id(0)
    blk = ATTN_BLOCK
    pair = 2 * HEAD_DIM
    n_bias_rows = N_ATTN_HEADS * blk
    seq_start = (r % blocks_per_seq) == 0

    @pl.when(r == 0)
    def _():
        bucket = bucket_ref[...]
        prev_cols = lax.broadcasted_iota(jnp.int32, bucket.shape, 1) < blk

        def per_head(h, carry):
            acc = jnp.full(bucket.shape, -jnp.inf, F32)
            for b in range(N_BUCKETS):
                acc = jnp.where(bucket == b, rb_ref[b, h], acc)
            row0 = pl.multiple_of(h * blk, blk)
            bias_ref[pl.ds(row0, blk), :] = acc
            bias_ref[pl.ds(n_bias_rows + row0, blk), :] = jnp.where(prev_cols, -jnp.inf, acc)
            return carry
        lax.fori_loop(0, N_ATTN_HEADS, per_head, 0)
        vdup_ref[:, :, pair:] = jnp.ones((N_KV_HEADS, 2 * blk, pair), BF16)

    @pl.when(seq_start)
    def _():
        kdup_ref[:, :blk, :] = jnp.zeros((N_KV_HEADS, blk, pair), BF16)
        vdup_ref[:, :blk, :pair] = jnp.zeros((N_KV_HEADS, blk, pair), BF16)

    @pl.when(jnp.logical_not(seq_start))
    def _():
        kdup_ref[:, :blk, :] = kdup_ref[:, blk:, :]
        vdup_ref[:, :blk, :pair] = vdup_ref[:, blk:, :pair]

    seg_ones = seg_ones_ref[...]
    low_half = lax.broadcasted_iota(jnp.int32, (blk, pair), 1) < HEAD_DIM

    def duplicate_half(x, odd):
        swapped = pltpu.roll(x, HEAD_DIM, axis=1)
        return jnp.where(low_half, swapped, x) if odd else jnp.where(low_half, x, swapped)

    kc = kc_ref[...]
    kn = kc * lax.rsqrt(_segment_mean_square(kc, seg_ones) + EPS) * gk_ref[...]
    kn_ref[...] = kn
    for j in range(N_KV_HEADS):
        col = slice((j // 2) * pair, (j // 2 + 1) * pair)
        kdup_ref[j, blk:, :] = duplicate_half(kn[:, col], j % 2).astype(BF16)
        vdup_ref[j, blk:, :pair] = duplicate_half(vc_ref[:, col], j % 2).astype(BF16)

    bias_base = jnp.where(seq_start, n_bias_rows, 0)
    chunk = 2 * pair
    n_heads = N_ATTN_HEADS
    q_gain = gq_ref[...] * HEAD_DIM ** -0.5
    q_chunks = [q_ref[:, c * chunk:(c + 1) * chunk] for c in range(ATTN_WIDTH // chunk)]
    q_ms = [_segment_mean_square(qc, seg_ones) for qc in q_chunks]
    q_norm = [qc * lax.rsqrt(ms + EPS) * q_gain for qc, ms in zip(q_chunks, q_ms)]
    lhs = []
    for qn in q_norm:
        for p in range(chunk // pair):
            qp = qn[:, p * pair:(p + 1) * pair]
            lhs.append(jnp.where(low_half, qp, 0.0).astype(BF16))
            lhs.append(jnp.where(low_half, 0.0, qp).astype(BF16))
    scores = [lax.dot_general(jnp.concatenate(lhs[j * GQA_GROUP:(j + 1) * GQA_GROUP], axis=0), kdup_ref[j],
                              (((1,), (1,)), ((), ())), preferred_element_type=F32)
              for j in range(N_KV_HEADS)]
    sg = [scores[h // GQA_GROUP][(h % GQA_GROUP) * blk:(h % GQA_GROUP + 1) * blk, :]
          + bias_ref[pl.ds(pl.multiple_of(bias_base + h * blk, blk), blk), :] for h in range(n_heads)]
    m = [jnp.maximum(jnp.max(sg[h], axis=-1, keepdims=True), sinks_ref[h]) for h in range(n_heads)]
    exps = [jnp.exp(sg[h] - m[h]).astype(BF16) for h in range(n_heads)]
    sink_terms = [jnp.exp(sinks_ref[h] - m[h]) for h in range(n_heads)]
    outs = [jnp.dot(jnp.concatenate(exps[j * GQA_GROUP:(j + 1) * GQA_GROUP], axis=0), vdup_ref[j],
                    preferred_element_type=F32) for j in range(N_KV_HEADS)]
    for h in range(0, n_heads, 2):
        o = outs[h // GQA_GROUP]
        g = h % GQA_GROUP
        even = o[g * blk:(g + 1) * blk, :]
        odd = o[(g + 1) * blk:(g + 2) * blk, :]
        num = jnp.where(low_half, even[:, :pair], odd[:, :pair])
        den = (jnp.where(low_half, even[:, pair:], odd[:, pair:])
               + jnp.where(low_half, sink_terms[h], sink_terms[h + 1]))
        o_ref[:, h * HEAD_DIM:h * HEAD_DIM + pair] = (num / den).astype(o_ref.dtype)


def _attn_prompt(proj, bucket_tile, rel_bias, sinks, g_q, g_k, n_seq):
    m = proj.shape[0]
    blk = ATTN_BLOCK
    nb = m // blk
    blocks_per_seq = nb // n_seq
    lanes = 4 * HEAD_DIM
    seg = np.arange(lanes) // HEAD_DIM
    seg_ones = (seg[:, None] == seg[None, :]).astype(BF16)
    tile4 = lambda g: jnp.tile(g.reshape(-1), lanes // HEAD_DIM).reshape(1, lanes)
    smem = pl.BlockSpec(memory_space=pltpu.SMEM)
    return pl.pallas_call(
        functools.partial(_attn_prompt_kernel, blocks_per_seq=blocks_per_seq),
        out_shape=(jax.ShapeDtypeStruct((m, ATTN_WIDTH), BF16),
                   jax.ShapeDtypeStruct((m, KV_WIDTH), F32)),
        grid=(nb,),
        in_specs=[pl.BlockSpec((blk, ATTN_WIDTH), lambda r: (r, COL_AQ // ATTN_WIDTH)),
                  pl.BlockSpec((blk, KV_WIDTH), lambda r: (r, COL_AK // KV_WIDTH)),
                  pl.BlockSpec((blk, KV_WIDTH), lambda r: (r, COL_AV // KV_WIDTH)),
                  pl.BlockSpec((blk, 2 * blk), lambda r: (0, 0)),
                  pl.BlockSpec((lanes, lanes), lambda r: (0, 0)),
                  smem, smem,
                  pl.BlockSpec((1, lanes), lambda r: (0, 0)),
                  pl.BlockSpec((1, lanes), lambda r: (0, 0))],
        out_specs=(pl.BlockSpec((blk, ATTN_WIDTH), lambda r: (r, 0)),
                   pl.BlockSpec((blk, KV_WIDTH), lambda r: (r, 0))),
        scratch_shapes=[pltpu.VMEM((2 * N_ATTN_HEADS * blk, 2 * blk), F32),
                        pltpu.VMEM((N_KV_HEADS, 2 * blk, 2 * HEAD_DIM), BF16),
                        pltpu.VMEM((N_KV_HEADS, 2 * blk, 4 * HEAD_DIM), BF16)],
        compiler_params=_params(1),
        name="attn_prompt",
    )(proj, proj, proj, bucket_tile, seg_ones, rel_bias, sinks, tile4(g_q), tile4(g_k))


def _attn_sample_kernel(q_ref, knew_ref, vnew_ref, ck_ref, cv_ref, bucket_ref, rbt_ref, sinks_ref,
                        gq_ref, gk_ref, o_ref, cko_ref, cvo_ref):
    win = ck_ref.shape[1]
    bucket = bucket_ref[...]
    bias = jnp.full((N_ATTN_HEADS, bucket.shape[1]), -jnp.inf, F32)
    for b in range(N_BUCKETS):
        bias = jnp.where(bucket == b, rbt_ref[:, b:b + 1], bias)
    bias_c = bias[:, :win]
    bias_n = bias[:, win:win + 1]
    sink = sinks_ref[...]
    head_group = jnp.right_shift(lax.broadcasted_iota(jnp.int32, (1, N_ATTN_HEADS, HEAD_DIM), 1),
                                 int(math.log2(GQA_GROUP)))
    lane_group = jnp.right_shift(lax.broadcasted_iota(jnp.int32, (1, 1, KV_WIDTH), 2),
                                 int(math.log2(HEAD_DIM)))
    scale = HEAD_DIM ** -0.5

    k_rows = knew_ref[...]
    k_sq = k_rows * k_rows
    inv = jnp.zeros_like(k_rows)
    for j in range(N_KV_HEADS):
        ms = jnp.sum(jnp.where(lane_group == j, k_sq, 0.0), axis=-1, keepdims=True) / HEAD_DIM
        inv = jnp.where(lane_group == j, lax.rsqrt(ms + EPS), inv)
    kn_rows = k_rows * inv * gk_ref[...]
    v_rows = vnew_ref[...]

    qn = _rms(q_ref[...], gq_ref[...])
    q_bd = jnp.concatenate([jnp.where(head_group == j, qn, 0.0) for j in range(N_KV_HEADS)], axis=2)
    s_c = jnp.einsum('bhd,bkd->bhk', q_bd.astype(BF16), ck_ref[...].astype(BF16),
                     preferred_element_type=F32) * scale + bias_c
    s_n = jnp.sum(q_bd * kn_rows, axis=-1, keepdims=True) * scale + bias_n
    m = jnp.maximum(jnp.maximum(jnp.max(s_c, axis=-1, keepdims=True), s_n), sink)
    e_c = jnp.exp(s_c - m)
    e_n = jnp.exp(s_n - m)
    denom = jnp.sum(e_c, axis=-1, keepdims=True) + e_n + jnp.exp(sink - m)
    o_full = jnp.einsum('bhk,bkd->bhd', (e_c / denom).astype(BF16), cv_ref[...].astype(BF16),
                        preferred_element_type=F32)
    o_full = o_full + (e_n / denom) * v_rows
    o = jnp.zeros(q_ref.shape, F32)
    for j in range(N_KV_HEADS):
        o = jnp.where(head_group == j, o_full[:, :, j * HEAD_DIM:(j + 1) * HEAD_DIM], o)
    o_ref[...] = o.astype(o_ref.dtype)

    cko_ref[:, pl.ds(0, win - 1), :] = ck_ref[:, pl.ds(1, win - 1), :]
    cko_ref[:, pl.ds(win - 1, 1), :] = kn_rows
    cvo_ref[:, pl.ds(0, win - 1), :] = cv_ref[:, pl.ds(1, win - 1), :]
    cvo_ref[:, pl.ds(win - 1, 1), :] = v_rows


def _attn_sample(q, k_new, v_new, cache_k, cache_v, bucket_row, rel_bias_t, sinks_col, g_q, g_k_row, bt):
    nb = q.shape[0]
    win = cache_k.shape[1]
    full = lambda shape: pl.BlockSpec(shape, lambda i: (0,) * len(shape))
    return pl.pallas_call(
        _attn_sample_kernel,
        out_shape=(jax.ShapeDtypeStruct((nb, N_ATTN_HEADS, HEAD_DIM), BF16),
                   jax.ShapeDtypeStruct(cache_k.shape, F32),
                   jax.ShapeDtypeStruct(cache_v.shape, F32)),
        grid=(nb // bt,),
        in_specs=[pl.BlockSpec((bt, N_ATTN_HEADS, HEAD_DIM), lambda i: (i, 0, 0)),
                  pl.BlockSpec((bt, 1, KV_WIDTH), lambda i: (i, 0, 0)),
                  pl.BlockSpec((bt, 1, KV_WIDTH), lambda i: (i, 0, 0)),
                  pl.BlockSpec((bt, win, KV_WIDTH), lambda i: (i, 0, 0)),
                  pl.BlockSpec((bt, win, KV_WIDTH), lambda i: (i, 0, 0)),
                  full(bucket_row.shape), full(rel_bias_t.shape), full(sinks_col.shape),
                  full(g_q.shape), full(g_k_row.shape)],
        out_specs=(pl.BlockSpec((bt, N_ATTN_HEADS, HEAD_DIM), lambda i: (i, 0, 0)),
                   pl.BlockSpec((bt, win, KV_WIDTH), lambda i: (i, 0, 0)),
                   pl.BlockSpec((bt, win, KV_WIDTH), lambda i: (i, 0, 0))),
        compiler_params=_params(1),
        name="attn_sample",
    )(q, k_new, v_new, cache_k, cache_v, bucket_row, rel_bias_t, sinks_col, g_q, g_k_row)


def _ret_prompt_kernel(q_ref, k_ref, v_ref, gate_ref, cos_ref, sin_ref, dmask_ref, qdec_ref, kdec_ref,
                       cdec_ref, o_ref, state_ref, s_ref):
    c = pl.program_id(0)

    @pl.when(c == 0)
    def _():
        s_ref[...] = jnp.zeros_like(s_ref)

    cosf = cos_ref[...]
    sinf = sin_ref[...]
    half = RET_KDIM // 2
    nt = (((1,), (1,)), ((), ()))
    tn = (((0,), (0,)), ((), ()))

    def rotary(x):
        return x * cosf + pltpu.roll(x, half, axis=1) * sinf

    for h in range(N_RET_HEADS):
        kd = slice(h * RET_KDIM, (h + 1) * RET_KDIM)
        vd = slice(h * RET_VDIM, (h + 1) * RET_VDIM)
        for b in range(q_ref.shape[0]):
            q = rotary(q_ref[b, :, kd])
            k = rotary(k_ref[b, :, kd]) * RET_KDIM ** -0.5
            v = v_ref[b, :, vd].astype(BF16)
            s0 = s_ref[b, h]
            scores = lax.dot_general(q.astype(BF16), k.astype(BF16), nt,
                                     preferred_element_type=F32) * dmask_ref[h]
            o_intra = jnp.dot(scores.astype(BF16), v, preferred_element_type=F32)
            o_inter = jnp.dot((q * qdec_ref[h]).astype(BF16), s0.astype(BF16), preferred_element_type=F32)
            s_ref[b, h] = cdec_ref[h] * s0 + lax.dot_general((k * kdec_ref[h]).astype(BF16), v, tn,
                                                              preferred_element_type=F32)
            ro = o_intra + o_inter
            ro = ro * lax.rsqrt(jnp.mean(ro * ro, axis=-1, keepdims=True) + EPS)
            o_ref[b, :, vd] = (jax.nn.silu(gate_ref[b, :, vd]) * ro).astype(o_ref.dtype)

    @pl.when(c == pl.num_programs(0) - 1)
    def _():
        state_ref[...] = s_ref[...]


def _ret_prompt(proj, cosf, sinf, dmask, qdec, kdec, cdec, n_seq):
    m, width = proj.shape
    seq = m // n_seq
    ch = RET_CHUNK
    proj3 = proj.reshape(n_seq, seq, width)
    const3 = pl.BlockSpec((N_RET_HEADS, ch, ch), lambda c: (0, 0, 0))
    state_shape = (n_seq, N_RET_HEADS, RET_KDIM, RET_VDIM)
    o, state = pl.pallas_call(
        _ret_prompt_kernel,
        out_shape=(jax.ShapeDtypeStruct((n_seq, seq, RET_V_WIDTH), BF16),
                   jax.ShapeDtypeStruct(state_shape, F32)),
        grid=(seq // ch,),
        in_specs=[pl.BlockSpec((n_seq, ch, RET_QK_WIDTH), lambda c: (0, c, COL_RQ // RET_QK_WIDTH)),
                  pl.BlockSpec((n_seq, ch, RET_QK_WIDTH), lambda c: (0, c, COL_RK // RET_QK_WIDTH)),
                  pl.BlockSpec((n_seq, ch, RET_V_WIDTH), lambda c: (0, c, COL_RV // RET_V_WIDTH)),
                  pl.BlockSpec((n_seq, ch, RET_V_WIDTH), lambda c: (0, c, COL_RG // RET_V_WIDTH)),
                  pl.BlockSpec((ch, RET_KDIM), lambda c: (c, 0)),
                  pl.BlockSpec((ch, RET_KDIM), lambda c: (c, 0)),
                  const3, const3, const3,
                  pl.BlockSpec(memory_space=pltpu.SMEM)],
        out_specs=(pl.BlockSpec((n_seq, ch, RET_V_WIDTH), lambda c: (0, c, 0)),
                   pl.BlockSpec(state_shape, lambda c: (0, 0, 0, 0))),
        scratch_shapes=[pltpu.VMEM(state_shape, F32)],
        compiler_params=_params(1),
        name="ret_prompt",
    )(proj3, proj3, proj3, proj3, cosf, sinf, dmask, qdec, kdec, cdec)
    return o.reshape(m, RET_V_WIDTH), state


def _ret_sample_kernel(qr_ref, qt_ref, kt_ref, cosr_ref, sinr_ref, cos_ref, sin_ref, v_ref, gate_ref, s_ref,
                       qdec_ref, kdec_ref, cdec_ref, o_ref, so_ref):
    bt = s_ref.shape[0]
    half = RET_KDIM // 2
    cos = cos_ref[...]
    sin = sin_ref[...]
    cosr = cosr_ref[...]
    sinr = sinr_ref[...]

    def rotary(x):
        x1, x2 = x[:half, :], x[half:, :]
        return jnp.concatenate([x1 * cos - x2 * sin, x2 * cos + x1 * sin], axis=0)

    for h in range(N_RET_HEADS):
        vd = slice(h * RET_VDIM, (h + 1) * RET_VDIM)
        q_all = rotary(qt_ref[0, h])
        k_all = rotary(kt_ref[0, h]) * RET_KDIM ** -0.5
        qk_all = jnp.sum(q_all * k_all, axis=0, keepdims=True)
        kd_all = k_all * kdec_ref[h]
        q_rows = qr_ref[0, :, h * RET_KDIM:(h + 1) * RET_KDIM]
        q_rows = q_rows * cosr + pltpu.roll(q_rows, half, axis=1) * sinr
        qd_rows = (q_rows * qdec_ref[h]).astype(BF16)
        for b in range(bt):
            v = v_ref[0, b:b + 1, vd]
            s0 = s_ref[b, h]
            o_intra = qk_all[:, b:b + 1] * v
            o_inter = jnp.dot(qd_rows, s0.astype(BF16), preferred_element_type=F32)[b:b + 1, :]
            so_ref[b, h] = cdec_ref[h] * s0 + kd_all[:, b:b + 1] * v
            ro = o_intra + o_inter
            ro = ro * lax.rsqrt(jnp.mean(ro * ro, axis=-1, keepdims=True) + EPS)
            o_ref[0, b:b + 1, vd] = (jax.nn.silu(gate_ref[0, b:b + 1, vd]) * ro).astype(o_ref.dtype)


def _ret_sample(qr, qt, kt, cos_row, sin_row, cos_col, sin_col, v, gate, state, qdec, kdec, cdec):
    nbt, _, _, bt = qt.shape
    smem = pl.BlockSpec(memory_space=pltpu.SMEM)
    st_spec = pl.BlockSpec((bt, N_RET_HEADS, RET_KDIM, RET_VDIM), lambda i: (i, 0, 0, 0))
    qk_spec = pl.BlockSpec((1, N_RET_HEADS, RET_KDIM, bt), lambda i: (i, 0, 0, 0))
    row_spec = pl.BlockSpec((1, bt, RET_V_WIDTH), lambda i: (i, 0, 0))
    col_spec = pl.BlockSpec(cos_col.shape, lambda i: (0, 0))
    trig_row_spec = pl.BlockSpec(cos_row.shape, lambda i: (0, 0))
    q_row_spec = pl.BlockSpec((1, bt, RET_QK_WIDTH), lambda i: (i, 0, 0))
    return pl.pallas_call(
        _ret_sample_kernel,
        out_shape=(jax.ShapeDtypeStruct((nbt, bt, RET_V_WIDTH), BF16),
                   jax.ShapeDtypeStruct(state.shape, F32)),
        grid=(nbt,),
        in_specs=[q_row_spec, qk_spec, qk_spec, trig_row_spec, trig_row_spec, col_spec, col_spec,
                  row_spec, row_spec, st_spec, smem, smem, smem],
        out_specs=(row_spec, st_spec),
        compiler_params=_params(1),
        name="ret_sample",
    )(qr, qt, kt, cos_row, sin_row, cos_col, sin_col, v, gate, state, qdec, kdec, cdec)


NPF = np.float32


def _t5_bucket(dist):
    n = np.maximum(dist, 0)
    max_exact = N_BUCKETS // 2
    nf = np.maximum(n, 1).astype(NPF)
    large = max_exact + (np.log(nf / NPF(max_exact)) / NPF(math.log(MAX_DISTANCE / max_exact))
                         * NPF(N_BUCKETS - max_exact)).astype(np.int32)
    return np.where(n < max_exact, n, np.minimum(large, N_BUCKETS - 1)).astype(np.int32)


def _rope_tables(pos):
    half = RET_KDIM // 2
    inv = NPF(ROPE_BASE) ** (-np.arange(half, dtype=NPF) / NPF(half))
    ang = pos.astype(NPF)[:, None] * inv[None]
    return np.cos(ang), np.sin(ang)


def _layer(xp, xs, pp, ps, cache_k, cache_v, state_ret, state_conv, rel_bias, lp, n_seq):
    (g_mix, w_in, g_q, g_k, sinks, w_out, g_ffn, w_up, conv_w, conv_b, w_down,
     g_ple, w_ple_gate, w_ple_proj) = lp
    mp = xp.shape[0]
    ms = xs.shape[0]
    seq = mp // n_seq
    win = cache_k.shape[1]

    row = lambda a: a.reshape(1, -1)

    log_decay = np.log(NPF(1.0) - NPF(2.0) ** (NPF(-5.0) - np.arange(N_RET_HEADS, dtype=NPF)))
    idx = np.arange(RET_CHUNK, dtype=NPF)
    diff = idx[:, None] - idx[None, :]
    dmask = np.where(diff[None] >= 0, np.exp(diff[None] * log_decay[:, None, None]), NPF(0.0))
    q_dec = np.exp((idx + NPF(1.0))[:, None] * log_decay[None])
    k_dec = np.exp((NPF(RET_CHUNK - 1.0) - idx)[:, None] * log_decay[None])
    qdec_t = np.broadcast_to(q_dec.T[:, :, None], (N_RET_HEADS, RET_CHUNK, RET_KDIM))
    kdec_t = np.broadcast_to(k_dec.T[:, :, None], (N_RET_HEADS, RET_CHUNK, RET_KDIM))
    cdec = np.exp(NPF(RET_CHUNK) * log_decay)
    cos_p, sin_p = _rope_tables(np.arange(seq, dtype=np.int32))
    cosf = np.concatenate([cos_p, cos_p], axis=1)
    sinf = np.concatenate([-sin_p, sin_p], axis=1)
    one = np.arange(1, dtype=NPF)
    qdec_s = np.exp((one + NPF(1.0))[:, None] * log_decay[None])[0]
    kdec_s = np.exp((NPF(1.0 - 1.0) - one)[:, None] * log_decay[None])[0]
    cdec_s = np.exp(NPF(1.0) * log_decay)
    cos_s, sin_s = _rope_tables(PAST_LEN + np.arange(1, dtype=np.int32))
    cos_col, sin_col = cos_s.reshape(-1, 1), sin_s.reshape(-1, 1)

    qi = np.arange(ATTN_BLOCK, dtype=np.int32)
    ki = np.arange(2 * ATTN_BLOCK, dtype=np.int32) - ATTN_BLOCK
    dist = qi[:, None] - ki[None, :]
    bucket_tile = np.where((dist >= 0) & (dist <= WINDOW), _t5_bucket(dist), -1).astype(np.int32)
    dist_s = win - np.arange(2 * ATTN_BLOCK, dtype=np.int32)
    bucket_row = np.where((dist_s >= 0) & (dist_s <= WINDOW), _t5_bucket(dist_s), -1).astype(np.int32).reshape(1, -1)

    tmp, tms = _row_tile(mp), _row_tile(ms)
    proj_s, w_in_b = _matmul(_norm(xs, row(g_mix), tms), w_in, tms, COL_TILE, IN_SRC_BLOCKS)
    proj_p, = _matmul(_norm(xp, row(g_mix), NORM_TILE), w_in_b, min(mp, IN_ROW_TILE), COL_TILE)

    attn_p, kn_p = _attn_prompt(proj_p, bucket_tile, rel_bias, sinks, row(g_q), row(g_k), n_seq)
    ret_p, ret_state_p = _ret_prompt(proj_p, cosf, sinf, dmask, qdec_t, kdec_t, cdec, n_seq)

    bt_a = 8
    attn_s, ck_new, cv_new = _attn_sample(
        proj_s[:, COL_AQ:COL_AQ + ATTN_WIDTH].reshape(ms, N_ATTN_HEADS, HEAD_DIM),
        proj_s[:, COL_AK:COL_AK + KV_WIDTH].reshape(ms, 1, KV_WIDTH),
        proj_s[:, COL_AV:COL_AV + KV_WIDTH].reshape(ms, 1, KV_WIDTH),
        cache_k.reshape(ms, win, KV_WIDTH), cache_v.reshape(ms, win, KV_WIDTH),
        bucket_row, rel_bias.T, sinks.reshape(-1, 1), row(g_q), jnp.tile(g_k, N_KV_HEADS).reshape(1, -1), bt_a)
    attn_s = attn_s.reshape(ms, ATTN_WIDTH)

    bt_r = 8
    to_cols = lambda a: a.reshape(ms // bt_r, bt_r, N_RET_HEADS, RET_KDIM).transpose(0, 2, 3, 1)
    rq_s = proj_s[:, COL_RQ:COL_RQ + RET_QK_WIDTH]
    ret_s, ret_state_s = _ret_sample(
        rq_s.reshape(ms // bt_r, bt_r, RET_QK_WIDTH), to_cols(rq_s), to_cols(proj_s[:, COL_RK:COL_RK + RET_QK_WIDTH]),
        np.concatenate([cos_s, cos_s], axis=1), np.concatenate([-sin_s, sin_s], axis=1), cos_col, sin_col,
        proj_s[:, COL_RV:COL_RV + RET_V_WIDTH].reshape(ms // bt_r, bt_r, RET_V_WIDTH),
        proj_s[:, COL_RG:COL_RG + RET_V_WIDTH].reshape(ms // bt_r, bt_r, RET_V_WIDTH),
        state_ret, qdec_s, kdec_s, cdec_s)
    ret_s = ret_s.reshape(ms, RET_V_WIDTH)

    xs, w_out_b = _out_proj(attn_s, ret_s, w_out, xs, tms, COL_TILE)
    xp, = _out_proj(attn_p, ret_p, w_out_b, xp, tmp, OUT_COL_TILE)

    act_s, conv_s, w_up_b = _convglu_sample(_norm(xs, row(g_ffn), tms), w_up, conv_w, row(conv_b), state_conv)
    act_p, conv_p = _convglu_prompt(_norm(xp, row(g_ffn), NORM_TILE), w_up_b, conv_w, row(conv_b), n_seq, UP_ROW_TILE)
    for part in range(DOWN_K_PARTS):
        last = part == DOWN_K_PARTS - 1
        xs, w_down_b = _down_proj(act_s, w_down, xs, tms, COL_TILE, DOWN_K_PARTS, part)
        xp, *stats_p = _down_proj(act_p, w_down_b, xp, tmp, COL_TILE, DOWN_K_PARTS, part,
                                  row(g_ple) if last else None)
    hp, ssq_p = stats_p

    xs, w_pgate_b, w_pproj_b = _ple(xs, _norm(xs, row(g_ple), tms), None, w_ple_gate, ps, w_ple_proj, tms, COL_TILE)
    xp, = _ple(xp, hp, ssq_p, w_pgate_b, pp, w_pproj_b, tmp, COL_TILE)

    last = lambda a: a.reshape(n_seq, seq, a.shape[1])[:, seq - WINDOW:, :]
    kp_new = last(kn_p).reshape(n_seq, WINDOW, N_KV_HEADS, HEAD_DIM)
    vp_new = last(proj_p)[:, :, COL_AV:COL_AV + KV_WIDTH].reshape(n_seq, WINDOW, N_KV_HEADS, HEAD_DIM)
    ks_new = ck_new.reshape(ms, win, N_KV_HEADS, HEAD_DIM)
    vs_new = cv_new.reshape(ms, win, N_KV_HEADS, HEAD_DIM)
    return xp, xs, kp_new, vp_new, ret_state_p, conv_p, ks_new, vs_new, ret_state_s, conv_s


def kernel(x_prompt, x_sample, p_prompt, p_sample, cache_win_k, cache_win_v, state_ret, state_conv, rel_bias, g_mix, w_in, g_q, g_k, sinks, w_out, g_ffn, w_up, conv_w, conv_b, w_down, g_ple, w_ple_gate, w_ple_proj):
    depth = g_mix.shape[0]
    n_seq, seq, d = x_prompt.shape
    nb, dec_seq, _ = x_sample.shape
    assert dec_seq == 1 and seq % 512 == 0 and d == D_MODEL
    xp = x_prompt.reshape(n_seq * seq, d)
    xs = x_sample.reshape(nb, d)
    outs = [[] for _ in range(8)]
    for l in range(depth):
        lp = (g_mix[l], w_in[l], g_q[l], g_k[l], sinks[l], w_out[l], g_ffn[l], w_up[l], conv_w[l],
              conv_b[l], w_down[l], g_ple[l], w_ple_gate[l], w_ple_proj[l])
        res = _layer(xp, xs, p_prompt[l].reshape(n_seq * seq, -1), p_sample[l].reshape(nb, -1),
                     cache_win_k[l], cache_win_v[l], state_ret[l], state_conv[l], rel_bias, lp, n_seq)
        xp, xs = res[0], res[1]
        for o, r in zip(outs, res[2:]):
            o.append(r)
    stacked = [jnp.stack(o) for o in outs]
    return (xp.reshape(n_seq, seq, d), xs.reshape(nb, 1, d), *stacked)
```

```python
import functools
import math

import jax
import jax.numpy as jnp
import numpy as np
from jax import lax
from jax.experimental import pallas as pl
from jax.experimental.pallas import tpu as pltpu

F32 = jnp.float32
BF16 = jnp.bfloat16

D_MODEL = 4096
HEAD_DIM = 64
N_ATTN_HEADS = 32
N_KV_HEADS = 4
GQA_GROUP = N_ATTN_HEADS // N_KV_HEADS
WINDOW = 128
ATTN_BLOCK = 128
N_BUCKETS = 32
MAX_DISTANCE = 128
N_RET_HEADS = 8
RET_KDIM = 128
RET_VDIM = 256
RET_CHUNK = 128
ROPE_BASE = 10000.0
D_FF = 11008
CONV_W = 3
EPS = 1e-6

ATTN_WIDTH = N_ATTN_HEADS * HEAD_DIM
KV_WIDTH = N_KV_HEADS * HEAD_DIM
RET_QK_WIDTH = N_RET_HEADS * RET_KDIM
RET_V_WIDTH = N_RET_HEADS * RET_VDIM
IN_SIZES = (ATTN_WIDTH, KV_WIDTH, KV_WIDTH, RET_QK_WIDTH, RET_QK_WIDTH, RET_V_WIDTH, RET_V_WIDTH)
IN_WIDTH = sum(IN_SIZES)
IN_SPLITS = tuple(sum(IN_SIZES[:n + 1]) for n in range(len(IN_SIZES) - 1))
PAST_LEN = 8192

COL_AQ = 0
COL_RV = ATTN_WIDTH
COL_RG = COL_RV + RET_V_WIDTH
COL_RQ = COL_RG + RET_V_WIDTH
COL_RK = COL_RQ + RET_QK_WIDTH
COL_AK = COL_RK + RET_QK_WIDTH
COL_AV = COL_AK + KV_WIDTH

VMEM_LIMIT_BYTES = 56 * 1024 * 1024
FF_TILE = 256
N_FF_TILES = D_FF // FF_TILE
NORM_ROWS = 128
NORM_TILE = 512
OUT_COL_TILE = 1024
F32_SUBLANES = 8
STAT_LANES = 128
IN_ROW_TILE = 2048
ROW_TILE = 1024
COL_TILE = 512
UP_ROW_TILE = 2048
CONV_CHUNK = 1024
DOWN_K_PARTS = 2


def _row_tile(m):
    return min(m, ROW_TILE)


def _in_src_blocks():
    start = dict(zip(("aq", "ak", "av", "rq", "rk", "rv", "rg"), (0,) + IN_SPLITS))
    order = (("aq", ATTN_WIDTH), ("rv", RET_V_WIDTH), ("rg", RET_V_WIDTH), ("rq", RET_QK_WIDTH),
             ("rk", RET_QK_WIDTH), ("ak", KV_WIDTH), ("av", KV_WIDTH))
    cols = [c for name, width in order for c in range(start[name], start[name] + width, KV_WIDTH)]
    per_tile = COL_TILE // KV_WIDTH
    blocks = []
    for b in range(0, len(cols), per_tile):
        group = cols[b:b + per_tile]
        assert group[0] % COL_TILE == 0 and all(c == group[0] + u * KV_WIDTH for u, c in enumerate(group))
        blocks.append(group[0] // COL_TILE)
    return tuple(blocks)


IN_SRC_BLOCKS = _in_src_blocks()


def _params(n_axes):
    return pltpu.CompilerParams(dimension_semantics=("arbitrary",) * n_axes,
                                vmem_limit_bytes=VMEM_LIMIT_BYTES)


def _rms(x, g):
    y = x * lax.rsqrt(jnp.mean(x * x, axis=-1, keepdims=True) + EPS)
    return y * g


def _norm_rows_to_bf16(x_ref, g_ref, h_ref):
    def body(c, carry):
        r0 = pl.multiple_of(c * NORM_ROWS, NORM_ROWS)
        h_ref[pl.ds(r0, NORM_ROWS), :] = _rms(x_ref[pl.ds(r0, NORM_ROWS), :], g_ref[...]).astype(h_ref.dtype)
        return carry
    lax.fori_loop(0, x_ref.shape[0] // NORM_ROWS, body, 0)


def _norm(x, g, tr):
    m, k = x.shape
    return pl.pallas_call(
        _norm_rows_to_bf16,
        out_shape=jax.ShapeDtypeStruct((m, k), BF16),
        grid=(m // tr,),
        in_specs=[pl.BlockSpec((tr, k), lambda i: (i, 0)),
                  pl.BlockSpec((1, k), lambda i: (0, 0))],
        out_specs=pl.BlockSpec((tr, k), lambda i: (i, 0)),
        compiler_params=_params(1),
        name="rmsnorm",
    )(x, g)


def _mxu_weights(w_ref, wb_ref):
    if not wb_ref:
        return w_ref[...]
    wb = w_ref[...].astype(BF16)
    wb_ref[0][...] = wb
    return wb


def _mm_kernel(h_ref, w_ref, o_ref, *wb_ref):
    o_ref[...] = jnp.dot(h_ref[...], _mxu_weights(w_ref, wb_ref), preferred_element_type=F32)


def _matmul(h, w, tm, tn, src_blocks=None):
    m, k = h.shape
    n = w.shape[1]
    emit = w.dtype != BF16
    if emit:
        w_map = lambda i, j: (0, sum(jnp.where(j == d, s, 0) for d, s in enumerate(src_blocks)))
    else:
        w_map = lambda i, j: (0, j)
    out_shape = [jax.ShapeDtypeStruct((m, n), F32)]
    out_specs = [pl.BlockSpec((tm, tn), lambda i, j: (i, j))]
    if emit:
        assert m == tm
        out_shape.append(jax.ShapeDtypeStruct((k, n), BF16))
        out_specs.append(pl.BlockSpec((k, tn), lambda i, j: (0, j)))
    return pl.pallas_call(
        _mm_kernel,
        out_shape=out_shape,
        grid=(m // tm, n // tn),
        in_specs=[pl.BlockSpec((tm, k), lambda i, j: (i, 0)),
                  pl.BlockSpec((k, tn), w_map)],
        out_specs=out_specs,
        compiler_params=_params(2),
        name="in_proj",
    )(h, w)


def _row_scale(ssq_ref, width, n_cols):
    r = lax.rsqrt(ssq_ref[...] / width + EPS)
    return jnp.concatenate([r] * (n_cols // STAT_LANES), axis=1)


def _residual_mm_kernel(*refs, n_lhs, emit_weights, norm_stats):
    lhs_refs = refs[:n_lhs]
    w_ref, x_ref = refs[n_lhs], refs[n_lhs + 1]
    rest = list(refs[n_lhs + 2:])
    g_ref = rest.pop(0) if norm_stats else None
    o_ref = rest.pop(0)
    wb_ref = (rest.pop(0),) if emit_weights else ()
    lhs = jnp.concatenate([r[...] for r in lhs_refs], axis=1) if n_lhs > 1 else lhs_refs[0][...]
    o = x_ref[...] + jnp.dot(lhs, _mxu_weights(w_ref, wb_ref), preferred_element_type=F32)
    o_ref[...] = o
    if norm_stats:
        hb_ref, ssq_ref = rest
        hb_ref[...] = (o * g_ref[...]).astype(hb_ref.dtype)
        part = jnp.broadcast_to(jnp.sum(o * o, axis=-1, keepdims=True), ssq_ref.shape)
        j = pl.program_id(1)

        @pl.when(j == 0)
        def _():
            ssq_ref[...] = part

        @pl.when(j > 0)
        def _():
            ssq_ref[...] += part


def _residual_matmul(lhs, lhs_specs, w, w_spec, wb_shape, x, g_next, tm, tn, name):
    m, n = x.shape
    emit = w.dtype != BF16
    stats = g_next is not None
    tile = pl.BlockSpec((tm, tn), lambda i, j: (i, j))
    operands = [*lhs, w, x]
    in_specs = [*lhs_specs, w_spec, tile]
    out_shape = [jax.ShapeDtypeStruct((m, n), F32)]
    out_specs = [tile]
    if stats:
        operands.append(g_next)
        in_specs.append(pl.BlockSpec((1, tn), lambda i, j: (0, j)))
    if emit:
        assert m == tm
        out_shape.append(jax.ShapeDtypeStruct(wb_shape, BF16))
        out_specs.append(pl.BlockSpec((wb_shape[0], tn), lambda i, j: (0, j)))
    if stats:
        out_shape += [jax.ShapeDtypeStruct((m, n), BF16), jax.ShapeDtypeStruct((m, STAT_LANES), F32)]
        out_specs += [tile, pl.BlockSpec((tm, STAT_LANES), lambda i, j: (i, 0))]
    return pl.pallas_call(
        functools.partial(_residual_mm_kernel, n_lhs=len(lhs), emit_weights=emit, norm_stats=stats),
        out_shape=out_shape,
        grid=(m // tm, n // tn),
        in_specs=in_specs,
        out_specs=out_specs,
        compiler_params=_params(2),
        name=name,
    )(*operands)


def _out_proj(a, r, w, x, tm, tn, g_next=None):
    ka, kr = a.shape[1], r.shape[1]
    lhs_specs = [pl.BlockSpec((tm, ka), lambda i, j: (i, 0)), pl.BlockSpec((tm, kr), lambda i, j: (i, 0))]
    w_spec = pl.BlockSpec((ka + kr, tn), lambda i, j: (0, j))
    return _residual_matmul([a, r], lhs_specs, w, w_spec, w.shape, x, g_next, tm, tn, "out_proj")


def _down_proj(a, w, x, tm, tn, k_parts, part, g_next=None):
    k = a.shape[1] // k_parts
    w_part = part if w.dtype != BF16 else 0
    lhs_specs = [pl.BlockSpec((tm, k), lambda i, j: (i, part))]
    w_spec = pl.BlockSpec((k, tn), lambda i, j: (w_part, j))
    return _residual_matmul([a], lhs_specs, w, w_spec, (k, w.shape[1]), x, g_next, tm, tn, "down_proj")


def _ple_kernel(*refs, emit_weights, row_scaled):
    h_ref, wg_ref, p_ref, wp_ref, x_ref = refs[:5]
    rest = list(refs[5:])
    ssq_ref = rest.pop(0) if row_scaled else None
    o_ref = rest.pop(0)
    wb_refs = tuple(rest) if emit_weights else ()
    z = jnp.dot(h_ref[...], _mxu_weights(wg_ref, wb_refs[:1]), preferred_element_type=F32)
    if row_scaled:
        z = z * _row_scale(ssq_ref, h_ref.shape[1], z.shape[1])
    pp = jnp.dot(p_ref[...].astype(BF16), _mxu_weights(wp_ref, wb_refs[1:]), preferred_element_type=F32)
    o_ref[...] = x_ref[...] + jax.nn.sigmoid(z) * pp


def _ple(x, h, ssq, wg, p, wp, tm, tn):
    m, k = h.shape
    n = wg.shape[1]
    kp = p.shape[1]
    emit = wg.dtype != BF16
    scaled = ssq is not None
    operands = [h, wg, p, wp, x]
    in_specs = [pl.BlockSpec((tm, k), lambda i, j: (i, 0)),
                pl.BlockSpec((k, tn), lambda i, j: (0, j)),
                pl.BlockSpec((tm, kp), lambda i, j: (i, 0)),
                pl.BlockSpec((kp, tn), lambda i, j: (0, j)),
                pl.BlockSpec((tm, tn), lambda i, j: (i, j))]
    if scaled:
        operands.append(ssq)
        in_specs.append(pl.BlockSpec((tm, STAT_LANES), lambda i, j: (i, 0)))
    out_shape = [jax.ShapeDtypeStruct((m, n), F32)]
    out_specs = [pl.BlockSpec((tm, tn), lambda i, j: (i, j))]
    if emit:
        assert m == tm
        out_shape += [jax.ShapeDtypeStruct(wg.shape, BF16), jax.ShapeDtypeStruct(wp.shape, BF16)]
        out_specs += [pl.BlockSpec((k, tn), lambda i, j: (0, j)), pl.BlockSpec((kp, tn), lambda i, j: (0, j))]
    return pl.pallas_call(
        functools.partial(_ple_kernel, emit_weights=emit, row_scaled=scaled),
        out_shape=out_shape,
        grid=(m // tm, n // tn),
        in_specs=in_specs,
        out_specs=out_specs,
        compiler_params=_params(2),
        name="ple",
    )(*operands)


def _gelu_erf(x):
    return 0.5 * x * (1.0 + lax.erf(x * math.sqrt(0.5)))


def _conv_taps(cb_ref, cw_ref, um2, um1, u):
    c = cb_ref[...] + cw_ref[0:1, :] * um2
    c = c + cw_ref[1:2, :] * um1
    return c + cw_ref[2:3, :] * u


def _convglu_prompt_kernel(h_ref, w_ref, cwg_ref, cwv_ref, cbg_ref, cbv_ref,
                           a_ref, sg_ref, sv_ref, carry_g_ref, carry_v_ref, *, tiles_per_seq):
    i = pl.program_id(0)
    j = pl.program_id(1)
    tm = h_ref.shape[0]
    rows = min(tm, CONV_CHUNK)
    seq_start = (i % tiles_per_seq) == 0
    sub = F32_SUBLANES
    row = lax.broadcasted_iota(jnp.int32, (sub, 1), 0)

    def half(u, cw_ref, cb_ref, prev):
        r1 = pltpu.roll(u, 1, axis=0)
        r2 = pltpu.roll(u, 2, axis=0)
        head1 = jnp.where(row == 0, prev[1:2, :], r1[:sub, :])
        head2 = jnp.where(row == 0, prev[0:1, :], jnp.where(row == 1, prev[1:2, :], r2[:sub, :]))
        um1 = jnp.concatenate([head1, r1[sub:, :]], axis=0)
        um2 = jnp.concatenate([head2, r2[sub:, :]], axis=0)
        return _conv_taps(cb_ref, cw_ref, um2, um1, u), u[rows - (CONV_W - 1):, :]

    prev_g = jnp.where(seq_start, 0.0, carry_g_ref[j])
    prev_v = jnp.where(seq_start, 0.0, carry_v_ref[j])
    for c in range(tm // rows):
        rs = slice(c * rows, (c + 1) * rows)
        u = jnp.dot(h_ref[rs, :], w_ref[...], preferred_element_type=F32)
        cg, prev_g = half(u[:, :FF_TILE], cwg_ref, cbg_ref, prev_g)
        cv, prev_v = half(u[:, FF_TILE:], cwv_ref, cbv_ref, prev_v)
        a_ref[rs, :] = (_gelu_erf(cg) * cv).astype(a_ref.dtype)
    carry_g_ref[j] = prev_g
    carry_v_ref[j] = prev_v
    sg_ref[0] = prev_g
    sv_ref[0] = prev_v


def _convglu_prompt(h, w_pairs, conv_w, conv_b, n_seq, tm):
    m, k = h.shape
    seq = m // n_seq
    tiles_per_seq = seq // tm
    nt = N_FF_TILES
    tn = FF_TILE
    a, sg, sv = pl.pallas_call(
        functools.partial(_convglu_prompt_kernel, tiles_per_seq=tiles_per_seq),
        out_shape=(jax.ShapeDtypeStruct((m, D_FF), BF16),
                   jax.ShapeDtypeStruct((m // tm, CONV_W - 1, D_FF), F32),
                   jax.ShapeDtypeStruct((m // tm, CONV_W - 1, D_FF), F32)),
        grid=(m // tm, nt),
        in_specs=[pl.BlockSpec((tm, k), lambda i, j: (i, 0)),
                  pl.BlockSpec((k, 2 * tn), lambda i, j: (0, j)),
                  pl.BlockSpec((CONV_W, tn), lambda i, j: (0, j)),
                  pl.BlockSpec((CONV_W, tn), lambda i, j: (0, j + nt)),
                  pl.BlockSpec((1, tn), lambda i, j: (0, j)),
                  pl.BlockSpec((1, tn), lambda i, j: (0, j + nt))],
        out_specs=(pl.BlockSpec((tm, tn), lambda i, j: (i, j)),
                   pl.BlockSpec((1, CONV_W - 1, tn), lambda i, j: (i, 0, j)),
                   pl.BlockSpec((1, CONV_W - 1, tn), lambda i, j: (i, 0, j))),
        scratch_shapes=[pltpu.VMEM((nt, CONV_W - 1, tn), F32),
                        pltpu.VMEM((nt, CONV_W - 1, tn), F32)],
        compiler_params=_params(2),
        name="convglu_prompt",
    )(h, w_pairs, conv_w, conv_w, conv_b, conv_b)
    tails = jnp.concatenate([sg, sv], axis=-1)
    return a, tails[tiles_per_seq - 1::tiles_per_seq]


def _convglu_sample_kernel(h_ref, wg_ref, wv_ref, cwg_ref, cwv_ref, cbg_ref, cbv_ref,
                           pg_ref, pv_ref, a_ref, ng_ref, nv_ref, wb_ref):
    def half(w_ref, wb_cols, cw_ref, cb_ref, p_ref, n_ref):
        u = jnp.dot(h_ref[...], _mxu_weights(w_ref, (wb_ref.at[:, wb_cols],)), preferred_element_type=F32)
        p0, p1 = p_ref[:, 0, :], p_ref[:, 1, :]
        n_ref[:, 0, :] = p1
        n_ref[:, 1, :] = u
        return _conv_taps(cb_ref, cw_ref, p0, p1, u)

    cg = half(wg_ref, slice(0, FF_TILE), cwg_ref, cbg_ref, pg_ref, ng_ref)
    cv = half(wv_ref, slice(FF_TILE, 2 * FF_TILE), cwv_ref, cbv_ref, pv_ref, nv_ref)
    a_ref[...] = (_gelu_erf(cg) * cv).astype(a_ref.dtype)


def _convglu_sample(h, w_up, conv_w, conv_b, state_conv):
    m, k = h.shape
    nt = N_FF_TILES
    tn = FF_TILE
    col = lambda off: (lambda j: (0, j + off))
    rows3 = lambda off: pl.BlockSpec((m, CONV_W - 1, tn), lambda j: (0, 0, j + off))
    a, new_g, new_v, w_pairs = pl.pallas_call(
        _convglu_sample_kernel,
        out_shape=(jax.ShapeDtypeStruct((m, D_FF), BF16),
                   jax.ShapeDtypeStruct((m, CONV_W - 1, D_FF), F32),
                   jax.ShapeDtypeStruct((m, CONV_W - 1, D_FF), F32),
                   jax.ShapeDtypeStruct((k, 2 * D_FF), BF16)),
        grid=(nt,),
        in_specs=[pl.BlockSpec((m, k), lambda j: (0, 0)),
                  pl.BlockSpec((k, tn), col(0)),
                  pl.BlockSpec((k, tn), col(nt)),
                  pl.BlockSpec((CONV_W, tn), col(0)),
                  pl.BlockSpec((CONV_W, tn), col(nt)),
                  pl.BlockSpec((1, tn), col(0)),
                  pl.BlockSpec((1, tn), col(nt)),
                  rows3(0), rows3(nt)],
        out_specs=(pl.BlockSpec((m, tn), col(0)),
                   rows3(0), rows3(0),
                   pl.BlockSpec((k, 2 * tn), col(0))),
        compiler_params=_params(1),
        name="convglu_sample",
    )(h, w_up, w_up, conv_w, conv_w, conv_b, conv_b, state_conv, state_conv)
    return a, jnp.concatenate([new_g, new_v], axis=-1), w_pairs


def _segment_mean_square(x, seg_ones):
    sq = x * x
    hi = sq.astype(BF16)
    lo = (sq - hi.astype(F32)).astype(BF16)
    total = (jnp.dot(hi, seg_ones, preferred_element_type=F32)
             + jnp.dot(lo, seg_ones, preferred_element_type=F32))
    return total / HEAD_DIM


def _attn_prompt_kernel(q_ref, kc_ref, vc_ref, bucket_ref, seg_ones_ref, rb_ref, sinks_ref, gq_ref, gk_ref,
                        o_ref, kn_ref, bias_ref, kdup_ref, vdup_ref, *, blocks_per_seq):
    r = pl.program_id(0)
    blk = ATTN_BLOCK
    pair = 2 * HEAD_DIM
    n_bias_rows = N_ATTN_HEADS * blk
    seq_start = (r % blocks_per_seq) == 0

    @pl.when(r == 0)
    def _():
        bucket = bucket_ref[...]
        prev_cols = lax.broadcasted_iota(jnp.int32, bucket.shape, 1) < blk

        def per_head(h, carry):
            acc = jnp.full(bucket.shape, -jnp.inf, F32)
            for b in range(N_BUCKETS):
                acc = jnp.where(bucket == b, rb_ref[b, h], acc)
            row0 = pl.multiple_of(h * blk, blk)
            bias_ref[pl.ds(row0, blk), :] = acc
            bias_ref[pl.ds(n_bias_rows + row0, blk), :] = jnp.where(prev_cols, -jnp.inf, acc)
            return carry
        lax.fori_loop(0, N_ATTN_HEADS, per_head, 0)
        vdup_ref[:, :, pair:] = jnp.ones((N_KV_HEADS, 2 * blk, pair), BF16)

    @pl.when(seq_start)
    def _():
        kdup_ref[:, :blk, :] = jnp.zeros((N_KV_HEADS, blk, pair), BF16)
        vdup_ref[:, :blk, :pair] = jnp.zeros((N_KV_HEADS, blk, pair), BF16)

    @pl.when(jnp.logical_not(seq_start))
    def _():
        kdup_ref[:, :blk, :] = kdup_ref[:, blk:, :]
        vdup_ref[:, :blk, :pair] = vdup_ref[:, blk:, :pair]

    seg_ones = seg_ones_ref[...]
    low_half = lax.broadcasted_iota(jnp.int32, (blk, pair), 1) < HEAD_DIM

    def duplicate_half(x, odd):
        swapped = pltpu.roll(x, HEAD_DIM, axis=1)
        return jnp.where(low_half, swapped, x) if odd else jnp.where(low_half, x, swapped)

    kc = kc_ref[...]
    kn = kc * lax.rsqrt(_segment_mean_square(kc, seg_ones) + EPS) * gk_ref[...]
    kn_ref[...] = kn
    for j in range(N_KV_HEADS):
        col = slice((j // 2) * pair, (j // 2 + 1) * pair)
        kdup_ref[j, blk:, :] = duplicate_half(kn[:, col], j % 2).astype(BF16)
        vdup_ref[j, blk:, :pair] = duplicate_half(vc_ref[:, col], j % 2).astype(BF16)

    bias_base = jnp.where(seq_start, n_bias_rows, 0)
    chunk = 2 * pair
    n_heads = N_ATTN_HEADS
    q_gain = gq_ref[...] * HEAD_DIM ** -0.5
    q_chunks = [q_ref[:, c * chunk:(c + 1) * chunk] for c in range(ATTN_WIDTH // chunk)]
    q_ms = [_segment_mean_square(qc, seg_ones) for qc in q_chunks]
    q_norm = [qc * lax.rsqrt(ms + EPS) * q_gain for qc, ms in zip(q_chunks, q_ms)]
    lhs = []
    for qn in q_norm:
        for p in range(chunk // pair):
            qp = qn[:, p * pair:(p + 1) * pair]
            lhs.append(jnp.where(low_half, qp, 0.0).astype(BF16))
            lhs.append(jnp.where(low_half, 0.0, qp).astype(BF16))
    scores = [lax.dot_general(jnp.concatenate(lhs[j * GQA_GROUP:(j + 1) * GQA_GROUP], axis=0), kdup_ref[j],
                              (((1,), (1,)), ((), ())), preferred_element_type=F32)
              for j in range(N_KV_HEADS)]
    sg = [scores[h // GQA_GROUP][(h % GQA_GROUP) * blk:(h % GQA_GROUP + 1) * blk, :]
          + bias_ref[pl.ds(pl.multiple_of(bias_base + h * blk, blk), blk), :] for h in range(n_heads)]
    m = [jnp.maximum(jnp.max(sg[h], axis=-1, keepdims=True), sinks_ref[h]) for h in range(n_heads)]
    exps = [jnp.exp(sg[h] - m[h]).astype(BF16) for h in range(n_heads)]
    sink_terms = [jnp.exp(sinks_ref[h] - m[h]) for h in range(n_heads)]
    outs = [jnp.dot(jnp.concatenate(exps[j * GQA_GROUP:(j + 1) * GQA_GROUP], axis=0), vdup_ref[j],
                    preferred_element_type=F32) for j in range(N_KV_HEADS)]
    for h in range(0, n_heads, 2):
        o = outs[h // GQA_GROUP]
        g = h % GQA_GROUP
        even = o[g * blk:(g + 1) * blk, :]
        odd = o[(g + 1) * blk:(g + 2) * blk, :]
        num = jnp.where(low_half, even[:, :pair], odd[:, :pair])
        den = (jnp.where(low_half, even[:, pair:], odd[:, pair:])
               + jnp.where(low_half, sink_terms[h], sink_terms[h + 1]))
        o_ref[:, h * HEAD_DIM:h * HEAD_DIM + pair] = (num / den).astype(o_ref.dtype)


def _attn_prompt(proj, bucket_tile, rel_bias, sinks, g_q, g_k, n_seq):
    m = proj.shape[0]
    blk = ATTN_BLOCK
    nb = m // blk
    blocks_per_seq = nb // n_seq
    lanes = 4 * HEAD_DIM
    seg = np.arange(lanes) // HEAD_DIM
    seg_ones = (seg[:, None] == seg[None, :]).astype(BF16)
    tile4 = lambda g: jnp.tile(g.reshape(-1), lanes // HEAD_DIM).reshape(1, lanes)
    smem = pl.BlockSpec(memory_space=pltpu.SMEM)
    return pl.pallas_call(
        functools.partial(_attn_prompt_kernel, blocks_per_seq=blocks_per_seq),
        out_shape=(jax.ShapeDtypeStruct((m, ATTN_WIDTH), BF16),
                   jax.ShapeDtypeStruct((m, KV_WIDTH), F32)),
        grid=(nb,),
        in_specs=[pl.BlockSpec((blk, ATTN_WIDTH), lambda r: (r, COL_AQ // ATTN_WIDTH)),
                  pl.BlockSpec((blk, KV_WIDTH), lambda r: (r, COL_AK // KV_WIDTH)),
                  pl.BlockSpec((blk, KV_WIDTH), lambda r: (r, COL_AV // KV_WIDTH)),
                  pl.BlockSpec((blk, 2 * blk), lambda r: (0, 0)),
                  pl.BlockSpec((lanes, lanes), lambda r: (0, 0)),
                  smem, smem,
                  pl.BlockSpec((1, lanes), lambda r: (0, 0)),
                  pl.BlockSpec((1, lanes), lambda r: (0, 0))],
        out_specs=(pl.BlockSpec((blk, ATTN_WIDTH), lambda r: (r, 0)),
                   pl.BlockSpec((blk, KV_WIDTH), lambda r: (r, 0))),
        scratch_shapes=[pltpu.VMEM((2 * N_ATTN_HEADS * blk, 2 * blk), F32),
                        pltpu.VMEM((N_KV_HEADS, 2 * blk, 2 * HEAD_DIM), BF16),
                        pltpu.VMEM((N_KV_HEADS, 2 * blk, 4 * HEAD_DIM), BF16)],
        compiler_params=_params(1),
        name="attn_prompt",
    )(proj, proj, proj, bucket_tile, seg_ones, rel_bias, sinks, tile4(g_q), tile4(g_k))


def _attn_sample_kernel(q_ref, knew_ref, vnew_ref, ck_ref, cv_ref, bucket_ref, rbt_ref, sinks_ref,
                        gq_ref, gk_ref, o_ref, cko_ref, cvo_ref):
    win = ck_ref.shape[1]
    bucket = bucket_ref[...]
    bias = jnp.full((N_ATTN_HEADS, bucket.shape[1]), -jnp.inf, F32)
    for b in range(N_BUCKETS):
        bias = jnp.where(bucket == b, rbt_ref[:, b:b + 1], bias)
    bias_c = bias[:, :win]
    bias_n = bias[:, win:win + 1]
    sink = sinks_ref[...]
    head_group = jnp.right_shift(lax.broadcasted_iota(jnp.int32, (1, N_ATTN_HEADS, HEAD_DIM), 1),
                                 int(math.log2(GQA_GROUP)))
    lane_group = jnp.right_shift(lax.broadcasted_iota(jnp.int32, (1, 1, KV_WIDTH), 2),
                                 int(math.log2(HEAD_DIM)))
    scale = HEAD_DIM ** -0.5

    k_rows = knew_ref[...]
    k_sq = k_rows * k_rows
    inv = jnp.zeros_like(k_rows)
    for j in range(N_KV_HEADS):
        ms = jnp.sum(jnp.where(lane_group == j, k_sq, 0.0), axis=-1, keepdims=True) / HEAD_DIM
        inv = jnp.where(lane_group == j, lax.rsqrt(ms + EPS), inv)
    kn_rows = k_rows * inv * gk_ref[...]
    v_rows = vnew_ref[...]

    qn = _rms(q_ref[...], gq_ref[...])
    q_bd = jnp.concatenate([jnp.where(head_group == j, qn, 0.0) for j in range(N_KV_HEADS)], axis=2)
    s_c = jnp.einsum('bhd,bkd->bhk', q_bd.astype(BF16), ck_ref[...].astype(BF16),
                     preferred_element_type=F32) * scale + bias_c
    s_n = jnp.sum(q_bd * kn_rows, axis=-1, keepdims=True) * scale + bias_n
    m = jnp.maximum(jnp.maximum(jnp.max(s_c, axis=-1, keepdims=True), s_n), sink)
    e_c = jnp.exp(s_c - m)
    e_n = jnp.exp(s_n - m)
    denom = jnp.sum(e_c, axis=-1, keepdims=True) + e_n + jnp.exp(sink - m)
    o_full = jnp.einsum('bhk,bkd->bhd', (e_c / denom).astype(BF16), cv_ref[...].astype(BF16),
                        preferred_element_type=F32)
    o_full = o_full + (e_n / denom) * v_rows
    o = jnp.zeros(q_ref.shape, F32)
    for j in range(N_KV_HEADS):
        o = jnp.where(head_group == j, o_full[:, :, j * HEAD_DIM:(j + 1) * HEAD_DIM], o)
    o_ref[...] = o.astype(o_ref.dtype)

    cko_ref[:, pl.ds(0, win - 1), :] = ck_ref[:, pl.ds(1, win - 1), :]
    cko_ref[:, pl.ds(win - 1, 1), :] = kn_rows
    cvo_ref[:, pl.ds(0, win - 1), :] = cv_ref[:, pl.ds(1, win - 1), :]
    cvo_ref[:, pl.ds(win - 1, 1), :] = v_rows


def _attn_sample(q, k_new, v_new, cache_k, cache_v, bucket_row, rel_bias_t, sinks_col, g_q, g_k_row, bt):
    nb = q.shape[0]
    win = cache_k.shape[1]
    full = lambda shape: pl.BlockSpec(shape, lambda i: (0,) * len(shape))
    return pl.pallas_call(
        _attn_sample_kernel,
        out_shape=(jax.ShapeDtypeStruct((nb, N_ATTN_HEADS, HEAD_DIM), BF16),
                   jax.ShapeDtypeStruct(cache_k.shape, F32),
                   jax.ShapeDtypeStruct(cache_v.shape, F32)),
        grid=(nb // bt,),
        in_specs=[pl.BlockSpec((bt, N_ATTN_HEADS, HEAD_DIM), lambda i: (i, 0, 0)),
                  pl.BlockSpec((bt, 1, KV_WIDTH), lambda i: (i, 0, 0)),
                  pl.BlockSpec((bt, 1, KV_WIDTH), lambda i: (i, 0, 0)),
                  pl.BlockSpec((bt, win, KV_WIDTH), lambda i: (i, 0, 0)),
                  pl.BlockSpec((bt, win, KV_WIDTH), lambda i: (i, 0, 0)),
                  full(bucket_row.shape), full(rel_bias_t.shape), full(sinks_col.shape),
                  full(g_q.shape), full(g_k_row.shape)],
        out_specs=(pl.BlockSpec((bt, N_ATTN_HEADS, HEAD_DIM), lambda i: (i, 0, 0)),
                   pl.BlockSpec((bt, win, KV_WIDTH), lambda i: (i, 0, 0)),
                   pl.BlockSpec((bt, win, KV_WIDTH), lambda i: (i, 0, 0))),
        compiler_params=_params(1),
        name="attn_sample",
    )(q, k_new, v_new, cache_k, cache_v, bucket_row, rel_bias_t, sinks_col, g_q, g_k_row)


def _ret_prompt_kernel(q_ref, k_ref, v_ref, gate_ref, cos_ref, sin_ref, dmask_ref, qdec_ref, kdec_ref,
                       cdec_ref, o_ref, state_ref, s_ref):
    c = pl.program_id(0)

    @pl.when(c == 0)
    def _():
        s_ref[...] = jnp.zeros_like(s_ref)

    cosf = cos_ref[...]
    sinf = sin_ref[...]
    half = RET_KDIM // 2
    nt = (((1,), (1,)), ((), ()))
    tn = (((0,), (0,)), ((), ()))

    def rotary(x):
        return x * cosf + pltpu.roll(x, half, axis=1) * sinf

    for h in range(N_RET_HEADS):
        kd = slice(h * RET_KDIM, (h + 1) * RET_KDIM)
        vd = slice(h * RET_VDIM, (h + 1) * RET_VDIM)
        for b in range(q_ref.shape[0]):
            q = rotary(q_ref[b, :, kd])
            k = rotary(k_ref[b, :, kd]) * RET_KDIM ** -0.5
            v = v_ref[b, :, vd].astype(BF16)
            s0 = s_ref[b, h]
            scores = lax.dot_general(q.astype(BF16), k.astype(BF16), nt,
                                     preferred_element_type=F32) * dmask_ref[h]
            o_intra = jnp.dot(scores.astype(BF16), v, preferred_element_type=F32)
            o_inter = jnp.dot((q * qdec_ref[h]).astype(BF16), s0.astype(BF16), preferred_element_type=F32)
            s_ref[b, h] = cdec_ref[h] * s0 + lax.dot_general((k * kdec_ref[h]).astype(BF16), v, tn,
                                                              preferred_element_type=F32)
            ro = o_intra + o_inter
            ro = ro * lax.rsqrt(jnp.mean(ro * ro, axis=-1, keepdims=True) + EPS)
            o_ref[b, :, vd] = (jax.nn.silu(gate_ref[b, :, vd]) * ro).astype(o_ref.dtype)

    @pl.when(c == pl.num_programs(0) - 1)
    def _():
        state_ref[...] = s_ref[...]


def _ret_prompt(proj, cosf, sinf, dmask, qdec, kdec, cdec, n_seq):
    m, width = proj.shape
    seq = m // n_seq
    ch = RET_CHUNK
    proj3 = proj.reshape(n_seq, seq, width)
    const3 = pl.BlockSpec((N_RET_HEADS, ch, ch), lambda c: (0, 0, 0))
    state_shape = (n_seq, N_RET_HEADS, RET_KDIM, RET_VDIM)
    o, state = pl.pallas_call(
        _ret_prompt_kernel,
        out_shape=(jax.ShapeDtypeStruct((n_seq, seq, RET_V_WIDTH), BF16),
                   jax.ShapeDtypeStruct(state_shape, F32)),
        grid=(seq // ch,),
        in_specs=[pl.BlockSpec((n_seq, ch, RET_QK_WIDTH), lambda c: (0, c, COL_RQ // RET_QK_WIDTH)),
                  pl.BlockSpec((n_seq, ch, RET_QK_WIDTH), lambda c: (0, c, COL_RK // RET_QK_WIDTH)),
                  pl.BlockSpec((n_seq, ch, RET_V_WIDTH), lambda c: (0, c, COL_RV // RET_V_WIDTH)),
                  pl.BlockSpec((n_seq, ch, RET_V_WIDTH), lambda c: (0, c, COL_RG // RET_V_WIDTH)),
                  pl.BlockSpec((ch, RET_KDIM), lambda c: (c, 0)),
                  pl.BlockSpec((ch, RET_KDIM), lambda c: (c, 0)),
                  const3, const3, const3,
                  pl.BlockSpec(memory_space=pltpu.SMEM)],
        out_specs=(pl.BlockSpec((n_seq, ch, RET_V_WIDTH), lambda c: (0, c, 0)),
                   pl.BlockSpec(state_shape, lambda c: (0, 0, 0, 0))),
        scratch_shapes=[pltpu.VMEM(state_shape, F32)],
        compiler_params=_params(1),
        name="ret_prompt",
    )(proj3, proj3, proj3, proj3, cosf, sinf, dmask, qdec, kdec, cdec)
    return o.reshape(m, RET_V_WIDTH), state


def _ret_sample_kernel(qr_ref, qt_ref, kt_ref, cosr_ref, sinr_ref, cos_ref, sin_ref, v_ref, gate_ref, s_ref,
                       qdec_ref, kdec_ref, cdec_ref, o_ref, so_ref):
    bt = s_ref.shape[0]
    half = RET_KDIM // 2
    cos = cos_ref[...]
    sin = sin_ref[...]
    cosr = cosr_ref[...]
    sinr = sinr_ref[...]

    def rotary(x):
        x1, x2 = x[:half, :], x[half:, :]
        return jnp.concatenate([x1 * cos - x2 * sin, x2 * cos + x1 * sin], axis=0)

    for h in range(N_RET_HEADS):
        vd = slice(h * RET_VDIM, (h + 1) * RET_VDIM)
        q_all = rotary(qt_ref[0, h])
        k_all = rotary(kt_ref[0, h]) * RET_KDIM ** -0.5
        qk_all = jnp.sum(q_all * k_all, axis=0, keepdims=True)
        kd_all = k_all * kdec_ref[h]
        q_rows = qr_ref[0, :, h * RET_KDIM:(h + 1) * RET_KDIM]
        q_rows = q_rows * cosr + pltpu.roll(q_rows, half, axis=1) * sinr
        qd_rows = (q_rows * qdec_ref[h]).astype(BF16)
        for b in range(bt):
            v = v_ref[0, b:b + 1, vd]
            s0 = s_ref[b, h]
            o_intra = qk_all[:, b:b + 1] * v
            o_inter = jnp.dot(qd_rows, s0.astype(BF16), preferred_element_type=F32)[b:b + 1, :]
            so_ref[b, h] = cdec_ref[h] * s0 + kd_all[:, b:b + 1] * v
            ro = o_intra + o_inter
            ro = ro * lax.rsqrt(jnp.mean(ro * ro, axis=-1, keepdims=True) + EPS)
            o_ref[0, b:b + 1, vd] = (jax.nn.silu(gate_ref[0, b:b + 1, vd]) * ro).astype(o_ref.dtype)


def _ret_sample(qr, qt, kt, cos_row, sin_row, cos_col, sin_col, v, gate, state, qdec, kdec, cdec):
    nbt, _, _, bt = qt.shape
    smem = pl.BlockSpec(memory_space=pltpu.SMEM)
    st_spec = pl.BlockSpec((bt, N_RET_HEADS, RET_KDIM, RET_VDIM), lambda i: (i, 0, 0, 0))
    qk_spec = pl.BlockSpec((1, N_RET_HEADS, RET_KDIM, bt), lambda i: (i, 0, 0, 0))
    row_spec = pl.BlockSpec((1, bt, RET_V_WIDTH), lambda i: (i, 0, 0))
    col_spec = pl.BlockSpec(cos_col.shape, lambda i: (0, 0))
    trig_row_spec = pl.BlockSpec(cos_row.shape, lambda i: (0, 0))
    q_row_spec = pl.BlockSpec((1, bt, RET_QK_WIDTH), lambda i: (i, 0, 0))
    return pl.pallas_call(
        _ret_sample_kernel,
        out_shape=(jax.ShapeDtypeStruct((nbt, bt, RET_V_WIDTH), BF16),
                   jax.ShapeDtypeStruct(state.shape, F32)),
        grid=(nbt,),
        in_specs=[q_row_spec, qk_spec, qk_spec, trig_row_spec, trig_row_spec, col_spec, col_spec,
                  row_spec, row_spec, st_spec, smem, smem, smem],
        out_specs=(row_spec, st_spec),
        compiler_params=_params(1),
        name="ret_sample",
    )(qr, qt, kt, cos_row, sin_row, cos_col, sin_col, v, gate, state, qdec, kdec, cdec)


NPF = np.float32


def _t5_bucket(dist):
    n = np.maximum(dist, 0)
    max_exact = N_BUCKETS // 2
    nf = np.maximum(n, 1).astype(NPF)
    large = max_exact + (np.log(nf / NPF(max_exact)) / NPF(math.log(MAX_DISTANCE / max_exact))
                         * NPF(N_BUCKETS - max_exact)).astype(np.int32)
    return np.where(n < max_exact, n, np.minimum(large, N_BUCKETS - 1)).astype(np.int32)


def _rope_tables(pos):
    half = RET_KDIM // 2
    inv = NPF(ROPE_BASE) ** (-np.arange(half, dtype=NPF) / NPF(half))
    ang = pos.astype(NPF)[:, None] * inv[None]
    return np.cos(ang), np.sin(ang)


def _layer(xp, xs, pp, ps, cache_k, cache_v, state_ret, state_conv, rel_bias, lp, n_seq):
    (g_mix, w_in, g_q, g_k, sinks, w_out, g_ffn, w_up, conv_w, conv_b, w_down,
     g_ple, w_ple_gate, w_ple_proj) = lp
    mp = xp.shape[0]
    ms = xs.shape[0]
    seq = mp // n_seq
    win = cache_k.shape[1]

    row = lambda a: a.reshape(1, -1)

    log_decay = np.log(NPF(1.0) - NPF(2.0) ** (NPF(-5.0) - np.arange(N_RET_HEADS, dtype=NPF)))
    idx = np.arange(RET_CHUNK, dtype=NPF)
    diff = idx[:, None] - idx[None, :]
    dmask = np.where(diff[None] >= 0, np.exp(diff[None] * log_decay[:, None, None]), NPF(0.0))
    q_dec = np.exp((idx + NPF(1.0))[:, None] * log_decay[None])
    k_dec = np.exp((NPF(RET_CHUNK - 1.0) - idx)[:, None] * log_decay[None])
    qdec_t = np.broadcast_to(q_dec.T[:, :, None], (N_RET_HEADS, RET_CHUNK, RET_KDIM))
    kdec_t = np.broadcast_to(k_dec.T[:, :, None], (N_RET_HEADS, RET_CHUNK, RET_KDIM))
    cdec = np.exp(NPF(RET_CHUNK) * log_decay)
    cos_p, sin_p = _rope_tables(np.arange(seq, dtype=np.int32))
    cosf = np.concatenate([cos_p, cos_p], axis=1)
    sinf = np.concatenate([-sin_p, sin_p], axis=1)
    one = np.arange(1, dtype=NPF)
    qdec_s = np.exp((one + NPF(1.0))[:, None] * log_decay[None])[0]
    kdec_s = np.exp((NPF(1.0 - 1.0) - one)[:, None] * log_decay[None])[0]
    cdec_s = np.exp(NPF(1.0) * log_decay)
    cos_s, sin_s = _rope_tables(PAST_LEN + np.arange(1, dtype=np.int32))
    cos_col, sin_col = cos_s.reshape(-1, 1), sin_s.reshape(-1, 1)

    qi = np.arange(ATTN_BLOCK, dtype=np.int32)
    ki = np.arange(2 * ATTN_BLOCK, dtype=np.int32) - ATTN_BLOCK
    dist = qi[:, None] - ki[None, :]
    bucket_tile = np.where((dist >= 0) & (dist <= WINDOW), _t5_bucket(dist), -1).astype(np.int32)
    dist_s = win - np.arange(2 * ATTN_BLOCK, dtype=np.int32)
    bucket_row = np.where((dist_s >= 0) & (dist_s <= WINDOW), _t5_bucket(dist_s), -1).astype(np.int32).reshape(1, -1)

    tmp, tms = _row_tile(mp), _row_tile(ms)
    proj_s, w_in_b = _matmul(_norm(xs, row(g_mix), tms), w_in, tms, COL_TILE, IN_SRC_BLOCKS)
    proj_p, = _matmul(_norm(xp, row(g_mix), NORM_TILE), w_in_b, min(mp, IN_ROW_TILE), COL_TILE)

    attn_p, kn_p = _attn_prompt(proj_p, bucket_tile, rel_bias, sinks, row(g_q), row(g_k), n_seq)
    ret_p, ret_state_p = _ret_prompt(proj_p, cosf, sinf, dmask, qdec_t, kdec_t, cdec, n_seq)

    bt_a = 8
    attn_s, ck_new, cv_new = _attn_sample(
        proj_s[:, COL_AQ:COL_AQ + ATTN_WIDTH].reshape(ms, N_ATTN_HEADS, HEAD_DIM),
        proj_s[:, COL_AK:COL_AK + KV_WIDTH].reshape(ms, 1, KV_WIDTH),
        proj_s[:, COL_AV:COL_AV + KV_WIDTH].reshape(ms, 1, KV_WIDTH),
        cache_k.reshape(ms, win, KV_WIDTH), cache_v.reshape(ms, win, KV_WIDTH),
        bucket_row, rel_bias.T, sinks.reshape(-1, 1), row(g_q), jnp.tile(g_k, N_KV_HEADS).reshape(1, -1), bt_a)
    attn_s = attn_s.reshape(ms, ATTN_WIDTH)

    bt_r = 8
    to_cols = lambda a: a.reshape(ms // bt_r, bt_r, N_RET_HEADS, RET_KDIM).transpose(0, 2, 3, 1)
    rq_s = proj_s[:, COL_RQ:COL_RQ + RET_QK_WIDTH]
    ret_s, ret_state_s = _ret_sample(
        rq_s.reshape(ms // bt_r, bt_r, RET_QK_WIDTH), to_cols(rq_s), to_cols(proj_s[:, COL_RK:COL_RK + RET_QK_WIDTH]),
        np.concatenate([cos_s, cos_s], axis=1), np.concatenate([-sin_s, sin_s], axis=1), cos_col, sin_col,
        proj_s[:, COL_RV:COL_RV + RET_V_WIDTH].reshape(ms // bt_r, bt_r, RET_V_WIDTH),
        proj_s[:, COL_RG:COL_RG + RET_V_WIDTH].reshape(ms // bt_r, bt_r, RET_V_WIDTH),
        state_ret, qdec_s, kdec_s, cdec_s)
    ret_s = ret_s.reshape(ms, RET_V_WIDTH)

    xs, w_out_b = _out_proj(attn_s, ret_s, w_out, xs, tms, COL_TILE)
    xp, = _out_proj(attn_p, ret_p, w_out_b, xp, tmp, OUT_COL_TILE)

    act_s, conv_s, w_up_b = _convglu_sample(_norm(xs, row(g_ffn), tms), w_up, conv_w, row(conv_b), state_conv)
    act_p, conv_p = _convglu_prompt(_norm(xp, row(g_ffn), NORM_TILE), w_up_b, conv_w, row(conv_b), n_seq, UP_ROW_TILE)
    for part in range(DOWN_K_PARTS):
        last = part == DOWN_K_PARTS - 1
        xs, w_down_b = _down_proj(act_s, w_down, xs, tms, COL_TILE, DOWN_K_PARTS, part)
        xp, *stats_p = _down_proj(act_p, w_down_b, xp, tmp, COL_TILE, DOWN_K_PARTS, part,
                                  row(g_ple) if last else None)
    hp, ssq_p = stats_p

    xs, w_pgate_b, w_pproj_b = _ple(xs, _norm(xs, row(g_ple), tms), None, w_ple_gate, ps, w_ple_proj, tms, COL_TILE)
    xp, = _ple(xp, hp, ssq_p, w_pgate_b, pp, w_pproj_b, tmp, COL_TILE)

    last = lambda a: a.reshape(n_seq, seq, a.shape[1])[:, seq - WINDOW:, :]
    kp_new = last(kn_p).reshape(n_seq, WINDOW, N_KV_HEADS, HEAD_DIM)
    vp_new = last(proj_p)[:, :, COL_AV:COL_AV + KV_WIDTH].reshape(n_seq, WINDOW, N_KV_HEADS, HEAD_DIM)
    ks_new = ck_new.reshape(ms, win, N_KV_HEADS, HEAD_DIM)
    vs_new = cv_new.reshape(ms, win, N_KV_HEADS, HEAD_DIM)
    return xp, xs, kp_new, vp_new, ret_state_p, conv_p, ks_new, vs_new, ret_state_s, conv_s


def kernel(x_prompt, x_sample, p_prompt, p_sample, cache_win_k, cache_win_v, state_ret, state_conv, rel_bias, g_mix, w_in, g_q, g_k, sinks, w_out, g_ffn, w_up, conv_w, conv_b, w_down, g_ple, w_ple_gate, w_ple_proj):
    depth = g_mix.shape[0]
    n_seq, seq, d = x_prompt.shape
    nb, dec_seq, _ = x_sample.shape
    assert dec_seq == 1 and seq % 512 == 0 and d == D_MODEL
    xp = x_prompt.reshape(n_seq * seq, d)
    xs = x_sample.reshape(nb, d)
    outs = [[] for _ in range(8)]
    for l in range(depth):
        lp = (g_mix[l], w_in[l], g_q[l], g_k[l], sinks[l], w_out[l], g_ffn[l], w_up[l], conv_w[l],
              conv_b[l], w_down[l], g_ple[l], w_ple_gate[l], w_ple_proj[l])
        res = _layer(xp, xs, p_prompt[l].reshape(n_seq * seq, -1), p_sample[l].reshape(nb, -1),
                     cache_win_k[l], cache_win_v[l], state_ret[l], state_conv[l], rel_bias, lp, n_seq)
        xp, xs = res[0], res[1]
        for o, r in zip(outs, res[2:]):
            o.append(r)
    stacked = [jnp.stack(o) for o in outs]
    return (xp.reshape(n_seq, seq, d), xs.reshape(nb, 1, d), *stacked)
```

```python
import functools
import math

import jax
import jax.numpy as jnp
import numpy as np
from jax import lax
from jax.experimental import pallas as pl
from jax.experimental.pallas import tpu as pltpu

F32 = jnp.float32
BF16 = jnp.bfloat16

D_MODEL = 4096
HEAD_DIM = 64
N_ATTN_HEADS = 32
N_KV_HEADS = 4
GQA_GROUP = N_ATTN_HEADS // N_KV_HEADS
WINDOW = 128
ATTN_BLOCK = 128
N_BUCKETS = 32
MAX_DISTANCE = 128
N_RET_HEADS = 8
RET_KDIM = 128
RET_VDIM = 256
RET_CHUNK = 128
ROPE_BASE = 10000.0
D_FF = 11008
CONV_W = 3
EPS = 1e-6

ATTN_WIDTH = N_ATTN_HEADS * HEAD_DIM
KV_WIDTH = N_KV_HEADS * HEAD_DIM
RET_QK_WIDTH = N_RET_HEADS * RET_KDIM
RET_V_WIDTH = N_RET_HEADS * RET_VDIM
IN_SIZES = (ATTN_WIDTH, KV_WIDTH, KV_WIDTH, RET_QK_WIDTH, RET_QK_WIDTH, RET_V_WIDTH, RET_V_WIDTH)
IN_WIDTH = sum(IN_SIZES)
IN_SPLITS = tuple(sum(IN_SIZES[:n + 1]) for n in range(len(IN_SIZES) - 1))
PAST_LEN = 8192

COL_AQ = 0
COL_RV = ATTN_WIDTH
COL_RG = COL_RV + RET_V_WIDTH
COL_RQ = COL_RG + RET_V_WIDTH
COL_RK = COL_RQ + RET_QK_WIDTH
COL_AK = COL_RK + RET_QK_WIDTH
COL_AV = COL_AK + KV_WIDTH

VMEM_LIMIT_BYTES = 56 * 1024 * 1024
FF_TILE = 256
N_FF_TILES = D_FF // FF_TILE
NORM_ROWS = 128
NORM_TILE = 1024
OUT_COL_TILE = 1024
F32_SUBLANES = 8
STAT_LANES = 128
IN_ROW_TILE = 2048
ROW_TILE = 1024
COL_TILE = 512
UP_ROW_TILE = 2048
CONV_CHUNK = 1024
DOWN_K_PARTS = 2


def _row_tile(m):
    return min(m, ROW_TILE)


def _in_src_blocks():
    start = dict(zip(("aq", "ak", "av", "rq", "rk", "rv", "rg"), (0,) + IN_SPLITS))
    order = (("aq", ATTN_WIDTH), ("rv", RET_V_WIDTH), ("rg", RET_V_WIDTH), ("rq", RET_QK_WIDTH),
             ("rk", RET_QK_WIDTH), ("ak", KV_WIDTH), ("av", KV_WIDTH))
    cols = [c for name, width in order for c in range(start[name], start[name] + width, KV_WIDTH)]
    per_tile = COL_TILE // KV_WIDTH
    blocks = []
    for b in range(0, len(cols), per_tile):
        group = cols[b:b + per_tile]
        assert group[0] % COL_TILE == 0 and all(c == group[0] + u * KV_WIDTH for u, c in enumerate(group))
        blocks.append(group[0] // COL_TILE)
    return tuple(blocks)


IN_SRC_BLOCKS = _in_src_blocks()


def _params(n_axes):
    return pltpu.CompilerParams(dimension_semantics=("arbitrary",) * n_axes,
                                vmem_limit_bytes=VMEM_LIMIT_BYTES)


def _rms(x, g):
    y = x * lax.rsqrt(jnp.mean(x * x, axis=-1, keepdims=True) + EPS)
    return y * g


def _norm_rows_to_bf16(x_ref, g_ref, h_ref):
    def body(c, carry):
        r0 = pl.multiple_of(c * NORM_ROWS, NORM_ROWS)
        h_ref[pl.ds(r0, NORM_ROWS), :] = _rms(x_ref[pl.ds(r0, NORM_ROWS), :], g_ref[...]).astype(h_ref.dtype)
        return carry
    lax.fori_loop(0, x_ref.shape[0] // NORM_ROWS, body, 0)


def _norm(x, g, tr):
    m, k = x.shape
    return pl.pallas_call(
        _norm_rows_to_bf16,
        out_shape=jax.ShapeDtypeStruct((m, k), BF16),
        grid=(m // tr,),
        in_specs=[pl.BlockSpec((tr, k), lambda i: (i, 0)),
                  pl.BlockSpec((1, k), lambda i: (0, 0))],
        out_specs=pl.BlockSpec((tr, k), lambda i: (i, 0)),
        compiler_params=_params(1),
        name="rmsnorm",
    )(x, g)


def _mxu_weights(w_ref, wb_ref):
    if not wb_ref:
        return w_ref[...]
    wb = w_ref[...].astype(BF16)
    wb_ref[0][...] = wb
    return wb


def _mm_kernel(h_ref, w_ref, o_ref, *wb_ref):
    o_ref[...] = jnp.dot(h_ref[...], _mxu_weights(w_ref, wb_ref), preferred_element_type=F32)


def _matmul(h, w, tm, tn, src_blocks=None):
    m, k = h.shape
    n = w.shape[1]
    emit = w.dtype != BF16
    if emit:
        w_map = lambda i, j: (0, sum(jnp.where(j == d, s, 0) for d, s in enumerate(src_blocks)))
    else:
        w_map = lambda i, j: (0, j)
    out_shape = [jax.ShapeDtypeStruct((m, n), F32)]
    out_specs = [pl.BlockSpec((tm, tn), lambda i, j: (i, j))]
    if emit:
        assert m == tm
        out_shape.append(jax.ShapeDtypeStruct((k, n), BF16))
        out_specs.append(pl.BlockSpec((k, tn), lambda i, j: (0, j)))
    return pl.pallas_call(
        _mm_kernel,
        out_shape=out_shape,
        grid=(m // tm, n // tn),
        in_specs=[pl.BlockSpec((tm, k), lambda i, j: (i, 0)),
                  pl.BlockSpec((k, tn), w_map)],
        out_specs=out_specs,
        compiler_params=_params(2),
        name="in_proj",
    )(h, w)


def _row_scale(ssq_ref, width, n_cols):
    r = lax.rsqrt(ssq_ref[...] / width + EPS)
    return jnp.concatenate([r] * (n_cols // STAT_LANES), axis=1)


def _residual_mm_kernel(*refs, n_lhs, emit_weights, norm_stats):
    lhs_refs = refs[:n_lhs]
    w_ref, x_ref = refs[n_lhs], refs[n_lhs + 1]
    rest = list(refs[n_lhs + 2:])
    g_ref = rest.pop(0) if norm_stats else None
    o_ref = rest.pop(0)
    wb_ref = (rest.pop(0),) if emit_weights else ()
    lhs = jnp.concatenate([r[...] for r in lhs_refs], axis=1) if n_lhs > 1 else lhs_refs[0][...]
    o = x_ref[...] + jnp.dot(lhs, _mxu_weights(w_ref, wb_ref), preferred_element_type=F32)
    o_ref[...] = o
    if norm_stats:
        hb_ref, ssq_ref = rest
        hb_ref[...] = (o * g_ref[...]).astype(hb_ref.dtype)
        part = jnp.broadcast_to(jnp.sum(o * o, axis=-1, keepdims=True), ssq_ref.shape)
        j = pl.program_id(1)

        @pl.when(j == 0)
        def _():
            ssq_ref[...] = part

        @pl.when(j > 0)
        def _():
            ssq_ref[...] += part


def _residual_matmul(lhs, lhs_specs, w, w_spec, wb_shape, x, g_next, tm, tn, name):
    m, n = x.shape
    emit = w.dtype != BF16
    stats = g_next is not None
    tile = pl.BlockSpec((tm, tn), lambda i, j: (i, j))
    operands = [*lhs, w, x]
    in_specs = [*lhs_specs, w_spec, tile]
    out_shape = [jax.ShapeDtypeStruct((m, n), F32)]
    out_specs = [tile]
    if stats:
        operands.append(g_next)
        in_specs.append(pl.BlockSpec((1, tn), lambda i, j: (0, j)))
    if emit:
        assert m == tm
        out_shape.append(jax.ShapeDtypeStruct(wb_shape, BF16))
        out_specs.append(pl.BlockSpec((wb_shape[0], tn), lambda i, j: (0, j)))
    if stats:
        out_shape += [jax.ShapeDtypeStruct((m, n), BF16), jax.ShapeDtypeStruct((m, STAT_LANES), F32)]
        out_specs += [tile, pl.BlockSpec((tm, STAT_LANES), lambda i, j: (i, 0))]
    return pl.pallas_call(
        functools.partial(_residual_mm_kernel, n_lhs=len(lhs), emit_weights=emit, norm_stats=stats),
        out_shape=out_shape,
        grid=(m // tm, n // tn),
        in_specs=in_specs,
        out_specs=out_specs,
        compiler_params=_params(2),
        name=name,
    )(*operands)


def _out_proj(a, r, w, x, tm, tn, g_next=None):
    ka, kr = a.shape[1], r.shape[1]
    lhs_specs = [pl.BlockSpec((tm, ka), lambda i, j: (i, 0)), pl.BlockSpec((tm, kr), lambda i, j: (i, 0))]
    w_spec = pl.BlockSpec((ka + kr, tn), lambda i, j: (0, j))
    return _residual_matmul([a, r], lhs_specs, w, w_spec, w.shape, x, g_next, tm, tn, "out_proj")


def _down_proj(a, w, x, tm, tn, k_parts, part, g_next=None):
    k = a.shape[1] // k_parts
    w_part = part if w.dtype != BF16 else 0
    lhs_specs = [pl.BlockSpec((tm, k), lambda i, j: (i, part))]
    w_spec = pl.BlockSpec((k, tn), lambda i, j: (w_part, j))
    return _residual_matmul([a], lhs_specs, w, w_spec, (k, w.shape[1]), x, g_next, tm, tn, "down_proj")


def _ple_kernel(*refs, emit_weights, row_scaled):
    h_ref, wg_ref, p_ref, wp_ref, x_ref = refs[:5]
    rest = list(refs[5:])
    ssq_ref = rest.pop(0) if row_scaled else None
    o_ref = rest.pop(0)
    wb_refs = tuple(rest) if emit_weights else ()
    z = jnp.dot(h_ref[...], _mxu_weights(wg_ref, wb_refs[:1]), preferred_element_type=F32)
    if row_scaled:
        z = z * _row_scale(ssq_ref, h_ref.shape[1], z.shape[1])
    pp = jnp.dot(p_ref[...].astype(BF16), _mxu_weights(wp_ref, wb_refs[1:]), preferred_element_type=F32)
    o_ref[...] = x_ref[...] + jax.nn.sigmoid(z) * pp


def _ple(x, h, ssq, wg, p, wp, tm, tn):
    m, k = h.shape
    n = wg.shape[1]
    kp = p.shape[1]
    emit = wg.dtype != BF16
    scaled = ssq is not None
    operands = [h, wg, p, wp, x]
    in_specs = [pl.BlockSpec((tm, k), lambda i, j: (i, 0)),
                pl.BlockSpec((k, tn), lambda i, j: (0, j)),
                pl.BlockSpec((tm, kp), lambda i, j: (i, 0)),
                pl.BlockSpec((kp, tn), lambda i, j: (0, j)),
                pl.BlockSpec((tm, tn), lambda i, j: (i, j))]
    if scaled:
        operands.append(ssq)
        in_specs.append(pl.BlockSpec((tm, STAT_LANES), lambda i, j: (i, 0)))
    out_shape = [jax.ShapeDtypeStruct((m, n), F32)]
    out_specs = [pl.BlockSpec((tm, tn), lambda i, j: (i, j))]
    if emit:
        assert m == tm
        out_shape += [jax.ShapeDtypeStruct(wg.shape, BF16), jax.ShapeDtypeStruct(wp.shape, BF16)]
        out_specs += [pl.BlockSpec((k, tn), lambda i, j: (0, j)), pl.BlockSpec((kp, tn), lambda i, j: (0, j))]
    return pl.pallas_call(
        functools.partial(_ple_kernel, emit_weights=emit, row_scaled=scaled),
        out_shape=out_shape,
        grid=(m // tm, n // tn),
        in_specs=in_specs,
        out_specs=out_specs,
        compiler_params=_params(2),
        name="ple",
    )(*operands)


def _gelu_erf(x):
    return 0.5 * x * (1.0 + lax.erf(x * math.sqrt(0.5)))


def _conv_taps(cb_ref, cw_ref, um2, um1, u):
    c = cb_ref[...] + cw_ref[0:1, :] * um2
    c = c + cw_ref[1:2, :] * um1
    return c + cw_ref[2:3, :] * u


def _convglu_prompt_kernel(h_ref, w_ref, cwg_ref, cwv_ref, cbg_ref, cbv_ref,
                           a_ref, sg_ref, sv_ref, carry_g_ref, carry_v_ref, *, tiles_per_seq):
    i = pl.program_id(0)
    j = pl.program_id(1)
    tm = h_ref.shape[0]
    rows = min(tm, CONV_CHUNK)
    seq_start = (i % tiles_per_seq) == 0
    sub = F32_SUBLANES
    row = lax.broadcasted_iota(jnp.int32, (sub, 1), 0)

    def half(u, cw_ref, cb_ref, prev):
        r1 = pltpu.roll(u, 1, axis=0)
        r2 = pltpu.roll(u, 2, axis=0)
        head1 = jnp.where(row == 0, prev[1:2, :], r1[:sub, :])
        head2 = jnp.where(row == 0, prev[0:1, :], jnp.where(row == 1, prev[1:2, :], r2[:sub, :]))
        um1 = jnp.concatenate([head1, r1[sub:, :]], axis=0)
        um2 = jnp.concatenate([head2, r2[sub:, :]], axis=0)
        return _conv_taps(cb_ref, cw_ref, um2, um1, u), u[rows - (CONV_W - 1):, :]

    prev_g = jnp.where(seq_start, 0.0, carry_g_ref[j])
    prev_v = jnp.where(seq_start, 0.0, carry_v_ref[j])
    for c in range(tm // rows):
        rs = slice(c * rows, (c + 1) * rows)
        u = jnp.dot(h_ref[rs, :], w_ref[...], preferred_element_type=F32)
        cg, prev_g = half(u[:, :FF_TILE], cwg_ref, cbg_ref, prev_g)
        cv, prev_v = half(u[:, FF_TILE:], cwv_ref, cbv_ref, prev_v)
        a_ref[rs, :] = (_gelu_erf(cg) * cv).astype(a_ref.dtype)
    carry_g_ref[j] = prev_g
    carry_v_ref[j] = prev_v
    sg_ref[0] = prev_g
    sv_ref[0] = prev_v


def _convglu_prompt(h, w_pairs, conv_w, conv_b, n_seq, tm):
    m, k = h.shape
    seq = m // n_seq
    tiles_per_seq = seq // tm
    nt = N_FF_TILES
    tn = FF_TILE
    a, sg, sv = pl.pallas_call(
        functools.partial(_convglu_prompt_kernel, tiles_per_seq=tiles_per_seq),
        out_shape=(jax.ShapeDtypeStruct((m, D_FF), BF16),
                   jax.ShapeDtypeStruct((m // tm, CONV_W - 1, D_FF), F32),
                   jax.ShapeDtypeStruct((m // tm, CONV_W - 1, D_FF), F32)),
        grid=(m // tm, nt),
        in_specs=[pl.BlockSpec((tm, k), lambda i, j: (i, 0)),
                  pl.BlockSpec((k, 2 * tn), lambda i, j: (0, j)),
                  pl.BlockSpec((CONV_W, tn), lambda i, j: (0, j)),
                  pl.BlockSpec((CONV_W, tn), lambda i, j: (0, j + nt)),
                  pl.BlockSpec((1, tn), lambda i, j: (0, j)),
                  pl.BlockSpec((1, tn), lambda i, j: (0, j + nt))],
        out_specs=(pl.BlockSpec((tm, tn), lambda i, j: (i, j)),
                   pl.BlockSpec((1, CONV_W - 1, tn), lambda i, j: (i, 0, j)),
                   pl.BlockSpec((1, CONV_W - 1, tn), lambda i, j: (i, 0, j))),
        scratch_shapes=[pltpu.VMEM((nt, CONV_W - 1, tn), F32),
                        pltpu.VMEM((nt, CONV_W - 1, tn), F32)],
        compiler_params=_params(2),
        name="convglu_prompt",
    )(h, w_pairs, conv_w, conv_w, conv_b, conv_b)
    tails = jnp.concatenate([sg, sv], axis=-1)
    return a, tails[tiles_per_seq - 1::tiles_per_seq]


def _convglu_sample_kernel(h_ref, wg_ref, wv_ref, cwg_ref, cwv_ref, cbg_ref, cbv_ref,
                           pg_ref, pv_ref, a_ref, ng_ref, nv_ref, wb_ref):
    def half(w_ref, wb_cols, cw_ref, cb_ref, p_ref, n_ref):
        u = jnp.dot(h_ref[...], _mxu_weights(w_ref, (wb_ref.at[:, wb_cols],)), preferred_element_type=F32)
        p0, p1 = p_ref[:, 0, :], p_ref[:, 1, :]
        n_ref[:, 0, :] = p1
        n_ref[:, 1, :] = u
        return _conv_taps(cb_ref, cw_ref, p0, p1, u)

    cg = half(wg_ref, slice(0, FF_TILE), cwg_ref, cbg_ref, pg_ref, ng_ref)
    cv = half(wv_ref, slice(FF_TILE, 2 * FF_TILE), cwv_ref, cbv_ref, pv_ref, nv_ref)
    a_ref[...] = (_gelu_erf(cg) * cv).astype(a_ref.dtype)


def _convglu_sample(h, w_up, conv_w, conv_b, state_conv):
    m, k = h.shape
    nt = N_FF_TILES
    tn = FF_TILE
    col = lambda off: (lambda j: (0, j + off))
    rows3 = lambda off: pl.BlockSpec((m, CONV_W - 1, tn), lambda j: (0, 0, j + off))
    a, new_g, new_v, w_pairs = pl.pallas_call(
        _convglu_sample_kernel,
        out_shape=(jax.ShapeDtypeStruct((m, D_FF), BF16),
                   jax.ShapeDtypeStruct((m, CONV_W - 1, D_FF), F32),
                   jax.ShapeDtypeStruct((m, CONV_W - 1, D_FF), F32),
                   jax.ShapeDtypeStruct((k, 2 * D_FF), BF16)),
        grid=(nt,),
        in_specs=[pl.BlockSpec((m, k), lambda j: (0, 0)),
                  pl.BlockSpec((k, tn), col(0)),
                  pl.BlockSpec((k, tn), col(nt)),
                  pl.BlockSpec((CONV_W, tn), col(0)),
                  pl.BlockSpec((CONV_W, tn), col(nt)),
                  pl.BlockSpec((1, tn), col(0)),
                  pl.BlockSpec((1, tn), col(nt)),
                  rows3(0), rows3(nt)],
        out_specs=(pl.BlockSpec((m, tn), col(0)),
                   rows3(0), rows3(0),
                   pl.BlockSpec((k, 2 * tn), col(0))),
        compiler_params=_params(1),
        name="convglu_sample",
    )(h, w_up, w_up, conv_w, conv_w, conv_b, conv_b, state_conv, state_conv)
    return a, jnp.concatenate([new_g, new_v], axis=-1), w_pairs


def _segment_mean_square(x, seg_ones):
    sq = x * x
    hi = sq.astype(BF16)
    lo = (sq - hi.astype(F32)).astype(BF16)
    total = (jnp.dot(hi, seg_ones, preferred_element_type=F32)
             + jnp.dot(lo, seg_ones, preferred_element_type=F32))
    return total / HEAD_DIM


def _attn_prompt_kernel(q_ref, kc_ref, vc_ref, bucket_ref, seg_ones_ref, rb_ref, sinks_ref, gq_ref, gk_ref,
                        o_ref, kn_ref, bias_ref, kdup_ref, vdup_ref, *, blocks_per_seq):
    r = pl.program_id(0)
    blk = ATTN_BLOCK
    pair = 2 * HEAD_DIM
    n_bias_rows = N_ATTN_HEADS * blk
    seq_start = (r % blocks_per_seq) == 0

    @pl.when(r == 0)
    def _():
        bucket = bucket_ref[...]
        prev_cols = lax.broadcasted_iota(jnp.int32, bucket.shape, 1) < blk

        def per_head(h, carry):
            acc = jnp.full(bucket.shape, -jnp.inf, F32)
            for b in range(N_BUCKETS):
                acc = jnp.where(bucket == b, rb_ref[b, h], acc)
            row0 = pl.multiple_of(h * blk, blk)
            bias_ref[pl.ds(row0, blk), :] = acc
            bias_ref[pl.ds(n_bias_rows + row0, blk), :] = jnp.where(prev_cols, -jnp.inf, acc)
            return carry
        lax.fori_loop(0, N_ATTN_HEADS, per_head, 0)
        vdup_ref[:, :, pair:] = jnp.ones((N_KV_HEADS, 2 * blk, pair), BF16)

    @pl.when(seq_start)
    def _():
        kdup_ref[:, :blk, :] = jnp.zeros((N_KV_HEADS, blk, pair), BF16)
        vdup_ref[:, :blk, :pair] = jnp.zeros((N_KV_HEADS, blk, pair), BF16)

    @pl.when(jnp.logical_not(seq_start))
    def _():
        kdup_ref[:, :blk, :] = kdup_ref[:, blk:, :]
        vdup_ref[:, :blk, :pair] = vdup_ref[:, blk:, :pair]

    seg_ones = seg_ones_ref[...]
    low_half = lax.broadcasted_iota(jnp.int32, (blk, pair), 1) < HEAD_DIM

    def duplicate_half(x, odd):
        swapped = pltpu.roll(x, HEAD_DIM, axis=1)
        return jnp.where(low_half, swapped, x) if odd else jnp.where(low_half, x, swapped)

    kc = kc_ref[...]
    kn = kc * lax.rsqrt(_segment_mean_square(kc, seg_ones) + EPS) * gk_ref[...]
    kn_ref[...] = kn
    for j in range(N_KV_HEADS):
        col = slice((j // 2) * pair, (j // 2 + 1) * pair)
        kdup_ref[j, blk:, :] = duplicate_half(kn[:, col], j % 2).astype(BF16)
        vdup_ref[j, blk:, :pair] = duplicate_half(vc_ref[:, col], j % 2).astype(BF16)

    bias_base = jnp.where(seq_start, n_bias_rows, 0)
    chunk = 2 * pair
    n_heads = N_ATTN_HEADS
    q_gain = gq_ref[...] * HEAD_DIM ** -0.5
    q_chunks = [q_ref[:, c * chunk:(c + 1) * chunk] for c in range(ATTN_WIDTH // chunk)]
    q_ms = [_segment_mean_square(qc, seg_ones) for qc in q_chunks]
    q_norm = [qc * lax.rsqrt(ms + EPS) * q_gain for qc, ms in zip(q_chunks, q_ms)]
    lhs = []
    for qn in q_norm:
        for p in range(chunk // pair):
            qp = qn[:, p * pair:(p + 1) * pair]
            lhs.append(jnp.where(low_half, qp, 0.0).astype(BF16))
            lhs.append(jnp.where(low_half, 0.0, qp).astype(BF16))
    scores = [lax.dot_general(jnp.concatenate(lhs[j * GQA_GROUP:(j + 1) * GQA_GROUP], axis=0), kdup_ref[j],
                              (((1,), (1,)), ((), ())), preferred_element_type=F32)
              for j in range(N_KV_HEADS)]
    sg = [scores[h // GQA_GROUP][(h % GQA_GROUP) * blk:(h % GQA_GROUP + 1) * blk, :]
          + bias_ref[pl.ds(pl.multiple_of(bias_base + h * blk, blk), blk), :] for h in range(n_heads)]
    m = [jnp.maximum(jnp.max(sg[h], axis=-1, keepdims=True), sinks_ref[h]) for h in range(n_heads)]
    exps = [jnp.exp(sg[h] - m[h]).astype(BF16) for h in range(n_heads)]
    sink_terms = [jnp.exp(sinks_ref[h] - m[h]) for h in range(n_heads)]
    outs = [jnp.dot(jnp.concatenate(exps[j * GQA_GROUP:(j + 1) * GQA_GROUP], axis=0), vdup_ref[j],
                    preferred_element_type=F32) for j in range(N_KV_HEADS)]
    for h in range(0, n_heads, 2):
        o = outs[h // GQA_GROUP]
        g = h % GQA_GROUP
        even = o[g * blk:(g + 1) * blk, :]
        odd = o[(g + 1) * blk:(g + 2) * blk, :]
        num = jnp.where(low_half, even[:, :pair], odd[:, :pair])
        den = (jnp.where(low_half, even[:, pair:], odd[:, pair:])
               + jnp.where(low_half, sink_terms[h], sink_terms[h + 1]))
        o_ref[:, h * HEAD_DIM:h * HEAD_DIM + pair] = (num / den).astype(o_ref.dtype)


def _attn_prompt(proj, bucket_tile, rel_bias, sinks, g_q, g_k, n_seq):
    m = proj.shape[0]
    blk = ATTN_BLOCK
    nb = m // blk
    blocks_per_seq = nb // n_seq
    lanes = 4 * HEAD_DIM
    seg = np.arange(lanes) // HEAD_DIM
    seg_ones = (seg[:, None] == seg[None, :]).astype(BF16)
    tile4 = lambda g: jnp.tile(g.reshape(-1), lanes // HEAD_DIM).reshape(1, lanes)
    smem = pl.BlockSpec(memory_space=pltpu.SMEM)
    return pl.pallas_call(
        functools.partial(_attn_prompt_kernel, blocks_per_seq=blocks_per_seq),
        out_shape=(jax.ShapeDtypeStruct((m, ATTN_WIDTH), BF16),
                   jax.ShapeDtypeStruct((m, KV_WIDTH), F32)),
        grid=(nb,),
        in_specs=[pl.BlockSpec((blk, ATTN_WIDTH), lambda r: (r, COL_AQ // ATTN_WIDTH)),
                  pl.BlockSpec((blk, KV_WIDTH), lambda r: (r, COL_AK // KV_WIDTH)),
                  pl.BlockSpec((blk, KV_WIDTH), lambda r: (r, COL_AV // KV_WIDTH)),
                  pl.BlockSpec((blk, 2 * blk), lambda r: (0, 0)),
                  pl.BlockSpec((lanes, lanes), lambda r: (0, 0)),
                  smem, smem,
                  pl.BlockSpec((1, lanes), lambda r: (0, 0)),
                  pl.BlockSpec((1, lanes), lambda r: (0, 0))],
        out_specs=(pl.BlockSpec((blk, ATTN_WIDTH), lambda r: (r, 0)),
                   pl.BlockSpec((blk, KV_WIDTH), lambda r: (r, 0))),
        scratch_shapes=[pltpu.VMEM((2 * N_ATTN_HEADS * blk, 2 * blk), F32),
                        pltpu.VMEM((N_KV_HEADS, 2 * blk, 2 * HEAD_DIM), BF16),
                        pltpu.VMEM((N_KV_HEADS, 2 * blk, 4 * HEAD_DIM), BF16)],
        compiler_params=_params(1),
        name="attn_prompt",
    )(proj, proj, proj, bucket_tile, seg_ones, rel_bias, sinks, tile4(g_q), tile4(g_k))


def _attn_sample_kernel(q_ref, knew_ref, vnew_ref, ck_ref, cv_ref, bucket_ref, rbt_ref, sinks_ref,
                        gq_ref, gk_ref, o_ref, cko_ref, cvo_ref):
    win = ck_ref.shape[1]
    bucket = bucket_ref[...]
    bias = jnp.full((N_ATTN_HEADS, bucket.shape[1]), -jnp.inf, F32)
    for b in range(N_BUCKETS):
        bias = jnp.where(bucket == b, rbt_ref[:, b:b + 1], bias)
    bias_c = bias[:, :win]
    bias_n = bias[:, win:win + 1]
    sink = sinks_ref[...]
    head_group = jnp.right_shift(lax.broadcasted_iota(jnp.int32, (1, N_ATTN_HEADS, HEAD_DIM), 1),
                                 int(math.log2(GQA_GROUP)))
    lane_group = jnp.right_shift(lax.broadcasted_iota(jnp.int32, (1, 1, KV_WIDTH), 2),
                                 int(math.log2(HEAD_DIM)))
    scale = HEAD_DIM ** -0.5

    k_rows = knew_ref[...]
    k_sq = k_rows * k_rows
    inv = jnp.zeros_like(k_rows)
    for j in range(N_KV_HEADS):
        ms = jnp.sum(jnp.where(lane_group == j, k_sq, 0.0), axis=-1, keepdims=True) / HEAD_DIM
        inv = jnp.where(lane_group == j, lax.rsqrt(ms + EPS), inv)
    kn_rows = k_rows * inv * gk_ref[...]
    v_rows = vnew_ref[...]

    qn = _rms(q_ref[...], gq_ref[...])
    q_bd = jnp.concatenate([jnp.where(head_group == j, qn, 0.0) for j in range(N_KV_HEADS)], axis=2)
    s_c = jnp.einsum('bhd,bkd->bhk', q_bd.astype(BF16), ck_ref[...].astype(BF16),
                     preferred_element_type=F32) * scale + bias_c
    s_n = jnp.sum(q_bd * kn_rows, axis=-1, keepdims=True) * scale + bias_n
    m = jnp.maximum(jnp.maximum(jnp.max(s_c, axis=-1, keepdims=True), s_n), sink)
    e_c = jnp.exp(s_c - m)
    e_n = jnp.exp(s_n - m)
    denom = jnp.sum(e_c, axis=-1, keepdims=True) + e_n + jnp.exp(sink - m)
    o_full = jnp.einsum('bhk,bkd->bhd', (e_c / denom).astype(BF16), cv_ref[...].astype(BF16),
                        preferred_element_type=F32)
    o_full = o_full + (e_n / denom) * v_rows
    o = jnp.zeros(q_ref.shape, F32)
    for j in range(N_KV_HEADS):
        o = jnp.where(head_group == j, o_full[:, :, j * HEAD_DIM:(j + 1) * HEAD_DIM], o)
    o_ref[...] = o.astype(o_ref.dtype)

    cko_ref[:, pl.ds(0, win - 1), :] = ck_ref[:, pl.ds(1, win - 1), :]
    cko_ref[:, pl.ds(win - 1, 1), :] = kn_rows
    cvo_ref[:, pl.ds(0, win - 1), :] = cv_ref[:, pl.ds(1, win - 1), :]
    cvo_ref[:, pl.ds(win - 1, 1), :] = v_rows


def _attn_sample(q, k_new, v_new, cache_k, cache_v, bucket_row, rel_bias_t, sinks_col, g_q, g_k_row, bt):
    nb = q.shape[0]
    win = cache_k.shape[1]
    full = lambda shape: pl.BlockSpec(shape, lambda i: (0,) * len(shape))
    return pl.pallas_call(
        _attn_sample_kernel,
        out_shape=(jax.ShapeDtypeStruct((nb, N_ATTN_HEADS, HEAD_DIM), BF16),
                   jax.ShapeDtypeStruct(cache_k.shape, F32),
                   jax.ShapeDtypeStruct(cache_v.shape, F32)),
        grid=(nb // bt,),
        in_specs=[pl.BlockSpec((bt, N_ATTN_HEADS, HEAD_DIM), lambda i: (i, 0, 0)),
                  pl.BlockSpec((bt, 1, KV_WIDTH), lambda i: (i, 0, 0)),
                  pl.BlockSpec((bt, 1, KV_WIDTH), lambda i: (i, 0, 0)),
                  pl.BlockSpec((bt, win, KV_WIDTH), lambda i: (i, 0, 0)),
                  pl.BlockSpec((bt, win, KV_WIDTH), lambda i: (i, 0, 0)),
                  full(bucket_row.shape), full(rel_bias_t.shape), full(sinks_col.shape),
                  full(g_q.shape), full(g_k_row.shape)],
        out_specs=(pl.BlockSpec((bt, N_ATTN_HEADS, HEAD_DIM), lambda i: (i, 0, 0)),
                   pl.BlockSpec((bt, win, KV_WIDTH), lambda i: (i, 0, 0)),
                   pl.BlockSpec((bt, win, KV_WIDTH), lambda i: (i, 0, 0))),
        compiler_params=_params(1),
        name="attn_sample",
    )(q, k_new, v_new, cache_k, cache_v, bucket_row, rel_bias_t, sinks_col, g_q, g_k_row)


def _ret_prompt_kernel(q_ref, k_ref, v_ref, gate_ref, cos_ref, sin_ref, dmask_ref, qdec_ref, kdec_ref,
                       cdec_ref, o_ref, state_ref, s_ref):
    c = pl.program_id(0)

    @pl.when(c == 0)
    def _():
        s_ref[...] = jnp.zeros_like(s_ref)

    cosf = cos_ref[...]
    sinf = sin_ref[...]
    half = RET_KDIM // 2
    nt = (((1,), (1,)), ((), ()))
    tn = (((0,), (0,)), ((), ()))

    def rotary(x):
        return x * cosf + pltpu.roll(x, half, axis=1) * sinf

    for h in range(N_RET_HEADS):
        kd = slice(h * RET_KDIM, (h + 1) * RET_KDIM)
        vd = slice(h * RET_VDIM, (h + 1) * RET_VDIM)
        for b in range(q_ref.shape[0]):
            q = rotary(q_ref[b, :, kd])
            k = rotary(k_ref[b, :, kd]) * RET_KDIM ** -0.5
            v = v_ref[b, :, vd].astype(BF16)
            s0 = s_ref[b, h]
            scores = lax.dot_general(q.astype(BF16), k.astype(BF16), nt,
                                     preferred_element_type=F32) * dmask_ref[h]
            o_intra = jnp.dot(scores.astype(BF16), v, preferred_element_type=F32)
            o_inter = jnp.dot((q * qdec_ref[h]).astype(BF16), s0.astype(BF16), preferred_element_type=F32)
            s_ref[b, h] = cdec_ref[h] * s0 + lax.dot_general((k * kdec_ref[h]).astype(BF16), v, tn,
                                                              preferred_element_type=F32)
            ro = o_intra + o_inter
            ro = ro * lax.rsqrt(jnp.mean(ro * ro, axis=-1, keepdims=True) + EPS)
            o_ref[b, :, vd] = (jax.nn.silu(gate_ref[b, :, vd]) * ro).astype(o_ref.dtype)

    @pl.when(c == pl.num_programs(0) - 1)
    def _():
        state_ref[...] = s_ref[...]


def _ret_prompt(proj, cosf, sinf, dmask, qdec, kdec, cdec, n_seq):
    m, width = proj.shape
    seq = m // n_seq
    ch = RET_CHUNK
    proj3 = proj.reshape(n_seq, seq, width)
    const3 = pl.BlockSpec((N_RET_HEADS, ch, ch), lambda c: (0, 0, 0))
    state_shape = (n_seq, N_RET_HEADS, RET_KDIM, RET_VDIM)
    o, state = pl.pallas_call(
        _ret_prompt_kernel,
        out_shape=(jax.ShapeDtypeStruct((n_seq, seq, RET_V_WIDTH), BF16),
                   jax.ShapeDtypeStruct(state_shape, F32)),
        grid=(seq // ch,),
        in_specs=[pl.BlockSpec((n_seq, ch, RET_QK_WIDTH), lambda c: (0, c, COL_RQ // RET_QK_WIDTH)),
                  pl.BlockSpec((n_seq, ch, RET_QK_WIDTH), lambda c: (0, c, COL_RK // RET_QK_WIDTH)),
                  pl.BlockSpec((n_seq, ch, RET_V_WIDTH), lambda c: (0, c, COL_RV // RET_V_WIDTH)),
                  pl.BlockSpec((n_seq, ch, RET_V_WIDTH), lambda c: (0, c, COL_RG // RET_V_WIDTH)),
                  pl.BlockSpec((ch, RET_KDIM), lambda c: (c, 0)),
                  pl.BlockSpec((ch, RET_KDIM), lambda c: (c, 0)),
                  const3, const3, const3,
                  pl.BlockSpec(memory_space=pltpu.SMEM)],
        out_specs=(pl.BlockSpec((n_seq, ch, RET_V_WIDTH), lambda c: (0, c, 0)),
                   pl.BlockSpec(state_shape, lambda c: (0, 0, 0, 0))),
        scratch_shapes=[pltpu.VMEM(state_shape, F32)],
        compiler_params=_params(1),
        name="ret_prompt",
    )(proj3, proj3, proj3, proj3, cosf, sinf, dmask, qdec, kdec, cdec)
    return o.reshape(m, RET_V_WIDTH), state


def _ret_sample_kernel(qr_ref, qt_ref, kt_ref, cosr_ref, sinr_ref, cos_ref, sin_ref, v_ref, gate_ref, s_ref,
                       qdec_ref, kdec_ref, cdec_ref, o_ref, so_ref):
    bt = s_ref.shape[0]
    half = RET_KDIM // 2
    cos = cos_ref[...]
    sin = sin_ref[...]
    cosr = cosr_ref[...]
    sinr = sinr_ref[...]

    def rotary(x):
        x1, x2 = x[:half, :], x[half:, :]
        return jnp.concatenate([x1 * cos - x2 * sin, x2 * cos + x1 * sin], axis=0)

    for h in range(N_RET_HEADS):
        vd = slice(h * RET_VDIM, (h + 1) * RET_VDIM)
        q_all = rotary(qt_ref[0, h])
        k_all = rotary(kt_ref[0, h]) * RET_KDIM ** -0.5
        qk_all = jnp.sum(q_all * k_all, axis=0, keepdims=True)
        kd_all = k_all * kdec_ref[h]
        q_rows = qr_ref[0, :, h * RET_KDIM:(h + 1) * RET_KDIM]
        q_rows = q_rows * cosr + pltpu.roll(q_rows, half, axis=1) * sinr
        qd_rows = (q_rows * qdec_ref[h]).astype(BF16)
        for b in range(bt):
            v = v_ref[0, b:b + 1, vd]
            s0 = s_ref[b, h]
            o_intra = qk_all[:, b:b + 1] * v
            o_inter = jnp.dot(qd_rows, s0.astype(BF16), preferred_element_type=F32)[b:b + 1, :]
            so_ref[b, h] = cdec_ref[h] * s0 + kd_all[:, b:b + 1] * v
            ro = o_intra + o_inter
            ro = ro * lax.rsqrt(jnp.mean(ro * ro, axis=-1, keepdims=True) + EPS)
            o_ref[0, b:b + 1, vd] = (jax.nn.silu(gate_ref[0, b:b + 1, vd]) * ro).astype(o_ref.dtype)


def _ret_sample(qr, qt, kt, cos_row, sin_row, cos_col, sin_col, v, gate, state, qdec, kdec, cdec):
    nbt, _, _, bt = qt.shape
    smem = pl.BlockSpec(memory_space=pltpu.SMEM)
    st_spec = pl.BlockSpec((bt, N_RET_HEADS, RET_KDIM, RET_VDIM), lambda i: (i, 0, 0, 0))
    qk_spec = pl.BlockSpec((1, N_RET_HEADS, RET_KDIM, bt), lambda i: (i, 0, 0, 0))
    row_spec = pl.BlockSpec((1, bt, RET_V_WIDTH), lambda i: (i, 0, 0))
    col_spec = pl.BlockSpec(cos_col.shape, lambda i: (0, 0))
    trig_row_spec = pl.BlockSpec(cos_row.shape, lambda i: (0, 0))
    q_row_spec = pl.BlockSpec((1, bt, RET_QK_WIDTH), lambda i: (i, 0, 0))
    return pl.pallas_call(
        _ret_sample_kernel,
        out_shape=(jax.ShapeDtypeStruct((nbt, bt, RET_V_WIDTH), BF16),
                   jax.ShapeDtypeStruct(state.shape, F32)),
        grid=(nbt,),
        in_specs=[q_row_spec, qk_spec, qk_spec, trig_row_spec, trig_row_spec, col_spec, col_spec,
                  row_spec, row_spec, st_spec, smem, smem, smem],
        out_specs=(row_spec, st_spec),
        compiler_params=_params(1),
        name="ret_sample",
    )(qr, qt, kt, cos_row, sin_row, cos_col, sin_col, v, gate, state, qdec, kdec, cdec)


NPF = np.float32


def _t5_bucket(dist):
    n = np.maximum(dist, 0)
    max_exact = N_BUCKETS // 2
    nf = np.maximum(n, 1).astype(NPF)
    large = max_exact + (np.log(nf / NPF(max_exact)) / NPF(math.log(MAX_DISTANCE / max_exact))
                         * NPF(N_BUCKETS - max_exact)).astype(np.int32)
    return np.where(n < max_exact, n, np.minimum(large, N_BUCKETS - 1)).astype(np.int32)


def _rope_tables(pos):
    half = RET_KDIM // 2
    inv = NPF(ROPE_BASE) ** (-np.arange(half, dtype=NPF) / NPF(half))
    ang = pos.astype(NPF)[:, None] * inv[None]
    return np.cos(ang), np.sin(ang)


def _layer(xp, xs, pp, ps, cache_k, cache_v, state_ret, state_conv, rel_bias, lp, n_seq):
    (g_mix, w_in, g_q, g_k, sinks, w_out, g_ffn, w_up, conv_w, conv_b, w_down,
     g_ple, w_ple_gate, w_ple_proj) = lp
    mp = xp.shape[0]
    ms = xs.shape[0]
    seq = mp // n_seq
    win = cache_k.shape[1]

    row = lambda a: a.reshape(1, -1)

    log_decay = np.log(NPF(1.0) - NPF(2.0) ** (NPF(-5.0) - np.arange(N_RET_HEADS, dtype=NPF)))
    idx = np.arange(RET_CHUNK, dtype=NPF)
    diff = idx[:, None] - idx[None, :]
    dmask = np.where(diff[None] >= 0, np.exp(diff[None] * log_decay[:, None, None]), NPF(0.0))
    q_dec = np.exp((idx + NPF(1.0))[:, None] * log_decay[None])
    k_dec = np.exp((NPF(RET_CHUNK - 1.0) - idx)[:, None] * log_decay[None])
    qdec_t = np.broadcast_to(q_dec.T[:, :, None], (N_RET_HEADS, RET_CHUNK, RET_KDIM))
    kdec_t = np.broadcast_to(k_dec.T[:, :, None], (N_RET_HEADS, RET_CHUNK, RET_KDIM))
    cdec = np.exp(NPF(RET_CHUNK) * log_decay)
    cos_p, sin_p = _rope_tables(np.arange(seq, dtype=np.int32))
    cosf = np.concatenate([cos_p, cos_p], axis=1)
    sinf = np.concatenate([-sin_p, sin_p], axis=1)
    one = np.arange(1, dtype=NPF)
    qdec_s = np.exp((one + NPF(1.0))[:, None] * log_decay[None])[0]
    kdec_s = np.exp((NPF(1.0 - 1.0) - one)[:, None] * log_decay[None])[0]
    cdec_s = np.exp(NPF(1.0) * log_decay)
    cos_s, sin_s = _rope_tables(PAST_LEN + np.arange(1, dtype=np.int32))
    cos_col, sin_col = cos_s.reshape(-1, 1), sin_s.reshape(-1, 1)

    qi = np.arange(ATTN_BLOCK, dtype=np.int32)
    ki = np.arange(2 * ATTN_BLOCK, dtype=np.int32) - ATTN_BLOCK
    dist = qi[:, None] - ki[None, :]
    bucket_tile = np.where((dist >= 0) & (dist <= WINDOW), _t5_bucket(dist), -1).astype(np.int32)
    dist_s = win - np.arange(2 * ATTN_BLOCK, dtype=np.int32)
    bucket_row = np.where((dist_s >= 0) & (dist_s <= WINDOW), _t5_bucket(dist_s), -1).astype(np.int32).reshape(1, -1)

    tmp, tms = _row_tile(mp), _row_tile(ms)
    proj_s, w_in_b = _matmul(_norm(xs, row(g_mix), tms), w_in, tms, COL_TILE, IN_SRC_BLOCKS)
    proj_p, = _matmul(_norm(xp, row(g_mix), NORM_TILE), w_in_b, min(mp, IN_ROW_TILE), COL_TILE)

    attn_p, kn_p = _attn_prompt(proj_p, bucket_tile, rel_bias, sinks, row(g_q), row(g_k), n_seq)
    ret_p, ret_state_p = _ret_prompt(proj_p, cosf, sinf, dmask, qdec_t, kdec_t, cdec, n_seq)

    bt_a = 16
    attn_s, ck_new, cv_new = _attn_sample(
        proj_s[:, COL_AQ:COL_AQ + ATTN_WIDTH].reshape(ms, N_ATTN_HEADS, HEAD_DIM),
        proj_s[:, COL_AK:COL_AK + KV_WIDTH].reshape(ms, 1, KV_WIDTH),
        proj_s[:, COL_AV:COL_AV + KV_WIDTH].reshape(ms, 1, KV_WIDTH),
        cache_k.reshape(ms, win, KV_WIDTH), cache_v.reshape(ms, win, KV_WIDTH),
        bucket_row, rel_bias.T, sinks.reshape(-1, 1), row(g_q), jnp.tile(g_k, N_KV_HEADS).reshape(1, -1), bt_a)
    attn_s = attn_s.reshape(ms, ATTN_WIDTH)

    bt_r = 8
    to_cols = lambda a: a.reshape(ms // bt_r, bt_r, N_RET_HEADS, RET_KDIM).transpose(0, 2, 3, 1)
    rq_s = proj_s[:, COL_RQ:COL_RQ + RET_QK_WIDTH]
    ret_s, ret_state_s = _ret_sample(
        rq_s.reshape(ms // bt_r, bt_r, RET_QK_WIDTH), to_cols(rq_s), to_cols(proj_s[:, COL_RK:COL_RK + RET_QK_WIDTH]),
        np.concatenate([cos_s, cos_s], axis=1), np.concatenate([-sin_s, sin_s], axis=1), cos_col, sin_col,
        proj_s[:, COL_RV:COL_RV + RET_V_WIDTH].reshape(ms // bt_r, bt_r, RET_V_WIDTH),
        proj_s[:, COL_RG:COL_RG + RET_V_WIDTH].reshape(ms // bt_r, bt_r, RET_V_WIDTH),
        state_ret, qdec_s, kdec_s, cdec_s)
    ret_s = ret_s.reshape(ms, RET_V_WIDTH)

    xs, w_out_b = _out_proj(attn_s, ret_s, w_out, xs, tms, COL_TILE)
    xp, = _out_proj(attn_p, ret_p, w_out_b, xp, tmp, OUT_COL_TILE)

    act_s, conv_s, w_up_b = _convglu_sample(_norm(xs, row(g_ffn), tms), w_up, conv_w, row(conv_b), state_conv)
    act_p, conv_p = _convglu_prompt(_norm(xp, row(g_ffn), NORM_TILE), w_up_b, conv_w, row(conv_b), n_seq, UP_ROW_TILE)
    for part in range(DOWN_K_PARTS):
        last = part == DOWN_K_PARTS - 1
        xs, w_down_b = _down_proj(act_s, w_down, xs, tms, COL_TILE, DOWN_K_PARTS, part)
        xp, *stats_p = _down_proj(act_p, w_down_b, xp, tmp, COL_TILE, DOWN_K_PARTS, part,
                                  row(g_ple) if last else None)
    hp, ssq_p = stats_p

    xs, w_pgate_b, w_pproj_b = _ple(xs, _norm(xs, row(g_ple), tms), None, w_ple_gate, ps, w_ple_proj, tms, COL_TILE)
    xp, = _ple(xp, hp, ssq_p, w_pgate_b, pp, w_pproj_b, tmp, COL_TILE)

    last = lambda a: a.reshape(n_seq, seq, a.shape[1])[:, seq - WINDOW:, :]
    kp_new = last(kn_p).reshape(n_seq, WINDOW, N_KV_HEADS, HEAD_DIM)
    vp_new = last(proj_p)[:, :, COL_AV:COL_AV + KV_WIDTH].reshape(n_seq, WINDOW, N_KV_HEADS, HEAD_DIM)
    ks_new = ck_new.reshape(ms, win, N_KV_HEADS, HEAD_DIM)
    vs_new = cv_new.reshape(ms, win, N_KV_HEADS, HEAD_DIM)
    return xp, xs, kp_new, vp_new, ret_state_p, conv_p, ks_new, vs_new, ret_state_s, conv_s


def kernel(x_prompt, x_sample, p_prompt, p_sample, cache_win_k, cache_win_v, state_ret, state_conv, rel_bias, g_mix, w_in, g_q, g_k, sinks, w_out, g_ffn, w_up, conv_w, conv_b, w_down, g_ple, w_ple_gate, w_ple_proj):
    depth = g_mix.shape[0]
    n_seq, seq, d = x_prompt.shape
    nb, dec_seq, _ = x_sample.shape
    assert dec_seq == 1 and seq % 512 == 0 and d == D_MODEL
    xp = x_prompt.reshape(n_seq * seq, d)
    xs = x_sample.reshape(nb, d)
    outs = [[] for _ in range(8)]
    for l in range(depth):
        lp = (g_mix[l], w_in[l], g_q[l], g_k[l], sinks[l], w_out[l], g_ffn[l], w_up[l], conv_w[l],
              conv_b[l], w_down[l], g_ple[l], w_ple_gate[l], w_ple_proj[l])
        res = _layer(xp, xs, p_prompt[l].reshape(n_seq * seq, -1), p_sample[l].reshape(nb, -1),
                     cache_win_k[l], cache_win_v[l], state_ret[l], state_conv[l], rel_bias, lp, n_seq)
        xp, xs = res[0], res[1]
        for o, r in zip(outs, res[2:]):
            o.append(r)
    stacked = [jnp.stack(o) for o in outs]
    return (xp.reshape(n_seq, seq, d), xs.reshape(nb, 1, d), *stacked)
```

```python
import functools
import math

import jax
import jax.numpy as jnp
import numpy as np
from jax import lax
from jax.experimental import pallas as pl
from jax.experimental.pallas import tpu as pltpu

F32 = jnp.float32
BF16 = jnp.bfloat16

D_MODEL = 4096
HEAD_DIM = 64
N_ATTN_HEADS = 32
N_KV_HEADS = 4
GQA_GROUP = N_ATTN_HEADS // N_KV_HEADS
WINDOW = 128
ATTN_BLOCK = 128
N_BUCKETS = 32
MAX_DISTANCE = 128
N_RET_HEADS = 8
RET_KDIM = 128
RET_VDIM = 256
RET_CHUNK = 128
ROPE_BASE = 10000.0
D_FF = 11008
CONV_W = 3
EPS = 1e-6

ATTN_WIDTH = N_ATTN_HEADS * HEAD_DIM
KV_WIDTH = N_KV_HEADS * HEAD_DIM
RET_QK_WIDTH = N_RET_HEADS * RET_KDIM
RET_V_WIDTH = N_RET_HEADS * RET_VDIM
IN_SIZES = (ATTN_WIDTH, KV_WIDTH, KV_WIDTH, RET_QK_WIDTH, RET_QK_WIDTH, RET_V_WIDTH, RET_V_WIDTH)
IN_WIDTH = sum(IN_SIZES)
IN_SPLITS = tuple(sum(IN_SIZES[:n + 1]) for n in range(len(IN_SIZES) - 1))
PAST_LEN = 8192

COL_AQ = 0
COL_RV = ATTN_WIDTH
COL_RG = COL_RV + RET_V_WIDTH
COL_RQ = COL_RG + RET_V_WIDTH
COL_RK = COL_RQ + RET_QK_WIDTH
COL_AK = COL_RK + RET_QK_WIDTH
COL_AV = COL_AK + KV_WIDTH

VMEM_LIMIT_BYTES = 56 * 1024 * 1024
FF_TILE = 256
N_FF_TILES = D_FF // FF_TILE
NORM_ROWS = 128
NORM_TILE = 1024
OUT_COL_TILE = 1024
F32_SUBLANES = 8
STAT_LANES = 128
IN_ROW_TILE = 2048
ROW_TILE = 1024
COL_TILE = 512
UP_ROW_TILE = 2048
CONV_CHUNK = 1024
DOWN_K_PARTS = 2


def _row_tile(m):
    return min(m, ROW_TILE)


def _in_src_blocks():
    start = dict(zip(("aq", "ak", "av", "rq", "rk", "rv", "rg"), (0,) + IN_SPLITS))
    order = (("aq", ATTN_WIDTH), ("rv", RET_V_WIDTH), ("rg", RET_V_WIDTH), ("rq", RET_QK_WIDTH),
             ("rk", RET_QK_WIDTH), ("ak", KV_WIDTH), ("av", KV_WIDTH))
    cols = [c for name, width in order for c in range(start[name], start[name] + width, KV_WIDTH)]
    per_tile = COL_TILE // KV_WIDTH
    blocks = []
    for b in range(0, len(cols), per_tile):
        group = cols[b:b + per_tile]
        assert group[0] % COL_TILE == 0 and all(c == group[0] + u * KV_WIDTH for u, c in enumerate(group))
        blocks.append(group[0] // COL_TILE)
    return tuple(blocks)


IN_SRC_BLOCKS = _in_src_blocks()


def _params(n_axes):
    return pltpu.CompilerParams(dimension_semantics=("arbitrary",) * n_axes,
                                vmem_limit_bytes=VMEM_LIMIT_BYTES)


def _rms(x, g):
    y = x * lax.rsqrt(jnp.mean(x * x, axis=-1, keepdims=True) + EPS)
    return y * g


def _norm_rows_to_bf16(x_ref, g_ref, h_ref):
    def body(c, carry):
        r0 = pl.multiple_of(c * NORM_ROWS, NORM_ROWS)
        h_ref[pl.ds(r0, NORM_ROWS), :] = _rms(x_ref[pl.ds(r0, NORM_ROWS), :], g_ref[...]).astype(h_ref.dtype)
        return carry
    lax.fori_loop(0, x_ref.shape[0] // NORM_ROWS, body, 0)


def _norm(x, g, tr):
    m, k = x.shape
    return pl.pallas_call(
        _norm_rows_to_bf16,
        out_shape=jax.ShapeDtypeStruct((m, k), BF16),
        grid=(m // tr,),
        in_specs=[pl.BlockSpec((tr, k), lambda i: (i, 0)),
                  pl.BlockSpec((1, k), lambda i: (0, 0))],
        out_specs=pl.BlockSpec((tr, k), lambda i: (i, 0)),
        compiler_params=_params(1),
        name="rmsnorm",
    )(x, g)


def _mxu_weights(w_ref, wb_ref):
    if not wb_ref:
        return w_ref[...]
    wb = w_ref[...].astype(BF16)
    wb_ref[0][...] = wb
    return wb


def _mm_kernel(h_ref, w_ref, o_ref, *wb_ref):
    o_ref[...] = jnp.dot(h_ref[...], _mxu_weights(w_ref, wb_ref), preferred_element_type=F32)


def _matmul(h, w, tm, tn, src_blocks=None):
    m, k = h.shape
    n = w.shape[1]
    emit = w.dtype != BF16
    if emit:
        w_map = lambda i, j: (0, sum(jnp.where(j == d, s, 0) for d, s in enumerate(src_blocks)))
    else:
        w_map = lambda i, j: (0, j)
    out_shape = [jax.ShapeDtypeStruct((m, n), F32)]
    out_specs = [pl.BlockSpec((tm, tn), lambda i, j: (i, j))]
    if emit:
        assert m == tm
        out_shape.append(jax.ShapeDtypeStruct((k, n), BF16))
        out_specs.append(pl.BlockSpec((k, tn), lambda i, j: (0, j)))
    return pl.pallas_call(
        _mm_kernel,
        out_shape=out_shape,
        grid=(m // tm, n // tn),
        in_specs=[pl.BlockSpec((tm, k), lambda i, j: (i, 0)),
                  pl.BlockSpec((k, tn), w_map)],
        out_specs=out_specs,
        compiler_params=_params(2),
        name="in_proj",
    )(h, w)


def _row_scale(ssq_ref, width, n_cols):
    r = lax.rsqrt(ssq_ref[...] / width + EPS)
    return jnp.concatenate([r] * (n_cols // STAT_LANES), axis=1)


def _residual_mm_kernel(*refs, n_lhs, emit_weights, norm_stats):
    lhs_refs = refs[:n_lhs]
    w_ref, x_ref = refs[n_lhs], refs[n_lhs + 1]
    rest = list(refs[n_lhs + 2:])
    g_ref = rest.pop(0) if norm_stats else None
    o_ref = rest.pop(0)
    wb_ref = (rest.pop(0),) if emit_weights else ()
    lhs = jnp.concatenate([r[...] for r in lhs_refs], axis=1) if n_lhs > 1 else lhs_refs[0][...]
    o = x_ref[...] + jnp.dot(lhs, _mxu_weights(w_ref, wb_ref), preferred_element_type=F32)
    o_ref[...] = o
    if norm_stats:
        hb_ref, ssq_ref = rest
        hb_ref[...] = (o * g_ref[...]).astype(hb_ref.dtype)
        part = jnp.broadcast_to(jnp.sum(o * o, axis=-1, keepdims=True), ssq_ref.shape)
        j = pl.program_id(1)

        @pl.when(j == 0)
        def _():
            ssq_ref[...] = part

        @pl.when(j > 0)
        def _():
            ssq_ref[...] += part


def _residual_matmul(lhs, lhs_specs, w, w_spec, wb_shape, x, g_next, tm, tn, name):
    m, n = x.shape
    emit = w.dtype != BF16
    stats = g_next is not None
    tile = pl.BlockSpec((tm, tn), lambda i, j: (i, j))
    operands = [*lhs, w, x]
    in_specs = [*lhs_specs, w_spec, tile]
    out_shape = [jax.ShapeDtypeStruct((m, n), F32)]
    out_specs = [tile]
    if stats:
        operands.append(g_next)
        in_specs.append(pl.BlockSpec((1, tn), lambda i, j: (0, j)))
    if emit:
        assert m == tm
        out_shape.append(jax.ShapeDtypeStruct(wb_shape, BF16))
        out_specs.append(pl.BlockSpec((wb_shape[0], tn), lambda i, j: (0, j)))
    if stats:
        out_shape += [jax.ShapeDtypeStruct((m, n), BF16), jax.ShapeDtypeStruct((m, STAT_LANES), F32)]
        out_specs += [tile, pl.BlockSpec((tm, STAT_LANES), lambda i, j: (i, 0))]
    return pl.pallas_call(
        functools.partial(_residual_mm_kernel, n_lhs=len(lhs), emit_weights=emit, norm_stats=stats),
        out_shape=out_shape,
        grid=(m // tm, n // tn),
        in_specs=in_specs,
        out_specs=out_specs,
        compiler_params=_params(2),
        name=name,
    )(*operands)


def _out_proj(a, r, w, x, tm, tn, g_next=None):
    ka, kr = a.shape[1], r.shape[1]
    lhs_specs = [pl.BlockSpec((tm, ka), lambda i, j: (i, 0)), pl.BlockSpec((tm, kr), lambda i, j: (i, 0))]
    w_spec = pl.BlockSpec((ka + kr, tn), lambda i, j: (0, j))
    return _residual_matmul([a, r], lhs_specs, w, w_spec, w.shape, x, g_next, tm, tn, "out_proj")


def _down_proj(a, w, x, tm, tn, k_parts, part, g_next=None):
    k = a.shape[1] // k_parts
    w_part = part if w.dtype != BF16 else 0
    lhs_specs = [pl.BlockSpec((tm, k), lambda i, j: (i, part))]
    w_spec = pl.BlockSpec((k, tn), lambda i, j: (w_part, j))
    return _residual_matmul([a], lhs_specs, w, w_spec, (k, w.shape[1]), x, g_next, tm, tn, "down_proj")


def _ple_kernel(*refs, emit_weights, row_scaled):
    h_ref, wg_ref, p_ref, wp_ref, x_ref = refs[:5]
    rest = list(refs[5:])
    ssq_ref = rest.pop(0) if row_scaled else None
    o_ref = rest.pop(0)
    wb_refs = tuple(rest) if emit_weights else ()
    z = jnp.dot(h_ref[...], _mxu_weights(wg_ref, wb_refs[:1]), preferred_element_type=F32)
    if row_scaled:
        z = z * _row_scale(ssq_ref, h_ref.shape[1], z.shape[1])
    pp = jnp.dot(p_ref[...].astype(BF16), _mxu_weights(wp_ref, wb_refs[1:]), preferred_element_type=F32)
    o_ref[...] = x_ref[...] + jax.nn.sigmoid(z) * pp


def _ple(x, h, ssq, wg, p, wp, tm, tn):
    m, k = h.shape
    n = wg.shape[1]
    kp = p.shape[1]
    emit = wg.dtype != BF16
    scaled = ssq is not None
    operands = [h, wg, p, wp, x]
    in_specs = [pl.BlockSpec((tm, k), lambda i, j: (i, 0)),
                pl.BlockSpec((k, tn), lambda i, j: (0, j)),
                pl.BlockSpec((tm, kp), lambda i, j: (i, 0)),
                pl.BlockSpec((kp, tn), lambda i, j: (0, j)),
                pl.BlockSpec((tm, tn), lambda i, j: (i, j))]
    if scaled:
        operands.append(ssq)
        in_specs.append(pl.BlockSpec((tm, STAT_LANES), lambda i, j: (i, 0)))
    out_shape = [jax.ShapeDtypeStruct((m, n), F32)]
    out_specs = [pl.BlockSpec((tm, tn), lambda i, j: (i, j))]
    if emit:
        assert m == tm
        out_shape += [jax.ShapeDtypeStruct(wg.shape, BF16), jax.ShapeDtypeStruct(wp.shape, BF16)]
        out_specs += [pl.BlockSpec((k, tn), lambda i, j: (0, j)), pl.BlockSpec((kp, tn), lambda i, j: (0, j))]
    return pl.pallas_call(
        functools.partial(_ple_kernel, emit_weights=emit, row_scaled=scaled),
        out_shape=out_shape,
        grid=(m // tm, n // tn),
        in_specs=in_specs,
        out_specs=out_specs,
        compiler_params=_params(2),
        name="ple",
    )(*operands)


def _gelu_erf(x):
    return 0.5 * x * (1.0 + lax.erf(x * math.sqrt(0.5)))


def _conv_taps(cb_ref, cw_ref, um2, um1, u):
    c = cb_ref[...] + cw_ref[0:1, :] * um2
    c = c + cw_ref[1:2, :] * um1
    return c + cw_ref[2:3, :] * u


def _convglu_prompt_kernel(h_ref, w_ref, cwg_ref, cwv_ref, cbg_ref, cbv_ref,
                           a_ref, sg_ref, sv_ref, carry_g_ref, carry_v_ref, *, tiles_per_seq):
    i = pl.program_id(0)
    j = pl.program_id(1)
    tm = h_ref.shape[0]
    rows = min(tm, CONV_CHUNK)
    seq_start = (i % tiles_per_seq) == 0
    sub = F32_SUBLANES
    row = lax.broadcasted_iota(jnp.int32, (sub, 1), 0)

    def half(u, cw_ref, cb_ref, prev):
        r1 = pltpu.roll(u, 1, axis=0)
        r2 = pltpu.roll(u, 2, axis=0)
        head1 = jnp.where(row == 0, prev[1:2, :], r1[:sub, :])
        head2 = jnp.where(row == 0, prev[0:1, :], jnp.where(row == 1, prev[1:2, :], r2[:sub, :]))
        um1 = jnp.concatenate([head1, r1[sub:, :]], axis=0)
        um2 = jnp.concatenate([head2, r2[sub:, :]], axis=0)
        return _conv_taps(cb_ref, cw_ref, um2, um1, u), u[rows - (CONV_W - 1):, :]

    prev_g = jnp.where(seq_start, 0.0, carry_g_ref[j])
    prev_v = jnp.where(seq_start, 0.0, carry_v_ref[j])
    for c in range(tm // rows):
        rs = slice(c * rows, (c + 1) * rows)
        u = jnp.dot(h_ref[rs, :], w_ref[...], preferred_element_type=F32)
        cg, prev_g = half(u[:, :FF_TILE], cwg_ref, cbg_ref, prev_g)
        cv, prev_v = half(u[:, FF_TILE:], cwv_ref, cbv_ref, prev_v)
        a_ref[rs, :] = (_gelu_erf(cg) * cv).astype(a_ref.dtype)
    carry_g_ref[j] = prev_g
    carry_v_ref[j] = prev_v
    sg_ref[0] = prev_g
    sv_ref[0] = prev_v


def _convglu_prompt(h, w_pairs, conv_w, conv_b, n_seq, tm):
    m, k = h.shape
    seq = m // n_seq
    tiles_per_seq = seq // tm
    nt = N_FF_TILES
    tn = FF_TILE
    a, sg, sv = pl.pallas_call(
        functools.partial(_convglu_prompt_kernel, tiles_per_seq=tiles_per_seq),
        out_shape=(jax.ShapeDtypeStruct((m, D_FF), BF16),
                   jax.ShapeDtypeStruct((m // tm, CONV_W - 1, D_FF), F32),
                   jax.ShapeDtypeStruct((m // tm, CONV_W - 1, D_FF), F32)),
        grid=(m // tm, nt),
        in_specs=[pl.BlockSpec((tm, k), lambda i, j: (i, 0)),
                  pl.BlockSpec((k, 2 * tn), lambda i, j: (0, j)),
                  pl.BlockSpec((CONV_W, tn), lambda i, j: (0, j)),
                  pl.BlockSpec((CONV_W, tn), lambda i, j: (0, j + nt)),
                  pl.BlockSpec((1, tn), lambda i, j: (0, j)),
                  pl.BlockSpec((1, tn), lambda i, j: (0, j + nt))],
        out_specs=(pl.BlockSpec((tm, tn), lambda i, j: (i, j)),
                   pl.BlockSpec((1, CONV_W - 1, tn), lambda i, j: (i, 0, j)),
                   pl.BlockSpec((1, CONV_W - 1, tn), lambda i, j: (i, 0, j))),
        scratch_shapes=[pltpu.VMEM((nt, CONV_W - 1, tn), F32),
                        pltpu.VMEM((nt, CONV_W - 1, tn), F32)],
        compiler_params=_params(2),
        name="convglu_prompt",
    )(h, w_pairs, conv_w, conv_w, conv_b, conv_b)
    tails = jnp.concatenate([sg, sv], axis=-1)
    return a, tails[tiles_per_seq - 1::tiles_per_seq]


def _convglu_sample_kernel(h_ref, wg_ref, wv_ref, cwg_ref, cwv_ref, cbg_ref, cbv_ref,
                           pg_ref, pv_ref, a_ref, ng_ref, nv_ref, wb_ref):
    def half(w_ref, wb_cols, cw_ref, cb_ref, p_ref, n_ref):
        u = jnp.dot(h_ref[...], _mxu_weights(w_ref, (wb_ref.at[:, wb_cols],)), preferred_element_type=F32)
        p0, p1 = p_ref[:, 0, :], p_ref[:, 1, :]
        n_ref[:, 0, :] = p1
        n_ref[:, 1, :] = u
        return _conv_taps(cb_ref, cw_ref, p0, p1, u)

    cg = half(wg_ref, slice(0, FF_TILE), cwg_ref, cbg_ref, pg_ref, ng_ref)
    cv = half(wv_ref, slice(FF_TILE, 2 * FF_TILE), cwv_ref, cbv_ref, pv_ref, nv_ref)
    a_ref[...] = (_gelu_erf(cg) * cv).astype(a_ref.dtype)


def _convglu_sample(h, w_up, conv_w, conv_b, state_conv):
    m, k = h.shape
    nt = N_FF_TILES
    tn = FF_TILE
    col = lambda off: (lambda j: (0, j + off))
    rows3 = lambda off: pl.BlockSpec((m, CONV_W - 1, tn), lambda j: (0, 0, j + off))
    a, new_g, new_v, w_pairs = pl.pallas_call(
        _convglu_sample_kernel,
        out_shape=(jax.ShapeDtypeStruct((m, D_FF), BF16),
                   jax.ShapeDtypeStruct((m, CONV_W - 1, D_FF), F32),
                   jax.ShapeDtypeStruct((m, CONV_W - 1, D_FF), F32),
                   jax.ShapeDtypeStruct((k, 2 * D_FF), BF16)),
        grid=(nt,),
        in_specs=[pl.BlockSpec((m, k), lambda j: (0, 0)),
                  pl.BlockSpec((k, tn), col(0)),
                  pl.BlockSpec((k, tn), col(nt)),
                  pl.BlockSpec((CONV_W, tn), col(0)),
                  pl.BlockSpec((CONV_W, tn), col(nt)),
                  pl.BlockSpec((1, tn), col(0)),
                  pl.BlockSpec((1, tn), col(nt)),
                  rows3(0), rows3(nt)],
        out_specs=(pl.BlockSpec((m, tn), col(0)),
                   rows3(0), rows3(0),
                   pl.BlockSpec((k, 2 * tn), col(0))),
        compiler_params=_params(1),
        name="convglu_sample",
    )(h, w_up, w_up, conv_w, conv_w, conv_b, conv_b, state_conv, state_conv)
    return a, jnp.concatenate([new_g, new_v], axis=-1), w_pairs


def _segment_mean_square(x, seg_ones):
    sq = x * x
    hi = sq.astype(BF16)
    lo = (sq - hi.astype(F32)).astype(BF16)
    total = (jnp.dot(hi, seg_ones, preferred_element_type=F32)
             + jnp.dot(lo, seg_ones, preferred_element_type=F32))
    return total / HEAD_DIM


def _attn_prompt_kernel(q_ref, kc_ref, vc_ref, bucket_ref, seg_ones_ref, rb_ref, sinks_ref, gq_ref, gk_ref,
                        o_ref, kn_ref, bias_ref, kdup_ref, vdup_ref, *, blocks_per_seq):
    r = pl.program_id(0)
    blk = ATTN_BLOCK
    pair = 2 * HEAD_DIM
    n_bias_rows = N_ATTN_HEADS * blk
    seq_start = (r % blocks_per_seq) == 0

    @pl.when(r == 0)
    def _():
        bucket = bucket_ref[...]
        prev_cols = lax.broadcasted_iota(jnp.int32, bucket.shape, 1) < blk

        def per_head(h, carry):
            acc = jnp.full(bucket.shape, -jnp.inf, F32)
            for b in range(N_BUCKETS):
                acc = jnp.where(bucket == b, rb_ref[b, h], acc)
            row0 = pl.multiple_of(h * blk, blk)
            bias_ref[pl.ds(row0, blk), :] = acc
            bias_ref[pl.ds(n_bias_rows + row0, blk), :] = jnp.where(prev_cols, -jnp.inf, acc)
            return carry
        lax.fori_loop(0, N_ATTN_HEADS, per_head, 0)
        vdup_ref[:, :, pair:] = jnp.ones((N_KV_HEADS, 2 * blk, pair), BF16)

    @pl.when(seq_start)
    def _():
        kdup_ref[:, :blk, :] = jnp.zeros((N_KV_HEADS, blk, pair), BF16)
        vdup_ref[:, :blk, :pair] = jnp.zeros((N_KV_HEADS, blk, pair), BF16)

    @pl.when(jnp.logical_not(seq_start))
    def _():
        kdup_ref[:, :blk, :] = kdup_ref[:, blk:, :]
        vdup_ref[:, :blk, :pair] = vdup_ref[:, blk:, :pair]

    seg_ones = seg_ones_ref[...]
    low_half = lax.broadcasted_iota(jnp.int32, (blk, pair), 1) < HEAD_DIM

    def duplicate_half(x, odd):
        swapped = pltpu.roll(x, HEAD_DIM, axis=1)
        return jnp.where(low_half, swapped, x) if odd else jnp.where(low_half, x, swapped)

    kc = kc_ref[...]
    kn = kc * lax.rsqrt(_segment_mean_square(kc, seg_ones) + EPS) * gk_ref[...]
    kn_ref[...] = kn
    for j in range(N_KV_HEADS):
        col = slice((j // 2) * pair, (j // 2 + 1) * pair)
        kdup_ref[j, blk:, :] = duplicate_half(kn[:, col], j % 2).astype(BF16)
        vdup_ref[j, blk:, :pair] = duplicate_half(vc_ref[:, col], j % 2).astype(BF16)

    bias_base = jnp.where(seq_start, n_bias_rows, 0)
    chunk = 2 * pair
    n_heads = N_ATTN_HEADS
    q_gain = gq_ref[...] * HEAD_DIM ** -0.5
    q_chunks = [q_ref[:, c * chunk:(c + 1) * chunk] for c in range(ATTN_WIDTH // chunk)]
    q_ms = [_segment_mean_square(qc, seg_ones) for qc in q_chunks]
    q_norm = [qc * lax.rsqrt(ms + EPS) * q_gain for qc, ms in zip(q_chunks, q_ms)]
    lhs = []
    for qn in q_norm:
        for p in range(chunk // pair):
            qp = qn[:, p * pair:(p + 1) * pair]
            lhs.append(jnp.where(low_half, qp, 0.0).astype(BF16))
            lhs.append(jnp.where(low_half, 0.0, qp).astype(BF16))
    scores = [lax.dot_general(jnp.concatenate(lhs[j * GQA_GROUP:(j + 1) * GQA_GROUP], axis=0), kdup_ref[j],
                              (((1,), (1,)), ((), ())), preferred_element_type=F32)
              for j in range(N_KV_HEADS)]
    sg = [scores[h // GQA_GROUP][(h % GQA_GROUP) * blk:(h % GQA_GROUP + 1) * blk, :]
          + bias_ref[pl.ds(pl.multiple_of(bias_base + h * blk, blk), blk), :] for h in range(n_heads)]
    m = [jnp.maximum(jnp.max(sg[h], axis=-1, keepdims=True), sinks_ref[h]) for h in range(n_heads)]
    exps = [jnp.exp(sg[h] - m[h]).astype(BF16) for h in range(n_heads)]
    sink_terms = [jnp.exp(sinks_ref[h] - m[h]) for h in range(n_heads)]
    outs = [jnp.dot(jnp.concatenate(exps[j * GQA_GROUP:(j + 1) * GQA_GROUP], axis=0), vdup_ref[j],
                    preferred_element_type=F32) for j in range(N_KV_HEADS)]
    for h in range(0, n_heads, 2):
        o = outs[h // GQA_GROUP]
        g = h % GQA_GROUP
        even = o[g * blk:(g + 1) * blk, :]
        odd = o[(g + 1) * blk:(g + 2) * blk, :]
        num = jnp.where(low_half, even[:, :pair], odd[:, :pair])
        den = (jnp.where(low_half, even[:, pair:], odd[:, pair:])
               + jnp.where(low_half, sink_terms[h], sink_terms[h + 1]))
        o_ref[:, h * HEAD_DIM:h * HEAD_DIM + pair] = (num / den).astype(o_ref.dtype)


def _attn_prompt(proj, bucket_tile, rel_bias, sinks, g_q, g_k, n_seq):
    m = proj.shape[0]
    blk = ATTN_BLOCK
    nb = m // blk
    blocks_per_seq = nb // n_seq
    lanes = 4 * HEAD_DIM
    seg = np.arange(lanes) // HEAD_DIM
    seg_ones = (seg[:, None] == seg[None, :]).astype(BF16)
    tile4 = lambda g: jnp.tile(g.reshape(-1), lanes // HEAD_DIM).reshape(1, lanes)
    smem = pl.BlockSpec(memory_space=pltpu.SMEM)
    return pl.pallas_call(
        functools.partial(_attn_prompt_kernel, blocks_per_seq=blocks_per_seq),
        out_shape=(jax.ShapeDtypeStruct((m, ATTN_WIDTH), BF16),
                   jax.ShapeDtypeStruct((m, KV_WIDTH), F32)),
        grid=(nb,),
        in_specs=[pl.BlockSpec((blk, ATTN_WIDTH), lambda r: (r, COL_AQ // ATTN_WIDTH)),
                  pl.BlockSpec((blk, KV_WIDTH), lambda r: (r, COL_AK // KV_WIDTH)),
                  pl.BlockSpec((blk, KV_WIDTH), lambda r: (r, COL_AV // KV_WIDTH)),
                  pl.BlockSpec((blk, 2 * blk), lambda r: (0, 0)),
                  pl.BlockSpec((lanes, lanes), lambda r: (0, 0)),
                  smem, smem,
                  pl.BlockSpec((1, lanes), lambda r: (0, 0)),
                  pl.BlockSpec((1, lanes), lambda r: (0, 0))],
        out_specs=(pl.BlockSpec((blk, ATTN_WIDTH), lambda r: (r, 0)),
                   pl.BlockSpec((blk, KV_WIDTH), lambda r: (r, 0))),
        scratch_shapes=[pltpu.VMEM((2 * N_ATTN_HEADS * blk, 2 * blk), F32),
                        pltpu.VMEM((N_KV_HEADS, 2 * blk, 2 * HEAD_DIM), BF16),
                        pltpu.VMEM((N_KV_HEADS, 2 * blk, 4 * HEAD_DIM), BF16)],
        compiler_params=_params(1),
        name="attn_prompt",
    )(proj, proj, proj, bucket_tile, seg_ones, rel_bias, sinks, tile4(g_q), tile4(g_k))


def _attn_sample_kernel(q_ref, knew_ref, vnew_ref, ck_ref, cv_ref, bucket_ref, rbt_ref, sinks_ref,
                        gq_ref, gk_ref, o_ref, cko_ref, cvo_ref):
    win = ck_ref.shape[2]
    bucket = bucket_ref[...]
    bias = jnp.full((N_ATTN_HEADS, bucket.shape[1]), -jnp.inf, F32)
    for b in range(N_BUCKETS):
        bias = jnp.where(bucket == b, rbt_ref[:, b:b + 1], bias)
    bias_c = bias[:, :win]
    bias_n = bias[:, win:win + 1]
    sink = sinks_ref[...]
    head_group = jnp.right_shift(lax.broadcasted_iota(jnp.int32, (1, N_ATTN_HEADS, HEAD_DIM), 1),
                                 int(math.log2(GQA_GROUP)))
    lane_group = jnp.right_shift(lax.broadcasted_iota(jnp.int32, (1, 1, KV_WIDTH), 2),
                                 int(math.log2(HEAD_DIM)))
    scale = HEAD_DIM ** -0.5

    k_rows = knew_ref[...]
    k_sq = k_rows * k_rows
    inv = jnp.zeros_like(k_rows)
    for j in range(N_KV_HEADS):
        ms = jnp.sum(jnp.where(lane_group == j, k_sq, 0.0), axis=-1, keepdims=True) / HEAD_DIM
        inv = jnp.where(lane_group == j, lax.rsqrt(ms + EPS), inv)
    kn_rows = k_rows * inv * gk_ref[...]
    v_rows = vnew_ref[...]

    qn = _rms(q_ref[...], gq_ref[...])
    q_bd = jnp.concatenate([jnp.where(head_group == j, qn, 0.0) for j in range(N_KV_HEADS)], axis=2)
    s_c = jnp.einsum('bhf,bfk->bhk', q_bd.astype(BF16), ck_ref[...].astype(BF16),
                     preferred_element_type=F32) * scale + bias_c
    s_n = jnp.sum(q_bd * kn_rows, axis=-1, keepdims=True) * scale + bias_n
    m = jnp.maximum(jnp.maximum(jnp.max(s_c, axis=-1, keepdims=True), s_n), sink)
    e_c = jnp.exp(s_c - m)
    e_n = jnp.exp(s_n - m)
    denom = jnp.sum(e_c, axis=-1, keepdims=True) + e_n + jnp.exp(sink - m)
    o_full = jnp.einsum('bhk,bfk->bhf', (e_c / denom).astype(BF16), cv_ref[...].astype(BF16),
                        preferred_element_type=F32)
    o_full = o_full + (e_n / denom) * v_rows
    o = jnp.zeros(q_ref.shape, F32)
    for j in range(N_KV_HEADS):
        o = jnp.where(head_group == j, o_full[:, :, j * HEAD_DIM:(j + 1) * HEAD_DIM], o)
    o_ref[...] = o.astype(o_ref.dtype)

    eye = (lax.broadcasted_iota(jnp.int32, (1, KV_WIDTH, KV_WIDTH), 1)
           == lax.broadcasted_iota(jnp.int32, (1, KV_WIDTH, KV_WIDTH), 2))
    last_key = lax.broadcasted_iota(jnp.int32, (1, 1, win), 2) == win - 1

    def slide(window, new_rows):
        new_col = jnp.sum(jnp.where(eye, new_rows, 0.0), axis=-1, keepdims=True)
        return jnp.where(last_key, new_col, pltpu.roll(window, win - 1, axis=2))

    cko_ref[...] = slide(ck_ref[...], kn_rows)
    cvo_ref[...] = slide(cv_ref[...], v_rows)


def _attn_sample(q, k_new, v_new, cache_k, cache_v, bucket_row, rel_bias_t, sinks_col, g_q, g_k_row, bt):
    nb = q.shape[0]
    win = cache_k.shape[2]
    full = lambda shape: pl.BlockSpec(shape, lambda i: (0,) * len(shape))
    return pl.pallas_call(
        _attn_sample_kernel,
        out_shape=(jax.ShapeDtypeStruct((nb, N_ATTN_HEADS, HEAD_DIM), BF16),
                   jax.ShapeDtypeStruct(cache_k.shape, F32),
                   jax.ShapeDtypeStruct(cache_v.shape, F32)),
        grid=(nb // bt,),
        in_specs=[pl.BlockSpec((bt, N_ATTN_HEADS, HEAD_DIM), lambda i: (i, 0, 0)),
                  pl.BlockSpec((bt, 1, KV_WIDTH), lambda i: (i, 0, 0)),
                  pl.BlockSpec((bt, 1, KV_WIDTH), lambda i: (i, 0, 0)),
                  pl.BlockSpec((bt, KV_WIDTH, win), lambda i: (i, 0, 0)),
                  pl.BlockSpec((bt, KV_WIDTH, win), lambda i: (i, 0, 0)),
                  full(bucket_row.shape), full(rel_bias_t.shape), full(sinks_col.shape),
                  full(g_q.shape), full(g_k_row.shape)],
        out_specs=(pl.BlockSpec((bt, N_ATTN_HEADS, HEAD_DIM), lambda i: (i, 0, 0)),
                   pl.BlockSpec((bt, KV_WIDTH, win), lambda i: (i, 0, 0)),
                   pl.BlockSpec((bt, KV_WIDTH, win), lambda i: (i, 0, 0))),
        compiler_params=_params(1),
        name="attn_sample",
    )(q, k_new, v_new, cache_k, cache_v, bucket_row, rel_bias_t, sinks_col, g_q, g_k_row)


def _ret_prompt_kernel(q_ref, k_ref, v_ref, gate_ref, cos_ref, sin_ref, dmask_ref, qdec_ref, kdec_ref,
                       cdec_ref, o_ref, state_ref, s_ref):
    c = pl.program_id(0)

    @pl.when(c == 0)
    def _():
        s_ref[...] = jnp.zeros_like(s_ref)

    cosf = cos_ref[...]
    sinf = sin_ref[...]
    half = RET_KDIM // 2
    nt = (((1,), (1,)), ((), ()))
    tn = (((0,), (0,)), ((), ()))

    def rotary(x):
        return x * cosf + pltpu.roll(x, half, axis=1) * sinf

    for h in range(N_RET_HEADS):
        kd = slice(h * RET_KDIM, (h + 1) * RET_KDIM)
        vd = slice(h * RET_VDIM, (h + 1) * RET_VDIM)
        for b in range(q_ref.shape[0]):
            q = rotary(q_ref[b, :, kd])
            k = rotary(k_ref[b, :, kd]) * RET_KDIM ** -0.5
            v = v_ref[b, :, vd].astype(BF16)
            s0 = s_ref[b, h]
            scores = lax.dot_general(q.astype(BF16), k.astype(BF16), nt,
                                     preferred_element_type=F32) * dmask_ref[h]
            o_intra = jnp.dot(scores.astype(BF16), v, preferred_element_type=F32)
            o_inter = jnp.dot((q * qdec_ref[h]).astype(BF16), s0.astype(BF16), preferred_element_type=F32)
            s_ref[b, h] = cdec_ref[h] * s0 + lax.dot_general((k * kdec_ref[h]).astype(BF16), v, tn,
                                                              preferred_element_type=F32)
            ro = o_intra + o_inter
            ro = ro * lax.rsqrt(jnp.mean(ro * ro, axis=-1, keepdims=True) + EPS)
            o_ref[b, :, vd] = (jax.nn.silu(gate_ref[b, :, vd]) * ro).astype(o_ref.dtype)

    @pl.when(c == pl.num_programs(0) - 1)
    def _():
        state_ref[...] = s_ref[...]


def _ret_prompt(proj, cosf, sinf, dmask, qdec, kdec, cdec, n_seq):
    m, width = proj.shape
    seq = m // n_seq
    ch = RET_CHUNK
    proj3 = proj.reshape(n_seq, seq, width)
    const3 = pl.BlockSpec((N_RET_HEADS, ch, ch), lambda c: (0, 0, 0))
    state_shape = (n_seq, N_RET_HEADS, RET_KDIM, RET_VDIM)
    o, state = pl.pallas_call(
        _ret_prompt_kernel,
        out_shape=(jax.ShapeDtypeStruct((n_seq, seq, RET_V_WIDTH), BF16),
                   jax.ShapeDtypeStruct(state_shape, F32)),
        grid=(seq // ch,),
        in_specs=[pl.BlockSpec((n_seq, ch, RET_QK_WIDTH), lambda c: (0, c, COL_RQ // RET_QK_WIDTH)),
                  pl.BlockSpec((n_seq, ch, RET_QK_WIDTH), lambda c: (0, c, COL_RK // RET_QK_WIDTH)),
                  pl.BlockSpec((n_seq, ch, RET_V_WIDTH), lambda c: (0, c, COL_RV // RET_V_WIDTH)),
                  pl.BlockSpec((n_seq, ch, RET_V_WIDTH), lambda c: (0, c, COL_RG // RET_V_WIDTH)),
                  pl.BlockSpec((ch, RET_KDIM), lambda c: (c, 0)),
                  pl.BlockSpec((ch, RET_KDIM), lambda c: (c, 0)),
                  const3, const3, const3,
                  pl.BlockSpec(memory_space=pltpu.SMEM)],
        out_specs=(pl.BlockSpec((n_seq, ch, RET_V_WIDTH), lambda c: (0, c, 0)),
                   pl.BlockSpec(state_shape, lambda c: (0, 0, 0, 0))),
        scratch_shapes=[pltpu.VMEM(state_shape, F32)],
        compiler_params=_params(1),
        name="ret_prompt",
    )(proj3, proj3, proj3, proj3, cosf, sinf, dmask, qdec, kdec, cdec)
    return o.reshape(m, RET_V_WIDTH), state


def _ret_sample_kernel(qr_ref, qt_ref, kt_ref, cosr_ref, sinr_ref, cos_ref, sin_ref, v_ref, gate_ref, s_ref,
                       qdec_ref, kdec_ref, cdec_ref, o_ref, so_ref):
    bt = s_ref.shape[0]
    half = RET_KDIM // 2
    cos = cos_ref[...]
    sin = sin_ref[...]
    cosr = cosr_ref[...]
    sinr = sinr_ref[...]

    def rotary(x):
        x1, x2 = x[:half, :], x[half:, :]
        return jnp.concatenate([x1 * cos - x2 * sin, x2 * cos + x1 * sin], axis=0)

    for h in range(N_RET_HEADS):
        vd = slice(h * RET_VDIM, (h + 1) * RET_VDIM)
        q_all = rotary(qt_ref[0, h])
        k_all = rotary(kt_ref[0, h]) * RET_KDIM ** -0.5
        qk_all = jnp.sum(q_all * k_all, axis=0, keepdims=True)
        kd_all = k_all * kdec_ref[h]
        q_rows = qr_ref[0, :, h * RET_KDIM:(h + 1) * RET_KDIM]
        q_rows = q_rows * cosr + pltpu.roll(q_rows, half, axis=1) * sinr
        qd_rows = (q_rows * qdec_ref[h]).astype(BF16)
        for b in range(bt):
            v = v_ref[0, b:b + 1, vd]
            s0 = s_ref[b, h]
            o_intra = qk_all[:, b:b + 1] * v
            o_inter = jnp.dot(qd_rows, s0.astype(BF16), preferred_element_type=F32)[b:b + 1, :]
            so_ref[b, h] = cdec_ref[h] * s0 + kd_all[:, b:b + 1] * v
            ro = o_intra + o_inter
            ro = ro * lax.rsqrt(jnp.mean(ro * ro, axis=-1, keepdims=True) + EPS)
            o_ref[0, b:b + 1, vd] = (jax.nn.silu(gate_ref[0, b:b + 1, vd]) * ro).astype(o_ref.dtype)


def _ret_sample(qr, qt, kt, cos_row, sin_row, cos_col, sin_col, v, gate, state, qdec, kdec, cdec):
    nbt, _, _, bt = qt.shape
    smem = pl.BlockSpec(memory_space=pltpu.SMEM)
    st_spec = pl.BlockSpec((bt, N_RET_HEADS, RET_KDIM, RET_VDIM), lambda i: (i, 0, 0, 0))
    qk_spec = pl.BlockSpec((1, N_RET_HEADS, RET_KDIM, bt), lambda i: (i, 0, 0, 0))
    row_spec = pl.BlockSpec((1, bt, RET_V_WIDTH), lambda i: (i, 0, 0))
    col_spec = pl.BlockSpec(cos_col.shape, lambda i: (0, 0))
    trig_row_spec = pl.BlockSpec(cos_row.shape, lambda i: (0, 0))
    q_row_spec = pl.BlockSpec((1, bt, RET_QK_WIDTH), lambda i: (i, 0, 0))
    return pl.pallas_call(
        _ret_sample_kernel,
        out_shape=(jax.ShapeDtypeStruct((nbt, bt, RET_V_WIDTH), BF16),
                   jax.ShapeDtypeStruct(state.shape, F32)),
        grid=(nbt,),
        in_specs=[q_row_spec, qk_spec, qk_spec, trig_row_spec, trig_row_spec, col_spec, col_spec,
                  row_spec, row_spec, st_spec, smem, smem, smem],
        out_specs=(row_spec, st_spec),
        compiler_params=_params(1),
        name="ret_sample",
    )(qr, qt, kt, cos_row, sin_row, cos_col, sin_col, v, gate, state, qdec, kdec, cdec)


NPF = np.float32


def _t5_bucket(dist):
    n = np.maximum(dist, 0)
    max_exact = N_BUCKETS // 2
    nf = np.maximum(n, 1).astype(NPF)
    large = max_exact + (np.log(nf / NPF(max_exact)) / NPF(math.log(MAX_DISTANCE / max_exact))
                         * NPF(N_BUCKETS - max_exact)).astype(np.int32)
    return np.where(n < max_exact, n, np.minimum(large, N_BUCKETS - 1)).astype(np.int32)


def _rope_tables(pos):
    half = RET_KDIM // 2
    inv = NPF(ROPE_BASE) ** (-np.arange(half, dtype=NPF) / NPF(half))
    ang = pos.astype(NPF)[:, None] * inv[None]
    return np.cos(ang), np.sin(ang)


def _layer(xp, xs, pp, ps, cache_k, cache_v, state_ret, state_conv, rel_bias, lp, n_seq):
    (g_mix, w_in, g_q, g_k, sinks, w_out, g_ffn, w_up, conv_w, conv_b, w_down,
     g_ple, w_ple_gate, w_ple_proj) = lp
    mp = xp.shape[0]
    ms = xs.shape[0]
    seq = mp // n_seq
    win = cache_k.shape[1]

    row = lambda a: a.reshape(1, -1)

    log_decay = np.log(NPF(1.0) - NPF(2.0) ** (NPF(-5.0) - np.arange(N_RET_HEADS, dtype=NPF)))
    idx = np.arange(RET_CHUNK, dtype=NPF)
    diff = idx[:, None] - idx[None, :]
    dmask = np.where(diff[None] >= 0, np.exp(diff[None] * log_decay[:, None, None]), NPF(0.0))
    q_dec = np.exp((idx + NPF(1.0))[:, None] * log_decay[None])
    k_dec = np.exp((NPF(RET_CHUNK - 1.0) - idx)[:, None] * log_decay[None])
    qdec_t = np.broadcast_to(q_dec.T[:, :, None], (N_RET_HEADS, RET_CHUNK, RET_KDIM))
    kdec_t = np.broadcast_to(k_dec.T[:, :, None], (N_RET_HEADS, RET_CHUNK, RET_KDIM))
    cdec = np.exp(NPF(RET_CHUNK) * log_decay)
    cos_p, sin_p = _rope_tables(np.arange(seq, dtype=np.int32))
    cosf = np.concatenate([cos_p, cos_p], axis=1)
    sinf = np.concatenate([-sin_p, sin_p], axis=1)
    one = np.arange(1, dtype=NPF)
    qdec_s = np.exp((one + NPF(1.0))[:, None] * log_decay[None])[0]
    kdec_s = np.exp((NPF(1.0 - 1.0) - one)[:, None] * log_decay[None])[0]
    cdec_s = np.exp(NPF(1.0) * log_decay)
    cos_s, sin_s = _rope_tables(PAST_LEN + np.arange(1, dtype=np.int32))
    cos_col, sin_col = cos_s.reshape(-1, 1), sin_s.reshape(-1, 1)

    qi = np.arange(ATTN_BLOCK, dtype=np.int32)
    ki = np.arange(2 * ATTN_BLOCK, dtype=np.int32) - ATTN_BLOCK
    dist = qi[:, None] - ki[None, :]
    bucket_tile = np.where((dist >= 0) & (dist <= WINDOW), _t5_bucket(dist), -1).astype(np.int32)
    dist_s = win - np.arange(2 * ATTN_BLOCK, dtype=np.int32)
    bucket_row = np.where((dist_s >= 0) & (dist_s <= WINDOW), _t5_bucket(dist_s), -1).astype(np.int32).reshape(1, -1)

    tmp, tms = _row_tile(mp), _row_tile(ms)
    proj_s, w_in_b = _matmul(_norm(xs, row(g_mix), tms), w_in, tms, COL_TILE, IN_SRC_BLOCKS)
    proj_p, = _matmul(_norm(xp, row(g_mix), NORM_TILE), w_in_b, min(mp, IN_ROW_TILE), COL_TILE)

    attn_p, kn_p = _attn_prompt(proj_p, bucket_tile, rel_bias, sinks, row(g_q), row(g_k), n_seq)
    ret_p, ret_state_p = _ret_prompt(proj_p, cosf, sinf, dmask, qdec_t, kdec_t, cdec, n_seq)

    bt_a = 16
    keys_minor = lambda c: c.reshape(ms, win, KV_WIDTH).transpose(0, 2, 1)
    keys_major = lambda c: c.transpose(0, 2, 1).reshape(ms, win, N_KV_HEADS, HEAD_DIM)
    attn_s, ck_new, cv_new = _attn_sample(
        proj_s[:, COL_AQ:COL_AQ + ATTN_WIDTH].reshape(ms, N_ATTN_HEADS, HEAD_DIM),
        proj_s[:, COL_AK:COL_AK + KV_WIDTH].reshape(ms, 1, KV_WIDTH),
        proj_s[:, COL_AV:COL_AV + KV_WIDTH].reshape(ms, 1, KV_WIDTH),
        keys_minor(cache_k), keys_minor(cache_v),
        bucket_row, rel_bias.T, sinks.reshape(-1, 1), row(g_q), jnp.tile(g_k, N_KV_HEADS).reshape(1, -1), bt_a)
    attn_s = attn_s.reshape(ms, ATTN_WIDTH)

    bt_r = 8
    to_cols = lambda a: a.reshape(ms // bt_r, bt_r, N_RET_HEADS, RET_KDIM).transpose(0, 2, 3, 1)
    rq_s = proj_s[:, COL_RQ:COL_RQ + RET_QK_WIDTH]
    ret_s, ret_state_s = _ret_sample(
        rq_s.reshape(ms // bt_r, bt_r, RET_QK_WIDTH), to_cols(rq_s), to_cols(proj_s[:, COL_RK:COL_RK + RET_QK_WIDTH]),
        np.concatenate([cos_s, cos_s], axis=1), np.concatenate([-sin_s, sin_s], axis=1), cos_col, sin_col,
        proj_s[:, COL_RV:COL_RV + RET_V_WIDTH].reshape(ms // bt_r, bt_r, RET_V_WIDTH),
        proj_s[:, COL_RG:COL_RG + RET_V_WIDTH].reshape(ms // bt_r, bt_r, RET_V_WIDTH),
        state_ret, qdec_s, kdec_s, cdec_s)
    ret_s = ret_s.reshape(ms, RET_V_WIDTH)

    xs, w_out_b = _out_proj(attn_s, ret_s, w_out, xs, tms, COL_TILE)
    xp, = _out_proj(attn_p, ret_p, w_out_b, xp, tmp, OUT_COL_TILE)

    act_s, conv_s, w_up_b = _convglu_sample(_norm(xs, row(g_ffn), tms), w_up, conv_w, row(conv_b), state_conv)
    act_p, conv_p = _convglu_prompt(_norm(xp, row(g_ffn), NORM_TILE), w_up_b, conv_w, row(conv_b), n_seq, UP_ROW_TILE)
    for part in range(DOWN_K_PARTS):
        last = part == DOWN_K_PARTS - 1
        xs, w_down_b = _down_proj(act_s, w_down, xs, tms, COL_TILE, DOWN_K_PARTS, part)
        xp, *stats_p = _down_proj(act_p, w_down_b, xp, tmp, COL_TILE, DOWN_K_PARTS, part,
                                  row(g_ple) if last else None)
    hp, ssq_p = stats_p

    xs, w_pgate_b, w_pproj_b = _ple(xs, _norm(xs, row(g_ple), tms), None, w_ple_gate, ps, w_ple_proj, tms, COL_TILE)
    xp, = _ple(xp, hp, ssq_p, w_pgate_b, pp, w_pproj_b, tmp, COL_TILE)

    last = lambda a: a.reshape(n_seq, seq, a.shape[1])[:, seq - WINDOW:, :]
    kp_new = last(kn_p).reshape(n_seq, WINDOW, N_KV_HEADS, HEAD_DIM)
    vp_new = last(proj_p)[:, :, COL_AV:COL_AV + KV_WIDTH].reshape(n_seq, WINDOW, N_KV_HEADS, HEAD_DIM)
    ks_new = keys_major(ck_new)
    vs_new = keys_major(cv_new)
    return xp, xs, kp_new, vp_new, ret_state_p, conv_p, ks_new, vs_new, ret_state_s, conv_s


def kernel(x_prompt, x_sample, p_prompt, p_sample, cache_win_k, cache_win_v, state_ret, state_conv, rel_bias, g_mix, w_in, g_q, g_k, sinks, w_out, g_ffn, w_up, conv_w, conv_b, w_down, g_ple, w_ple_gate, w_ple_proj):
    depth = g_mix.shape[0]
    n_seq, seq, d = x_prompt.shape
    nb, dec_seq, _ = x_sample.shape
    assert dec_seq == 1 and seq % 512 == 0 and d == D_MODEL
    xp = x_prompt.reshape(n_seq * seq, d)
    xs = x_sample.reshape(nb, d)
    outs = [[] for _ in range(8)]
    for l in range(depth):
        lp = (g_mix[l], w_in[l], g_q[l], g_k[l], sinks[l], w_out[l], g_ffn[l], w_up[l], conv_w[l],
              conv_b[l], w_down[l], g_ple[l], w_ple_gate[l], w_ple_proj[l])
        res = _layer(xp, xs, p_prompt[l].reshape(n_seq * seq, -1), p_sample[l].reshape(nb, -1),
                     cache_win_k[l], cache_win_v[l], state_ret[l], state_conv[l], rel_bias, lp, n_seq)
        xp, xs = res[0], res[1]
        for o, r in zip(outs, res[2:]):
            o.append(r)
    stacked = [jnp.stack(o) for o in outs]
    return (xp.reshape(n_seq, seq, d), xs.reshape(nb, 1, d), *stacked)
```

```python
import functools
import math

import jax
import jax.numpy as jnp
import numpy as np
from jax import lax
from jax.experimental import pallas as pl
from jax.experimental.pallas import tpu as pltpu

F32 = jnp.float32
BF16 = jnp.bfloat16

D_MODEL = 4096
HEAD_DIM = 64
N_ATTN_HEADS = 32
N_KV_HEADS = 4
GQA_GROUP = N_ATTN_HEADS // N_KV_HEADS
WINDOW = 128
ATTN_BLOCK = 128
N_BUCKETS = 32
MAX_DISTANCE = 128
N_RET_HEADS = 8
RET_KDIM = 128
RET_VDIM = 256
RET_CHUNK = 128
ROPE_BASE = 10000.0
D_FF = 11008
CONV_W = 3
EPS = 1e-6

ATTN_WIDTH = N_ATTN_HEADS * HEAD_DIM
KV_WIDTH = N_KV_HEADS * HEAD_DIM
RET_QK_WIDTH = N_RET_HEADS * RET_KDIM
RET_V_WIDTH = N_RET_HEADS * RET_VDIM
IN_SIZES = (ATTN_WIDTH, KV_WIDTH, KV_WIDTH, RET_QK_WIDTH, RET_QK_WIDTH, RET_V_WIDTH, RET_V_WIDTH)
IN_WIDTH = sum(IN_SIZES)
IN_SPLITS = tuple(sum(IN_SIZES[:n + 1]) for n in range(len(IN_SIZES) - 1))
PAST_LEN = 8192

COL_AQ = 0
COL_RV = ATTN_WIDTH
COL_RG = COL_RV + RET_V_WIDTH
COL_RQ = COL_RG + RET_V_WIDTH
COL_RK = COL_RQ + RET_QK_WIDTH
COL_AK = COL_RK + RET_QK_WIDTH
COL_AV = COL_AK + KV_WIDTH

VMEM_LIMIT_BYTES = 56 * 1024 * 1024
FF_TILE = 256
N_FF_TILES = D_FF // FF_TILE
NORM_ROWS = 128
NORM_TILE = 1024
OUT_COL_TILE = 1024
F32_SUBLANES = 8
STAT_LANES = 128
IN_ROW_TILE = 2048
ROW_TILE = 1024
COL_TILE = 512
UP_ROW_TILE = 2048
CONV_CHUNK = 1024
DOWN_K_PARTS = 2


def _row_tile(m):
    return min(m, ROW_TILE)


def _in_src_blocks():
    start = dict(zip(("aq", "ak", "av", "rq", "rk", "rv", "rg"), (0,) + IN_SPLITS))
    order = (("aq", ATTN_WIDTH), ("rv", RET_V_WIDTH), ("rg", RET_V_WIDTH), ("rq", RET_QK_WIDTH),
             ("rk", RET_QK_WIDTH), ("ak", KV_WIDTH), ("av", KV_WIDTH))
    cols = [c for name, width in order for c in range(start[name], start[name] + width, KV_WIDTH)]
    per_tile = COL_TILE // KV_WIDTH
    blocks = []
    for b in range(0, len(cols), per_tile):
        group = cols[b:b + per_tile]
        assert group[0] % COL_TILE == 0 and all(c == group[0] + u * KV_WIDTH for u, c in enumerate(group))
        blocks.append(group[0] // COL_TILE)
    return tuple(blocks)


IN_SRC_BLOCKS = _in_src_blocks()


def _params(n_axes):
    return pltpu.CompilerParams(dimension_semantics=("arbitrary",) * n_axes,
                                vmem_limit_bytes=VMEM_LIMIT_BYTES)


def _rms(x, g):
    y = x * lax.rsqrt(jnp.mean(x * x, axis=-1, keepdims=True) + EPS)
    return y * g


def _norm_rows_to_bf16(x_ref, g_ref, h_ref):
    def body(c, carry):
        r0 = pl.multiple_of(c * NORM_ROWS, NORM_ROWS)
        h_ref[pl.ds(r0, NORM_ROWS), :] = _rms(x_ref[pl.ds(r0, NORM_ROWS), :], g_ref[...]).astype(h_ref.dtype)
        return carry
    lax.fori_loop(0, x_ref.shape[0] // NORM_ROWS, body, 0)


def _norm(x, g, tr):
    m, k = x.shape
    return pl.pallas_call(
        _norm_rows_to_bf16,
        out_shape=jax.ShapeDtypeStruct((m, k), BF16),
        grid=(m // tr,),
        in_specs=[pl.BlockSpec((tr, k), lambda i: (i, 0)),
                  pl.BlockSpec((1, k), lambda i: (0, 0))],
        out_specs=pl.BlockSpec((tr, k), lambda i: (i, 0)),
        compiler_params=_params(1),
        name="rmsnorm",
    )(x, g)


def _mxu_weights(w_ref, wb_ref):
    if not wb_ref:
        return w_ref[...]
    wb = w_ref[...].astype(BF16)
    wb_ref[0][...] = wb
    return wb


def _mm_kernel(h_ref, w_ref, o_ref, *wb_ref):
    o_ref[...] = jnp.dot(h_ref[...], _mxu_weights(w_ref, wb_ref), preferred_element_type=F32)


def _matmul(h, w, tm, tn, src_blocks=None):
    m, k = h.shape
    n = w.shape[1]
    emit = w.dtype != BF16
    if emit:
        w_map = lambda i, j: (0, sum(jnp.where(j == d, s, 0) for d, s in enumerate(src_blocks)))
    else:
        w_map = lambda i, j: (0, j)
    out_shape = [jax.ShapeDtypeStruct((m, n), F32)]
    out_specs = [pl.BlockSpec((tm, tn), lambda i, j: (i, j))]
    if emit:
        assert m == tm
        out_shape.append(jax.ShapeDtypeStruct((k, n), BF16))
        out_specs.append(pl.BlockSpec((k, tn), lambda i, j: (0, j)))
    return pl.pallas_call(
        _mm_kernel,
        out_shape=out_shape,
        grid=(m // tm, n // tn),
        in_specs=[pl.BlockSpec((tm, k), lambda i, j: (i, 0)),
                  pl.BlockSpec((k, tn), w_map)],
        out_specs=out_specs,
        compiler_params=_params(2),
        name="in_proj",
    )(h, w)


def _row_scale(ssq_ref, width, n_cols):
    r = lax.rsqrt(ssq_ref[...] / width + EPS)
    return jnp.concatenate([r] * (n_cols // STAT_LANES), axis=1)


def _residual_mm_kernel(*refs, n_lhs, emit_weights, norm_stats):
    lhs_refs = refs[:n_lhs]
    w_ref, x_ref = refs[n_lhs], refs[n_lhs + 1]
    rest = list(refs[n_lhs + 2:])
    g_ref = rest.pop(0) if norm_stats else None
    o_ref = rest.pop(0)
    wb_ref = (rest.pop(0),) if emit_weights else ()
    lhs = jnp.concatenate([r[...] for r in lhs_refs], axis=1) if n_lhs > 1 else lhs_refs[0][...]
    o = x_ref[...] + jnp.dot(lhs, _mxu_weights(w_ref, wb_ref), preferred_element_type=F32)
    o_ref[...] = o
    if norm_stats:
        hb_ref, ssq_ref = rest
        hb_ref[...] = (o * g_ref[...]).astype(hb_ref.dtype)
        part = jnp.broadcast_to(jnp.sum(o * o, axis=-1, keepdims=True), ssq_ref.shape)
        j = pl.program_id(1)

        @pl.when(j == 0)
        def _():
            ssq_ref[...] = part

        @pl.when(j > 0)
        def _():
            ssq_ref[...] += part


def _residual_matmul(lhs, lhs_specs, w, w_spec, wb_shape, x, g_next, tm, tn, name):
    m, n = x.shape
    emit = w.dtype != BF16
    stats = g_next is not None
    tile = pl.BlockSpec((tm, tn), lambda i, j: (i, j))
    operands = [*lhs, w, x]
    in_specs = [*lhs_specs, w_spec, tile]
    out_shape = [jax.ShapeDtypeStruct((m, n), F32)]
    out_specs = [tile]
    if stats:
        operands.append(g_next)
        in_specs.append(pl.BlockSpec((1, tn), lambda i, j: (0, j)))
    if emit:
        assert m == tm
        out_shape.append(jax.ShapeDtypeStruct(wb_shape, BF16))
        out_specs.append(pl.BlockSpec((wb_shape[0], tn), lambda i, j: (0, j)))
    if stats:
        out_shape += [jax.ShapeDtypeStruct((m, n), BF16), jax.ShapeDtypeStruct((m, STAT_LANES), F32)]
        out_specs += [tile, pl.BlockSpec((tm, STAT_LANES), lambda i, j: (i, 0))]
    return pl.pallas_call(
        functools.partial(_residual_mm_kernel, n_lhs=len(lhs), emit_weights=emit, norm_stats=stats),
        out_shape=out_shape,
        grid=(m // tm, n // tn),
        in_specs=in_specs,
        out_specs=out_specs,
        compiler_params=_params(2),
        name=name,
    )(*operands)


def _out_proj(a, r, w, x, tm, tn, g_next=None):
    ka, kr = a.shape[1], r.shape[1]
    lhs_specs = [pl.BlockSpec((tm, ka), lambda i, j: (i, 0)), pl.BlockSpec((tm, kr), lambda i, j: (i, 0))]
    w_spec = pl.BlockSpec((ka + kr, tn), lambda i, j: (0, j))
    return _residual_matmul([a, r], lhs_specs, w, w_spec, w.shape, x, g_next, tm, tn, "out_proj")


def _down_proj(a, w, x, tm, tn, k_parts, part, g_next=None):
    k = a.shape[1] // k_parts
    w_part = part if w.dtype != BF16 else 0
    lhs_specs = [pl.BlockSpec((tm, k), lambda i, j: (i, part))]
    w_spec = pl.BlockSpec((k, tn), lambda i, j: (w_part, j))
    return _residual_matmul([a], lhs_specs, w, w_spec, (k, w.shape[1]), x, g_next, tm, tn, "down_proj")


def _ple_kernel(*refs, emit_weights, row_scaled):
    h_ref, wg_ref, p_ref, wp_ref, x_ref = refs[:5]
    rest = list(refs[5:])
    ssq_ref = rest.pop(0) if row_scaled else None
    o_ref = rest.pop(0)
    wb_refs = tuple(rest) if emit_weights else ()
    z = jnp.dot(h_ref[...], _mxu_weights(wg_ref, wb_refs[:1]), preferred_element_type=F32)
    if row_scaled:
        z = z * _row_scale(ssq_ref, h_ref.shape[1], z.shape[1])
    pp = jnp.dot(p_ref[...].astype(BF16), _mxu_weights(wp_ref, wb_refs[1:]), preferred_element_type=F32)
    o_ref[...] = x_ref[...] + jax.nn.sigmoid(z) * pp


def _ple(x, h, ssq, wg, p, wp, tm, tn):
    m, k = h.shape
    n = wg.shape[1]
    kp = p.shape[1]
    emit = wg.dtype != BF16
    scaled = ssq is not None
    operands = [h, wg, p, wp, x]
    resident = dict(pipeline_mode=pl.Buffered(1)) if tn > COL_TILE else {}
    in_specs = [pl.BlockSpec((tm, k), lambda i, j: (i, 0), **resident),
                pl.BlockSpec((k, tn), lambda i, j: (0, j)),
                pl.BlockSpec((tm, kp), lambda i, j: (i, 0), **resident),
                pl.BlockSpec((kp, tn), lambda i, j: (0, j)),
                pl.BlockSpec((tm, tn), lambda i, j: (i, j))]
    if scaled:
        operands.append(ssq)
        in_specs.append(pl.BlockSpec((tm, STAT_LANES), lambda i, j: (i, 0), **resident))
    out_shape = [jax.ShapeDtypeStruct((m, n), F32)]
    out_specs = [pl.BlockSpec((tm, tn), lambda i, j: (i, j))]
    if emit:
        assert m == tm
        out_shape += [jax.ShapeDtypeStruct(wg.shape, BF16), jax.ShapeDtypeStruct(wp.shape, BF16)]
        out_specs += [pl.BlockSpec((k, tn), lambda i, j: (0, j)), pl.BlockSpec((kp, tn), lambda i, j: (0, j))]
    return pl.pallas_call(
        functools.partial(_ple_kernel, emit_weights=emit, row_scaled=scaled),
        out_shape=out_shape,
        grid=(m // tm, n // tn),
        in_specs=in_specs,
        out_specs=out_specs,
        compiler_params=_params(2),
        name="ple",
    )(*operands)


def _gelu_erf(x):
    return 0.5 * x * (1.0 + lax.erf(x * math.sqrt(0.5)))


def _conv_taps(cb_ref, cw_ref, um2, um1, u):
    c = cb_ref[...] + cw_ref[0:1, :] * um2
    c = c + cw_ref[1:2, :] * um1
    return c + cw_ref[2:3, :] * u


def _convglu_prompt_kernel(h_ref, w_ref, cwg_ref, cwv_ref, cbg_ref, cbv_ref,
                           a_ref, sg_ref, sv_ref, carry_g_ref, carry_v_ref, *, tiles_per_seq):
    i = pl.program_id(0)
    j = pl.program_id(1)
    tm = h_ref.shape[0]
    rows = min(tm, CONV_CHUNK)
    seq_start = (i % tiles_per_seq) == 0
    sub = F32_SUBLANES
    row = lax.broadcasted_iota(jnp.int32, (sub, 1), 0)

    def half(u, cw_ref, cb_ref, prev):
        r1 = pltpu.roll(u, 1, axis=0)
        r2 = pltpu.roll(u, 2, axis=0)
        head1 = jnp.where(row == 0, prev[1:2, :], r1[:sub, :])
        head2 = jnp.where(row == 0, prev[0:1, :], jnp.where(row == 1, prev[1:2, :], r2[:sub, :]))
        um1 = jnp.concatenate([head1, r1[sub:, :]], axis=0)
        um2 = jnp.concatenate([head2, r2[sub:, :]], axis=0)
        return _conv_taps(cb_ref, cw_ref, um2, um1, u), u[rows - (CONV_W - 1):, :]

    prev_g = jnp.where(seq_start, 0.0, carry_g_ref[j])
    prev_v = jnp.where(seq_start, 0.0, carry_v_ref[j])
    for c in range(tm // rows):
        rs = slice(c * rows, (c + 1) * rows)
        u = jnp.dot(h_ref[rs, :], w_ref[...], preferred_element_type=F32)
        cg, prev_g = half(u[:, :FF_TILE], cwg_ref, cbg_ref, prev_g)
        cv, prev_v = half(u[:, FF_TILE:], cwv_ref, cbv_ref, prev_v)
        a_ref[rs, :] = (_gelu_erf(cg) * cv).astype(a_ref.dtype)
    carry_g_ref[j] = prev_g
    carry_v_ref[j] = prev_v
    sg_ref[0] = prev_g
    sv_ref[0] = prev_v


def _convglu_prompt(h, w_pairs, conv_w, conv_b, n_seq, tm):
    m, k = h.shape
    seq = m // n_seq
    tiles_per_seq = seq // tm
    nt = N_FF_TILES
    tn = FF_TILE
    a, sg, sv = pl.pallas_call(
        functools.partial(_convglu_prompt_kernel, tiles_per_seq=tiles_per_seq),
        out_shape=(jax.ShapeDtypeStruct((m, D_FF), BF16),
                   jax.ShapeDtypeStruct((m // tm, CONV_W - 1, D_FF), F32),
                   jax.ShapeDtypeStruct((m // tm, CONV_W - 1, D_FF), F32)),
        grid=(m // tm, nt),
        in_specs=[pl.BlockSpec((tm, k), lambda i, j: (i, 0)),
                  pl.BlockSpec((k, 2 * tn), lambda i, j: (0, j)),
                  pl.BlockSpec((CONV_W, tn), lambda i, j: (0, j)),
                  pl.BlockSpec((CONV_W, tn), lambda i, j: (0, j + nt)),
                  pl.BlockSpec((1, tn), lambda i, j: (0, j)),
                  pl.BlockSpec((1, tn), lambda i, j: (0, j + nt))],
        out_specs=(pl.BlockSpec((tm, tn), lambda i, j: (i, j)),
                   pl.BlockSpec((1, CONV_W - 1, tn), lambda i, j: (i, 0, j)),
                   pl.BlockSpec((1, CONV_W - 1, tn), lambda i, j: (i, 0, j))),
        scratch_shapes=[pltpu.VMEM((nt, CONV_W - 1, tn), F32),
                        pltpu.VMEM((nt, CONV_W - 1, tn), F32)],
        compiler_params=_params(2),
        name="convglu_prompt",
    )(h, w_pairs, conv_w, conv_w, conv_b, conv_b)
    tails = jnp.concatenate([sg, sv], axis=-1)
    return a, tails[tiles_per_seq - 1::tiles_per_seq]


def _convglu_sample_kernel(h_ref, wg_ref, wv_ref, cwg_ref, cwv_ref, cbg_ref, cbv_ref,
                           pg_ref, pv_ref, a_ref, ng_ref, nv_ref, wb_ref):
    def half(w_ref, wb_cols, cw_ref, cb_ref, p_ref, n_ref):
        u = jnp.dot(h_ref[...], _mxu_weights(w_ref, (wb_ref.at[:, wb_cols],)), preferred_element_type=F32)
        p0, p1 = p_ref[:, 0, :], p_ref[:, 1, :]
        n_ref[:, 0, :] = p1
        n_ref[:, 1, :] = u
        return _conv_taps(cb_ref, cw_ref, p0, p1, u)

    cg = half(wg_ref, slice(0, FF_TILE), cwg_ref, cbg_ref, pg_ref, ng_ref)
    cv = half(wv_ref, slice(FF_TILE, 2 * FF_TILE), cwv_ref, cbv_ref, pv_ref, nv_ref)
    a_ref[...] = (_gelu_erf(cg) * cv).astype(a_ref.dtype)


def _convglu_sample(h, w_up, conv_w, conv_b, state_conv):
    m, k = h.shape
    nt = N_FF_TILES
    tn = FF_TILE
    col = lambda off: (lambda j: (0, j + off))
    rows3 = lambda off: pl.BlockSpec((m, CONV_W - 1, tn), lambda j: (0, 0, j + off))
    a, new_g, new_v, w_pairs = pl.pallas_call(
        _convglu_sample_kernel,
        out_shape=(jax.ShapeDtypeStruct((m, D_FF), BF16),
                   jax.ShapeDtypeStruct((m, CONV_W - 1, D_FF), F32),
                   jax.ShapeDtypeStruct((m, CONV_W - 1, D_FF), F32),
                   jax.ShapeDtypeStruct((k, 2 * D_FF), BF16)),
        grid=(nt,),
        in_specs=[pl.BlockSpec((m, k), lambda j: (0, 0)),
                  pl.BlockSpec((k, tn), col(0)),
                  pl.BlockSpec((k, tn), col(nt)),
                  pl.BlockSpec((CONV_W, tn), col(0)),
                  pl.BlockSpec((CONV_W, tn), col(nt)),
                  pl.BlockSpec((1, tn), col(0)),
                  pl.BlockSpec((1, tn), col(nt)),
                  rows3(0), rows3(nt)],
        out_specs=(pl.BlockSpec((m, tn), col(0)),
                   rows3(0), rows3(0),
                   pl.BlockSpec((k, 2 * tn), col(0))),
        compiler_params=_params(1),
        name="convglu_sample",
    )(h, w_up, w_up, conv_w, conv_w, conv_b, conv_b, state_conv, state_conv)
    return a, jnp.concatenate([new_g, new_v], axis=-1), w_pairs


def _segment_mean_square(x, seg_ones):
    sq = x * x
    hi = sq.astype(BF16)
    lo = (sq - hi.astype(F32)).astype(BF16)
    total = (jnp.dot(hi, seg_ones, preferred_element_type=F32)
             + jnp.dot(lo, seg_ones, preferred_element_type=F32))
    return total / HEAD_DIM


def _attn_prompt_kernel(q_ref, kc_ref, vc_ref, bucket_ref, seg_ones_ref, rb_ref, sinks_ref, gq_ref, gk_ref,
                        o_ref, kn_ref, bias_ref, kdup_ref, vdup_ref, *, blocks_per_seq):
    r = pl.program_id(0)
    blk = ATTN_BLOCK
    pair = 2 * HEAD_DIM
    n_bias_rows = N_ATTN_HEADS * blk
    seq_start = (r % blocks_per_seq) == 0

    @pl.when(r == 0)
    def _():
        bucket = bucket_ref[...]
        prev_cols = lax.broadcasted_iota(jnp.int32, bucket.shape, 1) < blk

        def per_head(h, carry):
            acc = jnp.full(bucket.shape, -jnp.inf, F32)
            for b in range(N_BUCKETS):
                acc = jnp.where(bucket == b, rb_ref[b, h], acc)
            row0 = pl.multiple_of(h * blk, blk)
            bias_ref[pl.ds(row0, blk), :] = acc
            bias_ref[pl.ds(n_bias_rows + row0, blk), :] = jnp.where(prev_cols, -jnp.inf, acc)
            return carry
        lax.fori_loop(0, N_ATTN_HEADS, per_head, 0)
        vdup_ref[:, :, pair:] = jnp.ones((N_KV_HEADS, 2 * blk, pair), BF16)

    @pl.when(seq_start)
    def _():
        kdup_ref[:, :blk, :] = jnp.zeros((N_KV_HEADS, blk, pair), BF16)
        vdup_ref[:, :blk, :pair] = jnp.zeros((N_KV_HEADS, blk, pair), BF16)

    @pl.when(jnp.logical_not(seq_start))
    def _():
        kdup_ref[:, :blk, :] = kdup_ref[:, blk:, :]
        vdup_ref[:, :blk, :pair] = vdup_ref[:, blk:, :pair]

    seg_ones = seg_ones_ref[...]
    low_half = lax.broadcasted_iota(jnp.int32, (blk, pair), 1) < HEAD_DIM

    def duplicate_half(x, odd):
        swapped = pltpu.roll(x, HEAD_DIM, axis=1)
        return jnp.where(low_half, swapped, x) if odd else jnp.where(low_half, x, swapped)

    kc = kc_ref[...]
    kn = kc * lax.rsqrt(_segment_mean_square(kc, seg_ones) + EPS) * gk_ref[...]
    kn_ref[...] = kn
    for j in range(N_KV_HEADS):
        col = slice((j // 2) * pair, (j // 2 + 1) * pair)
        kdup_ref[j, blk:, :] = duplicate_half(kn[:, col], j % 2).astype(BF16)
        vdup_ref[j, blk:, :pair] = duplicate_half(vc_ref[:, col], j % 2).astype(BF16)

    bias_base = jnp.where(seq_start, n_bias_rows, 0)
    chunk = 2 * pair
    n_heads = N_ATTN_HEADS
    q_gain = gq_ref[...] * HEAD_DIM ** -0.5
    q_chunks = [q_ref[:, c * chunk:(c + 1) * chunk] for c in range(ATTN_WIDTH // chunk)]
    q_ms = [_segment_mean_square(qc, seg_ones) for qc in q_chunks]
    q_norm = [qc * lax.rsqrt(ms + EPS) * q_gain for qc, ms in zip(q_chunks, q_ms)]
    lhs = []
    for qn in q_norm:
        for p in range(chunk // pair):
            qp = qn[:, p * pair:(p + 1) * pair]
            lhs.append(jnp.where(low_half, qp, 0.0).astype(BF16))
            lhs.append(jnp.where(low_half, 0.0, qp).astype(BF16))
    scores = [lax.dot_general(jnp.concatenate(lhs[j * GQA_GROUP:(j + 1) * GQA_GROUP], axis=0), kdup_ref[j],
                              (((1,), (1,)), ((), ())), preferred_element_type=F32)
              for j in range(N_KV_HEADS)]
    sg = [scores[h // GQA_GROUP][(h % GQA_GROUP) * blk:(h % GQA_GROUP + 1) * blk, :]
          + bias_ref[pl.ds(pl.multiple_of(bias_base + h * blk, blk), blk), :] for h in range(n_heads)]
    m = [jnp.maximum(jnp.max(sg[h], axis=-1, keepdims=True), sinks_ref[h]) for h in range(n_heads)]
    exps = [jnp.exp(sg[h] - m[h]).astype(BF16) for h in range(n_heads)]
    sink_terms = [jnp.exp(sinks_ref[h] - m[h]) for h in range(n_heads)]
    outs = [jnp.dot(jnp.concatenate(exps[j * GQA_GROUP:(j + 1) * GQA_GROUP], axis=0), vdup_ref[j],
                    preferred_element_type=F32) for j in range(N_KV_HEADS)]
    for h in range(0, n_heads, 2):
        o = outs[h // GQA_GROUP]
        g = h % GQA_GROUP
        even = o[g * blk:(g + 1) * blk, :]
        odd = o[(g + 1) * blk:(g + 2) * blk, :]
        num = jnp.where(low_half, even[:, :pair], odd[:, :pair])
        den = (jnp.where(low_half, even[:, pair:], odd[:, pair:])
               + jnp.where(low_half, sink_terms[h], sink_terms[h + 1]))
        o_ref[:, h * HEAD_DIM:h * HEAD_DIM + pair] = (num / den).astype(o_ref.dtype)


def _attn_prompt(proj, bucket_tile, rel_bias, sinks, g_q, g_k, n_seq):
    m = proj.shape[0]
    blk = ATTN_BLOCK
    nb = m // blk
    blocks_per_seq = nb // n_seq
    lanes = 4 * HEAD_DIM
    seg = np.arange(lanes) // HEAD_DIM
    seg_ones = (seg[:, None] == seg[None, :]).astype(BF16)
    tile4 = lambda g: jnp.tile(g.reshape(-1), lanes // HEAD_DIM).reshape(1, lanes)
    smem = pl.BlockSpec(memory_space=pltpu.SMEM)
    return pl.pallas_call(
        functools.partial(_attn_prompt_kernel, blocks_per_seq=blocks_per_seq),
        out_shape=(jax.ShapeDtypeStruct((m, ATTN_WIDTH), BF16),
                   jax.ShapeDtypeStruct((m, KV_WIDTH), F32)),
        grid=(nb,),
        in_specs=[pl.BlockSpec((blk, ATTN_WIDTH), lambda r: (r, COL_AQ // ATTN_WIDTH)),
                  pl.BlockSpec((blk, KV_WIDTH), lambda r: (r, COL_AK // KV_WIDTH)),
                  pl.BlockSpec((blk, KV_WIDTH), lambda r: (r, COL_AV // KV_WIDTH)),
                  pl.BlockSpec((blk, 2 * blk), lambda r: (0, 0)),
                  pl.BlockSpec((lanes, lanes), lambda r: (0, 0)),
                  smem, smem,
                  pl.BlockSpec((1, lanes), lambda r: (0, 0)),
                  pl.BlockSpec((1, lanes), lambda r: (0, 0))],
        out_specs=(pl.BlockSpec((blk, ATTN_WIDTH), lambda r: (r, 0)),
                   pl.BlockSpec((blk, KV_WIDTH), lambda r: (r, 0))),
        scratch_shapes=[pltpu.VMEM((2 * N_ATTN_HEADS * blk, 2 * blk), F32),
                        pltpu.VMEM((N_KV_HEADS, 2 * blk, 2 * HEAD_DIM), BF16),
                        pltpu.VMEM((N_KV_HEADS, 2 * blk, 4 * HEAD_DIM), BF16)],
        compiler_params=_params(1),
        name="attn_prompt",
    )(proj, proj, proj, bucket_tile, seg_ones, rel_bias, sinks, tile4(g_q), tile4(g_k))


def _attn_sample_kernel(q_ref, knew_ref, vnew_ref, ck_ref, cv_ref, bucket_ref, rbt_ref, sinks_ref,
                        gq_ref, gk_ref, o_ref, cko_ref, cvo_ref):
    win = ck_ref.shape[2]
    bucket = bucket_ref[...]
    bias = jnp.full((N_ATTN_HEADS, bucket.shape[1]), -jnp.inf, F32)
    for b in range(N_BUCKETS):
        bias = jnp.where(bucket == b, rbt_ref[:, b:b + 1], bias)
    bias_c = bias[:, :win]
    bias_n = bias[:, win:win + 1]
    sink = sinks_ref[...]
    head_group = jnp.right_shift(lax.broadcasted_iota(jnp.int32, (1, N_ATTN_HEADS, HEAD_DIM), 1),
                                 int(math.log2(GQA_GROUP)))
    lane_group = jnp.right_shift(lax.broadcasted_iota(jnp.int32, (1, 1, KV_WIDTH), 2),
                                 int(math.log2(HEAD_DIM)))
    scale = HEAD_DIM ** -0.5

    k_rows = knew_ref[...]
    k_sq = k_rows * k_rows
    inv = jnp.zeros_like(k_rows)
    for j in range(N_KV_HEADS):
        ms = jnp.sum(jnp.where(lane_group == j, k_sq, 0.0), axis=-1, keepdims=True) / HEAD_DIM
        inv = jnp.where(lane_group == j, lax.rsqrt(ms + EPS), inv)
    kn_rows = k_rows * inv * gk_ref[...]
    v_rows = vnew_ref[...]

    qn = _rms(q_ref[...], gq_ref[...])
    q_bd = jnp.concatenate([jnp.where(head_group == j, qn, 0.0) for j in range(N_KV_HEADS)], axis=2)
    s_c = jnp.einsum('bhf,bfk->bhk', q_bd.astype(BF16), ck_ref[...].astype(BF16),
                     preferred_element_type=F32) * scale + bias_c
    s_n = jnp.sum(q_bd * kn_rows, axis=-1, keepdims=True) * scale + bias_n
    m = jnp.maximum(jnp.maximum(jnp.max(s_c, axis=-1, keepdims=True), s_n), sink)
    e_c = jnp.exp(s_c - m)
    e_n = jnp.exp(s_n - m)
    denom = jnp.sum(e_c, axis=-1, keepdims=True) + e_n + jnp.exp(sink - m)
    o_full = jnp.einsum('bhk,bfk->bhf', (e_c / denom).astype(BF16), cv_ref[...].astype(BF16),
                        preferred_element_type=F32)
    o_full = o_full + (e_n / denom) * v_rows
    o = jnp.zeros(q_ref.shape, F32)
    for j in range(N_KV_HEADS):
        o = jnp.where(head_group == j, o_full[:, :, j * HEAD_DIM:(j + 1) * HEAD_DIM], o)
    o_ref[...] = o.astype(o_ref.dtype)

    eye = (lax.broadcasted_iota(jnp.int32, (1, KV_WIDTH, KV_WIDTH), 1)
           == lax.broadcasted_iota(jnp.int32, (1, KV_WIDTH, KV_WIDTH), 2))
    last_key = lax.broadcasted_iota(jnp.int32, (1, 1, win), 2) == win - 1

    def slide(window, new_rows):
        new_col = jnp.sum(jnp.where(eye, new_rows, 0.0), axis=-1, keepdims=True)
        return jnp.where(last_key, new_col, pltpu.roll(window, win - 1, axis=2))

    cko_ref[...] = slide(ck_ref[...], kn_rows)
    cvo_ref[...] = slide(cv_ref[...], v_rows)


def _attn_sample(q, k_new, v_new, cache_k, cache_v, bucket_row, rel_bias_t, sinks_col, g_q, g_k_row, bt):
    nb = q.shape[0]
    win = cache_k.shape[2]
    full = lambda shape: pl.BlockSpec(shape, lambda i: (0,) * len(shape))
    return pl.pallas_call(
        _attn_sample_kernel,
        out_shape=(jax.ShapeDtypeStruct((nb, N_ATTN_HEADS, HEAD_DIM), BF16),
                   jax.ShapeDtypeStruct(cache_k.shape, F32),
                   jax.ShapeDtypeStruct(cache_v.shape, F32)),
        grid=(nb // bt,),
        in_specs=[pl.BlockSpec((bt, N_ATTN_HEADS, HEAD_DIM), lambda i: (i, 0, 0)),
                  pl.BlockSpec((bt, 1, KV_WIDTH), lambda i: (i, 0, 0)),
                  pl.BlockSpec((bt, 1, KV_WIDTH), lambda i: (i, 0, 0)),
                  pl.BlockSpec((bt, KV_WIDTH, win), lambda i: (i, 0, 0)),
                  pl.BlockSpec((bt, KV_WIDTH, win), lambda i: (i, 0, 0)),
                  full(bucket_row.shape), full(rel_bias_t.shape), full(sinks_col.shape),
                  full(g_q.shape), full(g_k_row.shape)],
        out_specs=(pl.BlockSpec((bt, N_ATTN_HEADS, HEAD_DIM), lambda i: (i, 0, 0)),
                   pl.BlockSpec((bt, KV_WIDTH, win), lambda i: (i, 0, 0)),
                   pl.BlockSpec((bt, KV_WIDTH, win), lambda i: (i, 0, 0))),
        compiler_params=_params(1),
        name="attn_sample",
    )(q, k_new, v_new, cache_k, cache_v, bucket_row, rel_bias_t, sinks_col, g_q, g_k_row)


def _ret_prompt_kernel(q_ref, k_ref, v_ref, gate_ref, cos_ref, sin_ref, dmask_ref, qdec_ref, kdec_ref,
                       cdec_ref, o_ref, state_ref, s_ref):
    c = pl.program_id(0)

    @pl.when(c == 0)
    def _():
        s_ref[...] = jnp.zeros_like(s_ref)

    cosf = cos_ref[...]
    sinf = sin_ref[...]
    half = RET_KDIM // 2
    nt = (((1,), (1,)), ((), ()))
    tn = (((0,), (0,)), ((), ()))

    def rotary(x):
        return x * cosf + pltpu.roll(x, half, axis=1) * sinf

    for h in range(N_RET_HEADS):
        kd = slice(h * RET_KDIM, (h + 1) * RET_KDIM)
        vd = slice(h * RET_VDIM, (h + 1) * RET_VDIM)
        for b in range(q_ref.shape[0]):
            q = rotary(q_ref[b, :, kd])
            k = rotary(k_ref[b, :, kd]) * RET_KDIM ** -0.5
            v = v_ref[b, :, vd].astype(BF16)
            s0 = s_ref[b, h]
            scores = lax.dot_general(q.astype(BF16), k.astype(BF16), nt,
                                     preferred_element_type=F32) * dmask_ref[h]
            o_intra = jnp.dot(scores.astype(BF16), v, preferred_element_type=F32)
            o_inter = jnp.dot((q * qdec_ref[h]).astype(BF16), s0.astype(BF16), preferred_element_type=F32)
            s_ref[b, h] = cdec_ref[h] * s0 + lax.dot_general((k * kdec_ref[h]).astype(BF16), v, tn,
                                                              preferred_element_type=F32)
            ro = o_intra + o_inter
            ro = ro * lax.rsqrt(jnp.mean(ro * ro, axis=-1, keepdims=True) + EPS)
            o_ref[b, :, vd] = (jax.nn.silu(gate_ref[b, :, vd]) * ro).astype(o_ref.dtype)

    @pl.when(c == pl.num_programs(0) - 1)
    def _():
        state_ref[...] = s_ref[...]


def _ret_prompt(proj, cosf, sinf, dmask, qdec, kdec, cdec, n_seq):
    m, width = proj.shape
    seq = m // n_seq
    ch = RET_CHUNK
    proj3 = proj.reshape(n_seq, seq, width)
    const3 = pl.BlockSpec((N_RET_HEADS, ch, ch), lambda c: (0, 0, 0))
    state_shape = (n_seq, N_RET_HEADS, RET_KDIM, RET_VDIM)
    o, state = pl.pallas_call(
        _ret_prompt_kernel,
        out_shape=(jax.ShapeDtypeStruct((n_seq, seq, RET_V_WIDTH), BF16),
                   jax.ShapeDtypeStruct(state_shape, F32)),
        grid=(seq // ch,),
        in_specs=[pl.BlockSpec((n_seq, ch, RET_QK_WIDTH), lambda c: (0, c, COL_RQ // RET_QK_WIDTH)),
                  pl.BlockSpec((n_seq, ch, RET_QK_WIDTH), lambda c: (0, c, COL_RK // RET_QK_WIDTH)),
                  pl.BlockSpec((n_seq, ch, RET_V_WIDTH), lambda c: (0, c, COL_RV // RET_V_WIDTH)),
                  pl.BlockSpec((n_seq, ch, RET_V_WIDTH), lambda c: (0, c, COL_RG // RET_V_WIDTH)),
                  pl.BlockSpec((ch, RET_KDIM), lambda c: (c, 0)),
                  pl.BlockSpec((ch, RET_KDIM), lambda c: (c, 0)),
                  const3, const3, const3,
                  pl.BlockSpec(memory_space=pltpu.SMEM)],
        out_specs=(pl.BlockSpec((n_seq, ch, RET_V_WIDTH), lambda c: (0, c, 0)),
                   pl.BlockSpec(state_shape, lambda c: (0, 0, 0, 0))),
        scratch_shapes=[pltpu.VMEM(state_shape, F32)],
        compiler_params=_params(1),
        name="ret_prompt",
    )(proj3, proj3, proj3, proj3, cosf, sinf, dmask, qdec, kdec, cdec)
    return o.reshape(m, RET_V_WIDTH), state


def _ret_sample_kernel(qr_ref, qt_ref, kt_ref, cosr_ref, sinr_ref, cos_ref, sin_ref, v_ref, gate_ref, s_ref,
                       qdec_ref, kdec_ref, cdec_ref, o_ref, so_ref):
    bt = s_ref.shape[0]
    half = RET_KDIM // 2
    cos = cos_ref[...]
    sin = sin_ref[...]
    cosr = cosr_ref[...]
    sinr = sinr_ref[...]

    def rotary(x):
        x1, x2 = x[:half, :], x[half:, :]
        return jnp.concatenate([x1 * cos - x2 * sin, x2 * cos + x1 * sin], axis=0)

    for h in range(N_RET_HEADS):
        vd = slice(h * RET_VDIM, (h + 1) * RET_VDIM)
        q_all = rotary(qt_ref[0, h])
        k_all = rotary(kt_ref[0, h]) * RET_KDIM ** -0.5
        qk_all = jnp.sum(q_all * k_all, axis=0, keepdims=True)
        kd_all = k_all * kdec_ref[h]
        q_rows = qr_ref[0, :, h * RET_KDIM:(h + 1) * RET_KDIM]
        q_rows = q_rows * cosr + pltpu.roll(q_rows, half, axis=1) * sinr
        qd_rows = (q_rows * qdec_ref[h]).astype(BF16)
        for b in range(bt):
            v = v_ref[0, b:b + 1, vd]
            s0 = s_ref[b, h]
            o_intra = qk_all[:, b:b + 1] * v
            o_inter = jnp.dot(qd_rows, s0.astype(BF16), preferred_element_type=F32)[b:b + 1, :]
            so_ref[b, h] = cdec_ref[h] * s0 + kd_all[:, b:b + 1] * v
            ro = o_intra + o_inter
            ro = ro * lax.rsqrt(jnp.mean(ro * ro, axis=-1, keepdims=True) + EPS)
            o_ref[0, b:b + 1, vd] = (jax.nn.silu(gate_ref[0, b:b + 1, vd]) * ro).astype(o_ref.dtype)


def _ret_sample(qr, qt, kt, cos_row, sin_row, cos_col, sin_col, v, gate, state, qdec, kdec, cdec):
    nbt, _, _, bt = qt.shape
    smem = pl.BlockSpec(memory_space=pltpu.SMEM)
    st_spec = pl.BlockSpec((bt, N_RET_HEADS, RET_KDIM, RET_VDIM), lambda i: (i, 0, 0, 0))
    qk_spec = pl.BlockSpec((1, N_RET_HEADS, RET_KDIM, bt), lambda i: (i, 0, 0, 0))
    row_spec = pl.BlockSpec((1, bt, RET_V_WIDTH), lambda i: (i, 0, 0))
    col_spec = pl.BlockSpec(cos_col.shape, lambda i: (0, 0))
    trig_row_spec = pl.BlockSpec(cos_row.shape, lambda i: (0, 0))
    q_row_spec = pl.BlockSpec((1, bt, RET_QK_WIDTH), lambda i: (i, 0, 0))
    return pl.pallas_call(
        _ret_sample_kernel,
        out_shape=(jax.ShapeDtypeStruct((nbt, bt, RET_V_WIDTH), BF16),
                   jax.ShapeDtypeStruct(state.shape, F32)),
        grid=(nbt,),
        in_specs=[q_row_spec, qk_spec, qk_spec, trig_row_spec, trig_row_spec, col_spec, col_spec,
                  row_spec, row_spec, st_spec, smem, smem, smem],
        out_specs=(row_spec, st_spec),
        compiler_params=_params(1),
        name="ret_sample",
    )(qr, qt, kt, cos_row, sin_row, cos_col, sin_col, v, gate, state, qdec, kdec, cdec)


NPF = np.float32


def _t5_bucket(dist):
    n = np.maximum(dist, 0)
    max_exact = N_BUCKETS // 2
    nf = np.maximum(n, 1).astype(NPF)
    large = max_exact + (np.log(nf / NPF(max_exact)) / NPF(math.log(MAX_DISTANCE / max_exact))
                         * NPF(N_BUCKETS - max_exact)).astype(np.int32)
    return np.where(n < max_exact, n, np.minimum(large, N_BUCKETS - 1)).astype(np.int32)


def _rope_tables(pos):
    half = RET_KDIM // 2
    inv = NPF(ROPE_BASE) ** (-np.arange(half, dtype=NPF) / NPF(half))
    ang = pos.astype(NPF)[:, None] * inv[None]
    return np.cos(ang), np.sin(ang)


def _layer(xp, xs, pp, ps, cache_k, cache_v, state_ret, state_conv, rel_bias, lp, n_seq):
    (g_mix, w_in, g_q, g_k, sinks, w_out, g_ffn, w_up, conv_w, conv_b, w_down,
     g_ple, w_ple_gate, w_ple_proj) = lp
    mp = xp.shape[0]
    ms = xs.shape[0]
    seq = mp // n_seq
    win = cache_k.shape[1]

    row = lambda a: a.reshape(1, -1)

    log_decay = np.log(NPF(1.0) - NPF(2.0) ** (NPF(-5.0) - np.arange(N_RET_HEADS, dtype=NPF)))
    idx = np.arange(RET_CHUNK, dtype=NPF)
    diff = idx[:, None] - idx[None, :]
    dmask = np.where(diff[None] >= 0, np.exp(diff[None] * log_decay[:, None, None]), NPF(0.0))
    q_dec = np.exp((idx + NPF(1.0))[:, None] * log_decay[None])
    k_dec = np.exp((NPF(RET_CHUNK - 1.0) - idx)[:, None] * log_decay[None])
    qdec_t = np.broadcast_to(q_dec.T[:, :, None], (N_RET_HEADS, RET_CHUNK, RET_KDIM))
    kdec_t = np.broadcast_to(k_dec.T[:, :, None], (N_RET_HEADS, RET_CHUNK, RET_KDIM))
    cdec = np.exp(NPF(RET_CHUNK) * log_decay)
    cos_p, sin_p = _rope_tables(np.arange(seq, dtype=np.int32))
    cosf = np.concatenate([cos_p, cos_p], axis=1)
    sinf = np.concatenate([-sin_p, sin_p], axis=1)
    one = np.arange(1, dtype=NPF)
    qdec_s = np.exp((one + NPF(1.0))[:, None] * log_decay[None])[0]
    kdec_s = np.exp((NPF(1.0 - 1.0) - one)[:, None] * log_decay[None])[0]
    cdec_s = np.exp(NPF(1.0) * log_decay)
    cos_s, sin_s = _rope_tables(PAST_LEN + np.arange(1, dtype=np.int32))
    cos_col, sin_col = cos_s.reshape(-1, 1), sin_s.reshape(-1, 1)

    qi = np.arange(ATTN_BLOCK, dtype=np.int32)
    ki = np.arange(2 * ATTN_BLOCK, dtype=np.int32) - ATTN_BLOCK
    dist = qi[:, None] - ki[None, :]
    bucket_tile = np.where((dist >= 0) & (dist <= WINDOW), _t5_bucket(dist), -1).astype(np.int32)
    dist_s = win - np.arange(2 * ATTN_BLOCK, dtype=np.int32)
    bucket_row = np.where((dist_s >= 0) & (dist_s <= WINDOW), _t5_bucket(dist_s), -1).astype(np.int32).reshape(1, -1)

    tmp, tms = _row_tile(mp), _row_tile(ms)
    proj_s, w_in_b = _matmul(_norm(xs, row(g_mix), tms), w_in, tms, COL_TILE, IN_SRC_BLOCKS)
    proj_p, = _matmul(_norm(xp, row(g_mix), NORM_TILE), w_in_b, min(mp, IN_ROW_TILE), COL_TILE)

    attn_p, kn_p = _attn_prompt(proj_p, bucket_tile, rel_bias, sinks, row(g_q), row(g_k), n_seq)
    ret_p, ret_state_p = _ret_prompt(proj_p, cosf, sinf, dmask, qdec_t, kdec_t, cdec, n_seq)

    bt_a = 16
    keys_minor = lambda c: c.reshape(ms, win, KV_WIDTH).transpose(0, 2, 1)
    keys_major = lambda c: c.transpose(0, 2, 1).reshape(ms, win, N_KV_HEADS, HEAD_DIM)
    attn_s, ck_new, cv_new = _attn_sample(
        proj_s[:, COL_AQ:COL_AQ + ATTN_WIDTH].reshape(ms, N_ATTN_HEADS, HEAD_DIM),
        proj_s[:, COL_AK:COL_AK + KV_WIDTH].reshape(ms, 1, KV_WIDTH),
        proj_s[:, COL_AV:COL_AV + KV_WIDTH].reshape(ms, 1, KV_WIDTH),
        keys_minor(cache_k), keys_minor(cache_v),
        bucket_row, rel_bias.T, sinks.reshape(-1, 1), row(g_q), jnp.tile(g_k, N_KV_HEADS).reshape(1, -1), bt_a)
    attn_s = attn_s.reshape(ms, ATTN_WIDTH)

    bt_r = 8
    to_cols = lambda a: a.reshape(ms // bt_r, bt_r, N_RET_HEADS, RET_KDIM).transpose(0, 2, 3, 1)
    rq_s = proj_s[:, COL_RQ:COL_RQ + RET_QK_WIDTH]
    ret_s, ret_state_s = _ret_sample(
        rq_s.reshape(ms // bt_r, bt_r, RET_QK_WIDTH), to_cols(rq_s), to_cols(proj_s[:, COL_RK:COL_RK + RET_QK_WIDTH]),
        np.concatenate([cos_s, cos_s], axis=1), np.concatenate([-sin_s, sin_s], axis=1), cos_col, sin_col,
        proj_s[:, COL_RV:COL_RV + RET_V_WIDTH].reshape(ms // bt_r, bt_r, RET_V_WIDTH),
        proj_s[:, COL_RG:COL_RG + RET_V_WIDTH].reshape(ms // bt_r, bt_r, RET_V_WIDTH),
        state_ret, qdec_s, kdec_s, cdec_s)
    ret_s = ret_s.reshape(ms, RET_V_WIDTH)

    xs, w_out_b = _out_proj(attn_s, ret_s, w_out, xs, tms, COL_TILE)
    xp, = _out_proj(attn_p, ret_p, w_out_b, xp, tmp, OUT_COL_TILE)

    act_s, conv_s, w_up_b = _convglu_sample(_norm(xs, row(g_ffn), tms), w_up, conv_w, row(conv_b), state_conv)
    act_p, conv_p = _convglu_prompt(_norm(xp, row(g_ffn), NORM_TILE), w_up_b, conv_w, row(conv_b), n_seq, UP_ROW_TILE)
    for part in range(DOWN_K_PARTS):
        last = part == DOWN_K_PARTS - 1
        xs, w_down_b = _down_proj(act_s, w_down, xs, tms, COL_TILE, DOWN_K_PARTS, part)
        xp, *stats_p = _down_proj(act_p, w_down_b, xp, tmp, COL_TILE, DOWN_K_PARTS, part,
                                  row(g_ple) if last else None)
    hp, ssq_p = stats_p

    xs, w_pgate_b, w_pproj_b = _ple(xs, _norm(xs, row(g_ple), tms), None, w_ple_gate, ps, w_ple_proj, tms, COL_TILE)
    xp, = _ple(xp, hp, ssq_p, w_pgate_b, pp, w_pproj_b, tmp, OUT_COL_TILE)

    last = lambda a: a.reshape(n_seq, seq, a.shape[1])[:, seq - WINDOW:, :]
    kp_new = last(kn_p).reshape(n_seq, WINDOW, N_KV_HEADS, HEAD_DIM)
    vp_new = last(proj_p)[:, :, COL_AV:COL_AV + KV_WIDTH].reshape(n_seq, WINDOW, N_KV_HEADS, HEAD_DIM)
    ks_new = keys_major(ck_new)
    vs_new = keys_major(cv_new)
    return xp, xs, kp_new, vp_new, ret_state_p, conv_p, ks_new, vs_new, ret_state_s, conv_s


def kernel(x_prompt, x_sample, p_prompt, p_sample, cache_win_k, cache_win_v, state_ret, state_conv, rel_bias, g_mix, w_in, g_q, g_k, sinks, w_out, g_ffn, w_up, conv_w, conv_b, w_down, g_ple, w_ple_gate, w_ple_proj):
    depth = g_mix.shape[0]
    n_seq, seq, d = x_prompt.shape
    nb, dec_seq, _ = x_sample.shape
    assert dec_seq == 1 and seq % 512 == 0 and d == D_MODEL
    xp = x_prompt.reshape(n_seq * seq, d)
    xs = x_sample.reshape(nb, d)
    outs = [[] for _ in range(8)]
    for l in range(depth):
        lp = (g_mix[l], w_in[l], g_q[l], g_k[l], sinks[l], w_out[l], g_ffn[l], w_up[l], conv_w[l],
              conv_b[l], w_down[l], g_ple[l], w_ple_gate[l], w_ple_proj[l])
        res = _layer(xp, xs, p_prompt[l].reshape(n_seq * seq, -1), p_sample[l].reshape(nb, -1),
                     cache_win_k[l], cache_win_v[l], state_ret[l], state_conv[l], rel_bias, lp, n_seq)
        xp, xs = res[0], res[1]
        for o, r in zip(outs, res[2:]):
            o.append(r)
    stacked = [jnp.stack(o) for o in outs]
    return (xp.reshape(n_seq, seq, d), xs.reshape(nb, 1, d), *stacked)
```
